```python
import numpy as np
import jax
import jax.numpy as jnp
from jax import lax

D_MODEL = 1024
BATCH = 2
SEQ = 8192
DEPTH = 2

HEAD_DIM = 64
N_HEADS_A = D_MODEL // (2 * HEAD_DIM)
N_KV_GROUPS = 2
HEADS_PER_GROUP = N_HEADS_A // N_KV_GROUPS
A_WIDTH = N_HEADS_A * HEAD_DIM
KV_WIDTH = N_KV_GROUPS * HEAD_DIM
ROPE_DIM = HEAD_DIM // 4
ROPE_THETA = 500000.0
CMP_BLOCK = 32
CMP_STRIDE = 16
CMP_HIDDEN = 2 * HEAD_DIM
SEL_BLOCK = 64
SEL_TOPK = 16
WINDOW = 512
Q_CHUNK = 128
N_BRANCH = 3
CONV_WIDTH = 3
B_WIDTH = D_MODEL - A_WIDTH
P_IN = A_WIDTH + 6 * KV_WIDTH + N_BRANCH * N_HEADS_A + 3 * B_WIDTH
POOL_WINDOWS = (2, 4, 8, 16)
POOL_GROUP = D_MODEL // len(POOL_WINDOWS)
D_FF = 2816
N_EVEN = (DEPTH + 1) // 2
N_ODD = DEPTH // 2
EPS = 1e-6
NEG = -1e30

kernel_name = 'hybrid_nsa_shortconv_pool_macaron'


def rms_norm(x, g):
    xf = x.astype(jnp.float32)
    y = xf * lax.rsqrt(jnp.mean(xf * xf, axis=-1, keepdims=True) + EPS)
    return (y * g.astype(jnp.float32)).astype(x.dtype)


def swiglu(h, wg, wu, wd):
    return (jax.nn.silu(h @ wg) * (h @ wu)) @ wd


def rope_tables(pos):
    freqs = ROPE_THETA ** (-jnp.arange(0, ROPE_DIM, 2, dtype=jnp.float32) / ROPE_DIM)
    ang = pos.astype(jnp.float32)[..., None] * freqs
    return jnp.cos(ang), jnp.sin(ang)


def apply_partial_rope(x, cos, sin):
    half = ROPE_DIM // 2
    shp = cos.shape[:2] + (1,) * (x.ndim - 3) + (half,)
    c = cos.reshape(shp)
    s = sin.reshape(shp)
    xf = x.astype(jnp.float32)
    x1, x2, rest = xf[..., :half], xf[..., half:ROPE_DIM], xf[..., ROPE_DIM:]
    out = jnp.concatenate([x1 * c - x2 * s, x2 * c + x1 * s, rest], axis=-1)
    return out.astype(x.dtype)


def cmp_block_index(seq):
    n_cmp = (seq - CMP_BLOCK) // CMP_STRIDE + 1
    return np.arange(n_cmp)[:, None] * CMP_STRIDE + np.arange(CMP_BLOCK)[None, :]


def compress(kv, pos_emb, w1, w2, idx):
    bsz = kv.shape[0]
    blocks = kv[:, idx] + pos_emb[None, None, :, None, :]
    blocks = blocks.transpose(0, 1, 3, 2, 4).reshape(bsz, idx.shape[0], N_KV_GROUPS, CMP_BLOCK * HEAD_DIM)
    return jax.nn.gelu(blocks @ w1) @ w2


def masked_softmax(s, mask):
    return jax.nn.softmax(jnp.where(mask, s.astype(jnp.float32), NEG), axis=-1)


def nsa_core(q, kc, vc, ks, vs, kw, vw, gates, blk_end):
    bsz, seq = q.shape[0], q.shape[1]
    n_chunks = seq // Q_CHUNK
    n_sel = seq // SEL_BLOCK
    k_eff = min(SEL_TOPK, n_sel)
    scale = HEAD_DIM ** -0.5
    G, Hg, d = N_KV_GROUPS, HEADS_PER_GROUP, HEAD_DIM

    cmp_start = np.arange(blk_end.shape[0]) * CMP_STRIDE
    sel_start_np = np.arange(n_sel) * SEL_BLOCK
    overlap = jnp.asarray(((cmp_start[:, None] < sel_start_np[None, :] + SEL_BLOCK)
                           & (cmp_start[:, None] + CMP_BLOCK > sel_start_np[None, :])).astype(np.float32))
    sel_start = jnp.asarray(sel_start_np)
    blk_end_j = jnp.asarray(blk_end)

    ks_blocks = ks.reshape(bsz, n_sel, SEL_BLOCK, G, d).transpose(0, 3, 1, 2, 4)
    vs_blocks = vs.reshape(bsz, n_sel, SEL_BLOCK, G, d).transpose(0, 3, 1, 2, 4)
    kw_pad = jnp.pad(kw, ((0, 0), (WINDOW, 0), (0, 0), (0, 0)))
    vw_pad = jnp.pad(vw, ((0, 0), (WINDOW, 0), (0, 0), (0, 0)))
    gather_blocks = jax.vmap(jax.vmap(lambda blocks, ix: blocks[ix]))

    def chunk(args):
        c, q_c, g_c = args
        t = c * Q_CHUNK + jnp.arange(Q_CHUNK)

        s = jnp.einsum('bqghd,bngd->bghqn', q_c, kc) * scale
        valid = blk_end_j[None, :] <= t[:, None]
        p_cmp = masked_softmax(s, valid) * jnp.any(valid, axis=-1)[:, None].astype(jnp.float32)
        o_cmp = jnp.einsum('bghqn,bngd->bqghd', p_cmp.astype(vc.dtype), vc)

        imp = jnp.einsum('bghqn,nm->bgqm', p_cmp, overlap)
        cur = (t // SEL_BLOCK) * SEL_BLOCK
        forced = (sel_start[None, :] == cur[:, None]) | (sel_start[None, :] == 0)
        imp = jnp.where(sel_start[None, :] <= t[:, None], imp, -1.0)
        imp = jnp.where(forced, 1e4, imp)
        _, sel = lax.top_k(imp, k_eff)
        kg = gather_blocks(ks_blocks, sel)
        vg = gather_blocks(vs_blocks, sel).reshape(bsz, G, Q_CHUNK, k_eff * SEL_BLOCK, d)
        kpos = sel[..., None] * SEL_BLOCK + jnp.arange(SEL_BLOCK)
        mask_s = (kpos <= t[:, None, None]).reshape(bsz, G, 1, Q_CHUNK, k_eff * SEL_BLOCK)
        s = jnp.einsum('bqghd,bgqksd->bghqks', q_c, kg) * scale
        s = s.reshape(bsz, G, Hg, Q_CHUNK, k_eff * SEL_BLOCK)
        p = masked_softmax(s, mask_s)
        o_slc = jnp.einsum('bghqj,bgqjd->bqghd', p.astype(vg.dtype), vg)

        kwc = lax.dynamic_slice_in_dim(kw_pad, c * Q_CHUNK, WINDOW + Q_CHUNK, axis=1)
        vwc = lax.dynamic_slice_in_dim(vw_pad, c * Q_CHUNK, WINDOW + Q_CHUNK, axis=1)
        kpos_w = c * Q_CHUNK - WINDOW + jnp.arange(WINDOW + Q_CHUNK)
        diff = t[:, None] - kpos_w[None, :]
        mask_w = (diff >= 0) & (diff < WINDOW) & (kpos_w[None, :] >= 0)
        s = jnp.einsum('bqghd,bkgd->bghqk', q_c, kwc) * scale
        p = masked_softmax(s, mask_w)
        o_win = jnp.einsum('bghqk,bkgd->bqghd', p.astype(vwc.dtype), vwc)

        return g_c[..., 0:1] * o_cmp + g_c[..., 1:2] * o_slc + g_c[..., 2:3] * o_win

    q_chunks = q.reshape(bsz, n_chunks, Q_CHUNK, G, Hg, d).swapaxes(0, 1)
    g_chunks = gates.reshape(bsz, n_chunks, Q_CHUNK, G, Hg, N_BRANCH).swapaxes(0, 1)
    out = lax.map(chunk, (jnp.arange(n_chunks), q_chunks, g_chunks))
    return out.swapaxes(0, 1).reshape(bsz, seq, A_WIDTH)


def causal_shift(v, n):
    return jnp.pad(v, ((0, 0), (n, 0), (0, 0)))[:, :v.shape[1]]


def hybrid_mixer(h, positions, w_in, q_norm, k_norm, cmp_pos, cmp_w1, cmp_w2, conv_w, w_out):
    bsz, seq, _ = h.shape
    G, Hg, d = N_KV_GROUPS, HEADS_PER_GROUP, HEAD_DIM
    proj = h @ w_in
    sizes = [A_WIDTH] + [KV_WIDTH] * 6 + [N_BRANCH * N_HEADS_A] + [B_WIDTH] * 3
    q, kc, vc, ks, vs, kw, vw, gates, u, gate_b, gate_c = jnp.split(
        proj, np.cumsum(sizes)[:-1].tolist(), axis=-1)

    cos, sin = rope_tables(positions)
    q = apply_partial_rope(rms_norm(q.reshape(bsz, seq, G, Hg, d), q_norm), cos, sin)
    ks = apply_partial_rope(rms_norm(ks.reshape(bsz, seq, G, d), k_norm[1]), cos, sin)
    kw = apply_partial_rope(rms_norm(kw.reshape(bsz, seq, G, d), k_norm[2]), cos, sin)
    vs = vs.reshape(bsz, seq, G, d)
    vw = vw.reshape(bsz, seq, G, d)
    idx = cmp_block_index(seq)
    blk_end = idx[:, -1]
    kc = compress(kc.reshape(bsz, seq, G, d), cmp_pos[0], cmp_w1[0], cmp_w2[0], idx)
    vc = compress(vc.reshape(bsz, seq, G, d), cmp_pos[1], cmp_w1[1], cmp_w2[1], idx)
    cos_c, sin_c = rope_tables(positions[:, blk_end])
    kc = apply_partial_rope(rms_norm(kc, k_norm[0]), cos_c, sin_c)
    gates = jax.nn.sigmoid(gates.astype(jnp.float32)).reshape(bsz, seq, G, Hg, N_BRANCH).astype(h.dtype)
    o_a = nsa_core(q, kc, vc, ks, vs, kw, vw, gates, blk_end)

    v = gate_c * u
    y = conv_w[2] * v + conv_w[1] * causal_shift(v, 1) + conv_w[0] * causal_shift(v, 2)
    o_b = gate_b * y

    return jnp.concatenate([o_a, o_b], axis=-1) @ w_out


def pool_mixer(h, w, scale):
    seq = h.shape[1]
    hf = h.astype(jnp.float32)
    cs = jnp.concatenate([jnp.zeros_like(hf[:, :1]), jnp.cumsum(hf, axis=1)], axis=1)
    t = np.arange(seq)
    outs = []
    for gi, win in enumerate(POOL_WINDOWS):
        sl = slice(gi * POOL_GROUP, (gi + 1) * POOL_GROUP)
        lo = np.maximum(t + 1 - win, 0)
        cnt = jnp.asarray(np.minimum(t + 1, win).astype(np.float32))
        mean = (cs[:, 1:, sl] - cs[:, lo, sl]) / cnt[None, :, None]
        outs.append((mean - hf[..., sl]).astype(h.dtype) @ w[gi])
    return jnp.concatenate(outs, axis=-1) * scale


def setup_inputs(seed: int = 0) -> dict:
    key = jax.random.key(seed)
    ks = jax.random.split(key, 20)
    f32 = jnp.float32

    def nrm(k, shape, fan_in):
        return jax.random.normal(k, shape, f32) * (fan_in ** -0.5)

    def gain(k, shape):
        return 1.0 + 0.02 * jax.random.normal(k, shape, f32)

    x = jax.random.normal(ks[0], (BATCH, SEQ, D_MODEL), f32)
    positions = jnp.broadcast_to(jnp.arange(SEQ, dtype=jnp.int32), (BATCH, SEQ))
    return {
        'x': x,
        'positions': positions,
        'ffn_norm': gain(ks[1], (DEPTH, 2, D_MODEL)),
        'ffn_w_gate': nrm(ks[2], (DEPTH, 2, D_MODEL, D_FF), D_MODEL),
        'ffn_w_up': nrm(ks[3], (DEPTH, 2, D_MODEL, D_FF), D_MODEL),
        'ffn_w_down': nrm(ks[4], (DEPTH, 2, D_FF, D_MODEL), D_FF),
        'mix_norm': gain(ks[5], (DEPTH, D_MODEL)),
        'hyb_w_in': nrm(ks[6], (N_EVEN, D_MODEL, P_IN), D_MODEL),
        'hyb_q_norm': gain(ks[7], (N_EVEN, HEAD_DIM)),
        'hyb_k_norm': gain(ks[8], (N_EVEN, N_BRANCH, HEAD_DIM)),
        'hyb_cmp_pos': 0.02 * jax.random.normal(ks[9], (N_EVEN, 2, CMP_BLOCK, HEAD_DIM), f32),
        'hyb_cmp_w1': nrm(ks[10], (N_EVEN, 2, CMP_BLOCK * HEAD_DIM, CMP_HIDDEN), CMP_BLOCK * HEAD_DIM),
        'hyb_cmp_w2': nrm(ks[11], (N_EVEN, 2, CMP_HIDDEN, HEAD_DIM), CMP_HIDDEN),
        'hyb_conv_w': nrm(ks[12], (N_EVEN, CONV_WIDTH, B_WIDTH), CONV_WIDTH),
        'hyb_w_out': nrm(ks[13], (N_EVEN, A_WIDTH + B_WIDTH, D_MODEL), A_WIDTH + B_WIDTH),
        'pool_w': nrm(ks[14], (N_ODD, len(POOL_WINDOWS), POOL_GROUP, POOL_GROUP), POOL_GROUP),
        'pool_scale': 0.5 + 0.05 * jax.random.normal(ks[15], (N_ODD, D_MODEL), f32),
    }


def reference(x, positions, ffn_norm, ffn_w_gate, ffn_w_up, ffn_w_down, mix_norm,
              hyb_w_in, hyb_q_norm, hyb_k_norm, hyb_cmp_pos, hyb_cmp_w1, hyb_cmp_w2,
              hyb_conv_w, hyb_w_out, pool_w, pool_scale):
    for layer in range(DEPTH):
        h = rms_norm(x, ffn_norm[layer, 0])
        x = x + 0.5 * swiglu(h, ffn_w_gate[layer, 0], ffn_w_up[layer, 0], ffn_w_down[layer, 0])
        h = rms_norm(x, mix_norm[layer])
        i = layer // 2
        if layer % 2 == 0:
            x = x + hybrid_mixer(h, positions, hyb_w_in[i], hyb_q_norm[i], hyb_k_norm[i],
                                 hyb_cmp_pos[i], hyb_cmp_w1[i], hyb_cmp_w2[i],
                                 hyb_conv_w[i], hyb_w_out[i])
        else:
            x = x + pool_mixer(h, pool_w[i], pool_scale[i])
        h = rms_norm(x, ffn_norm[layer, 1])
        x = x + 0.5 * swiglu(h, ffn_w_gate[layer, 1], ffn_w_up[layer, 1], ffn_w_down[layer, 1])
    return x
```

```python
import functools
import math

import jax
import jax.numpy as jnp
from jax import lax
from jax.experimental import pallas as pl
from jax.experimental.pallas import tpu as pltpu

F32 = jnp.float32
BF16 = jnp.bfloat16

HEAD_DIM = 64
N_GROUPS = 2
HEADS_PER_GROUP = 4
N_HEADS = N_GROUPS * HEADS_PER_GROUP
A_WIDTH = N_HEADS * HEAD_DIM
KV_WIDTH = N_GROUPS * HEAD_DIM
ROPE_DIM = HEAD_DIM // 4
ROPE_HALF = ROPE_DIM // 2
ROPE_THETA = 500000.0
CMP_BLOCK = 32
CMP_STRIDE = 16
CMP_HIDDEN = 2 * HEAD_DIM
SEL_BLOCK = 64
SEL_TOPK = 16
WINDOW = 512
N_BRANCH = 3
CONV_WIDTH = 3
POOL_WINDOWS = (2, 4, 8, 16)
POOL_HALO = 16
EPS = 1e-6
NEG = -1e30

LANES = 128
VMEM_LIMIT_BYTES = 56 * 1024 * 1024


def _cparams(sem):
    return pltpu.CompilerParams(dimension_semantics=sem, vmem_limit_bytes=VMEM_LIMIT_BYTES)


def _dot(a, b):
    return jnp.dot(a, b, preferred_element_type=F32)


def _dot_nt(a, b):
    return lax.dot_general(a, b, (((1,), (1,)), ((), ())), preferred_element_type=F32)


def _dot_split(a, b):
    hi = a.astype(BF16)
    lo = (a - hi.astype(F32)).astype(BF16)
    return _dot(hi, b) + _dot(lo, b)


def _rms(x, g):
    ms = jnp.mean(x * x, axis=-1, keepdims=True)
    return x * lax.rsqrt(ms + EPS) * g


def _group_rms(x, g, pmat):
    ms = _dot_split(x * x, pmat)
    return x * lax.rsqrt(ms + EPS) * g


def _rope_tables(pos_col, freq_row):
    ang = pos_col * freq_row
    lane = lax.broadcasted_iota(jnp.int32, (1, LANES), 1) & (HEAD_DIM - 1)
    sign = jnp.where(lane < ROPE_HALF, -1.0, 1.0).astype(F32)
    return jnp.cos(ang), jnp.sin(ang) * sign


def _rope(x, cos_t, sin_t):
    w = x.shape[1]
    reps = w // LANES
    if reps > 1:
        cos_t = jnp.concatenate([cos_t] * reps, axis=1)
        sin_t = jnp.concatenate([sin_t] * reps, axis=1)
    lane = lax.broadcasted_iota(jnp.int32, (1, w), 1) & (HEAD_DIM - 1)
    partner = jnp.where(lane < ROPE_HALF,
                        pltpu.roll(x, w - ROPE_HALF, axis=1),
                        pltpu.roll(x, ROPE_HALF, axis=1))
    return x * cos_t + partner * sin_t


def _ffn_kernel(x_ref, g_ref, wg_ref, wu_ref, wd_ref, o_ref, h_ref, acc_ref):
    j = pl.program_id(1)

    @pl.when(j == 0)
    def _():
        h_ref[...] = _rms(x_ref[...], g_ref[...]).astype(BF16)
        acc_ref[...] = jnp.zeros_like(acc_ref)

    h = h_ref[...]
    a = _dot(h, wg_ref[...])
    b = _dot(h, wu_ref[...])
    act = a * (1.0 / (1.0 + jnp.exp(-a))) * b
    acc_ref[...] += _dot(act.astype(BF16), wd_ref[...])

    @pl.when(j == pl.num_programs(1) - 1)
    def _():
        o_ref[...] = x_ref[...] + 0.5 * acc_ref[...]


def _ffn(x, g, wg, wu, wd, *, tm=512, tf=1408):
    n, d = x.shape
    dff = wg.shape[1]
    return pl.pallas_call(
        _ffn_kernel,
        out_shape=jax.ShapeDtypeStruct((n, d), F32),
        grid=(n // tm, dff // tf),
        in_specs=[
            pl.BlockSpec((tm, d), lambda i, j: (i, 0)),
            pl.BlockSpec((1, d), lambda i, j: (0, 0)),
            pl.BlockSpec((d, tf), lambda i, j: (0, j)),
            pl.BlockSpec((d, tf), lambda i, j: (0, j)),
            pl.BlockSpec((tf, d), lambda i, j: (j, 0)),
        ],
        out_specs=pl.BlockSpec((tm, d), lambda i, j: (i, 0)),
        scratch_shapes=[pltpu.VMEM((tm, d), BF16), pltpu.VMEM((tm, d), F32)],
        compiler_params=_cparams(("parallel", "arbitrary")),
        name="ffn",
    )(x, g.reshape(1, d), wg, wu, wd)


_C_Q = 0
_C_KS = _C_Q + A_WIDTH
_C_KW = _C_KS + KV_WIDTH
_C_VS = _C_KW + KV_WIDTH
_C_VW = _C_VS + KV_WIDTH
_C_KC = _C_VW + KV_WIDTH
_C_VC = _C_KC + KV_WIDTH
_C_GATE = _C_VC + KV_WIDTH
_C_U = _C_GATE + N_BRANCH * A_WIDTH
_C_GB = _C_U + A_WIDTH
_C_GC = _C_GB + A_WIDTH
_C_END = _C_GC + A_WIDTH
CONV_HALO = 8


def _inproj_kernel(x_ref, pos_ref, g_ref, w_ref, qnw_ref, knw_ref, cw_ref, pq_ref, pk_ref, freq_ref,
                   q_ref, qsw_ref, ks_ref, kw_ref, vs_ref, vw_ref, kc_ref, vc_ref, gates_ref, ob_ref,
                   vext_ref, *, tm, tiles_per_batch):
    i = pl.program_id(0)
    h = _rms(x_ref[...], g_ref[...]).astype(BF16)
    cos_t, sin_t = _rope_tables(pos_ref[...].astype(F32), freq_ref[...])

    def proj(c0, width):
        return _dot(h, w_ref[:, c0:c0 + width])

    q = _group_rms(proj(_C_Q, A_WIDTH), qnw_ref[...], pq_ref[...])
    q = _rope(q, cos_t, sin_t) * (HEAD_DIM ** -0.5)
    q_ref[...] = q.astype(BF16)
    qsw = jnp.concatenate(
        [pltpu.roll(q[:, c * LANES:(c + 1) * LANES], HEAD_DIM, axis=1) for c in range(A_WIDTH // LANES)], axis=1)
    qsw_ref[...] = qsw.astype(BF16)

    ks = _group_rms(proj(_C_KS, KV_WIDTH), knw_ref[1:2, :], pk_ref[...])
    ks_ref[...] = _rope(ks, cos_t, sin_t).astype(BF16)
    kw = _group_rms(proj(_C_KW, KV_WIDTH), knw_ref[2:3, :], pk_ref[...])
    kw_ref[...] = _rope(kw, cos_t, sin_t).astype(BF16)
    vs_ref[...] = proj(_C_VS, KV_WIDTH).astype(BF16)
    vw_ref[...] = proj(_C_VW, KV_WIDTH).astype(BF16)
    kc_ref[...] = proj(_C_KC, KV_WIDTH)
    vc_ref[...] = proj(_C_VC, KV_WIDTH)

    gl = proj(_C_GATE, N_BRANCH * A_WIDTH)
    gates_ref[...] = 1.0 / (1.0 + jnp.exp(-gl))

    @pl.when(i % tiles_per_batch == 0)
    def _():
        vext_ref[0:CONV_HALO, :] = jnp.zeros((CONV_HALO, A_WIDTH), F32)

    v = proj(_C_GC, A_WIDTH) * proj(_C_U, A_WIDTH)
    vext_ref[CONV_HALO:CONV_HALO + tm, :] = v
    y = (cw_ref[2:3, :] * v
         + cw_ref[1:2, :] * vext_ref[pl.ds(CONV_HALO - 1, tm), :]
         + cw_ref[0:1, :] * vext_ref[pl.ds(CONV_HALO - 2, tm), :])
    ob_ref[...] = (proj(_C_GB, A_WIDTH) * y).astype(BF16)
    vext_ref[0:CONV_HALO, :] = vext_ref[tm:tm + CONV_HALO, :]


def _inproj(x, pos_col, g, w, qnw, knw, cw, pq, pk, freq, *, seq, tm=256):
    n, d = x.shape
    row = lambda i: (i, 0)
    fixed = lambda i: (0, 0)
    widths = [(A_WIDTH, BF16), (A_WIDTH, BF16), (KV_WIDTH, BF16), (KV_WIDTH, BF16), (KV_WIDTH, BF16),
              (KV_WIDTH, BF16), (KV_WIDTH, F32), (KV_WIDTH, F32), (N_BRANCH * A_WIDTH, F32), (A_WIDTH, BF16)]
    return pl.pallas_call(
        functools.partial(_inproj_kernel, tm=tm, tiles_per_batch=seq // tm),
        out_shape=[jax.ShapeDtypeStruct((n, wd), dt) for wd, dt in widths],
        grid=(n // tm,),
        in_specs=[
            pl.BlockSpec((tm, d), row),
            pl.BlockSpec((tm, 1), row),
            pl.BlockSpec((1, d), fixed),
            pl.BlockSpec(w.shape, fixed),
            pl.BlockSpec(qnw.shape, fixed),
            pl.BlockSpec(knw.shape, fixed),
            pl.BlockSpec(cw.shape, fixed),
            pl.BlockSpec(pq.shape, fixed),
            pl.BlockSpec(pk.shape, fixed),
            pl.BlockSpec(freq.shape, fixed),
        ],
        out_specs=[pl.BlockSpec((tm, wd), row) for wd, _ in widths],
        scratch_shapes=[pltpu.VMEM((tm + CONV_HALO, A_WIDTH), F32)],
        compiler_params=_cparams(("arbitrary",)),
        name="inproj",
    )(x, pos_col, g.reshape(1, d), w, qnw, knw, cw, pq, pk, freq)


def _gelu_tanh(x):
    return 0.5 * x * (1.0 + jnp.tanh(math.sqrt(2.0 / math.pi) * (x + 0.044715 * (x * x * x))))


def _compress_kernel(xk_ref, xv_ref, pos_ref, pe_ref, wk_ref, wv_ref, w2_ref, knw_ref, pk_ref, freq_ref,
                     kc_ref, vc_ref, *, ncp):
    def branch(x, pe_a, pe_b, w_ref, w2):
        a = _dot((x + pe_a).astype(BF16), w_ref[0])
        b = _dot((x + pe_b).astype(BF16), w_ref[1])
        hid = a + pltpu.roll(b, ncp - 1, axis=0)
        return _dot(_gelu_tanh(hid).astype(BF16), w2)

    kc = branch(xk_ref[...], pe_ref[0:1, :], pe_ref[1:2, :], wk_ref, w2_ref[0])
    kc = _group_rms(kc, knw_ref[0:1, :], pk_ref[...])
    cos_t, sin_t = _rope_tables(pos_ref[...].astype(F32), freq_ref[...])
    kc_ref[...] = _rope(kc, cos_t, sin_t).astype(BF16)
    vc_ref[...] = branch(xv_ref[...], pe_ref[2:3, :], pe_ref[3:4, :], wv_ref, w2_ref[1]).astype(BF16)


def _compress(xk, xv, pos_c, pe, wk, wv, w2, knw, pk, freq):
    bsz, ncp, seg = xk.shape
    bat = lambda b: (b, 0, 0)
    fix2 = lambda b: (0, 0)
    fix3 = lambda b: (0, 0, 0)
    return pl.pallas_call(
        functools.partial(_compress_kernel, ncp=ncp),
        out_shape=[jax.ShapeDtypeStruct((bsz, ncp, KV_WIDTH), BF16)] * 2,
        grid=(bsz,),
        in_specs=[
            pl.BlockSpec((None, ncp, seg), bat),
            pl.BlockSpec((None, ncp, seg), bat),
            pl.BlockSpec((None, ncp, 1), bat),
            pl.BlockSpec(pe.shape, fix2),
            pl.BlockSpec(wk.shape, fix3),
            pl.BlockSpec(wv.shape, fix3),
            pl.BlockSpec(w2.shape, fix3),
            pl.BlockSpec(knw.shape, fix2),
            pl.BlockSpec(pk.shape, fix2),
            pl.BlockSpec(freq.shape, fix2),
        ],
        out_specs=[pl.BlockSpec((None, ncp, KV_WIDTH), bat)] * 2,
        compiler_params=_cparams(("parallel",)),
        name="compress",
    )(xk, xv, pos_c, pe, wk, wv, w2, knw, pk, freq)


def _head_slabs(q_ref, qsw_ref):
    q = q_ref[...]
    qs = qsw_ref[...]
    return [q[:, :LANES], qs[:, :LANES], q[:, LANES:], qs[:, LANES:]]


def _cmp_kernel(q_ref, qsw_ref, kc_ref, vc_ref, ov_ref, o_ref, sb_ref, *, tq, ncp):
    i = pl.program_id(2)
    t = i * tq + lax.broadcasted_iota(jnp.int32, (tq, 1), 0)
    blk_end = lax.broadcasted_iota(jnp.int32, (1, ncp), 1) * CMP_STRIDE + (CMP_BLOCK - 1)
    valid = blk_end <= t
    any_valid = (t >= CMP_BLOCK - 1).astype(F32)
    kc = kc_ref[...]
    vc = vc_ref[...]
    psum = jnp.zeros((tq, ncp), F32)
    for hd, qh in enumerate(_head_slabs(q_ref, qsw_ref)):
        s = jnp.where(valid, _dot_nt(qh, kc), NEG)
        e = jnp.exp(s - jnp.max(s, axis=-1, keepdims=True))
        p = e * (any_valid / jnp.sum(e, axis=-1, keepdims=True))
        o_ref[:, hd * HEAD_DIM:(hd + 1) * HEAD_DIM] = _dot(p.astype(BF16), vc)[:, :HEAD_DIM]
        psum = psum + p

    imp = _dot_split(psum, ov_ref[...])
    sel_start = lax.broadcasted_iota(jnp.int32, (1, LANES), 1) * SEL_BLOCK
    cur = (t >> int(math.log2(SEL_BLOCK))) << int(math.log2(SEL_BLOCK))
    imp = jnp.where(sel_start <= t, imp, -1.0)
    imp = jnp.where(sel_start == cur, 1e4, jnp.where(sel_start == 0, 1e4, imp))

    lane_f = lax.broadcasted_iota(jnp.int32, (1, LANES), 1).astype(F32)
    bias = jnp.full((tq, LANES), NEG, F32)
    for _ in range(SEL_TOPK):
        mx = jnp.max(imp, axis=-1, keepdims=True)
        first = jnp.min(jnp.where(imp == mx, lane_f, float(LANES)), axis=-1, keepdims=True)
        pick = lane_f == first
        bias = jnp.where(pick, 0.0, bias)
        imp = jnp.where(pick, -3e38, imp)
    sb_ref[...] = bias.astype(BF16)


def _cmp_select(q, qsw, kc, vc, ov, *, bsz, seq, tq=128):
    ncp = kc.shape[2]
    nq = seq // tq
    gw = HEADS_PER_GROUP * HEAD_DIM
    qmap = lambda b, g, i: (b * nq + i, g)
    kvmap = lambda b, g, i: (b, g, 0, 0)
    return pl.pallas_call(
        functools.partial(_cmp_kernel, tq=tq, ncp=ncp),
        out_shape=[jax.ShapeDtypeStruct((bsz * seq, A_WIDTH), F32),
                   jax.ShapeDtypeStruct((bsz, N_GROUPS, seq, LANES), BF16)],
        grid=(bsz, N_GROUPS, nq),
        in_specs=[
            pl.BlockSpec((tq, gw), qmap),
            pl.BlockSpec((tq, gw), qmap),
            pl.BlockSpec((None, None, ncp, LANES), kvmap),
            pl.BlockSpec((None, None, ncp, LANES), kvmap),
            pl.BlockSpec(ov.shape, lambda b, g, i: (0, 0)),
        ],
        out_specs=[pl.BlockSpec((tq, gw), qmap),
                   pl.BlockSpec((None, None, tq, LANES), lambda b, g, i: (b, g, i, 0))],
        compiler_params=_cparams(("parallel", "parallel", "parallel")),
        name="cmp_select",
    )(q, qsw, kc, vc, ov)


def _slc_kernel(q_ref, qsw_ref, sb_ref, k_ref, v_ref, o_ref, *, tq, tk):
    i = pl.program_id(2)
    sb = sb_ref[...]
    qa = jnp.concatenate([jnp.concatenate([sb, qh], axis=1) for qh in _head_slabs(q_ref, qsw_ref)], axis=0)
    rows = HEADS_PER_GROUP * tq
    t_row = i * tq + (lax.broadcasted_iota(jnp.int32, (rows, 1), 0) & (tq - 1))

    def step(j, carry, causal):
        m, acc = carry
        start = pl.multiple_of(j * tk, tk)
        s = _dot_nt(qa, k_ref[pl.ds(start, tk), :])
        if causal:
            kpos = start + lax.broadcasted_iota(jnp.int32, (1, tk), 1)
            s = jnp.where(kpos <= t_row, s, NEG)
        m_new = jnp.maximum(m, jnp.max(s, axis=-1, keepdims=True))
        p = jnp.exp(s - m_new)
        acc = jnp.exp(m - m_new) * acc + _dot(p.astype(BF16), v_ref[pl.ds(start, tk), :])
        return m_new, acc

    diag = (i * tq) // tk
    carry = (jnp.full((rows, 1), -3e38, F32), jnp.zeros((rows, LANES), F32))
    carry = lax.fori_loop(0, diag, functools.partial(step, causal=False), carry)
    _, acc = step(diag, carry, True)
    out = acc[:, :HEAD_DIM] / acc[:, HEAD_DIM:HEAD_DIM + 1]
    for hd in range(HEADS_PER_GROUP):
        o_ref[:, hd * HEAD_DIM:(hd + 1) * HEAD_DIM] = out[hd * tq:(hd + 1) * tq, :]


def _slc(q, qsw, sb, k_aug, v_aug, *, bsz, seq, tq=128, tk=512):
    nq = seq // tq
    gw = HEADS_PER_GROUP * HEAD_DIM
    qmap = lambda b, g, i: (b * nq + i, g)
    kvmap = lambda b, g, i: (b, g, 0, 0)
    return pl.pallas_call(
        functools.partial(_slc_kernel, tq=tq, tk=tk),
        out_shape=jax.ShapeDtypeStruct((bsz * seq, A_WIDTH), F32),
        grid=(bsz, N_GROUPS, nq),
        in_specs=[
            pl.BlockSpec((tq, gw), qmap),
            pl.BlockSpec((tq, gw), qmap),
            pl.BlockSpec((None, None, tq, LANES), lambda b, g, i: (b, g, i, 0)),
            pl.BlockSpec((None, None, seq, 2 * LANES), kvmap),
            pl.BlockSpec((None, None, seq, LANES), kvmap),
        ],
        out_specs=pl.BlockSpec((tq, gw), qmap),
        compiler_params=_cparams(("parallel", "parallel", "parallel")),
        name="slc_attn",
    )(q, qsw, sb, k_aug, v_aug)


def _win_kernel(q_ref, qsw_ref, k_ref, v_ref, o_ref, *, tq):
    i = pl.program_id(2)
    qa = jnp.concatenate(_head_slabs(q_ref, qsw_ref), axis=0)
    rows = HEADS_PER_GROUP * tq
    span = WINDOW + tq
    start = pl.multiple_of(i * tq, tq)
    t_row = i * tq + (lax.broadcasted_iota(jnp.int32, (rows, 1), 0) & (tq - 1))
    kpos = i * tq - WINDOW + lax.broadcasted_iota(jnp.int32, (1, span), 1)
    diff = t_row - kpos
    s = _dot_nt(qa, k_ref[pl.ds(start, span), :])
    s = jnp.where(diff >= 0, jnp.where(diff < WINDOW, jnp.where(kpos >= 0, s, NEG), NEG), NEG)
    p = jnp.exp(s - jnp.max(s, axis=-1, keepdims=True))
    acc = _dot(p.astype(BF16), v_ref[pl.ds(start, span), :])
    out = acc[:, :HEAD_DIM] / acc[:, HEAD_DIM:HEAD_DIM + 1]
    for hd in range(HEADS_PER_GROUP):
        o_ref[:, hd * HEAD_DIM:(hd + 1) * HEAD_DIM] = out[hd * tq:(hd + 1) * tq, :]


def _win(q, qsw, k_pad, v_pad, *, bsz, seq, tq=128):
    nq = seq // tq
    gw = HEADS_PER_GROUP * HEAD_DIM
    qmap = lambda b, g, i: (b * nq + i, g)
    kvmap = lambda b, g, i: (b, g, 0, 0)
    return pl.pallas_call(
        functools.partial(_win_kernel, tq=tq),
        out_shape=jax.ShapeDtypeStruct((bsz * seq, A_WIDTH), F32),
        grid=(bsz, N_GROUPS, nq),
        in_specs=[
            pl.BlockSpec((tq, gw), qmap),
            pl.BlockSpec((tq, gw), qmap),
            pl.BlockSpec((None, None, seq + WINDOW, LANES), kvmap),
            pl.BlockSpec((None, None, seq + WINDOW, LANES), kvmap),
        ],
        out_specs=pl.BlockSpec((tq, gw), qmap),
        compiler_params=_cparams(("parallel", "parallel", "parallel")),
        name="win_attn",
    )(q, qsw, k_pad, v_pad)


def _outproj_kernel(x_ref, oc_ref, os_ref, ow_ref, g_ref, ob_ref, w_ref, o_ref):
    oa = (g_ref[:, 0:A_WIDTH] * oc_ref[...]
          + g_ref[:, A_WIDTH:2 * A_WIDTH] * os_ref[...]
          + g_ref[:, 2 * A_WIDTH:3 * A_WIDTH] * ow_ref[...])
    y = _dot(oa.astype(BF16), w_ref[0:A_WIDTH, :]) + _dot(ob_ref[...], w_ref[A_WIDTH:, :])
    o_ref[...] = x_ref[...] + y


def _outproj(x, oc, os_, ow, gates, ob, w, *, tm=512):
    n, d = x.shape
    row = lambda i: (i, 0)
    return pl.pallas_call(
        _outproj_kernel,
        out_shape=jax.ShapeDtypeStruct((n, d), F32),
        grid=(n // tm,),
        in_specs=[
            pl.BlockSpec((tm, d), row),
            pl.BlockSpec((tm, A_WIDTH), row),
            pl.BlockSpec((tm, A_WIDTH), row),
            pl.BlockSpec((tm, A_WIDTH), row),
            pl.BlockSpec((tm, N_BRANCH * A_WIDTH), row),
            pl.BlockSpec((tm, A_WIDTH), row),
            pl.BlockSpec(w.shape, lambda i: (0, 0)),
        ],
        out_specs=pl.BlockSpec((tm, d), row),
        compiler_params=_cparams(("parallel",)),
        name="outproj",
    )(x, oc, os_, ow, gates, ob, w)


def _pool_kernel(x_ref, halo_ref, g_ref, w_ref, sc_ref, o_ref, ext_ref, *, tm, tiles_per_batch):
    i = pl.program_id(0)
    first = i % tiles_per_batch == 0
    x = x_ref[...]
    h = _rms(x, g_ref[...])
    hh = _rms(halo_ref[...], g_ref[...])
    ext_ref[0:POOL_HALO, :] = jnp.where(first, 0.0, hh)
    ext_ref[POOL_HALO:POOL_HALO + tm, :] = h
    t = (i % tiles_per_batch) * tm + lax.broadcasted_iota(jnp.int32, (tm, 1), 0)
    gw = h.shape[1] // len(POOL_WINDOWS)
    for gi, win in enumerate(POOL_WINDOWS):
        c0 = gi * gw
        hg = h[:, c0:c0 + gw]
        tot = hg
        for sft in range(1, win):
            tot = tot + ext_ref[pl.ds(POOL_HALO - sft, tm), c0:c0 + gw]
        cnt = jnp.minimum(t + 1, win).astype(F32)
        y = _dot((tot / cnt - hg).astype(BF16), w_ref[gi])
        o_ref[:, c0:c0 + gw] = x[:, c0:c0 + gw] + y * sc_ref[:, c0:c0 + gw]


def _pool(x, g, w, scale, *, seq, tm=512):
    n, d = x.shape
    hb = tm // POOL_HALO
    return pl.pallas_call(
        functools.partial(_pool_kernel, tm=tm, tiles_per_batch=seq // tm),
        out_shape=jax.ShapeDtypeStruct((n, d), F32),
        grid=(n // tm,),
        in_specs=[
            pl.BlockSpec((tm, d), lambda i: (i, 0)),
            pl.BlockSpec((POOL_HALO, d), lambda i: (jnp.maximum(i * hb - 1, 0), 0)),
            pl.BlockSpec((1, d), lambda i: (0, 0)),
            pl.BlockSpec(w.shape, lambda i: (0, 0, 0)),
            pl.BlockSpec((1, d), lambda i: (0, 0)),
        ],
        out_specs=pl.BlockSpec((tm, d), lambda i: (i, 0)),
        scratch_shapes=[pltpu.VMEM((tm + POOL_HALO, d), F32)],
        compiler_params=_cparams(("parallel",)),
        name="pool_mixer",
    )(x, x, g.reshape(1, d), w, scale.reshape(1, d))


def _pack_w_in(w_in):
    o = 0
    q = w_in[:, o:o + A_WIDTH]; o += A_WIDTH
    kc, vc, ks, vs, kw, vw = [w_in[:, o + k * KV_WIDTH:o + (k + 1) * KV_WIDTH] for k in range(6)]
    o += 6 * KV_WIDTH
    gates = w_in[:, o:o + N_BRANCH * N_HEADS]; o += N_BRANCH * N_HEADS
    u, gb, gc = [w_in[:, o + k * A_WIDTH:o + (k + 1) * A_WIDTH] for k in range(3)]
    gates = gates.reshape(-1, N_HEADS, N_BRANCH).transpose(0, 2, 1)
    gates = jnp.repeat(gates, HEAD_DIM, axis=2).reshape(-1, N_BRANCH * A_WIDTH)
    return jnp.concatenate([q, ks, kw, vs, vw, kc, vc, gates, u, gb, gc], axis=1).astype(BF16)


def _group_mean_matrix(width):
    r = jnp.arange(width) // HEAD_DIM
    return jnp.where(r[:, None] == r[None, :], 1.0 / HEAD_DIM, 0.0).astype(BF16)


def _rope_freq_row():
    lane = jnp.arange(LANES) % HEAD_DIM
    freqs = ROPE_THETA ** (-jnp.arange(0, ROPE_DIM, 2, dtype=F32) / ROPE_DIM)
    return jnp.where(lane < ROPE_DIM, freqs[lane % ROPE_HALF], 0.0).astype(F32).reshape(1, LANES)


def _per_group(a, bsz, rows):
    return a.reshape(bsz, rows, N_GROUPS, HEAD_DIM).transpose(0, 2, 1, 3)


def _hybrid_mixer(x, positions, g, w_in, q_norm, k_norm, cmp_pos, cmp_w1, cmp_w2, conv_w, w_out, *, bsz, seq):
    n = bsz * seq
    ncp = seq // CMP_STRIDE
    freq = _rope_freq_row()
    pq = _group_mean_matrix(A_WIDTH)
    pk = _group_mean_matrix(KV_WIDTH)
    qnw = jnp.tile(q_norm, N_HEADS).reshape(1, A_WIDTH)
    knw = jnp.tile(k_norm, (1, N_GROUPS))

    q, qsw, ks, kw, vs, vw, kc_raw, vc_raw, gates, ob = _inproj(
        x, positions.reshape(n, 1), g, _pack_w_in(w_in), qnw, knw, conv_w, pq, pk, freq, seq=seq)

    eye_g = jnp.eye(N_GROUPS, dtype=F32)
    seg = CMP_STRIDE * KV_WIDTH

    def seg_weights(w1):
        w1r = w1.reshape(CMP_BLOCK, HEAD_DIM, CMP_HIDDEN)
        halves = [jnp.einsum('jdh,gk->jgdkh', w1r[a:a + CMP_STRIDE], eye_g).reshape(seg, N_GROUPS * CMP_HIDDEN)
                  for a in (0, CMP_STRIDE)]
        return jnp.stack(halves).astype(BF16)

    def seg_pos(pe):
        return [jnp.broadcast_to(pe[a:a + CMP_STRIDE, None, :], (CMP_STRIDE, N_GROUPS, HEAD_DIM)).reshape(seg)
                for a in (0, CMP_STRIDE)]

    pe = jnp.stack(seg_pos(cmp_pos[0]) + seg_pos(cmp_pos[1]))
    w2 = jnp.stack([jnp.einsum('hd,gk->ghkd', cmp_w2[a], eye_g).reshape(N_GROUPS * CMP_HIDDEN, KV_WIDTH)
                    for a in range(2)]).astype(BF16)
    blk_end = jnp.minimum(jnp.arange(ncp) * CMP_STRIDE + CMP_BLOCK - 1, seq - 1)
    pos_c = positions[:, blk_end].reshape(bsz, ncp, 1)
    kc, vc = _compress(kc_raw.reshape(bsz, ncp, seg), vc_raw.reshape(bsz, ncp, seg), pos_c, pe,
                       seg_weights(cmp_w1[0]), seg_weights(cmp_w1[1]), w2, knw, pk, freq)

    pad64 = ((0, 0), (0, 0), (0, 0), (0, LANES - HEAD_DIM))
    kc_g = jnp.pad(_per_group(kc, bsz, ncp), pad64)
    vc_g = jnp.pad(_per_group(vc, bsz, ncp), pad64)
    cstart = jnp.arange(ncp) * CMP_STRIDE
    sstart = jnp.arange(LANES) * SEL_BLOCK
    ov = ((cstart[:, None] < sstart[None, :] + SEL_BLOCK)
          & (cstart[:, None] + CMP_BLOCK > sstart[None, :])).astype(BF16)
    o_cmp, sel_bias = _cmp_select(q, qsw, kc_g, vc_g, ov, bsz=bsz, seq=seq)

    blk_onehot = (jnp.arange(seq)[:, None] // SEL_BLOCK == jnp.arange(LANES)[None, :]).astype(BF16)
    ones_col = jnp.zeros((LANES - HEAD_DIM,), BF16).at[0].set(1.0)

    def with_ones(v):
        vg = _per_group(v.reshape(bsz, seq, KV_WIDTH), bsz, seq)
        return jnp.concatenate([vg, jnp.broadcast_to(ones_col, vg.shape[:3] + (LANES - HEAD_DIM,))], axis=-1)

    ks_g = _per_group(ks.reshape(bsz, seq, KV_WIDTH), bsz, seq)
    k_aug = jnp.concatenate([jnp.broadcast_to(blk_onehot, (bsz, N_GROUPS, seq, LANES)), ks_g,
                             jnp.zeros((bsz, N_GROUPS, seq, LANES - HEAD_DIM), BF16)], axis=-1)
    o_slc = _slc(q, qsw, sel_bias, k_aug, with_ones(vs), bsz=bsz, seq=seq)

    front = ((0, 0), (0, 0), (WINDOW, 0), (0, 0))
    kw_g = jnp.pad(jnp.pad(_per_group(kw.reshape(bsz, seq, KV_WIDTH), bsz, seq), pad64), front)
    o_win = _win(q, qsw, kw_g, jnp.pad(with_ones(vw), front), bsz=bsz, seq=seq)

    return _outproj(x, o_cmp, o_slc, o_win, gates, ob, w_out.astype(BF16))


def kernel(x, positions, ffn_norm, ffn_w_gate, ffn_w_up, ffn_w_down, mix_norm, hyb_w_in, hyb_q_norm, hyb_k_norm,
           hyb_cmp_pos, hyb_cmp_w1, hyb_cmp_w2, hyb_conv_w, hyb_w_out, pool_w, pool_scale):
    bsz, seq, d = x.shape
    depth = ffn_norm.shape[0]
    assert seq % 2048 == 0 and seq // SEL_BLOCK <= LANES and seq // SEL_BLOCK >= SEL_TOPK
    wg, wu, wd = (w.astype(BF16) for w in (ffn_w_gate, ffn_w_up, ffn_w_down))
    xf = x.reshape(bsz * seq, d)
    for layer in range(depth):
        xf = _ffn(xf, ffn_norm[layer, 0], wg[layer, 0], wu[layer, 0], wd[layer, 0])
        i = layer // 2
        if layer % 2 == 0:
            xf = _hybrid_mixer(xf, positions, mix_norm[layer], hyb_w_in[i], hyb_q_norm[i], hyb_k_norm[i],
                               hyb_cmp_pos[i], hyb_cmp_w1[i], hyb_cmp_w2[i], hyb_conv_w[i], hyb_w_out[i],
                               bsz=bsz, seq=seq)
        else:
            xf = _pool(xf, mix_norm[layer], pool_w[i].astype(BF16), pool_scale[i], seq=seq)
        xf = _ffn(xf, ffn_norm[layer, 1], wg[layer, 1], wu[layer, 1], wd[layer, 1])
    return xf.reshape(bsz, seq, d)
```

```python
import functools
import math

import jax
import jax.numpy as jnp
from jax import lax
from jax.experimental import pallas as pl
from jax.experimental.pallas import tpu as pltpu

F32 = jnp.float32
BF16 = jnp.bfloat16

HEAD_DIM = 64
N_GROUPS = 2
HEADS_PER_GROUP = 4
N_HEADS = N_GROUPS * HEADS_PER_GROUP
A_WIDTH = N_HEADS * HEAD_DIM
KV_WIDTH = N_GROUPS * HEAD_DIM
ROPE_DIM = HEAD_DIM // 4
ROPE_HALF = ROPE_DIM // 2
ROPE_THETA = 500000.0
CMP_BLOCK = 32
CMP_STRIDE = 16
CMP_HIDDEN = 2 * HEAD_DIM
SEL_BLOCK = 64
SEL_TOPK = 16
WINDOW = 512
N_BRANCH = 3
CONV_WIDTH = 3
POOL_WINDOWS = (2, 4, 8, 16)
POOL_HALO = 16
EPS = 1e-6
NEG = -1e30

LANES = 128
VMEM_LIMIT_BYTES = 56 * 1024 * 1024


def _cparams(sem):
    return pltpu.CompilerParams(dimension_semantics=sem, vmem_limit_bytes=VMEM_LIMIT_BYTES)


def _dot(a, b):
    return jnp.dot(a, b, preferred_element_type=F32)


def _dot_nt(a, b):
    return lax.dot_general(a, b, (((1,), (1,)), ((), ())), preferred_element_type=F32)


def _dot_split(a, b):
    hi = a.astype(BF16)
    lo = (a - hi.astype(F32)).astype(BF16)
    return _dot(hi, b) + _dot(lo, b)


def _rms(x, g):
    ms = jnp.mean(x * x, axis=-1, keepdims=True)
    return x * lax.rsqrt(ms + EPS) * g


def _group_rms(x, g, pmat):
    ms = _dot_split(x * x, pmat)
    return x * lax.rsqrt(ms + EPS) * g


def _rope_tables(pos_col, freq_row):
    ang = pos_col * freq_row
    lane = lax.broadcasted_iota(jnp.int32, (1, LANES), 1) & (HEAD_DIM - 1)
    sign = jnp.where(lane < ROPE_HALF, -1.0, 1.0).astype(F32)
    return jnp.cos(ang), jnp.sin(ang) * sign


def _rope(x, cos_t, sin_t):
    w = x.shape[1]
    reps = w // LANES
    if reps > 1:
        cos_t = jnp.concatenate([cos_t] * reps, axis=1)
        sin_t = jnp.concatenate([sin_t] * reps, axis=1)
    lane = lax.broadcasted_iota(jnp.int32, (1, w), 1) & (HEAD_DIM - 1)
    partner = jnp.where(lane < ROPE_HALF,
                        pltpu.roll(x, w - ROPE_HALF, axis=1),
                        pltpu.roll(x, ROPE_HALF, axis=1))
    return x * cos_t + partner * sin_t


def _ffn_kernel(x_ref, g_ref, wg_ref, wu_ref, wd_ref, o_ref, h_ref, acc_ref):
    j = pl.program_id(1)

    @pl.when(j == 0)
    def _():
        h_ref[...] = _rms(x_ref[...], g_ref[...]).astype(BF16)
        acc_ref[...] = jnp.zeros_like(acc_ref)

    h = h_ref[...]
    a = _dot(h, wg_ref[...])
    b = _dot(h, wu_ref[...])
    act = a * (1.0 / (1.0 + jnp.exp(-a))) * b
    acc_ref[...] += _dot(act.astype(BF16), wd_ref[...])

    @pl.when(j == pl.num_programs(1) - 1)
    def _():
        o_ref[...] = x_ref[...] + 0.5 * acc_ref[...]


def _ffn(x, g, wg, wu, wd, *, tm=512, tf=1408):
    n, d = x.shape
    dff = wg.shape[1]
    return pl.pallas_call(
        _ffn_kernel,
        out_shape=jax.ShapeDtypeStruct((n, d), F32),
        grid=(n // tm, dff // tf),
        in_specs=[
            pl.BlockSpec((tm, d), lambda i, j: (i, 0)),
            pl.BlockSpec((1, d), lambda i, j: (0, 0)),
            pl.BlockSpec((d, tf), lambda i, j: (0, j)),
            pl.BlockSpec((d, tf), lambda i, j: (0, j)),
            pl.BlockSpec((tf, d), lambda i, j: (j, 0)),
        ],
        out_specs=pl.BlockSpec((tm, d), lambda i, j: (i, 0)),
        scratch_shapes=[pltpu.VMEM((tm, d), BF16), pltpu.VMEM((tm, d), F32)],
        compiler_params=_cparams(("parallel", "arbitrary")),
        name="ffn",
    )(x, g.reshape(1, d), wg, wu, wd)


_C_Q = 0
_C_KS = _C_Q + A_WIDTH
_C_KW = _C_KS + KV_WIDTH
_C_VS = _C_KW + KV_WIDTH
_C_VW = _C_VS + KV_WIDTH
_C_KC = _C_VW + KV_WIDTH
_C_VC = _C_KC + KV_WIDTH
_C_GATE = _C_VC + KV_WIDTH
_C_U = _C_GATE + N_BRANCH * A_WIDTH
_C_GB = _C_U + A_WIDTH
_C_GC = _C_GB + A_WIDTH
_C_END = _C_GC + A_WIDTH
CONV_HALO = 8


def _inproj_kernel(x_ref, pos_ref, g_ref, w_ref, qnw_ref, knw_ref, cw_ref, pq_ref, pk_ref, freq_ref,
                   q_ref, qsw_ref, ks_ref, kw_ref, vs_ref, vw_ref, kc_ref, vc_ref, gates_ref, ob_ref,
                   vext_ref, *, tm, tiles_per_batch):
    i = pl.program_id(0)
    h = _rms(x_ref[...], g_ref[...]).astype(BF16)
    cos_t, sin_t = _rope_tables(pos_ref[...].astype(F32), freq_ref[...])

    def proj(c0, width):
        return _dot(h, w_ref[:, c0:c0 + width])

    q = _group_rms(proj(_C_Q, A_WIDTH), qnw_ref[...], pq_ref[...])
    q = _rope(q, cos_t, sin_t) * (HEAD_DIM ** -0.5)
    q_ref[...] = q.astype(BF16)
    qsw = jnp.concatenate(
        [pltpu.roll(q[:, c * LANES:(c + 1) * LANES], HEAD_DIM, axis=1) for c in range(A_WIDTH // LANES)], axis=1)
    qsw_ref[...] = qsw.astype(BF16)

    ks = _group_rms(proj(_C_KS, KV_WIDTH), knw_ref[1:2, :], pk_ref[...])
    ks_ref[...] = _rope(ks, cos_t, sin_t).astype(BF16)
    kw = _group_rms(proj(_C_KW, KV_WIDTH), knw_ref[2:3, :], pk_ref[...])
    kw_ref[...] = _rope(kw, cos_t, sin_t).astype(BF16)
    vs_ref[...] = proj(_C_VS, KV_WIDTH).astype(BF16)
    vw_ref[...] = proj(_C_VW, KV_WIDTH).astype(BF16)
    kc_ref[...] = proj(_C_KC, KV_WIDTH)
    vc_ref[...] = proj(_C_VC, KV_WIDTH)

    gl = proj(_C_GATE, N_BRANCH * A_WIDTH)
    gates_ref[...] = 1.0 / (1.0 + jnp.exp(-gl))

    @pl.when(i % tiles_per_batch == 0)
    def _():
        vext_ref[0:CONV_HALO, :] = jnp.zeros((CONV_HALO, A_WIDTH), F32)

    v = proj(_C_GC, A_WIDTH) * proj(_C_U, A_WIDTH)
    vext_ref[CONV_HALO:CONV_HALO + tm, :] = v
    y = (cw_ref[2:3, :] * v
         + cw_ref[1:2, :] * vext_ref[pl.ds(CONV_HALO - 1, tm), :]
         + cw_ref[0:1, :] * vext_ref[pl.ds(CONV_HALO - 2, tm), :])
    ob_ref[...] = (proj(_C_GB, A_WIDTH) * y).astype(BF16)
    vext_ref[0:CONV_HALO, :] = vext_ref[tm:tm + CONV_HALO, :]


def _inproj(x, pos_col, g, w, qnw, knw, cw, pq, pk, freq, *, seq, tm=256):
    n, d = x.shape
    row = lambda i: (i, 0)
    fixed = lambda i: (0, 0)
    widths = [(A_WIDTH, BF16), (A_WIDTH, BF16), (KV_WIDTH, BF16), (KV_WIDTH, BF16), (KV_WIDTH, BF16),
              (KV_WIDTH, BF16), (KV_WIDTH, F32), (KV_WIDTH, F32), (N_BRANCH * A_WIDTH, F32), (A_WIDTH, BF16)]
    return pl.pallas_call(
        functools.partial(_inproj_kernel, tm=tm, tiles_per_batch=seq // tm),
        out_shape=[jax.ShapeDtypeStruct((n, wd), dt) for wd, dt in widths],
        grid=(n // tm,),
        in_specs=[
            pl.BlockSpec((tm, d), row),
            pl.BlockSpec((tm, 1), row),
            pl.BlockSpec((1, d), fixed),
            pl.BlockSpec(w.shape, fixed),
            pl.BlockSpec(qnw.shape, fixed),
            pl.BlockSpec(knw.shape, fixed),
            pl.BlockSpec(cw.shape, fixed),
            pl.BlockSpec(pq.shape, fixed),
            pl.BlockSpec(pk.shape, fixed),
            pl.BlockSpec(freq.shape, fixed),
        ],
        out_specs=[pl.BlockSpec((tm, wd), row) for wd, _ in widths],
        scratch_shapes=[pltpu.VMEM((tm + CONV_HALO, A_WIDTH), F32)],
        compiler_params=_cparams(("arbitrary",)),
        name="inproj",
    )(x, pos_col, g.reshape(1, d), w, qnw, knw, cw, pq, pk, freq)


def _gelu_tanh(x):
    return 0.5 * x * (1.0 + jnp.tanh(math.sqrt(2.0 / math.pi) * (x + 0.044715 * (x * x * x))))


def _compress_kernel(xk_ref, xv_ref, pos_ref, pe_ref, wk_ref, wv_ref, w2_ref, knw_ref, pk_ref, freq_ref,
                     kc_ref, vc_ref, *, ncp):
    def branch(x, pe_a, pe_b, w_ref, w2):
        a = _dot((x + pe_a).astype(BF16), w_ref[0])
        b = _dot((x + pe_b).astype(BF16), w_ref[1])
        hid = a + pltpu.roll(b, ncp - 1, axis=0)
        return _dot(_gelu_tanh(hid).astype(BF16), w2)

    kc = branch(xk_ref[...], pe_ref[0:1, :], pe_ref[1:2, :], wk_ref, w2_ref[0])
    kc = _group_rms(kc, knw_ref[0:1, :], pk_ref[...])
    cos_t, sin_t = _rope_tables(pos_ref[...].astype(F32), freq_ref[...])
    kc_ref[...] = _rope(kc, cos_t, sin_t).astype(BF16)
    vc_ref[...] = branch(xv_ref[...], pe_ref[2:3, :], pe_ref[3:4, :], wv_ref, w2_ref[1]).astype(BF16)


def _compress(xk, xv, pos_c, pe, wk, wv, w2, knw, pk, freq):
    bsz, ncp, seg = xk.shape
    bat = lambda b: (b, 0, 0)
    fix2 = lambda b: (0, 0)
    fix3 = lambda b: (0, 0, 0)
    return pl.pallas_call(
        functools.partial(_compress_kernel, ncp=ncp),
        out_shape=[jax.ShapeDtypeStruct((bsz, ncp, KV_WIDTH), BF16)] * 2,
        grid=(bsz,),
        in_specs=[
            pl.BlockSpec((None, ncp, seg), bat),
            pl.BlockSpec((None, ncp, seg), bat),
            pl.BlockSpec((None, ncp, 1), bat),
            pl.BlockSpec(pe.shape, fix2),
            pl.BlockSpec(wk.shape, fix3),
            pl.BlockSpec(wv.shape, fix3),
            pl.BlockSpec(w2.shape, fix3),
            pl.BlockSpec(knw.shape, fix2),
            pl.BlockSpec(pk.shape, fix2),
            pl.BlockSpec(freq.shape, fix2),
        ],
        out_specs=[pl.BlockSpec((None, ncp, KV_WIDTH), bat)] * 2,
        compiler_params=_cparams(("parallel",)),
        name="compress",
    )(xk, xv, pos_c, pe, wk, wv, w2, knw, pk, freq)


def _head_slabs(q_ref, qsw_ref):
    q = q_ref[...]
    qs = qsw_ref[...]
    return [q[:, :LANES], qs[:, :LANES], q[:, LANES:], qs[:, LANES:]]


def _cmp_kernel(q_ref, qsw_ref, kc_ref, vc_ref, ov_ref, o_ref, sb_ref, *, tq, ncp):
    i = pl.program_id(2)
    t = i * tq + lax.broadcasted_iota(jnp.int32, (tq, 1), 0)
    blk_end = lax.broadcasted_iota(jnp.int32, (1, ncp), 1) * CMP_STRIDE + (CMP_BLOCK - 1)
    valid = blk_end <= t
    any_valid = (t >= CMP_BLOCK - 1).astype(F32)
    kc = kc_ref[...]
    vc = vc_ref[...]
    psum = jnp.zeros((tq, ncp), F32)
    for hd, qh in enumerate(_head_slabs(q_ref, qsw_ref)):
        s = jnp.where(valid, _dot_nt(qh, kc), NEG)
        e = jnp.exp(s - jnp.max(s, axis=-1, keepdims=True))
        p = e * (any_valid / jnp.sum(e, axis=-1, keepdims=True))
        o_ref[:, hd * HEAD_DIM:(hd + 1) * HEAD_DIM] = _dot(p.astype(BF16), vc)[:, :HEAD_DIM]
        psum = psum + p

    imp = _dot_split(psum, ov_ref[...])
    sel_start = lax.broadcasted_iota(jnp.int32, (1, LANES), 1) * SEL_BLOCK
    cur = (t >> int(math.log2(SEL_BLOCK))) << int(math.log2(SEL_BLOCK))
    imp = jnp.where(sel_start <= t, imp, -1.0)
    imp = jnp.where(sel_start == cur, 1e4, jnp.where(sel_start == 0, 1e4, imp))

    chunks = range(tq // LANES)
    imp_t = jnp.concatenate([imp[c * LANES:(c + 1) * LANES, :].T for c in chunks], axis=1)
    blk_f = lax.broadcasted_iota(jnp.int32, (LANES, 1), 0).astype(F32)
    bias_t = jnp.full((LANES, tq), NEG, F32)
    for _ in range(SEL_TOPK):
        mx = jnp.max(imp_t, axis=0, keepdims=True)
        first = jnp.min(jnp.where(imp_t == mx, blk_f, float(LANES)), axis=0, keepdims=True)
        pick = blk_f == first
        bias_t = jnp.where(pick, 0.0, bias_t)
        imp_t = jnp.where(pick, -3e38, imp_t)
    sb_ref[...] = jnp.concatenate([bias_t[:, c * LANES:(c + 1) * LANES].T for c in chunks], axis=0).astype(BF16)


def _cmp_select(q, qsw, kc, vc, ov, *, bsz, seq, tq=256):
    ncp = kc.shape[2]
    nq = seq // tq
    gw = HEADS_PER_GROUP * HEAD_DIM
    qmap = lambda b, g, i: (b * nq + i, g)
    kvmap = lambda b, g, i: (b, g, 0, 0)
    return pl.pallas_call(
        functools.partial(_cmp_kernel, tq=tq, ncp=ncp),
        out_shape=[jax.ShapeDtypeStruct((bsz * seq, A_WIDTH), F32),
                   jax.ShapeDtypeStruct((bsz, N_GROUPS, seq, LANES), BF16)],
        grid=(bsz, N_GROUPS, nq),
        in_specs=[
            pl.BlockSpec((tq, gw), qmap),
            pl.BlockSpec((tq, gw), qmap),
            pl.BlockSpec((None, None, ncp, LANES), kvmap),
            pl.BlockSpec((None, None, ncp, LANES), kvmap),
            pl.BlockSpec(ov.shape, lambda b, g, i: (0, 0)),
        ],
        out_specs=[pl.BlockSpec((tq, gw), qmap),
                   pl.BlockSpec((None, None, tq, LANES), lambda b, g, i: (b, g, i, 0))],
        compiler_params=_cparams(("parallel", "parallel", "parallel")),
        name="cmp_select",
    )(q, qsw, kc, vc, ov)


def _slc_kernel(q_ref, qsw_ref, sb_ref, k_ref, vt_ref, o_ref, *, tq, tk):
    i = pl.program_id(2)
    sb = sb_ref[...]
    qa = jnp.concatenate([jnp.concatenate([sb, qh], axis=1) for qh in _head_slabs(q_ref, qsw_ref)], axis=0)
    rows = HEADS_PER_GROUP * tq
    t_col = i * tq + (lax.broadcasted_iota(jnp.int32, (1, rows), 1) & (tq - 1))

    def step(j, carry, causal):
        m, acc = carry
        start = pl.multiple_of(j * tk, tk)
        s = _dot_nt(k_ref[pl.ds(start, tk), :], qa)
        if causal:
            kpos = start + lax.broadcasted_iota(jnp.int32, (tk, 1), 0)
            s = jnp.where(kpos <= t_col, s, NEG)
        m_new = jnp.maximum(m, jnp.max(s, axis=0, keepdims=True))
        p = jnp.exp(s - m_new)
        acc = jnp.exp(m - m_new) * acc + _dot(vt_ref[:, pl.ds(start, tk)], p.astype(BF16))
        return m_new, acc

    diag = (i * tq) // tk
    carry = (jnp.full((1, rows), -3e38, F32), jnp.zeros((LANES, rows), F32))
    carry = lax.fori_loop(0, diag, functools.partial(step, causal=False), carry)
    _, acc = step(diag, carry, True)
    for hd in range(HEADS_PER_GROUP):
        a = acc[:, hd * tq:(hd + 1) * tq].T
        o_ref[:, hd * HEAD_DIM:(hd + 1) * HEAD_DIM] = a[:, :HEAD_DIM] / a[:, HEAD_DIM:HEAD_DIM + 1]


def _slc(q, qsw, sb, k_aug, vt_aug, *,bsz, seq, tq=128, tk=512):
    nq = seq // tq
    gw = HEADS_PER_GROUP * HEAD_DIM
    qmap = lambda b, g, i: (b * nq + i, g)
    kvmap = lambda b, g, i: (b, g, 0, 0)
    return pl.pallas_call(
        functools.partial(_slc_kernel, tq=tq, tk=tk),
        out_shape=jax.ShapeDtypeStruct((bsz * seq, A_WIDTH), F32),
        grid=(bsz, N_GROUPS, nq),
        in_specs=[
            pl.BlockSpec((tq, gw), qmap),
            pl.BlockSpec((tq, gw), qmap),
            pl.BlockSpec((None, None, tq, LANES), lambda b, g, i: (b, g, i, 0)),
            pl.BlockSpec((None, None, seq, 2 * LANES), kvmap),
            pl.BlockSpec((None, None, LANES, seq), kvmap),
        ],
        out_specs=pl.BlockSpec((tq, gw), qmap),
        compiler_params=_cparams(("parallel", "parallel", "parallel")),
        name="slc_attn",
    )(q, qsw, sb, k_aug, vt_aug)


def _win_kernel(q_ref, qsw_ref, k_ref, v_ref, o_ref, *, tq):
    i = pl.program_id(2)
    qa = jnp.concatenate(_head_slabs(q_ref, qsw_ref), axis=0)
    rows = HEADS_PER_GROUP * tq
    span = WINDOW + tq
    start = pl.multiple_of(i * tq, tq)
    t_row = i * tq + (lax.broadcasted_iota(jnp.int32, (rows, 1), 0) & (tq - 1))
    kpos = i * tq - WINDOW + lax.broadcasted_iota(jnp.int32, (1, span), 1)
    diff = t_row - kpos
    s = _dot_nt(qa, k_ref[pl.ds(start, span), :])
    s = jnp.where(diff >= 0, jnp.where(diff < WINDOW, jnp.where(kpos >= 0, s, NEG), NEG), NEG)
    p = jnp.exp(s - jnp.max(s, axis=-1, keepdims=True))
    acc = _dot(p.astype(BF16), v_ref[pl.ds(start, span), :])
    out = acc[:, :HEAD_DIM] / acc[:, HEAD_DIM:HEAD_DIM + 1]
    for hd in range(HEADS_PER_GROUP):
        o_ref[:, hd * HEAD_DIM:(hd + 1) * HEAD_DIM] = out[hd * tq:(hd + 1) * tq, :]


def _win(q, qsw, k_pad, v_pad, *, bsz, seq, tq=128):
    nq = seq // tq
    gw = HEADS_PER_GROUP * HEAD_DIM
    qmap = lambda b, g, i: (b * nq + i, g)
    kvmap = lambda b, g, i: (b, g, 0, 0)
    return pl.pallas_call(
        functools.partial(_win_kernel, tq=tq),
        out_shape=jax.ShapeDtypeStruct((bsz * seq, A_WIDTH), F32),
        grid=(bsz, N_GROUPS, nq),
        in_specs=[
            pl.BlockSpec((tq, gw), qmap),
            pl.BlockSpec((tq, gw), qmap),
            pl.BlockSpec((None, None, seq + WINDOW, LANES), kvmap),
            pl.BlockSpec((None, None, seq + WINDOW, LANES), kvmap),
        ],
        out_specs=pl.BlockSpec((tq, gw), qmap),
        compiler_params=_cparams(("parallel", "parallel", "parallel")),
        name="win_attn",
    )(q, qsw, k_pad, v_pad)


def _outproj_kernel(x_ref, oc_ref, os_ref, ow_ref, g_ref, ob_ref, w_ref, o_ref):
    oa = (g_ref[:, 0:A_WIDTH] * oc_ref[...]
          + g_ref[:, A_WIDTH:2 * A_WIDTH] * os_ref[...]
          + g_ref[:, 2 * A_WIDTH:3 * A_WIDTH] * ow_ref[...])
    y = _dot(oa.astype(BF16), w_ref[0:A_WIDTH, :]) + _dot(ob_ref[...], w_ref[A_WIDTH:, :])
    o_ref[...] = x_ref[...] + y


def _outproj(x, oc, os_, ow, gates, ob, w, *, tm=512):
    n, d = x.shape
    row = lambda i: (i, 0)
    return pl.pallas_call(
        _outproj_kernel,
        out_shape=jax.ShapeDtypeStruct((n, d), F32),
        grid=(n // tm,),
        in_specs=[
            pl.BlockSpec((tm, d), row),
            pl.BlockSpec((tm, A_WIDTH), row),
            pl.BlockSpec((tm, A_WIDTH), row),
            pl.BlockSpec((tm, A_WIDTH), row),
            pl.BlockSpec((tm, N_BRANCH * A_WIDTH), row),
            pl.BlockSpec((tm, A_WIDTH), row),
            pl.BlockSpec(w.shape, lambda i: (0, 0)),
        ],
        out_specs=pl.BlockSpec((tm, d), row),
        compiler_params=_cparams(("parallel",)),
        name="outproj",
    )(x, oc, os_, ow, gates, ob, w)


def _pool_kernel(x_ref, halo_ref, g_ref, w_ref, sc_ref, o_ref, ext_ref, *, tm, tiles_per_batch):
    i = pl.program_id(0)
    first = i % tiles_per_batch == 0
    x = x_ref[...]
    h = _rms(x, g_ref[...])
    hh = _rms(halo_ref[...], g_ref[...])
    ext_ref[0:POOL_HALO, :] = jnp.where(first, 0.0, hh)
    ext_ref[POOL_HALO:POOL_HALO + tm, :] = h
    t = (i % tiles_per_batch) * tm + lax.broadcasted_iota(jnp.int32, (tm, 1), 0)
    gw = h.shape[1] // len(POOL_WINDOWS)
    for gi, win in enumerate(POOL_WINDOWS):
        c0 = gi * gw
        hg = h[:, c0:c0 + gw]
        tot = hg
        for sft in range(1, win):
            tot = tot + ext_ref[pl.ds(POOL_HALO - sft, tm), c0:c0 + gw]
        cnt = jnp.minimum(t + 1, win).astype(F32)
        y = _dot((tot / cnt - hg).astype(BF16), w_ref[gi])
        o_ref[:, c0:c0 + gw] = x[:, c0:c0 + gw] + y * sc_ref[:, c0:c0 + gw]


def _pool(x, g, w, scale, *, seq, tm=512):
    n, d = x.shape
    hb = tm // POOL_HALO
    return pl.pallas_call(
        functools.partial(_pool_kernel, tm=tm, tiles_per_batch=seq // tm),
        out_shape=jax.ShapeDtypeStruct((n, d), F32),
        grid=(n // tm,),
        in_specs=[
            pl.BlockSpec((tm, d), lambda i: (i, 0)),
            pl.BlockSpec((POOL_HALO, d), lambda i: (jnp.maximum(i * hb - 1, 0), 0)),
            pl.BlockSpec((1, d), lambda i: (0, 0)),
            pl.BlockSpec(w.shape, lambda i: (0, 0, 0)),
            pl.BlockSpec((1, d), lambda i: (0, 0)),
        ],
        out_specs=pl.BlockSpec((tm, d), lambda i: (i, 0)),
        scratch_shapes=[pltpu.VMEM((tm + POOL_HALO, d), F32)],
        compiler_params=_cparams(("parallel",)),
        name="pool_mixer",
    )(x, x, g.reshape(1, d), w, scale.reshape(1, d))


def _pack_w_in(w_in):
    o = 0
    q = w_in[:, o:o + A_WIDTH]; o += A_WIDTH
    kc, vc, ks, vs, kw, vw = [w_in[:, o + k * KV_WIDTH:o + (k + 1) * KV_WIDTH] for k in range(6)]
    o += 6 * KV_WIDTH
    gates = w_in[:, o:o + N_BRANCH * N_HEADS]; o += N_BRANCH * N_HEADS
    u, gb, gc = [w_in[:, o + k * A_WIDTH:o + (k + 1) * A_WIDTH] for k in range(3)]
    gates = gates.reshape(-1, N_HEADS, N_BRANCH).transpose(0, 2, 1)
    gates = jnp.repeat(gates, HEAD_DIM, axis=2).reshape(-1, N_BRANCH * A_WIDTH)
    return jnp.concatenate([q, ks, kw, vs, vw, kc, vc, gates, u, gb, gc], axis=1).astype(BF16)


def _group_mean_matrix(width):
    r = jnp.arange(width) // HEAD_DIM
    return jnp.where(r[:, None] == r[None, :], 1.0 / HEAD_DIM, 0.0).astype(BF16)


def _rope_freq_row():
    lane = jnp.arange(LANES) % HEAD_DIM
    freqs = ROPE_THETA ** (-jnp.arange(0, ROPE_DIM, 2, dtype=F32) / ROPE_DIM)
    return jnp.where(lane < ROPE_DIM, freqs[lane % ROPE_HALF], 0.0).astype(F32).reshape(1, LANES)


def _per_group(a, bsz, rows):
    return a.reshape(bsz, rows, N_GROUPS, HEAD_DIM).transpose(0, 2, 1, 3)


def _hybrid_mixer(x, positions, g, w_in, q_norm, k_norm, cmp_pos, cmp_w1, cmp_w2, conv_w, w_out, *, bsz, seq):
    n = bsz * seq
    ncp = seq // CMP_STRIDE
    freq = _rope_freq_row()
    pq = _group_mean_matrix(A_WIDTH)
    pk = _group_mean_matrix(KV_WIDTH)
    qnw = jnp.tile(q_norm, N_HEADS).reshape(1, A_WIDTH)
    knw = jnp.tile(k_norm, (1, N_GROUPS))

    q, qsw, ks, kw, vs, vw, kc_raw, vc_raw, gates, ob = _inproj(
        x, positions.reshape(n, 1), g, _pack_w_in(w_in), qnw, knw, conv_w, pq, pk, freq, seq=seq)

    eye_g = jnp.eye(N_GROUPS, dtype=F32)
    seg = CMP_STRIDE * KV_WIDTH

    def seg_weights(w1):
        w1r = w1.reshape(CMP_BLOCK, HEAD_DIM, CMP_HIDDEN)
        halves = [jnp.einsum('jdh,gk->jgdkh', w1r[a:a + CMP_STRIDE], eye_g).reshape(seg, N_GROUPS * CMP_HIDDEN)
                  for a in (0, CMP_STRIDE)]
        return jnp.stack(halves).astype(BF16)

    def seg_pos(pe):
        return [jnp.broadcast_to(pe[a:a + CMP_STRIDE, None, :], (CMP_STRIDE, N_GROUPS, HEAD_DIM)).reshape(seg)
                for a in (0, CMP_STRIDE)]

    pe = jnp.stack(seg_pos(cmp_pos[0]) + seg_pos(cmp_pos[1]))
    w2 = jnp.stack([jnp.einsum('hd,gk->ghkd', cmp_w2[a], eye_g).reshape(N_GROUPS * CMP_HIDDEN, KV_WIDTH)
                    for a in range(2)]).astype(BF16)
    blk_end = jnp.minimum(jnp.arange(ncp) * CMP_STRIDE + CMP_BLOCK - 1, seq - 1)
    pos_c = positions[:, blk_end].reshape(bsz, ncp, 1)
    kc, vc = _compress(kc_raw.reshape(bsz, ncp, seg), vc_raw.reshape(bsz, ncp, seg), pos_c, pe,
                       seg_weights(cmp_w1[0]), seg_weights(cmp_w1[1]), w2, knw, pk, freq)

    pad64 = ((0, 0), (0, 0), (0, 0), (0, LANES - HEAD_DIM))
    kc_g = jnp.pad(_per_group(kc, bsz, ncp), pad64)
    vc_g = jnp.pad(_per_group(vc, bsz, ncp), pad64)
    cstart = jnp.arange(ncp) * CMP_STRIDE
    sstart = jnp.arange(LANES) * SEL_BLOCK
    ov = ((cstart[:, None] < sstart[None, :] + SEL_BLOCK)
          & (cstart[:, None] + CMP_BLOCK > sstart[None, :])).astype(BF16)
    o_cmp, sel_bias = _cmp_select(q, qsw, kc_g, vc_g, ov, bsz=bsz, seq=seq)

    blk_onehot = (jnp.arange(seq)[:, None] // SEL_BLOCK == jnp.arange(LANES)[None, :]).astype(BF16)
    ones_col = jnp.zeros((LANES - HEAD_DIM,), BF16).at[0].set(1.0)

    def with_ones(v):
        vg = _per_group(v.reshape(bsz, seq, KV_WIDTH), bsz, seq)
        return jnp.concatenate([vg, jnp.broadcast_to(ones_col, vg.shape[:3] + (LANES - HEAD_DIM,))], axis=-1)

    ks_g = _per_group(ks.reshape(bsz, seq, KV_WIDTH), bsz, seq)
    k_aug = jnp.concatenate([jnp.broadcast_to(blk_onehot, (bsz, N_GROUPS, seq, LANES)), ks_g,
                             jnp.zeros((bsz, N_GROUPS, seq, LANES - HEAD_DIM), BF16)], axis=-1)
    o_slc = _slc(q, qsw, sel_bias, k_aug, jnp.swapaxes(with_ones(vs), 2, 3), bsz=bsz, seq=seq)

    front = ((0, 0), (0, 0), (WINDOW, 0), (0, 0))
    kw_g = jnp.pad(jnp.pad(_per_group(kw.reshape(bsz, seq, KV_WIDTH), bsz, seq), pad64), front)
    o_win = _win(q, qsw, kw_g, jnp.pad(with_ones(vw), front), bsz=bsz, seq=seq)

    return _outproj(x, o_cmp, o_slc, o_win, gates, ob, w_out.astype(BF16))


def kernel(x, positions, ffn_norm, ffn_w_gate, ffn_w_up, ffn_w_down, mix_norm, hyb_w_in, hyb_q_norm, hyb_k_norm,
           hyb_cmp_pos, hyb_cmp_w1, hyb_cmp_w2, hyb_conv_w, hyb_w_out, pool_w, pool_scale):
    bsz, seq, d = x.shape
    depth = ffn_norm.shape[0]
    assert seq % 2048 == 0 and seq // SEL_BLOCK <= LANES and seq // SEL_BLOCK >= SEL_TOPK
    wg, wu, wd = (w.astype(BF16) for w in (ffn_w_gate, ffn_w_up, ffn_w_down))
    xf = x.reshape(bsz * seq, d)
    for layer in range(depth):
        xf = _ffn(xf, ffn_norm[layer, 0], wg[layer, 0], wu[layer, 0], wd[layer, 0])
        i = layer // 2
        if layer % 2 == 0:
            xf = _hybrid_mixer(xf, positions, mix_norm[layer], hyb_w_in[i], hyb_q_norm[i], hyb_k_norm[i],
                               hyb_cmp_pos[i], hyb_cmp_w1[i], hyb_cmp_w2[i], hyb_conv_w[i], hyb_w_out[i],
                               bsz=bsz, seq=seq)
        else:
            xf = _pool(xf, mix_norm[layer], pool_w[i].astype(BF16), pool_scale[i], seq=seq)
        xf = _ffn(xf, ffn_norm[layer, 1], wg[layer, 1], wu[layer, 1], wd[layer, 1])
    return xf.reshape(bsz, seq, d)
```

```python
import functools
import math

import jax
import jax.numpy as jnp
from jax import lax
from jax.experimental import pallas as pl
from jax.experimental.pallas import tpu as pltpu

F32 = jnp.float32
BF16 = jnp.bfloat16

HEAD_DIM = 64
N_GROUPS = 2
HEADS_PER_GROUP = 4
N_HEADS = N_GROUPS * HEADS_PER_GROUP
A_WIDTH = N_HEADS * HEAD_DIM
KV_WIDTH = N_GROUPS * HEAD_DIM
ROPE_DIM = HEAD_DIM // 4
ROPE_HALF = ROPE_DIM // 2
ROPE_THETA = 500000.0
CMP_BLOCK = 32
CMP_STRIDE = 16
CMP_HIDDEN = 2 * HEAD_DIM
SEL_BLOCK = 64
SEL_TOPK = 16
WINDOW = 512
N_BRANCH = 3
CONV_WIDTH = 3
POOL_WINDOWS = (2, 4, 8, 16)
POOL_HALO = 16
EPS = 1e-6
NEG = -1e30

LANES = 128
VMEM_LIMIT_BYTES = 56 * 1024 * 1024


def _cparams(sem):
    return pltpu.CompilerParams(dimension_semantics=sem, vmem_limit_bytes=VMEM_LIMIT_BYTES)


def _dot(a, b):
    return jnp.dot(a, b, preferred_element_type=F32)


def _dot_nt(a, b):
    return lax.dot_general(a, b, (((1,), (1,)), ((), ())), preferred_element_type=F32)


def _dot_split(a, b):
    hi = a.astype(BF16)
    lo = (a - hi.astype(F32)).astype(BF16)
    return _dot(hi, b) + _dot(lo, b)


def _rms(x, g):
    ms = jnp.mean(x * x, axis=-1, keepdims=True)
    return x * lax.rsqrt(ms + EPS) * g


def _group_rms(x, g, pmat):
    ms = _dot_split(x * x, pmat)
    return x * lax.rsqrt(ms + EPS) * g


def _rope_tables(pos_col, freq_row):
    ang = pos_col * freq_row
    lane = lax.broadcasted_iota(jnp.int32, (1, LANES), 1) & (HEAD_DIM - 1)
    sign = jnp.where(lane < ROPE_HALF, -1.0, 1.0).astype(F32)
    return jnp.cos(ang), jnp.sin(ang) * sign


def _rope(x, cos_t, sin_t):
    w = x.shape[1]
    reps = w // LANES
    if reps > 1:
        cos_t = jnp.concatenate([cos_t] * reps, axis=1)
        sin_t = jnp.concatenate([sin_t] * reps, axis=1)
    lane = lax.broadcasted_iota(jnp.int32, (1, w), 1) & (HEAD_DIM - 1)
    partner = jnp.where(lane < ROPE_HALF,
                        pltpu.roll(x, w - ROPE_HALF, axis=1),
                        pltpu.roll(x, ROPE_HALF, axis=1))
    return x * cos_t + partner * sin_t


def _ffn_kernel(x_ref, g_ref, wg_ref, wu_ref, wd_ref, o_ref, h_ref, acc_ref):
    j = pl.program_id(1)

    @pl.when(j == 0)
    def _():
        h_ref[...] = _rms(x_ref[...], g_ref[...]).astype(BF16)
        acc_ref[...] = jnp.zeros_like(acc_ref)

    h = h_ref[...]
    a = _dot(h, wg_ref[...])
    b = _dot(h, wu_ref[...])
    act = a * (1.0 / (1.0 + jnp.exp(-a))) * b
    acc_ref[...] += _dot(act.astype(BF16), wd_ref[...])

    @pl.when(j == pl.num_programs(1) - 1)
    def _():
        o_ref[...] = x_ref[...] + 0.5 * acc_ref[...]


def _ffn(x, g, wg, wu, wd, *, tm=512, tf=1408):
    n, d = x.shape
    dff = wg.shape[1]
    return pl.pallas_call(
        _ffn_kernel,
        out_shape=jax.ShapeDtypeStruct((n, d), F32),
        grid=(n // tm, dff // tf),
        in_specs=[
            pl.BlockSpec((tm, d), lambda i, j: (i, 0)),
            pl.BlockSpec((1, d), lambda i, j: (0, 0)),
            pl.BlockSpec((d, tf), lambda i, j: (0, j)),
            pl.BlockSpec((d, tf), lambda i, j: (0, j)),
            pl.BlockSpec((tf, d), lambda i, j: (j, 0)),
        ],
        out_specs=pl.BlockSpec((tm, d), lambda i, j: (i, 0)),
        scratch_shapes=[pltpu.VMEM((tm, d), BF16), pltpu.VMEM((tm, d), F32)],
        compiler_params=_cparams(("parallel", "arbitrary")),
        name="ffn",
    )(x, g.reshape(1, d), wg, wu, wd)


_C_Q = 0
_C_KS = _C_Q + A_WIDTH
_C_KW = _C_KS + KV_WIDTH
_C_VS = _C_KW + KV_WIDTH
_C_VW = _C_VS + KV_WIDTH
_C_KC = _C_VW + KV_WIDTH
_C_VC = _C_KC + KV_WIDTH
_C_GATE = _C_VC + KV_WIDTH
_C_U = _C_GATE + N_BRANCH * A_WIDTH
_C_GB = _C_U + A_WIDTH
_C_GC = _C_GB + A_WIDTH
_C_END = _C_GC + A_WIDTH
CONV_HALO = 8


def _inproj_kernel(x_ref, pos_ref, g_ref, w_ref, qnw_ref, knw_ref, cw_ref, pq_ref, pk_ref, freq_ref,
                   q_ref, qsw_ref, ks_ref, kw_ref, vs_ref, vw_ref, kc_ref, vc_ref, gates_ref, ob_ref,
                   vext_ref, *, tm, tiles_per_batch):
    i = pl.program_id(0)
    h = _rms(x_ref[...], g_ref[...]).astype(BF16)
    cos_t, sin_t = _rope_tables(pos_ref[...].astype(F32), freq_ref[...])

    def proj(c0, width):
        return _dot(h, w_ref[:, c0:c0 + width])

    q = _group_rms(proj(_C_Q, A_WIDTH), qnw_ref[...], pq_ref[...])
    q = _rope(q, cos_t, sin_t) * (HEAD_DIM ** -0.5)
    q_ref[...] = q.astype(BF16)
    qsw = jnp.concatenate(
        [pltpu.roll(q[:, c * LANES:(c + 1) * LANES], HEAD_DIM, axis=1) for c in range(A_WIDTH // LANES)], axis=1)
    qsw_ref[...] = qsw.astype(BF16)

    ks = _group_rms(proj(_C_KS, KV_WIDTH), knw_ref[1:2, :], pk_ref[...])
    ks_ref[...] = _rope(ks, cos_t, sin_t).astype(BF16)
    kw = _group_rms(proj(_C_KW, KV_WIDTH), knw_ref[2:3, :], pk_ref[...])
    kw_ref[...] = _rope(kw, cos_t, sin_t).astype(BF16)
    vs_ref[...] = proj(_C_VS, KV_WIDTH).astype(BF16)
    vw_ref[...] = proj(_C_VW, KV_WIDTH).astype(BF16)
    kc_ref[...] = proj(_C_KC, KV_WIDTH)
    vc_ref[...] = proj(_C_VC, KV_WIDTH)

    gl = proj(_C_GATE, N_BRANCH * A_WIDTH)
    gates_ref[...] = 1.0 / (1.0 + jnp.exp(-gl))

    @pl.when(i % tiles_per_batch == 0)
    def _():
        vext_ref[0:CONV_HALO, :] = jnp.zeros((CONV_HALO, A_WIDTH), F32)

    v = proj(_C_GC, A_WIDTH) * proj(_C_U, A_WIDTH)
    vext_ref[CONV_HALO:CONV_HALO + tm, :] = v
    y = (cw_ref[2:3, :] * v
         + cw_ref[1:2, :] * vext_ref[pl.ds(CONV_HALO - 1, tm), :]
         + cw_ref[0:1, :] * vext_ref[pl.ds(CONV_HALO - 2, tm), :])
    ob_ref[...] = (proj(_C_GB, A_WIDTH) * y).astype(BF16)
    vext_ref[0:CONV_HALO, :] = vext_ref[tm:tm + CONV_HALO, :]


def _inproj(x, pos_col, g, w, qnw, knw, cw, pq, pk, freq, *, seq, tm=256):
    n, d = x.shape
    row = lambda i: (i, 0)
    fixed = lambda i: (0, 0)
    widths = [(A_WIDTH, BF16), (A_WIDTH, BF16), (KV_WIDTH, BF16), (KV_WIDTH, BF16), (KV_WIDTH, BF16),
              (KV_WIDTH, BF16), (KV_WIDTH, F32), (KV_WIDTH, F32), (N_BRANCH * A_WIDTH, F32), (A_WIDTH, BF16)]
    return pl.pallas_call(
        functools.partial(_inproj_kernel, tm=tm, tiles_per_batch=seq // tm),
        out_shape=[jax.ShapeDtypeStruct((n, wd), dt) for wd, dt in widths],
        grid=(n // tm,),
        in_specs=[
            pl.BlockSpec((tm, d), row),
            pl.BlockSpec((tm, 1), row),
            pl.BlockSpec((1, d), fixed),
            pl.BlockSpec(w.shape, fixed),
            pl.BlockSpec(qnw.shape, fixed),
            pl.BlockSpec(knw.shape, fixed),
            pl.BlockSpec(cw.shape, fixed),
            pl.BlockSpec(pq.shape, fixed),
            pl.BlockSpec(pk.shape, fixed),
            pl.BlockSpec(freq.shape, fixed),
        ],
        out_specs=[pl.BlockSpec((tm, wd), row) for wd, _ in widths],
        scratch_shapes=[pltpu.VMEM((tm + CONV_HALO, A_WIDTH), F32)],
        compiler_params=_cparams(("arbitrary",)),
        name="inproj",
    )(x, pos_col, g.reshape(1, d), w, qnw, knw, cw, pq, pk, freq)


def _gelu_tanh(x):
    return 0.5 * x * (1.0 + jnp.tanh(math.sqrt(2.0 / math.pi) * (x + 0.044715 * (x * x * x))))


def _compress_kernel(xk_ref, xv_ref, pos_ref, pe_ref, wk_ref, wv_ref, w2_ref, knw_ref, pk_ref, freq_ref,
                     kc_ref, vc_ref, *, ncp):
    def branch(x, pe_a, pe_b, w_ref, w2):
        a = _dot((x + pe_a).astype(BF16), w_ref[0])
        b = _dot((x + pe_b).astype(BF16), w_ref[1])
        hid = a + pltpu.roll(b, ncp - 1, axis=0)
        return _dot(_gelu_tanh(hid).astype(BF16), w2)

    kc = branch(xk_ref[...], pe_ref[0:1, :], pe_ref[1:2, :], wk_ref, w2_ref[0])
    kc = _group_rms(kc, knw_ref[0:1, :], pk_ref[...])
    cos_t, sin_t = _rope_tables(pos_ref[...].astype(F32), freq_ref[...])
    kc_ref[...] = _rope(kc, cos_t, sin_t).astype(BF16)
    vc_ref[...] = branch(xv_ref[...], pe_ref[2:3, :], pe_ref[3:4, :], wv_ref, w2_ref[1]).astype(BF16)


def _compress(xk, xv, pos_c, pe, wk, wv, w2, knw, pk, freq):
    bsz, ncp, seg = xk.shape
    bat = lambda b: (b, 0, 0)
    fix2 = lambda b: (0, 0)
    fix3 = lambda b: (0, 0, 0)
    return pl.pallas_call(
        functools.partial(_compress_kernel, ncp=ncp),
        out_shape=[jax.ShapeDtypeStruct((bsz, ncp, KV_WIDTH), BF16)] * 2,
        grid=(bsz,),
        in_specs=[
            pl.BlockSpec((None, ncp, seg), bat),
            pl.BlockSpec((None, ncp, seg), bat),
            pl.BlockSpec((None, ncp, 1), bat),
            pl.BlockSpec(pe.shape, fix2),
            pl.BlockSpec(wk.shape, fix3),
            pl.BlockSpec(wv.shape, fix3),
            pl.BlockSpec(w2.shape, fix3),
            pl.BlockSpec(knw.shape, fix2),
            pl.BlockSpec(pk.shape, fix2),
            pl.BlockSpec(freq.shape, fix2),
        ],
        out_specs=[pl.BlockSpec((None, ncp, KV_WIDTH), bat)] * 2,
        compiler_params=_cparams(("parallel",)),
        name="compress",
    )(xk, xv, pos_c, pe, wk, wv, w2, knw, pk, freq)


def _head_slabs(q_ref, qsw_ref):
    q = q_ref[...]
    qs = qsw_ref[...]
    return [q[:, :LANES], qs[:, :LANES], q[:, LANES:], qs[:, LANES:]]


def _cmp_kernel(q_ref, qsw_ref, kct_ref, vc_ref, ov_ref, o_ref, sbt_ref, *, tq, ncp):
    i = pl.program_id(2)
    t = i * tq + lax.broadcasted_iota(jnp.int32, (tq, 1), 0)
    blk_end = lax.broadcasted_iota(jnp.int32, (1, ncp), 1) * CMP_STRIDE + (CMP_BLOCK - 1)
    valid = blk_end <= t
    any_valid = (t >= CMP_BLOCK - 1).astype(F32)
    kct = kct_ref[...]
    vc = vc_ref[...]
    psum = jnp.zeros((tq, ncp), F32)
    for hd, qh in enumerate(_head_slabs(q_ref, qsw_ref)):
        s = jnp.where(valid, _dot(qh, kct), NEG)
        e = jnp.exp(s - jnp.max(s, axis=-1, keepdims=True))
        p = e * (any_valid / jnp.sum(e, axis=-1, keepdims=True))
        o_ref[:, hd * HEAD_DIM:(hd + 1) * HEAD_DIM] = _dot(p.astype(BF16), vc)[:, :HEAD_DIM]
        psum = psum + p

    imp = _dot_split(psum, ov_ref[...])
    sel_start = lax.broadcasted_iota(jnp.int32, (1, LANES), 1) * SEL_BLOCK
    cur = (t >> int(math.log2(SEL_BLOCK))) << int(math.log2(SEL_BLOCK))
    imp = jnp.where(sel_start <= t, imp, -1.0)
    imp = jnp.where(sel_start == cur, 1e4, jnp.where(sel_start == 0, 1e4, imp))

    chunks = range(tq // LANES)
    imp_t = jnp.concatenate([imp[c * LANES:(c + 1) * LANES, :].T for c in chunks], axis=1)
    blk_f = lax.broadcasted_iota(jnp.int32, (LANES, 1), 0).astype(F32)
    bias_t = jnp.full((LANES, tq), NEG, F32)
    for _ in range(SEL_TOPK):
        mx = jnp.max(imp_t, axis=0, keepdims=True)
        first = jnp.min(jnp.where(imp_t == mx, blk_f, float(LANES)), axis=0, keepdims=True)
        pick = blk_f == first
        bias_t = jnp.where(pick, 0.0, bias_t)
        imp_t = jnp.where(pick, -3e38, imp_t)
    sbt_ref[...] = bias_t.astype(BF16)


def _cmp_select(q, qsw, kct, vc, ov, *, bsz, seq, tq=256):
    ncp = vc.shape[2]
    nq = seq // tq
    gw = HEADS_PER_GROUP * HEAD_DIM
    qmap = lambda b, g, i: (b * nq + i, g)
    kvmap = lambda b, g, i: (b, g, 0, 0)
    return pl.pallas_call(
        functools.partial(_cmp_kernel, tq=tq, ncp=ncp),
        out_shape=[jax.ShapeDtypeStruct((bsz * seq, A_WIDTH), F32),
                   jax.ShapeDtypeStruct((bsz, N_GROUPS, LANES, seq), BF16)],
        grid=(bsz, N_GROUPS, nq),
        in_specs=[
            pl.BlockSpec((tq, gw), qmap),
            pl.BlockSpec((tq, gw), qmap),
            pl.BlockSpec((None, None, LANES, ncp), kvmap),
            pl.BlockSpec((None, None, ncp, LANES), kvmap),
            pl.BlockSpec(ov.shape, lambda b, g, i: (0, 0)),
        ],
        out_specs=[pl.BlockSpec((tq, gw), qmap),
                   pl.BlockSpec((None, None, LANES, tq), lambda b, g, i: (b, g, 0, i))],
        compiler_params=_cparams(("parallel", "parallel", "parallel")),
        name="cmp_select",
    )(q, qsw, kct, vc, ov)


def _slc_kernel(qt_ref, sbt_ref, k_ref, vt_ref, o_ref,
                qa_ref, s_ref, p_ref, acc_ref, *, tq):
    i = pl.program_id(2)
    rows = HEADS_PER_GROUP * tq
    sbt = sbt_ref[...]
    cols = []
    for h in range(HEADS_PER_GROUP // 2):
        pair = qt_ref[h * LANES:(h + 1) * LANES, :]
        cols += [pair, jnp.concatenate([pair[HEAD_DIM:, :], pair[:HEAD_DIM, :]], axis=0)]
    qa_ref[...] = jnp.concatenate([jnp.concatenate([sbt] * HEADS_PER_GROUP, axis=1),
                                   jnp.concatenate(cols, axis=1)], axis=0)
    s_ref[...] = jnp.full(s_ref.shape, -jnp.inf, F32)
    p_ref[...] = jnp.zeros(p_ref.shape, BF16)
    acc_ref[...] = jnp.zeros(acc_ref.shape, F32)

    def accumulate(j, alpha):
        j = jnp.maximum(j, 0)
        pv = _dot(vt_ref[:, pl.ds(pl.multiple_of(j * tq, tq), tq)], p_ref[...])
        acc_ref[...] = alpha * acc_ref[...] + pv

    def probs(m):
        s = s_ref[...]
        m_new = jnp.maximum(m, jnp.max(s, axis=0, keepdims=True))
        p_ref[...] = jnp.exp(s - m_new).astype(BF16)
        return m_new, jnp.exp(m - m_new)

    def scores(j, masked):
        s = _dot(k_ref[pl.ds(pl.multiple_of(j * tq, tq), tq), :], qa_ref[...])
        if masked:
            causal = (lax.broadcasted_iota(jnp.int32, (tq, 1), 0)
                      <= (lax.broadcasted_iota(jnp.int32, (1, rows), 1) & (tq - 1)))
            s = jnp.where(causal, s, NEG)
        s_ref[...] = s

    def body(it, carry):
        m, alpha = carry
        accumulate(it - 2, alpha)
        m, alpha = probs(m)
        scores(it, False)
        return m, alpha

    carry = (jnp.full((1, rows), -3e38, F32), jnp.ones((1, rows), F32))
    carry = lax.fori_loop(0, i >> 1, lambda k, c: body(2 * k + 1, body(2 * k, c)), carry)
    m, alpha = lax.cond((i & 1) == 1, lambda c: body(i - 1, c), lambda c: c, carry)
    accumulate(i - 2, alpha)
    m, alpha = probs(m)
    scores(i, True)
    accumulate(i - 1, alpha)
    m, alpha = probs(m)
    accumulate(i, alpha)

    out_t = acc_ref[0:HEAD_DIM, :] / acc_ref[HEAD_DIM:HEAD_DIM + 1, :]
    for pair in range(HEADS_PER_GROUP // 2):
        for c in range(tq // LANES):
            col = 2 * pair * tq + c * LANES
            both = jnp.concatenate([out_t[:, col:col + LANES], out_t[:, col + tq:col + tq + LANES]], axis=0)
            o_ref[c * LANES:(c + 1) * LANES, pair * LANES:(pair + 1) * LANES] = both.T


def _slc(qt, sbt, k_aug, vt_aug, *, bsz, seq, tq=256):
    nq = seq // tq
    gw = HEADS_PER_GROUP * HEAD_DIM
    rows = HEADS_PER_GROUP * tq
    qmap = lambda b, g, i: (b * nq + i, g)
    kvmap = lambda b, g, i: (b, g, 0, 0)
    return pl.pallas_call(
        functools.partial(_slc_kernel, tq=tq),
        out_shape=jax.ShapeDtypeStruct((bsz * seq, A_WIDTH), F32),
        grid=(bsz, N_GROUPS, nq),
        in_specs=[
            pl.BlockSpec((None, gw, tq), lambda b, g, i: (b, g, i)),
            pl.BlockSpec((None, None, LANES, tq), lambda b, g, i: (b, g, 0, i)),
            pl.BlockSpec((None, None, seq, 2 * LANES), kvmap),
            pl.BlockSpec((None, None, LANES, seq), kvmap),
        ],
        out_specs=pl.BlockSpec((tq, gw), qmap),
        scratch_shapes=[pltpu.VMEM((2 * LANES, rows), BF16), pltpu.VMEM((tq, rows), F32),
                        pltpu.VMEM((tq, rows), BF16), pltpu.VMEM((LANES, rows), F32)],
        compiler_params=_cparams(("parallel", "parallel", "parallel")),
        name="slc_attn",
    )(qt, sbt, k_aug, vt_aug)


def _win_kernel(q_ref, qsw_ref, kt_ref, v_ref, o_ref, *, tq):
    i = pl.program_id(2)
    qa = jnp.concatenate(_head_slabs(q_ref, qsw_ref), axis=0)
    rows = HEADS_PER_GROUP * tq
    span = WINDOW + tq
    start = pl.multiple_of(i * tq, tq)
    t_row = i * tq + (lax.broadcasted_iota(jnp.int32, (rows, 1), 0) & (tq - 1))
    kpos = i * tq - WINDOW + lax.broadcasted_iota(jnp.int32, (1, span), 1)
    diff = t_row - kpos
    s = _dot(qa, kt_ref[:, pl.ds(start, span)])
    s = jnp.where(diff >= 0, jnp.where(diff < WINDOW, jnp.where(kpos >= 0, s, NEG), NEG), NEG)
    p = jnp.exp(s - jnp.max(s, axis=-1, keepdims=True))
    acc = _dot(p.astype(BF16), v_ref[pl.ds(start, span), :])
    out = acc[:, :HEAD_DIM] / acc[:, HEAD_DIM:HEAD_DIM + 1]
    for hd in range(HEADS_PER_GROUP):
        o_ref[:, hd * HEAD_DIM:(hd + 1) * HEAD_DIM] = out[hd * tq:(hd + 1) * tq, :]


def _win(q, qsw, kt_pad, v_pad, *, bsz, seq, tq=128):
    nq = seq // tq
    gw = HEADS_PER_GROUP * HEAD_DIM
    qmap = lambda b, g, i: (b * nq + i, g)
    kvmap = lambda b, g, i: (b, g, 0, 0)
    return pl.pallas_call(
        functools.partial(_win_kernel, tq=tq),
        out_shape=jax.ShapeDtypeStruct((bsz * seq, A_WIDTH), F32),
        grid=(bsz, N_GROUPS, nq),
        in_specs=[
            pl.BlockSpec((tq, gw), qmap),
            pl.BlockSpec((tq, gw), qmap),
            pl.BlockSpec((None, None, LANES, seq + WINDOW), kvmap),
            pl.BlockSpec((None, None, seq + WINDOW, LANES), kvmap),
        ],
        out_specs=pl.BlockSpec((tq, gw), qmap),
        compiler_params=_cparams(("parallel", "parallel", "parallel")),
        name="win_attn",
    )(q, qsw, kt_pad, v_pad)


def _outproj_kernel(x_ref, oc_ref, os_ref, ow_ref, g_ref, ob_ref, w_ref, o_ref):
    oa = (g_ref[:, 0:A_WIDTH] * oc_ref[...]
          + g_ref[:, A_WIDTH:2 * A_WIDTH] * os_ref[...]
          + g_ref[:, 2 * A_WIDTH:3 * A_WIDTH] * ow_ref[...])
    y = _dot(oa.astype(BF16), w_ref[0:A_WIDTH, :]) + _dot(ob_ref[...], w_ref[A_WIDTH:, :])
    o_ref[...] = x_ref[...] + y


def _outproj(x, oc, os_, ow, gates, ob, w, *, tm=512):
    n, d = x.shape
    row = lambda i: (i, 0)
    return pl.pallas_call(
        _outproj_kernel,
        out_shape=jax.ShapeDtypeStruct((n, d), F32),
        grid=(n // tm,),
        in_specs=[
            pl.BlockSpec((tm, d), row),
            pl.BlockSpec((tm, A_WIDTH), row),
            pl.BlockSpec((tm, A_WIDTH), row),
            pl.BlockSpec((tm, A_WIDTH), row),
            pl.BlockSpec((tm, N_BRANCH * A_WIDTH), row),
            pl.BlockSpec((tm, A_WIDTH), row),
            pl.BlockSpec(w.shape, lambda i: (0, 0)),
        ],
        out_specs=pl.BlockSpec((tm, d), row),
        compiler_params=_cparams(("parallel",)),
        name="outproj",
    )(x, oc, os_, ow, gates, ob, w)


def _pool_kernel(x_ref, halo_ref, g_ref, w_ref, sc_ref, o_ref, ext_ref, *, tm, tiles_per_batch):
    i = pl.program_id(0)
    first = i % tiles_per_batch == 0
    x = x_ref[...]
    h = _rms(x, g_ref[...])
    hh = _rms(halo_ref[...], g_ref[...])
    ext_ref[0:POOL_HALO, :] = jnp.where(first, 0.0, hh)
    ext_ref[POOL_HALO:POOL_HALO + tm, :] = h
    t = (i % tiles_per_batch) * tm + lax.broadcasted_iota(jnp.int32, (tm, 1), 0)
    gw = h.shape[1] // len(POOL_WINDOWS)
    for gi, win in enumerate(POOL_WINDOWS):
        c0 = gi * gw
        hg = h[:, c0:c0 + gw]
        tot = hg
        for sft in range(1, win):
            tot = tot + ext_ref[pl.ds(POOL_HALO - sft, tm), c0:c0 + gw]
        cnt = jnp.minimum(t + 1, win).astype(F32)
        y = _dot((tot / cnt - hg).astype(BF16), w_ref[gi])
        o_ref[:, c0:c0 + gw] = x[:, c0:c0 + gw] + y * sc_ref[:, c0:c0 + gw]


def _pool(x, g, w, scale, *, seq, tm=512):
    n, d = x.shape
    hb = tm // POOL_HALO
    return pl.pallas_call(
        functools.partial(_pool_kernel, tm=tm, tiles_per_batch=seq // tm),
        out_shape=jax.ShapeDtypeStruct((n, d), F32),
        grid=(n // tm,),
        in_specs=[
            pl.BlockSpec((tm, d), lambda i: (i, 0)),
            pl.BlockSpec((POOL_HALO, d), lambda i: (jnp.maximum(i * hb - 1, 0), 0)),
            pl.BlockSpec((1, d), lambda i: (0, 0)),
            pl.BlockSpec(w.shape, lambda i: (0, 0, 0)),
            pl.BlockSpec((1, d), lambda i: (0, 0)),
        ],
        out_specs=pl.BlockSpec((tm, d), lambda i: (i, 0)),
        scratch_shapes=[pltpu.VMEM((tm + POOL_HALO, d), F32)],
        compiler_params=_cparams(("parallel",)),
        name="pool_mixer",
    )(x, x, g.reshape(1, d), w, scale.reshape(1, d))


def _pack_w_in(w_in):
    o = 0
    q = w_in[:, o:o + A_WIDTH]; o += A_WIDTH
    kc, vc, ks, vs, kw, vw = [w_in[:, o + k * KV_WIDTH:o + (k + 1) * KV_WIDTH] for k in range(6)]
    o += 6 * KV_WIDTH
    gates = w_in[:, o:o + N_BRANCH * N_HEADS]; o += N_BRANCH * N_HEADS
    u, gb, gc = [w_in[:, o + k * A_WIDTH:o + (k + 1) * A_WIDTH] for k in range(3)]
    gates = gates.reshape(-1, N_HEADS, N_BRANCH).transpose(0, 2, 1)
    gates = jnp.repeat(gates, HEAD_DIM, axis=2).reshape(-1, N_BRANCH * A_WIDTH)
    return jnp.concatenate([q, ks, kw, vs, vw, kc, vc, gates, u, gb, gc], axis=1).astype(BF16)


def _group_mean_matrix(width):
    r = jnp.arange(width) // HEAD_DIM
    return jnp.where(r[:, None] == r[None, :], 1.0 / HEAD_DIM, 0.0).astype(BF16)


def _rope_freq_row():
    lane = jnp.arange(LANES) % HEAD_DIM
    freqs = ROPE_THETA ** (-jnp.arange(0, ROPE_DIM, 2, dtype=F32) / ROPE_DIM)
    return jnp.where(lane < ROPE_DIM, freqs[lane % ROPE_HALF], 0.0).astype(F32).reshape(1, LANES)


def _per_group(a, bsz, rows):
    return a.reshape(bsz, rows, N_GROUPS, HEAD_DIM).transpose(0, 2, 1, 3)


def _hybrid_mixer(x, positions, g, w_in, q_norm, k_norm, cmp_pos, cmp_w1, cmp_w2, conv_w, w_out, *, bsz, seq):
    n = bsz * seq
    ncp = seq // CMP_STRIDE
    freq = _rope_freq_row()
    pq = _group_mean_matrix(A_WIDTH)
    pk = _group_mean_matrix(KV_WIDTH)
    qnw = jnp.tile(q_norm, N_HEADS).reshape(1, A_WIDTH)
    knw = jnp.tile(k_norm, (1, N_GROUPS))

    q, qsw, ks, kw, vs, vw, kc_raw, vc_raw, gates, ob = _inproj(
        x, positions.reshape(n, 1), g, _pack_w_in(w_in), qnw, knw, conv_w, pq, pk, freq, seq=seq)

    eye_g = jnp.eye(N_GROUPS, dtype=F32)
    seg = CMP_STRIDE * KV_WIDTH

    def seg_weights(w1):
        w1r = w1.reshape(CMP_BLOCK, HEAD_DIM, CMP_HIDDEN)
        halves = [jnp.einsum('jdh,gk->jgdkh', w1r[a:a + CMP_STRIDE], eye_g).reshape(seg, N_GROUPS * CMP_HIDDEN)
                  for a in (0, CMP_STRIDE)]
        return jnp.stack(halves).astype(BF16)

    def seg_pos(pe):
        return [jnp.broadcast_to(pe[a:a + CMP_STRIDE, None, :], (CMP_STRIDE, N_GROUPS, HEAD_DIM)).reshape(seg)
                for a in (0, CMP_STRIDE)]

    pe = jnp.stack(seg_pos(cmp_pos[0]) + seg_pos(cmp_pos[1]))
    w2 = jnp.stack([jnp.einsum('hd,gk->ghkd', cmp_w2[a], eye_g).reshape(N_GROUPS * CMP_HIDDEN, KV_WIDTH)
                    for a in range(2)]).astype(BF16)
    blk_end = jnp.minimum(jnp.arange(ncp) * CMP_STRIDE + CMP_BLOCK - 1, seq - 1)
    pos_c = positions[:, blk_end].reshape(bsz, ncp, 1)
    kc, vc = _compress(kc_raw.reshape(bsz, ncp, seg), vc_raw.reshape(bsz, ncp, seg), pos_c, pe,
                       seg_weights(cmp_w1[0]), seg_weights(cmp_w1[1]), w2, knw, pk, freq)

    pad64 = ((0, 0), (0, 0), (0, 0), (0, LANES - HEAD_DIM))
    kc_g = jnp.pad(_per_group(kc, bsz, ncp), pad64)
    vc_g = jnp.pad(_per_group(vc, bsz, ncp), pad64)
    cstart = jnp.arange(ncp) * CMP_STRIDE
    sstart = jnp.arange(LANES) * SEL_BLOCK
    ov = ((cstart[:, None] < sstart[None, :] + SEL_BLOCK)
          & (cstart[:, None] + CMP_BLOCK > sstart[None, :])).astype(BF16)
    o_cmp, sel_bias = _cmp_select(q, qsw, jnp.swapaxes(kc_g, 2, 3), vc_g, ov, bsz=bsz, seq=seq)

    blk_onehot = (jnp.arange(seq)[:, None] // SEL_BLOCK == jnp.arange(LANES)[None, :]).astype(BF16)
    ones_col = jnp.zeros((LANES - HEAD_DIM,), BF16).at[0].set(1.0)

    def with_ones(v):
        vg = _per_group(v.reshape(bsz, seq, KV_WIDTH), bsz, seq)
        return jnp.concatenate([vg, jnp.broadcast_to(ones_col, vg.shape[:3] + (LANES - HEAD_DIM,))], axis=-1)

    ks_g = _per_group(ks.reshape(bsz, seq, KV_WIDTH), bsz, seq)
    k_aug = jnp.concatenate([jnp.broadcast_to(blk_onehot, (bsz, N_GROUPS, seq, LANES)), ks_g,
                             jnp.zeros((bsz, N_GROUPS, seq, LANES - HEAD_DIM), BF16)], axis=-1)
    q_t = jnp.swapaxes(q.reshape(bsz, seq, A_WIDTH), 1, 2)
    o_slc = _slc(q_t, sel_bias, k_aug, jnp.swapaxes(with_ones(vs), 2, 3), bsz=bsz, seq=seq)

    front = ((0, 0), (0, 0), (WINDOW, 0), (0, 0))
    kw_g = jnp.pad(jnp.pad(_per_group(kw.reshape(bsz, seq, KV_WIDTH), bsz, seq), pad64), front)
    o_win = _win(q, qsw, jnp.swapaxes(kw_g, 2, 3), jnp.pad(with_ones(vw), front), bsz=bsz, seq=seq)

    return _outproj(x, o_cmp, o_slc, o_win, gates, ob, w_out.astype(BF16))


def kernel(x, positions, ffn_norm, ffn_w_gate, ffn_w_up, ffn_w_down, mix_norm, hyb_w_in, hyb_q_norm, hyb_k_norm,
           hyb_cmp_pos, hyb_cmp_w1, hyb_cmp_w2, hyb_conv_w, hyb_w_out, pool_w, pool_scale):
    bsz, seq, d = x.shape
    depth = ffn_norm.shape[0]
    assert seq % 2048 == 0 and seq // SEL_BLOCK <= LANES and seq // SEL_BLOCK >= SEL_TOPK
    wg, wu, wd = (w.astype(BF16) for w in (ffn_w_gate, ffn_w_up, ffn_w_down))
    xf = x.reshape(bsz * seq, d)
    for layer in range(depth):
        xf = _ffn(xf, ffn_norm[layer, 0], wg[layer, 0], wu[layer, 0], wd[layer, 0])
        i = layer // 2
        if layer % 2 == 0:
            xf = _hybrid_mixer(xf, positions, mix_norm[layer], hyb_w_in[i], hyb_q_norm[i], hyb_k_norm[i],
                               hyb_cmp_pos[i], hyb_cmp_w1[i], hyb_cmp_w2[i], hyb_conv_w[i], hyb_w_out[i],
                               bsz=bsz, seq=seq)
        else:
            xf = _pool(xf, mix_norm[layer], pool_w[i].astype(BF16), pool_scale[i], seq=seq)
        xf = _ffn(xf, ffn_norm[layer, 1], wg[layer, 1], wu[layer, 1], wd[layer, 1])
    return xf.reshape(bsz, seq, d)
```

```python
import functools
import math

import jax
import jax.numpy as jnp
from jax import lax
from jax.experimental import pallas as pl
from jax.experimental.pallas import tpu as pltpu

F32 = jnp.float32
BF16 = jnp.bfloat16

HEAD_DIM = 64
N_GROUPS = 2
HEADS_PER_GROUP = 4
N_HEADS = N_GROUPS * HEADS_PER_GROUP
A_WIDTH = N_HEADS * HEAD_DIM
KV_WIDTH = N_GROUPS * HEAD_DIM
ROPE_DIM = HEAD_DIM // 4
ROPE_HALF = ROPE_DIM // 2
ROPE_THETA = 500000.0
CMP_BLOCK = 32
CMP_STRIDE = 16
CMP_HIDDEN = 2 * HEAD_DIM
SEL_BLOCK = 64
SEL_TOPK = 16
WINDOW = 512
N_BRANCH = 3
_BR_CMP, _BR_SLC, _BR_WIN = 0, 1, 2
CONV_WIDTH = 3
POOL_WINDOWS = (2, 4, 8, 16)
POOL_HALO = 16
EPS = 1e-6
NEG = -1e30

LANES = 128
VMEM_LIMIT_BYTES = 56 * 1024 * 1024


def _cparams(sem):
    return pltpu.CompilerParams(dimension_semantics=sem, vmem_limit_bytes=VMEM_LIMIT_BYTES)


def _dot(a, b):
    return jnp.dot(a, b, preferred_element_type=F32)


def _dot_nt(a, b):
    return lax.dot_general(a, b, (((1,), (1,)), ((), ())), preferred_element_type=F32)


def _dot_split(a, b):
    hi = a.astype(BF16)
    lo = (a - hi.astype(F32)).astype(BF16)
    return _dot(hi, b) + _dot(lo, b)


def _rms(x, g):
    ms = jnp.mean(x * x, axis=-1, keepdims=True)
    return x * lax.rsqrt(ms + EPS) * g


def _group_rms(x, g, pmat):
    ms = _dot_split(x * x, pmat)
    return x * lax.rsqrt(ms + EPS) * g


def _rope_tables(pos_col, freq_row):
    ang = pos_col * freq_row
    lane = lax.broadcasted_iota(jnp.int32, (1, LANES), 1) & (HEAD_DIM - 1)
    sign = jnp.where(lane < ROPE_HALF, -1.0, 1.0).astype(F32)
    return jnp.cos(ang), jnp.sin(ang) * sign


def _rope(x, cos_t, sin_t):
    w = x.shape[1]
    reps = w // LANES
    if reps > 1:
        cos_t = jnp.concatenate([cos_t] * reps, axis=1)
        sin_t = jnp.concatenate([sin_t] * reps, axis=1)
    lane = lax.broadcasted_iota(jnp.int32, (1, w), 1) & (HEAD_DIM - 1)
    partner = jnp.where(lane < ROPE_HALF,
                        pltpu.roll(x, w - ROPE_HALF, axis=1),
                        pltpu.roll(x, ROPE_HALF, axis=1))
    return x * cos_t + partner * sin_t


def _ffn_kernel(x_ref, g_ref, wg_ref, wu_ref, wd_ref, o_ref, h_ref, acc_ref):
    j = pl.program_id(1)

    @pl.when(j == 0)
    def _():
        h_ref[...] = _rms(x_ref[...], g_ref[...]).astype(BF16)
        acc_ref[...] = jnp.zeros_like(acc_ref)

    h = h_ref[...]
    a = _dot(h, wg_ref[...])
    b = _dot(h, wu_ref[...])
    act = a * (1.0 / (1.0 + jnp.exp(-a))) * b
    acc_ref[...] += _dot(act.astype(BF16), wd_ref[...])

    @pl.when(j == pl.num_programs(1) - 1)
    def _():
        o_ref[...] = x_ref[...] + 0.5 * acc_ref[...]


def _ffn(x, g, wg, wu, wd, *, tm=512, tf=1408):
    n, d = x.shape
    dff = wg.shape[1]
    return pl.pallas_call(
        _ffn_kernel,
        out_shape=jax.ShapeDtypeStruct((n, d), F32),
        grid=(n // tm, dff // tf),
        in_specs=[
            pl.BlockSpec((tm, d), lambda i, j: (i, 0)),
            pl.BlockSpec((1, d), lambda i, j: (0, 0)),
            pl.BlockSpec((d, tf), lambda i, j: (0, j)),
            pl.BlockSpec((d, tf), lambda i, j: (0, j)),
            pl.BlockSpec((tf, d), lambda i, j: (j, 0)),
        ],
        out_specs=pl.BlockSpec((tm, d), lambda i, j: (i, 0)),
        scratch_shapes=[pltpu.VMEM((tm, d), BF16), pltpu.VMEM((tm, d), F32)],
        compiler_params=_cparams(("parallel", "arbitrary")),
        name="ffn",
    )(x, g.reshape(1, d), wg, wu, wd)


GATE_WIDTH = N_GROUPS * LANES
_C_Q = 0
_C_K = _C_Q + A_WIDTH
_C_V = _C_K + 2 * KV_WIDTH
_C_C = _C_V + 2 * KV_WIDTH
_C_GATE = _C_C + 2 * KV_WIDTH
_C_U = _C_GATE + GATE_WIDTH
_C_GB = _C_U + A_WIDTH
_C_GC = _C_GB + A_WIDTH
_C_END = _C_GC + A_WIDTH
CONV_HALO = 8


def _inproj_kernel(x_ref, pos_ref, g_ref, w_ref, qnw_ref, knw_ref, cw_ref, pq_ref, pk_ref, freq_ref,
                   q_ref, ks_ref, kw_ref, vs_ref, vw_ref, kc_ref, vc_ref, gates_ref, ob_ref,
                   vext_ref, *, tm, tiles_per_batch):
    i = pl.program_id(0)
    h = _rms(x_ref[...], g_ref[...]).astype(BF16)
    cos_t, sin_t = _rope_tables(pos_ref[...].astype(F32), freq_ref[...])

    def proj(c0, width):
        return _dot(h, w_ref[:, c0:c0 + width])

    q = _group_rms(proj(_C_Q, A_WIDTH), qnw_ref[...], pq_ref[...])
    q = _rope(q, cos_t, sin_t) * (HEAD_DIM ** -0.5)
    q_ref[...] = q.astype(BF16)

    k2 = proj(_C_K, 2 * KV_WIDTH)
    ks = _group_rms(k2[:, :KV_WIDTH], knw_ref[1:2, :], pk_ref[...])
    ks_ref[...] = _rope(ks, cos_t, sin_t).astype(BF16)
    kw = _group_rms(k2[:, KV_WIDTH:], knw_ref[2:3, :], pk_ref[...])
    kw_ref[...] = _rope(kw, cos_t, sin_t).astype(BF16)
    v2 = proj(_C_V, 2 * KV_WIDTH).astype(BF16)
    vs_ref[...] = v2[:, :KV_WIDTH]
    vw_ref[...] = v2[:, KV_WIDTH:]
    c2 = proj(_C_C, 2 * KV_WIDTH)
    kc_ref[...] = c2[:, :KV_WIDTH]
    vc_ref[...] = c2[:, KV_WIDTH:]

    gates_ref[...] = 1.0 / (1.0 + jnp.exp(-proj(_C_GATE, GATE_WIDTH)))

    @pl.when(i % tiles_per_batch == 0)
    def _():
        vext_ref[0:CONV_HALO, :] = jnp.zeros((CONV_HALO, A_WIDTH), F32)

    v = proj(_C_GC, A_WIDTH) * proj(_C_U, A_WIDTH)
    vext_ref[CONV_HALO:CONV_HALO + tm, :] = v
    y = (cw_ref[2:3, :] * v
         + cw_ref[1:2, :] * vext_ref[pl.ds(CONV_HALO - 1, tm), :]
         + cw_ref[0:1, :] * vext_ref[pl.ds(CONV_HALO - 2, tm), :])
    ob_ref[...] = (proj(_C_GB, A_WIDTH) * y).astype(BF16)
    vext_ref[0:CONV_HALO, :] = vext_ref[tm:tm + CONV_HALO, :]


def _inproj(x, pos_col, g, w, qnw, knw, cw, pq, pk, freq, *, seq, tm=256):
    n, d = x.shape
    row = lambda i: (i, 0)
    fixed = lambda i: (0, 0)
    widths = [(A_WIDTH, BF16), (KV_WIDTH, BF16), (KV_WIDTH, BF16), (KV_WIDTH, BF16),
              (KV_WIDTH, BF16), (KV_WIDTH, F32), (KV_WIDTH, F32), (GATE_WIDTH, F32), (A_WIDTH, BF16)]
    return pl.pallas_call(
        functools.partial(_inproj_kernel, tm=tm, tiles_per_batch=seq // tm),
        out_shape=[jax.ShapeDtypeStruct((n, wd), dt) for wd, dt in widths],
        grid=(n // tm,),
        in_specs=[
            pl.BlockSpec((tm, d), row),
            pl.BlockSpec((tm, 1), row),
            pl.BlockSpec((1, d), fixed),
            pl.BlockSpec(w.shape, fixed),
            pl.BlockSpec(qnw.shape, fixed),
            pl.BlockSpec(knw.shape, fixed),
            pl.BlockSpec(cw.shape, fixed),
            pl.BlockSpec(pq.shape, fixed),
            pl.BlockSpec(pk.shape, fixed),
            pl.BlockSpec(freq.shape, fixed),
        ],
        out_specs=[pl.BlockSpec((tm, wd), row) for wd, _ in widths],
        scratch_shapes=[pltpu.VMEM((tm + CONV_HALO, A_WIDTH), F32)],
        compiler_params=_cparams(("arbitrary",)),
        name="inproj",
    )(x, pos_col, g.reshape(1, d), w, qnw, knw, cw, pq, pk, freq)


def _gelu_tanh(x):
    return 0.5 * x * (1.0 + jnp.tanh(math.sqrt(2.0 / math.pi) * (x + 0.044715 * (x * x * x))))


def _compress_kernel(xk_ref, xv_ref, pos_ref, pe_ref, wk_ref, wv_ref, w2_ref, knw_ref, pk_ref, freq_ref,
                     kc_ref, vc_ref, *, ncp):
    def branch(x, pe_a, pe_b, w_ref, w2):
        a = _dot((x + pe_a).astype(BF16), w_ref[0])
        b = _dot((x + pe_b).astype(BF16), w_ref[1])
        hid = a + pltpu.roll(b, ncp - 1, axis=0)
        return _dot(_gelu_tanh(hid).astype(BF16), w2)

    kc = branch(xk_ref[...], pe_ref[0:1, :], pe_ref[1:2, :], wk_ref, w2_ref[0])
    kc = _group_rms(kc, knw_ref[0:1, :], pk_ref[...])
    cos_t, sin_t = _rope_tables(pos_ref[...].astype(F32), freq_ref[...])
    kc_ref[...] = _rope(kc, cos_t, sin_t).astype(BF16)
    vc_ref[...] = branch(xv_ref[...], pe_ref[2:3, :], pe_ref[3:4, :], wv_ref, w2_ref[1]).astype(BF16)


def _compress(xk, xv, pos_c, pe, wk, wv, w2, knw, pk, freq):
    bsz, ncp, seg = xk.shape
    bat = lambda b: (b, 0, 0)
    fix2 = lambda b: (0, 0)
    fix3 = lambda b: (0, 0, 0)
    return pl.pallas_call(
        functools.partial(_compress_kernel, ncp=ncp),
        out_shape=[jax.ShapeDtypeStruct((bsz, ncp, KV_WIDTH), BF16)] * 2,
        grid=(bsz,),
        in_specs=[
            pl.BlockSpec((None, ncp, seg), bat),
            pl.BlockSpec((None, ncp, seg), bat),
            pl.BlockSpec((None, ncp, 1), bat),
            pl.BlockSpec(pe.shape, fix2),
            pl.BlockSpec(wk.shape, fix3),
            pl.BlockSpec(wv.shape, fix3),
            pl.BlockSpec(w2.shape, fix3),
            pl.BlockSpec(knw.shape, fix2),
            pl.BlockSpec(pk.shape, fix2),
            pl.BlockSpec(freq.shape, fix2),
        ],
        out_specs=[pl.BlockSpec((None, ncp, KV_WIDTH), bat)] * 2,
        compiler_params=_cparams(("parallel",)),
        name="compress",
    )(xk, xv, pos_c, pe, wk, wv, w2, knw, pk, freq)


def _head_columns(qt_ref):
    cols = []
    for pair in range(HEADS_PER_GROUP // 2):
        both = qt_ref[pair * LANES:(pair + 1) * LANES, :]
        cols += [both, jnp.concatenate([both[HEAD_DIM:, :], both[:HEAD_DIM, :]], axis=0)]
    return cols


def _software_pipeline(n, scores, probs, finish):
    s, p = {}, {}
    for step in range(n + 2):
        if step >= 2:
            finish(step - 2, p.pop(step - 2))
        if 1 <= step <= n:
            p[step - 1] = probs(step - 1, s.pop(step - 1))
        if step < n:
            s[step] = scores(step)


def _gate_row(gt_ref, branch, hd):
    r = branch * HEADS_PER_GROUP + hd
    return gt_ref[r:r + 1, :]


def _cmp_kernel(qt_ref, kc_ref, vct_ref, ovt_ref, gt_ref, ot_ref, sbt_ref, *, tq, ncp):
    i = pl.program_id(2)
    t = i * tq + lax.broadcasted_iota(jnp.int32, (1, tq), 1)
    blk_end = lax.broadcasted_iota(jnp.int32, (ncp, 1), 0) * CMP_STRIDE + (CMP_BLOCK - 1)
    valid = blk_end <= t
    any_valid = (t >= CMP_BLOCK - 1).astype(F32)
    kc = kc_ref[...]
    vct = vct_ref[...]
    heads = _head_columns(qt_ref)
    psum_parts = []

    def scores(hd):
        return jnp.where(valid, _dot(kc, heads[hd]), NEG)

    def probs(hd, s):
        e = jnp.exp(s - jnp.max(s, axis=0, keepdims=True))
        return e * (any_valid / jnp.sum(e, axis=0, keepdims=True))

    def finish(hd, p):
        o_t = _dot(vct, p.astype(BF16))[:HEAD_DIM, :]
        ot_ref[hd * HEAD_DIM:(hd + 1) * HEAD_DIM, :] = o_t * _gate_row(gt_ref, _BR_CMP, hd)
        psum_parts.append(p)

    _software_pipeline(HEADS_PER_GROUP, scores, probs, finish)
    psum = (psum_parts[0] + psum_parts[1]) + (psum_parts[2] + psum_parts[3])

    hi = psum.astype(BF16)
    lo = (psum - hi.astype(F32)).astype(BF16)
    imp = _dot(ovt_ref[...], hi) + _dot(ovt_ref[...], lo)
    blk = lax.broadcasted_iota(jnp.int32, (LANES, 1), 0)
    sel_start = blk * SEL_BLOCK
    cur = (t >> int(math.log2(SEL_BLOCK))) << int(math.log2(SEL_BLOCK))
    imp = jnp.where(sel_start <= t, imp, -1.0)
    imp = jnp.where(sel_start == cur, 1e4, jnp.where(sel_start == 0, 1e4, imp))

    blk_f = blk.astype(F32)
    taken = -3e38
    for _ in range(SEL_TOPK):
        mx = jnp.max(imp, axis=0, keepdims=True)
        first = jnp.min(jnp.where(imp == mx, blk_f, float(LANES)), axis=0, keepdims=True)
        imp = jnp.where(blk_f == first, taken, imp)
    sbt_ref[...] = jnp.where(imp == taken, 0.0, NEG).astype(BF16)


def _tmaps(nq):
    return (lambda b, g, i: (b, g, i)), (lambda b, g, i: (g, b * nq + i)), (lambda b, g, i: (b, g, 0, 0))


def _cmp_select(qt, kc, vct, ovt, gates_t, *, bsz, seq, tq=256):
    ncp = kc.shape[2]
    nq = seq // tq
    gw = HEADS_PER_GROUP * HEAD_DIM
    tmap, gmap, kvmap = _tmaps(nq)
    return pl.pallas_call(
        functools.partial(_cmp_kernel, tq=tq, ncp=ncp),
        out_shape=[jax.ShapeDtypeStruct((bsz, A_WIDTH, seq), F32),
                   jax.ShapeDtypeStruct((bsz, N_GROUPS, LANES, seq), BF16)],
        grid=(bsz, N_GROUPS, nq),
        in_specs=[
            pl.BlockSpec((None, gw, tq), tmap),
            pl.BlockSpec((None, None, ncp, LANES), kvmap),
            pl.BlockSpec((None, None, LANES, ncp), kvmap),
            pl.BlockSpec(ovt.shape, lambda b, g, i: (0, 0)),
            pl.BlockSpec((LANES, tq), gmap),
        ],
        out_specs=[pl.BlockSpec((None, gw, tq), tmap),
                   pl.BlockSpec((None, None, LANES, tq), lambda b, g, i: (b, g, 0, i))],
        compiler_params=_cparams(("parallel", "parallel", "parallel")),
        name="cmp_select",
    )(qt, kc, vct, ovt, gates_t)


def _slc_kernel(qt_ref, sbt_ref, k_ref, vt_ref, gt_ref, prev_ref, o_ref,
                qa_ref, s_ref, p_ref, acc_ref, *, tq):
    i = pl.program_id(2)
    rows = HEADS_PER_GROUP * tq
    qa_ref[...] = jnp.concatenate([jnp.concatenate([sbt_ref[...]] * HEADS_PER_GROUP, axis=1),
                                   jnp.concatenate(_head_columns(qt_ref), axis=1)], axis=0)
    s_ref[...] = jnp.full(s_ref.shape, -jnp.inf, F32)
    p_ref[...] = jnp.zeros(p_ref.shape, BF16)
    acc_ref[...] = jnp.zeros(acc_ref.shape, F32)

    def accumulate(j, alpha):
        j = jnp.maximum(j, 0)
        pv = _dot(vt_ref[:, pl.ds(pl.multiple_of(j * tq, tq), tq)], p_ref[...])
        acc_ref[...] = alpha * acc_ref[...] + pv

    def probs(m):
        s = s_ref[...]
        m_new = jnp.maximum(m, jnp.max(s, axis=0, keepdims=True))
        p_ref[...] = jnp.exp(s - m_new).astype(BF16)
        return m_new, jnp.exp(m - m_new)

    def scores(j, masked):
        s = _dot(k_ref[pl.ds(pl.multiple_of(j * tq, tq), tq), :], qa_ref[...])
        if masked:
            causal = (lax.broadcasted_iota(jnp.int32, (tq, 1), 0)
                      <= (lax.broadcasted_iota(jnp.int32, (1, rows), 1) & (tq - 1)))
            s = jnp.where(causal, s, NEG)
        s_ref[...] = s

    def body(it, carry):
        m, alpha = carry
        accumulate(it - 2, alpha)
        m, alpha = probs(m)
        scores(it, False)
        return m, alpha

    carry = (jnp.full((1, rows), -3e38, F32), jnp.ones((1, rows), F32))
    carry = lax.fori_loop(0, i >> 1, lambda k, c: body(2 * k + 1, body(2 * k, c)), carry)
    m, alpha = lax.cond((i & 1) == 1, lambda c: body(i - 1, c), lambda c: c, carry)
    accumulate(i - 2, alpha)
    m, alpha = probs(m)
    scores(i, True)
    accumulate(i - 1, alpha)
    m, alpha = probs(m)
    accumulate(i, alpha)

    gate = jnp.concatenate([_gate_row(gt_ref, _BR_SLC, hd) for hd in range(HEADS_PER_GROUP)], axis=1)
    out_t = acc_ref[0:HEAD_DIM, :] * (gate / acc_ref[HEAD_DIM:HEAD_DIM + 1, :])
    for pair in range(HEADS_PER_GROUP // 2):
        for c in range(tq // LANES):
            col = 2 * pair * tq + c * LANES
            both = jnp.concatenate([out_t[:, col:col + LANES], out_t[:, col + tq:col + tq + LANES]], axis=0)
            both = both + prev_ref[pair * LANES:(pair + 1) * LANES, c * LANES:(c + 1) * LANES]
            o_ref[c * LANES:(c + 1) * LANES, pair * LANES:(pair + 1) * LANES] = both.T.astype(BF16)


def _slc(qt, sbt, k_aug, vt_aug, gates_t, prev, *, bsz, seq, tq=256):
    nq = seq // tq
    gw = HEADS_PER_GROUP * HEAD_DIM
    rows = HEADS_PER_GROUP * tq
    tmap, gmap, kvmap = _tmaps(nq)
    return pl.pallas_call(
        functools.partial(_slc_kernel, tq=tq),
        out_shape=jax.ShapeDtypeStruct((bsz * seq, A_WIDTH), BF16),
        grid=(bsz, N_GROUPS, nq),
        in_specs=[
            pl.BlockSpec((None, gw, tq), tmap),
            pl.BlockSpec((None, None, LANES, tq), lambda b, g, i: (b, g, 0, i)),
            pl.BlockSpec((None, None, seq, 2 * LANES), kvmap),
            pl.BlockSpec((None, None, LANES, seq), kvmap),
            pl.BlockSpec((LANES, tq), gmap),
            pl.BlockSpec((None, gw, tq), tmap),
        ],
        out_specs=pl.BlockSpec((tq, gw), lambda b, g, i: (b * nq + i, g)),
        scratch_shapes=[pltpu.VMEM((2 * LANES, rows), BF16), pltpu.VMEM((tq, rows), F32),
                        pltpu.VMEM((tq, rows), BF16), pltpu.VMEM((LANES, rows), F32)],
        compiler_params=_cparams(("parallel", "parallel", "parallel")),
        name="slc_attn",
    )(qt, sbt, k_aug, vt_aug, gates_t, prev)


def _win_kernel(qt_ref, k_ref, vt_ref, band_ref, gt_ref, prev_ref, ot_ref, *, tq):
    i = pl.program_id(2)
    span = WINDOW + tq
    start = pl.multiple_of(i * tq, tq)
    k = k_ref[pl.ds(start, span), :]
    vt = vt_ref[:, pl.ds(start, span)]
    heads = _head_columns(qt_ref)

    def scores(hd):
        return _dot(k, heads[hd]) + band_ref[...]

    def probs(hd, s):
        return jnp.exp(s - jnp.max(s, axis=0, keepdims=True)).astype(BF16)

    def finish(hd, p):
        acc = _dot(vt, p)
        rows = slice(hd * HEAD_DIM, (hd + 1) * HEAD_DIM)
        scale = _gate_row(gt_ref, _BR_WIN, hd) / acc[HEAD_DIM:HEAD_DIM + 1, :]
        ot_ref[rows, :] = prev_ref[rows, :] + acc[:HEAD_DIM, :] * scale

    _software_pipeline(HEADS_PER_GROUP, scores, probs, finish)


def _window_band(tq):
    v = jnp.arange(WINDOW // tq + 1)[:, None, None]
    kpos = v * tq - WINDOW + jnp.arange(WINDOW + tq)[None, :, None]
    t = v * tq + jnp.arange(tq)[None, None, :]
    ok = (t - kpos >= 0) & (t - kpos < WINDOW) & (kpos >= 0)
    return jnp.where(ok, 0.0, NEG).astype(F32)


def _win(qt, k_pad, vt_pad, gates_t, prev, *, bsz, seq, tq=256):
    nq = seq // tq
    gw = HEADS_PER_GROUP * HEAD_DIM
    band = _window_band(tq)
    last = band.shape[0] - 1
    tmap, gmap, kvmap = _tmaps(nq)
    return pl.pallas_call(
        functools.partial(_win_kernel, tq=tq),
        out_shape=jax.ShapeDtypeStruct((bsz, A_WIDTH, seq), F32),
        grid=(bsz, N_GROUPS, nq),
        in_specs=[
            pl.BlockSpec((None, gw, tq), tmap),
            pl.BlockSpec((None, None, seq + WINDOW, LANES), kvmap),
            pl.BlockSpec((None, None, LANES, seq + WINDOW), kvmap),
            pl.BlockSpec((None,) + band.shape[1:], lambda b, g, i: (jnp.minimum(i, last), 0, 0)),
            pl.BlockSpec((LANES, tq), gmap),
            pl.BlockSpec((None, gw, tq), tmap),
        ],
        out_specs=pl.BlockSpec((None, gw, tq), tmap),
        compiler_params=_cparams(("parallel", "parallel", "parallel")),
        name="win_attn",
    )(qt, k_pad, vt_pad, band, gates_t, prev)


def _outproj_kernel(x_ref, oa_ref, ob_ref, w_ref, o_ref):
    y = _dot(oa_ref[...], w_ref[0:A_WIDTH, :]) + _dot(ob_ref[...], w_ref[A_WIDTH:, :])
    o_ref[...] = x_ref[...] + y


def _outproj(x, oa, ob, w, *, tm=512):
    n, d = x.shape
    row = lambda i: (i, 0)
    return pl.pallas_call(
        _outproj_kernel,
        out_shape=jax.ShapeDtypeStruct((n, d), F32),
        grid=(n // tm,),
        in_specs=[
            pl.BlockSpec((tm, d), row),
            pl.BlockSpec((tm, A_WIDTH), row),
            pl.BlockSpec((tm, A_WIDTH), row),
            pl.BlockSpec(w.shape, lambda i: (0, 0)),
        ],
        out_specs=pl.BlockSpec((tm, d), row),
        compiler_params=_cparams(("parallel",)),
        name="outproj",
    )(x, oa, ob, w)


def _pool_kernel(x_ref, halo_ref, g_ref, w_ref, sc_ref, o_ref, ext_ref, *, tm, tiles_per_batch):
    i = pl.program_id(0)
    first = i % tiles_per_batch == 0
    x = x_ref[...]
    h = _rms(x, g_ref[...])
    hh = _rms(halo_ref[...], g_ref[...])
    ext_ref[0:POOL_HALO, :] = jnp.where(first, 0.0, hh)
    ext_ref[POOL_HALO:POOL_HALO + tm, :] = h
    t = (i % tiles_per_batch) * tm + lax.broadcasted_iota(jnp.int32, (tm, 1), 0)
    gw = h.shape[1] // len(POOL_WINDOWS)
    for gi, win in enumerate(POOL_WINDOWS):
        c0 = gi * gw
        hg = h[:, c0:c0 + gw]
        tot = hg
        for sft in range(1, win):
            tot = tot + ext_ref[pl.ds(POOL_HALO - sft, tm), c0:c0 + gw]
        cnt = jnp.minimum(t + 1, win).astype(F32)
        y = _dot((tot / cnt - hg).astype(BF16), w_ref[gi])
        o_ref[:, c0:c0 + gw] = x[:, c0:c0 + gw] + y * sc_ref[:, c0:c0 + gw]


def _pool(x, g, w, scale, *, seq, tm=512):
    n, d = x.shape
    hb = tm // POOL_HALO
    return pl.pallas_call(
        functools.partial(_pool_kernel, tm=tm, tiles_per_batch=seq // tm),
        out_shape=jax.ShapeDtypeStruct((n, d), F32),
        grid=(n // tm,),
        in_specs=[
            pl.BlockSpec((tm, d), lambda i: (i, 0)),
            pl.BlockSpec((POOL_HALO, d), lambda i: (jnp.maximum(i * hb - 1, 0), 0)),
            pl.BlockSpec((1, d), lambda i: (0, 0)),
            pl.BlockSpec(w.shape, lambda i: (0, 0, 0)),
            pl.BlockSpec((1, d), lambda i: (0, 0)),
        ],
        out_specs=pl.BlockSpec((tm, d), lambda i: (i, 0)),
        scratch_shapes=[pltpu.VMEM((tm + POOL_HALO, d), F32)],
        compiler_params=_cparams(("parallel",)),
        name="pool_mixer",
    )(x, x, g.reshape(1, d), w, scale.reshape(1, d))


def _pack_w_in(w_in):
    o = 0
    q = w_in[:, o:o + A_WIDTH]; o += A_WIDTH
    kc, vc, ks, vs, kw, vw = [w_in[:, o + k * KV_WIDTH:o + (k + 1) * KV_WIDTH] for k in range(6)]
    o += 6 * KV_WIDTH
    gates = w_in[:, o:o + N_BRANCH * N_HEADS]; o += N_BRANCH * N_HEADS
    u, gb, gc = [w_in[:, o + k * A_WIDTH:o + (k + 1) * A_WIDTH] for k in range(3)]
    gates = gates.reshape(-1, N_GROUPS, HEADS_PER_GROUP, N_BRANCH).transpose(0, 1, 3, 2)
    gates = gates.reshape(-1, N_GROUPS, N_BRANCH * HEADS_PER_GROUP)
    gates = jnp.pad(gates, ((0, 0), (0, 0), (0, LANES - N_BRANCH * HEADS_PER_GROUP))).reshape(-1, GATE_WIDTH)
    return jnp.concatenate([q, ks, kw, vs, vw, kc, vc, gates, u, gb, gc], axis=1).astype(BF16)


def _group_mean_matrix(width):
    r = jnp.arange(width) // HEAD_DIM
    return jnp.where(r[:, None] == r[None, :], 1.0 / HEAD_DIM, 0.0).astype(BF16)


def _rope_freq_row():
    lane = jnp.arange(LANES) % HEAD_DIM
    freqs = ROPE_THETA ** (-jnp.arange(0, ROPE_DIM, 2, dtype=F32) / ROPE_DIM)
    return jnp.where(lane < ROPE_DIM, freqs[lane % ROPE_HALF], 0.0).astype(F32).reshape(1, LANES)


def _per_group(a, bsz, rows):
    return a.reshape(bsz, rows, N_GROUPS, HEAD_DIM).transpose(0, 2, 1, 3)


def _hybrid_mixer(x, positions, g, w_in, q_norm, k_norm, cmp_pos, cmp_w1, cmp_w2, conv_w, w_out, *, bsz, seq):
    n = bsz * seq
    ncp = seq // CMP_STRIDE
    freq = _rope_freq_row()
    pq = _group_mean_matrix(A_WIDTH)
    pk = _group_mean_matrix(KV_WIDTH)
    qnw = jnp.tile(q_norm, N_HEADS).reshape(1, A_WIDTH)
    knw = jnp.tile(k_norm, (1, N_GROUPS))

    q, ks, kw, vs, vw, kc_raw, vc_raw, gates, ob = _inproj(
        x, positions.reshape(n, 1), g, _pack_w_in(w_in), qnw, knw, conv_w, pq, pk, freq, seq=seq)

    eye_g = jnp.eye(N_GROUPS, dtype=F32)
    seg = CMP_STRIDE * KV_WIDTH

    def seg_weights(w1):
        w1r = w1.reshape(CMP_BLOCK, HEAD_DIM, CMP_HIDDEN)
        halves = [jnp.einsum('jdh,gk->jgdkh', w1r[a:a + CMP_STRIDE], eye_g).reshape(seg, N_GROUPS * CMP_HIDDEN)
                  for a in (0, CMP_STRIDE)]
        return jnp.stack(halves).astype(BF16)

    def seg_pos(pe):
        return [jnp.broadcast_to(pe[a:a + CMP_STRIDE, None, :], (CMP_STRIDE, N_GROUPS, HEAD_DIM)).reshape(seg)
                for a in (0, CMP_STRIDE)]

    pe = jnp.stack(seg_pos(cmp_pos[0]) + seg_pos(cmp_pos[1]))
    w2 = jnp.stack([jnp.einsum('hd,gk->ghkd', cmp_w2[a], eye_g).reshape(N_GROUPS * CMP_HIDDEN, KV_WIDTH)
                    for a in range(2)]).astype(BF16)
    blk_end = jnp.minimum(jnp.arange(ncp) * CMP_STRIDE + CMP_BLOCK - 1, seq - 1)
    pos_c = positions[:, blk_end].reshape(bsz, ncp, 1)
    kc, vc = _compress(kc_raw.reshape(bsz, ncp, seg), vc_raw.reshape(bsz, ncp, seg), pos_c, pe,
                       seg_weights(cmp_w1[0]), seg_weights(cmp_w1[1]), w2, knw, pk, freq)

    q_t = jnp.swapaxes(q.reshape(bsz, seq, A_WIDTH), 1, 2)
    gates_t = gates.T
    pad64 = ((0, 0), (0, 0), (0, 0), (0, LANES - HEAD_DIM))
    kc_g = jnp.pad(_per_group(kc, bsz, ncp), pad64)
    vc_t = jnp.swapaxes(jnp.pad(_per_group(vc, bsz, ncp), pad64), 2, 3)
    cstart = jnp.arange(ncp) * CMP_STRIDE
    sstart = jnp.arange(LANES) * SEL_BLOCK
    ovt = ((cstart[None, :] < sstart[:, None] + SEL_BLOCK)
           & (cstart[None, :] + CMP_BLOCK > sstart[:, None])).astype(BF16)
    o_t, sel_bias = _cmp_select(q_t, kc_g, vc_t, ovt, gates_t, bsz=bsz, seq=seq)

    blk_onehot = (jnp.arange(seq)[:, None] // SEL_BLOCK == jnp.arange(LANES)[None, :]).astype(BF16)
    ones_col = jnp.zeros((LANES - HEAD_DIM,), BF16).at[0].set(1.0)

    def with_ones(v):
        vg = _per_group(v.reshape(bsz, seq, KV_WIDTH), bsz, seq)
        return jnp.concatenate([vg, jnp.broadcast_to(ones_col, vg.shape[:3] + (LANES - HEAD_DIM,))], axis=-1)

    ks_g = _per_group(ks.reshape(bsz, seq, KV_WIDTH), bsz, seq)
    k_aug = jnp.concatenate([jnp.broadcast_to(blk_onehot, (bsz, N_GROUPS, seq, LANES)), ks_g,
                             jnp.zeros((bsz, N_GROUPS, seq, LANES - HEAD_DIM), BF16)], axis=-1)
    front = ((0, 0), (0, 0), (WINDOW, 0), (0, 0))
    kw_g = jnp.pad(jnp.pad(_per_group(kw.reshape(bsz, seq, KV_WIDTH), bsz, seq), pad64), front)
    vw_t = jnp.swapaxes(jnp.pad(with_ones(vw), front), 2, 3)
    o_t = _win(q_t, kw_g, vw_t, gates_t, o_t, bsz=bsz, seq=seq)
    o_a = _slc(q_t, sel_bias, k_aug, jnp.swapaxes(with_ones(vs), 2, 3), gates_t, o_t, bsz=bsz, seq=seq)

    return _outproj(x, o_a, ob, w_out.astype(BF16))


def kernel(x, positions, ffn_norm, ffn_w_gate, ffn_w_up, ffn_w_down, mix_norm, hyb_w_in, hyb_q_norm, hyb_k_norm,
           hyb_cmp_pos, hyb_cmp_w1, hyb_cmp_w2, hyb_conv_w, hyb_w_out, pool_w, pool_scale):
    bsz, seq, d = x.shape
    depth = ffn_norm.shape[0]
    assert seq % 2048 == 0 and seq // SEL_BLOCK <= LANES and seq // SEL_BLOCK >= SEL_TOPK
    wg, wu, wd = (w.astype(BF16) for w in (ffn_w_gate, ffn_w_up, ffn_w_down))
    xf = x.reshape(bsz * seq, d)
    for layer in range(depth):
        xf = _ffn(xf, ffn_norm[layer, 0], wg[layer, 0], wu[layer, 0], wd[layer, 0])
        i = layer // 2
        if layer % 2 == 0:
            xf = _hybrid_mixer(xf, positions, mix_norm[layer], hyb_w_in[i], hyb_q_norm[i], hyb_k_norm[i],
                               hyb_cmp_pos[i], hyb_cmp_w1[i], hyb_cmp_w2[i], hyb_conv_w[i], hyb_w_out[i],
                               bsz=bsz, seq=seq)
        else:
            xf = _pool(xf, mix_norm[layer], pool_w[i].astype(BF16), pool_scale[i], seq=seq)
        xf = _ffn(xf, ffn_norm[layer, 1], wg[layer, 1], wu[layer, 1], wd[layer, 1])
    return xf.reshape(bsz, seq, d)
```

```python
import functools
import math

import jax
import jax.numpy as jnp
from jax import lax
from jax.experimental import pallas as pl
from jax.experimental.pallas import tpu as pltpu

F32 = jnp.float32
BF16 = jnp.bfloat16

HEAD_DIM = 64
N_GROUPS = 2
HEADS_PER_GROUP = 4
N_HEADS = N_GROUPS * HEADS_PER_GROUP
A_WIDTH = N_HEADS * HEAD_DIM
KV_WIDTH = N_GROUPS * HEAD_DIM
ROPE_DIM = HEAD_DIM // 4
ROPE_HALF = ROPE_DIM // 2
ROPE_THETA = 500000.0
CMP_BLOCK = 32
CMP_STRIDE = 16
CMP_HIDDEN = 2 * HEAD_DIM
SEL_BLOCK = 64
SEL_TOPK = 16
WINDOW = 512
N_BRANCH = 3
_BR_CMP, _BR_SLC, _BR_WIN = 0, 1, 2
CONV_WIDTH = 3
POOL_WINDOWS = (2, 4, 8, 16)
POOL_HALO = 16
EPS = 1e-6
NEG = -1e30

LANES = 128
VMEM_LIMIT_BYTES = 56 * 1024 * 1024


def _cparams(sem):
    return pltpu.CompilerParams(dimension_semantics=sem, vmem_limit_bytes=VMEM_LIMIT_BYTES)


def _dot(a, b):
    return jnp.dot(a, b, preferred_element_type=F32)


def _dot_nt(a, b):
    return lax.dot_general(a, b, (((1,), (1,)), ((), ())), preferred_element_type=F32)


def _dot_split(a, b):
    hi = a.astype(BF16)
    lo = (a - hi.astype(F32)).astype(BF16)
    return _dot(hi, b) + _dot(lo, b)


def _rms(x, g):
    ms = jnp.mean(x * x, axis=-1, keepdims=True)
    return x * lax.rsqrt(ms + EPS) * g


def _group_rms(x, g, pmat):
    ms = _dot_split(x * x, pmat)
    return x * lax.rsqrt(ms + EPS) * g


def _rope_tables(pos_col, freq_row):
    ang = pos_col * freq_row
    lane = lax.broadcasted_iota(jnp.int32, (1, LANES), 1) & (HEAD_DIM - 1)
    sign = jnp.where(lane < ROPE_HALF, -1.0, 1.0).astype(F32)
    return jnp.cos(ang), jnp.sin(ang) * sign


def _rope(x, cos_t, sin_t):
    w = x.shape[1]
    reps = w // LANES
    if reps > 1:
        cos_t = jnp.concatenate([cos_t] * reps, axis=1)
        sin_t = jnp.concatenate([sin_t] * reps, axis=1)
    lane = lax.broadcasted_iota(jnp.int32, (1, w), 1) & (HEAD_DIM - 1)
    partner = jnp.where(lane < ROPE_HALF,
                        pltpu.roll(x, w - ROPE_HALF, axis=1),
                        pltpu.roll(x, ROPE_HALF, axis=1))
    return x * cos_t + partner * sin_t


def _ffn_kernel(x_ref, xnext_ref, g_ref, wg_ref, wu_ref, wd_ref, o_ref, h_ref, acc_ref, *, nj):
    i, j = pl.program_id(0), pl.program_id(1)
    slot = i % 2

    @pl.when((i == 0) & (j == 0))
    def _():
        h_ref[0] = _rms(x_ref[...], g_ref[...]).astype(BF16)

    def step(first, last):
        h = h_ref[slot]
        a = _dot(h, wg_ref[...])
        b = _dot(h, wu_ref[...])
        act = a * (1.0 / (1.0 + jnp.exp(-a))) * b
        y = _dot(act.astype(BF16), wd_ref[...])
        if not first:
            y = acc_ref[...] + y
        if last:
            h_ref[1 - slot] = _rms(xnext_ref[...], g_ref[...]).astype(BF16)
            o_ref[...] = x_ref[...] + 0.5 * y
        else:
            acc_ref[...] = y

    for jj in range(nj):
        pl.when(j == jj)(functools.partial(step, jj == 0, jj == nj - 1))


def _ffn(x, g, wg, wu, wd, *, tm=512, tf=1408):
    n, d = x.shape
    dff = wg.shape[1]
    ni, nj = n // tm, dff // tf
    return pl.pallas_call(
        functools.partial(_ffn_kernel, nj=nj),
        out_shape=jax.ShapeDtypeStruct((n, d), F32),
        grid=(ni, nj),
        in_specs=[
            pl.BlockSpec((tm, d), lambda i, j: (i, 0)),
            pl.BlockSpec((tm, d), lambda i, j: (jnp.minimum(i + 1, ni - 1), 0)),
            pl.BlockSpec((1, d), lambda i, j: (0, 0)),
            pl.BlockSpec((d, tf), lambda i, j: (0, j)),
            pl.BlockSpec((d, tf), lambda i, j: (0, j)),
            pl.BlockSpec((tf, d), lambda i, j: (j, 0)),
        ],
        out_specs=pl.BlockSpec((tm, d), lambda i, j: (i, 0)),
        scratch_shapes=[pltpu.VMEM((2, tm, d), BF16), pltpu.VMEM((tm, d), F32)],
        compiler_params=_cparams(("arbitrary", "arbitrary")),
        name="ffn",
    )(x, x, g.reshape(1, d), wg, wu, wd)


GATE_WIDTH = N_GROUPS * LANES
_C_Q = 0
_C_K = _C_Q + A_WIDTH
_C_V = _C_K + 2 * KV_WIDTH
_C_C = _C_V + 2 * KV_WIDTH
_C_GATE = _C_C + 2 * KV_WIDTH
_C_U = _C_GATE + GATE_WIDTH
_C_GB = _C_U + A_WIDTH
_C_GC = _C_GB + A_WIDTH
_C_END = _C_GC + A_WIDTH
CONV_HALO = 8


def _inproj_kernel(x_ref, pos_ref, g_ref, w_ref, qnw_ref, knw_ref, cw_ref, pq_ref, pk_ref, freq_ref,
                   q_ref, ks_ref, kw_ref, vs_ref, vw_ref, kc_ref, vc_ref, gates_ref, ob_ref,
                   vext_ref, *, tm, tiles_per_batch):
    i = pl.program_id(0)
    @pl.when(i % tiles_per_batch == 0)
    def _():
        vext_ref[0:CONV_HALO, :] = jnp.zeros((CONV_HALO, A_WIDTH), F32)

    h = _rms(x_ref[...], g_ref[...]).astype(BF16)

    def proj(c0, width):
        return _dot(h, w_ref[:, c0:c0 + width])

    q_raw = proj(_C_Q, A_WIDTH)
    k2 = proj(_C_K, 2 * KV_WIDTH)
    cos_t, sin_t = _rope_tables(pos_ref[...].astype(F32), freq_ref[...])
    v = proj(_C_GC, A_WIDTH) * proj(_C_U, A_WIDTH)
    gate_b = proj(_C_GB, A_WIDTH)

    q = _group_rms(q_raw, qnw_ref[...], pq_ref[...])
    q_ref[...] = (_rope(q, cos_t, sin_t) * (HEAD_DIM ** -0.5)).astype(BF16)
    ks = _group_rms(k2[:, :KV_WIDTH], knw_ref[1:2, :], pk_ref[...])
    ks_ref[...] = _rope(ks, cos_t, sin_t).astype(BF16)
    kw = _group_rms(k2[:, KV_WIDTH:], knw_ref[2:3, :], pk_ref[...])
    kw_ref[...] = _rope(kw, cos_t, sin_t).astype(BF16)

    v2 = proj(_C_V, 2 * KV_WIDTH).astype(BF16)
    vs_ref[...] = v2[:, :KV_WIDTH]
    vw_ref[...] = v2[:, KV_WIDTH:]
    c2 = proj(_C_C, 2 * KV_WIDTH)
    kc_ref[...] = c2[:, :KV_WIDTH]
    vc_ref[...] = c2[:, KV_WIDTH:]
    gate_logits = proj(_C_GATE, GATE_WIDTH)

    vext_ref[CONV_HALO:CONV_HALO + tm, :] = v
    y = (cw_ref[2:3, :] * v
         + cw_ref[1:2, :] * vext_ref[pl.ds(CONV_HALO - 1, tm), :]
         + cw_ref[0:1, :] * vext_ref[pl.ds(CONV_HALO - 2, tm), :])
    ob_ref[...] = (gate_b * y).astype(BF16)
    vext_ref[0:CONV_HALO, :] = vext_ref[tm:tm + CONV_HALO, :]
    gates_ref[...] = 1.0 / (1.0 + jnp.exp(-gate_logits))


def _inproj(x, pos_col, g, w, qnw, knw, cw, pq, pk, freq, *, seq, tm=256):
    n, d = x.shape
    row = lambda i: (i, 0)
    fixed = lambda i: (0, 0)
    widths = [(A_WIDTH, BF16), (KV_WIDTH, BF16), (KV_WIDTH, BF16), (KV_WIDTH, BF16),
              (KV_WIDTH, BF16), (KV_WIDTH, F32), (KV_WIDTH, F32), (GATE_WIDTH, F32), (A_WIDTH, BF16)]
    return pl.pallas_call(
        functools.partial(_inproj_kernel, tm=tm, tiles_per_batch=seq // tm),
        out_shape=[jax.ShapeDtypeStruct((n, wd), dt) for wd, dt in widths],
        grid=(n // tm,),
        in_specs=[
            pl.BlockSpec((tm, d), row),
            pl.BlockSpec((tm, 1), row),
            pl.BlockSpec((1, d), fixed),
            pl.BlockSpec(w.shape, fixed),
            pl.BlockSpec(qnw.shape, fixed),
            pl.BlockSpec(knw.shape, fixed),
            pl.BlockSpec(cw.shape, fixed),
            pl.BlockSpec(pq.shape, fixed),
            pl.BlockSpec(pk.shape, fixed),
            pl.BlockSpec(freq.shape, fixed),
        ],
        out_specs=[pl.BlockSpec((tm, wd), row) for wd, _ in widths],
        scratch_shapes=[pltpu.VMEM((tm + CONV_HALO, A_WIDTH), F32)],
        compiler_params=_cparams(("arbitrary",)),
        name="inproj",
    )(x, pos_col, g.reshape(1, d), w, qnw, knw, cw, pq, pk, freq)


def _gelu_tanh(x):
    return 0.5 * x * (1.0 + jnp.tanh(math.sqrt(2.0 / math.pi) * (x + 0.044715 * (x * x * x))))


def _compress_kernel(xk_ref, xv_ref, pos_ref, pe_ref, wk_ref, wv_ref, w2_ref, knw_ref, pk_ref, freq_ref,
                     kc_ref, vc_ref, *, ncp):
    def branch(x, pe_a, pe_b, w_ref, w2):
        a = _dot((x + pe_a).astype(BF16), w_ref[0])
        b = _dot((x + pe_b).astype(BF16), w_ref[1])
        hid = a + pltpu.roll(b, ncp - 1, axis=0)
        return _dot(_gelu_tanh(hid).astype(BF16), w2)

    kc = branch(xk_ref[...], pe_ref[0:1, :], pe_ref[1:2, :], wk_ref, w2_ref[0])
    kc = _group_rms(kc, knw_ref[0:1, :], pk_ref[...])
    cos_t, sin_t = _rope_tables(pos_ref[...].astype(F32), freq_ref[...])
    kc_ref[...] = _rope(kc, cos_t, sin_t).astype(BF16)
    vc_ref[...] = branch(xv_ref[...], pe_ref[2:3, :], pe_ref[3:4, :], wv_ref, w2_ref[1]).astype(BF16)


def _compress(xk, xv, pos_c, pe, wk, wv, w2, knw, pk, freq):
    bsz, ncp, seg = xk.shape
    bat = lambda b: (b, 0, 0)
    fix2 = lambda b: (0, 0)
    fix3 = lambda b: (0, 0, 0)
    return pl.pallas_call(
        functools.partial(_compress_kernel, ncp=ncp),
        out_shape=[jax.ShapeDtypeStruct((bsz, ncp, KV_WIDTH), BF16)] * 2,
        grid=(bsz,),
        in_specs=[
            pl.BlockSpec((None, ncp, seg), bat),
            pl.BlockSpec((None, ncp, seg), bat),
            pl.BlockSpec((None, ncp, 1), bat),
            pl.BlockSpec(pe.shape, fix2),
            pl.BlockSpec(wk.shape, fix3),
            pl.BlockSpec(wv.shape, fix3),
            pl.BlockSpec(w2.shape, fix3),
            pl.BlockSpec(knw.shape, fix2),
            pl.BlockSpec(pk.shape, fix2),
            pl.BlockSpec(freq.shape, fix2),
        ],
        out_specs=[pl.BlockSpec((None, ncp, KV_WIDTH), bat)] * 2,
        compiler_params=_cparams(("parallel",)),
        name="compress",
    )(xk, xv, pos_c, pe, wk, wv, w2, knw, pk, freq)


def _head_columns(qt_ref):
    cols = []
    for pair in range(HEADS_PER_GROUP // 2):
        both = qt_ref[pair * LANES:(pair + 1) * LANES, :]
        cols += [both, jnp.concatenate([both[HEAD_DIM:, :], both[:HEAD_DIM, :]], axis=0)]
    return cols


def _software_pipeline(n, scores, probs, finish):
    s, p = {}, {}
    for step in range(n + 2):
        if step >= 2:
            finish(step - 2, p.pop(step - 2))
        if 1 <= step <= n:
            p[step - 1] = probs(step - 1, s.pop(step - 1))
        if step < n:
            s[step] = scores(step)


def _gate_row(gt_ref, branch, hd):
    r = branch * HEADS_PER_GROUP + hd
    return gt_ref[r:r + 1, :]


def _cmp_kernel(qt_ref, kc_ref, vct_ref, ovt_ref, gt_ref, ot_ref, sbt_ref, *, tq, ncp):
    i = pl.program_id(2)
    t = i * tq + lax.broadcasted_iota(jnp.int32, (1, tq), 1)
    blk_end = lax.broadcasted_iota(jnp.int32, (ncp, 1), 0) * CMP_STRIDE + (CMP_BLOCK - 1)
    valid = blk_end <= t
    any_valid = (t >= CMP_BLOCK - 1).astype(F32)
    kc = kc_ref[...]
    vct = vct_ref[...]
    heads = _head_columns(qt_ref)
    psum_parts = []

    def scores(hd):
        return jnp.where(valid, _dot(kc, heads[hd]), NEG)

    def probs(hd, s):
        e = jnp.exp(s - jnp.max(s, axis=0, keepdims=True))
        return e * (any_valid / jnp.sum(e, axis=0, keepdims=True))

    def finish(hd, p):
        o_t = _dot(vct, p.astype(BF16))[:HEAD_DIM, :]
        ot_ref[hd * HEAD_DIM:(hd + 1) * HEAD_DIM, :] = o_t * _gate_row(gt_ref, _BR_CMP, hd)
        psum_parts.append(p)

    _software_pipeline(HEADS_PER_GROUP, scores, probs, finish)
    psum = (psum_parts[0] + psum_parts[1]) + (psum_parts[2] + psum_parts[3])

    hi = psum.astype(BF16)
    lo = (psum - hi.astype(F32)).astype(BF16)
    imp = _dot(ovt_ref[...], hi) + _dot(ovt_ref[...], lo)
    blk = lax.broadcasted_iota(jnp.int32, (LANES, 1), 0)
    sel_start = blk * SEL_BLOCK
    cur = (t >> int(math.log2(SEL_BLOCK))) << int(math.log2(SEL_BLOCK))
    imp = jnp.where(sel_start <= t, imp, -1.0)
    imp = jnp.where(sel_start == cur, 1e4, jnp.where(sel_start == 0, 1e4, imp))

    blk_f = blk.astype(F32)
    bias = jnp.full((LANES, tq), NEG, F32)
    for _ in range(SEL_TOPK):
        mx = jnp.max(imp, axis=0, keepdims=True)
        first = jnp.min(jnp.where(imp == mx, blk_f, float(LANES)), axis=0, keepdims=True)
        pick = blk_f == first
        bias = jnp.where(pick, 0.0, bias)
        imp = jnp.where(pick, -3e38, imp)
    sbt_ref[...] = bias.astype(BF16)


def _tmaps(nq):
    return (lambda b, g, i: (b, g, i)), (lambda b, g, i: (g, b * nq + i)), (lambda b, g, i: (b, g, 0, 0))


def _cmp_select(qt, kc, vct, ovt, gates_t, *, bsz, seq, tq=256):
    ncp = kc.shape[2]
    nq = seq // tq
    gw = HEADS_PER_GROUP * HEAD_DIM
    tmap, gmap, kvmap = _tmaps(nq)
    return pl.pallas_call(
        functools.partial(_cmp_kernel, tq=tq, ncp=ncp),
        out_shape=[jax.ShapeDtypeStruct((bsz, A_WIDTH, seq), F32),
                   jax.ShapeDtypeStruct((bsz, N_GROUPS, LANES, seq), BF16)],
        grid=(bsz, N_GROUPS, nq),
        in_specs=[
            pl.BlockSpec((None, gw, tq), tmap),
            pl.BlockSpec((None, None, ncp, LANES), kvmap),
            pl.BlockSpec((None, None, LANES, ncp), kvmap),
            pl.BlockSpec(ovt.shape, lambda b, g, i: (0, 0)),
            pl.BlockSpec((LANES, tq), gmap),
        ],
        out_specs=[pl.BlockSpec((None, gw, tq), tmap),
                   pl.BlockSpec((None, None, LANES, tq), lambda b, g, i: (b, g, 0, i))],
        compiler_params=_cparams(("parallel", "parallel", "parallel")),
        name="cmp_select",
    )(qt, kc, vct, ovt, gates_t)


def _slc_kernel(qt_ref, sbt_ref, k_ref, vt_ref, gt_ref, prev_ref, o_ref,
                qa_ref, s_ref, p_ref, acc_ref, *, tq):
    i = pl.program_id(2)
    rows = HEADS_PER_GROUP * tq
    qa_ref[...] = jnp.concatenate([jnp.concatenate([sbt_ref[...]] * HEADS_PER_GROUP, axis=1),
                                   jnp.concatenate(_head_columns(qt_ref), axis=1)], axis=0)
    s_ref[...] = jnp.full(s_ref.shape, -jnp.inf, F32)
    p_ref[...] = jnp.zeros(p_ref.shape, BF16)
    acc_ref[...] = jnp.zeros(acc_ref.shape, F32)

    def accumulate(j, alpha):
        j = jnp.maximum(j, 0)
        pv = _dot(vt_ref[:, pl.ds(pl.multiple_of(j * tq, tq), tq)], p_ref[...])
        acc_ref[...] = alpha * acc_ref[...] + pv

    def probs(m):
        s = s_ref[...]
        m_new = jnp.maximum(m, jnp.max(s, axis=0, keepdims=True))
        p_ref[...] = jnp.exp(s - m_new).astype(BF16)
        return m_new, jnp.exp(m - m_new)

    def scores(j, masked):
        s = _dot(k_ref[pl.ds(pl.multiple_of(j * tq, tq), tq), :], qa_ref[...])
        if masked:
            causal = (lax.broadcasted_iota(jnp.int32, (tq, 1), 0)
                      <= (lax.broadcasted_iota(jnp.int32, (1, rows), 1) & (tq - 1)))
            s = jnp.where(causal, s, NEG)
        s_ref[...] = s

    def body(it, carry):
        m, alpha = carry
        accumulate(it - 2, alpha)
        m, alpha = probs(m)
        scores(it, False)
        return m, alpha

    carry = (jnp.full((1, rows), -3e38, F32), jnp.ones((1, rows), F32))
    carry = lax.fori_loop(0, i >> 1, lambda k, c: body(2 * k + 1, body(2 * k, c)), carry)
    m, alpha = lax.cond((i & 1) == 1, lambda c: body(i - 1, c), lambda c: c, carry)
    accumulate(i - 2, alpha)
    m, alpha = probs(m)
    scores(i, True)
    accumulate(i - 1, alpha)
    m, alpha = probs(m)
    accumulate(i, alpha)

    gate = jnp.concatenate([_gate_row(gt_ref, _BR_SLC, hd) for hd in range(HEADS_PER_GROUP)], axis=1)
    out_t = acc_ref[0:HEAD_DIM, :] * (gate / acc_ref[HEAD_DIM:HEAD_DIM + 1, :])
    for pair in range(HEADS_PER_GROUP // 2):
        for c in range(tq // LANES):
            col = 2 * pair * tq + c * LANES
            both = jnp.concatenate([out_t[:, col:col + LANES], out_t[:, col + tq:col + tq + LANES]], axis=0)
            both = both + prev_ref[pair * LANES:(pair + 1) * LANES, c * LANES:(c + 1) * LANES]
            o_ref[c * LANES:(c + 1) * LANES, pair * LANES:(pair + 1) * LANES] = both.T.astype(BF16)


def _slc(qt, sbt, k_aug, vt_aug, gates_t, prev, *, bsz, seq, tq=256):
    nq = seq // tq
    gw = HEADS_PER_GROUP * HEAD_DIM
    rows = HEADS_PER_GROUP * tq
    tmap, gmap, kvmap = _tmaps(nq)
    return pl.pallas_call(
        functools.partial(_slc_kernel, tq=tq),
        out_shape=jax.ShapeDtypeStruct((bsz * seq, A_WIDTH), BF16),
        grid=(bsz, N_GROUPS, nq),
        in_specs=[
            pl.BlockSpec((None, gw, tq), tmap),
            pl.BlockSpec((None, None, LANES, tq), lambda b, g, i: (b, g, 0, i)),
            pl.BlockSpec((None, None, seq, 2 * LANES), kvmap),
            pl.BlockSpec((None, None, LANES, seq), kvmap),
            pl.BlockSpec((LANES, tq), gmap),
            pl.BlockSpec((None, gw, tq), tmap),
        ],
        out_specs=pl.BlockSpec((tq, gw), lambda b, g, i: (b * nq + i, g)),
        scratch_shapes=[pltpu.VMEM((2 * LANES, rows), BF16), pltpu.VMEM((tq, rows), F32),
                        pltpu.VMEM((tq, rows), BF16), pltpu.VMEM((LANES, rows), F32)],
        compiler_params=_cparams(("parallel", "parallel", "parallel")),
        name="slc_attn",
    )(qt, sbt, k_aug, vt_aug, gates_t, prev)


def _win_kernel(qt_ref, k_ref, vt_ref, band_ref, gt_ref, prev_ref, ot_ref, *, tq):
    i = pl.program_id(2)
    span = WINDOW + tq
    start = pl.multiple_of(i * tq, tq)
    k = k_ref[pl.ds(start, span), :]
    vt = vt_ref[:, pl.ds(start, span)]
    heads = _head_columns(qt_ref)

    def scores(hd):
        return _dot(k, heads[hd]) + band_ref[...]

    def probs(hd, s):
        return jnp.exp(s - jnp.max(s, axis=0, keepdims=True)).astype(BF16)

    def finish(hd, p):
        acc = _dot(vt, p)
        rows = slice(hd * HEAD_DIM, (hd + 1) * HEAD_DIM)
        scale = _gate_row(gt_ref, _BR_WIN, hd) / acc[HEAD_DIM:HEAD_DIM + 1, :]
        ot_ref[rows, :] = prev_ref[rows, :] + acc[:HEAD_DIM, :] * scale

    _software_pipeline(HEADS_PER_GROUP, scores, probs, finish)


def _window_band(tq):
    v = jnp.arange(WINDOW // tq + 1)[:, None, None]
    kpos = v * tq - WINDOW + jnp.arange(WINDOW + tq)[None, :, None]
    t = v * tq + jnp.arange(tq)[None, None, :]
    ok = (t - kpos >= 0) & (t - kpos < WINDOW) & (kpos >= 0)
    return jnp.where(ok, 0.0, NEG).astype(F32)


def _win(qt, k_pad, vt_pad, gates_t, prev, *, bsz, seq, tq=256):
    nq = seq // tq
    gw = HEADS_PER_GROUP * HEAD_DIM
    band = _window_band(tq)
    last = band.shape[0] - 1
    tmap, gmap, kvmap = _tmaps(nq)
    return pl.pallas_call(
        functools.partial(_win_kernel, tq=tq),
        out_shape=jax.ShapeDtypeStruct((bsz, A_WIDTH, seq), F32),
        grid=(bsz, N_GROUPS, nq),
        in_specs=[
            pl.BlockSpec((None, gw, tq), tmap),
            pl.BlockSpec((None, None, seq + WINDOW, LANES), kvmap),
            pl.BlockSpec((None, None, LANES, seq + WINDOW), kvmap),
            pl.BlockSpec((None,) + band.shape[1:], lambda b, g, i: (jnp.minimum(i, last), 0, 0)),
            pl.BlockSpec((LANES, tq), gmap),
            pl.BlockSpec((None, gw, tq), tmap),
        ],
        out_specs=pl.BlockSpec((None, gw, tq), tmap),
        compiler_params=_cparams(("parallel", "parallel", "parallel")),
        name="win_attn",
    )(qt, k_pad, vt_pad, band, gates_t, prev)


def _outproj_kernel(x_ref, oa_ref, ob_ref, w_ref, o_ref):
    y = _dot(oa_ref[...], w_ref[0:A_WIDTH, :]) + _dot(ob_ref[...], w_ref[A_WIDTH:, :])
    o_ref[...] = x_ref[...] + y


def _outproj(x, oa, ob, w, *, tm=512):
    n, d = x.shape
    row = lambda i: (i, 0)
    return pl.pallas_call(
        _outproj_kernel,
        out_shape=jax.ShapeDtypeStruct((n, d), F32),
        grid=(n // tm,),
        in_specs=[
            pl.BlockSpec((tm, d), row),
            pl.BlockSpec((tm, A_WIDTH), row),
            pl.BlockSpec((tm, A_WIDTH), row),
            pl.BlockSpec(w.shape, lambda i: (0, 0)),
        ],
        out_specs=pl.BlockSpec((tm, d), row),
        compiler_params=_cparams(("parallel",)),
        name="outproj",
    )(x, oa, ob, w)


def _pool_kernel(x_ref, halo_ref, g_ref, w_ref, sc_ref, o_ref, ext_ref, *, tm, tiles_per_batch):
    i = pl.program_id(0)
    first = i % tiles_per_batch == 0
    x = x_ref[...]
    h = _rms(x, g_ref[...])
    hh = _rms(halo_ref[...], g_ref[...])
    ext_ref[0:POOL_HALO, :] = jnp.where(first, 0.0, hh)
    ext_ref[POOL_HALO:POOL_HALO + tm, :] = h
    t = (i % tiles_per_batch) * tm + lax.broadcasted_iota(jnp.int32, (tm, 1), 0)
    gw = h.shape[1] // len(POOL_WINDOWS)
    for gi, win in enumerate(POOL_WINDOWS):
        c0 = gi * gw
        hg = h[:, c0:c0 + gw]
        tot = hg
        for sft in range(1, win):
            tot = tot + ext_ref[pl.ds(POOL_HALO - sft, tm), c0:c0 + gw]
        cnt = jnp.minimum(t + 1, win).astype(F32)
        y = _dot((tot / cnt - hg).astype(BF16), w_ref[gi])
        o_ref[:, c0:c0 + gw] = x[:, c0:c0 + gw] + y * sc_ref[:, c0:c0 + gw]


def _pool(x, g, w, scale, *, seq, tm=512):
    n, d = x.shape
    hb = tm // POOL_HALO
    return pl.pallas_call(
        functools.partial(_pool_kernel, tm=tm, tiles_per_batch=seq // tm),
        out_shape=jax.ShapeDtypeStruct((n, d), F32),
        grid=(n // tm,),
        in_specs=[
            pl.BlockSpec((tm, d), lambda i: (i, 0)),
            pl.BlockSpec((POOL_HALO, d), lambda i: (jnp.maximum(i * hb - 1, 0), 0)),
            pl.BlockSpec((1, d), lambda i: (0, 0)),
            pl.BlockSpec(w.shape, lambda i: (0, 0, 0)),
            pl.BlockSpec((1, d), lambda i: (0, 0)),
        ],
        out_specs=pl.BlockSpec((tm, d), lambda i: (i, 0)),
        scratch_shapes=[pltpu.VMEM((tm + POOL_HALO, d), F32)],
        compiler_params=_cparams(("parallel",)),
        name="pool_mixer",
    )(x, x, g.reshape(1, d), w, scale.reshape(1, d))


def _pack_w_in(w_in):
    o = 0
    q = w_in[:, o:o + A_WIDTH]; o += A_WIDTH
    kc, vc, ks, vs, kw, vw = [w_in[:, o + k * KV_WIDTH:o + (k + 1) * KV_WIDTH] for k in range(6)]
    o += 6 * KV_WIDTH
    gates = w_in[:, o:o + N_BRANCH * N_HEADS]; o += N_BRANCH * N_HEADS
    u, gb, gc = [w_in[:, o + k * A_WIDTH:o + (k + 1) * A_WIDTH] for k in range(3)]
    gates = gates.reshape(-1, N_GROUPS, HEADS_PER_GROUP, N_BRANCH).transpose(0, 1, 3, 2)
    gates = gates.reshape(-1, N_GROUPS, N_BRANCH * HEADS_PER_GROUP)
    gates = jnp.pad(gates, ((0, 0), (0, 0), (0, LANES - N_BRANCH * HEADS_PER_GROUP))).reshape(-1, GATE_WIDTH)
    return jnp.concatenate([q, ks, kw, vs, vw, kc, vc, gates, u, gb, gc], axis=1).astype(BF16)


def _group_mean_matrix(width):
    r = jnp.arange(width) // HEAD_DIM
    return jnp.where(r[:, None] == r[None, :], 1.0 / HEAD_DIM, 0.0).astype(BF16)


def _rope_freq_row():
    lane = jnp.arange(LANES) % HEAD_DIM
    freqs = ROPE_THETA ** (-jnp.arange(0, ROPE_DIM, 2, dtype=F32) / ROPE_DIM)
    return jnp.where(lane < ROPE_DIM, freqs[lane % ROPE_HALF], 0.0).astype(F32).reshape(1, LANES)


def _per_group(a, bsz, rows):
    return a.reshape(bsz, rows, N_GROUPS, HEAD_DIM).transpose(0, 2, 1, 3)


def _hybrid_mixer(x, positions, g, w_in, q_norm, k_norm, cmp_pos, cmp_w1, cmp_w2, conv_w, w_out, *, bsz, seq):
    n = bsz * seq
    ncp = seq // CMP_STRIDE
    freq = _rope_freq_row()
    pq = _group_mean_matrix(A_WIDTH)
    pk = _group_mean_matrix(KV_WIDTH)
    qnw = jnp.tile(q_norm, N_HEADS).reshape(1, A_WIDTH)
    knw = jnp.tile(k_norm, (1, N_GROUPS))

    q, ks, kw, vs, vw, kc_raw, vc_raw, gates, ob = _inproj(
        x, positions.reshape(n, 1), g, _pack_w_in(w_in), qnw, knw, conv_w, pq, pk, freq, seq=seq)

    eye_g = jnp.eye(N_GROUPS, dtype=F32)
    seg = CMP_STRIDE * KV_WIDTH

    def seg_weights(w1):
        w1r = w1.reshape(CMP_BLOCK, HEAD_DIM, CMP_HIDDEN)
        halves = [jnp.einsum('jdh,gk->jgdkh', w1r[a:a + CMP_STRIDE], eye_g).reshape(seg, N_GROUPS * CMP_HIDDEN)
                  for a in (0, CMP_STRIDE)]
        return jnp.stack(halves).astype(BF16)

    def seg_pos(pe):
        return [jnp.broadcast_to(pe[a:a + CMP_STRIDE, None, :], (CMP_STRIDE, N_GROUPS, HEAD_DIM)).reshape(seg)
                for a in (0, CMP_STRIDE)]

    pe = jnp.stack(seg_pos(cmp_pos[0]) + seg_pos(cmp_pos[1]))
    w2 = jnp.stack([jnp.einsum('hd,gk->ghkd', cmp_w2[a], eye_g).reshape(N_GROUPS * CMP_HIDDEN, KV_WIDTH)
                    for a in range(2)]).astype(BF16)
    blk_end = jnp.minimum(jnp.arange(ncp) * CMP_STRIDE + CMP_BLOCK - 1, seq - 1)
    pos_c = positions[:, blk_end].reshape(bsz, ncp, 1)
    kc, vc = _compress(kc_raw.reshape(bsz, ncp, seg), vc_raw.reshape(bsz, ncp, seg), pos_c, pe,
                       seg_weights(cmp_w1[0]), seg_weights(cmp_w1[1]), w2, knw, pk, freq)

    q_t = jnp.swapaxes(q.reshape(bsz, seq, A_WIDTH), 1, 2)
    gates_t = gates.T
    pad64 = ((0, 0), (0, 0), (0, 0), (0, LANES - HEAD_DIM))
    kc_g = jnp.pad(_per_group(kc, bsz, ncp), pad64)
    vc_t = jnp.swapaxes(jnp.pad(_per_group(vc, bsz, ncp), pad64), 2, 3)
    cstart = jnp.arange(ncp) * CMP_STRIDE
    sstart = jnp.arange(LANES) * SEL_BLOCK
    ovt = ((cstart[None, :] < sstart[:, None] + SEL_BLOCK)
           & (cstart[None, :] + CMP_BLOCK > sstart[:, None])).astype(BF16)
    o_t, sel_bias = _cmp_select(q_t, kc_g, vc_t, ovt, gates_t, bsz=bsz, seq=seq)

    blk_onehot = (jnp.arange(seq)[:, None] // SEL_BLOCK == jnp.arange(LANES)[None, :]).astype(BF16)
    ones_col = jnp.zeros((LANES - HEAD_DIM,), BF16).at[0].set(1.0)

    def with_ones(v):
        vg = _per_group(v.reshape(bsz, seq, KV_WIDTH), bsz, seq)
        return jnp.concatenate([vg, jnp.broadcast_to(ones_col, vg.shape[:3] + (LANES - HEAD_DIM,))], axis=-1)

    ks_g = _per_group(ks.reshape(bsz, seq, KV_WIDTH), bsz, seq)
    k_aug = jnp.concatenate([jnp.broadcast_to(blk_onehot, (bsz, N_GROUPS, seq, LANES)), ks_g,
                             jnp.zeros((bsz, N_GROUPS, seq, LANES - HEAD_DIM), BF16)], axis=-1)
    front = ((0, 0), (0, 0), (WINDOW, 0), (0, 0))
    kw_g = jnp.pad(jnp.pad(_per_group(kw.reshape(bsz, seq, KV_WIDTH), bsz, seq), pad64), front)
    vw_t = jnp.swapaxes(jnp.pad(with_ones(vw), front), 2, 3)
    o_t = _win(q_t, kw_g, vw_t, gates_t, o_t, bsz=bsz, seq=seq)
    o_a = _slc(q_t, sel_bias, k_aug, jnp.swapaxes(with_ones(vs), 2, 3), gates_t, o_t, bsz=bsz, seq=seq)

    return _outproj(x, o_a, ob, w_out.astype(BF16))


def kernel(x, positions, ffn_norm, ffn_w_gate, ffn_w_up, ffn_w_down, mix_norm, hyb_w_in, hyb_q_norm, hyb_k_norm,
           hyb_cmp_pos, hyb_cmp_w1, hyb_cmp_w2, hyb_conv_w, hyb_w_out, pool_w, pool_scale):
    bsz, seq, d = x.shape
    depth = ffn_norm.shape[0]
    assert seq % 2048 == 0 and seq // SEL_BLOCK <= LANES and seq // SEL_BLOCK >= SEL_TOPK
    wg, wu, wd = (w.astype(BF16) for w in (ffn_w_gate, ffn_w_up, ffn_w_down))
    xf = x.reshape(bsz * seq, d)
    for layer in range(depth):
        xf = _ffn(xf, ffn_norm[layer, 0], wg[layer, 0], wu[layer, 0], wd[layer, 0])
        i = layer // 2
        if layer % 2 == 0:
            xf = _hybrid_mixer(xf, positions, mix_norm[layer], hyb_w_in[i], hyb_q_norm[i], hyb_k_norm[i],
                               hyb_cmp_pos[i], hyb_cmp_w1[i], hyb_cmp_w2[i], hyb_conv_w[i], hyb_w_out[i],
                               bsz=bsz, seq=seq)
        else:
            xf = _pool(xf, mix_norm[layer], pool_w[i].astype(BF16), pool_scale[i], seq=seq)
        xf = _ffn(xf, ffn_norm[layer, 1], wg[layer, 1], wu[layer, 1], wd[layer, 1])
    return xf.reshape(bsz, seq, d)
```

```python
import functools
import math

import jax
import jax.numpy as jnp
from jax import lax
from jax.experimental import pallas as pl
from jax.experimental.pallas import tpu as pltpu

F32 = jnp.float32
BF16 = jnp.bfloat16

HEAD_DIM = 64
N_GROUPS = 2
HEADS_PER_GROUP = 4
N_HEADS = N_GROUPS * HEADS_PER_GROUP
A_WIDTH = N_HEADS * HEAD_DIM
KV_WIDTH = N_GROUPS * HEAD_DIM
ROPE_DIM = HEAD_DIM // 4
ROPE_HALF = ROPE_DIM // 2
ROPE_THETA = 500000.0
CMP_BLOCK = 32
CMP_STRIDE = 16
CMP_HIDDEN = 2 * HEAD_DIM
SEL_BLOCK = 64
SEL_TOPK = 16
WINDOW = 512
N_BRANCH = 3
_BR_CMP, _BR_SLC, _BR_WIN = 0, 1, 2
CONV_WIDTH = 3
POOL_WINDOWS = (2, 4, 8, 16)
POOL_HALO = 16
EPS = 1e-6
NEG = -1e30

LANES = 128
VMEM_LIMIT_BYTES = 56 * 1024 * 1024


def _cparams(sem):
    return pltpu.CompilerParams(dimension_semantics=sem, vmem_limit_bytes=VMEM_LIMIT_BYTES)


def _dot(a, b):
    return jnp.dot(a, b, preferred_element_type=F32)


def _dot_nt(a, b):
    return lax.dot_general(a, b, (((1,), (1,)), ((), ())), preferred_element_type=F32)


def _dot_split(a, b):
    hi = a.astype(BF16)
    lo = (a - hi.astype(F32)).astype(BF16)
    return _dot(hi, b) + _dot(lo, b)


def _rms(x, g):
    ms = jnp.mean(x * x, axis=-1, keepdims=True)
    return x * lax.rsqrt(ms + EPS) * g


def _group_rms(x, g, pmat):
    ms = _dot_split(x * x, pmat)
    return x * lax.rsqrt(ms + EPS) * g


def _rope_tables(pos_col, freq_row):
    ang = pos_col * freq_row
    lane = lax.broadcasted_iota(jnp.int32, (1, LANES), 1) & (HEAD_DIM - 1)
    sign = jnp.where(lane < ROPE_HALF, -1.0, 1.0).astype(F32)
    return jnp.cos(ang), jnp.sin(ang) * sign


def _rope(x, cos_t, sin_t):
    w = x.shape[1]
    reps = w // LANES
    if reps > 1:
        cos_t = jnp.concatenate([cos_t] * reps, axis=1)
        sin_t = jnp.concatenate([sin_t] * reps, axis=1)
    lane = lax.broadcasted_iota(jnp.int32, (1, w), 1) & (HEAD_DIM - 1)
    partner = jnp.where(lane < ROPE_HALF,
                        pltpu.roll(x, w - ROPE_HALF, axis=1),
                        pltpu.roll(x, ROPE_HALF, axis=1))
    return x * cos_t + partner * sin_t


def _ffn_kernel(x_ref, xnext_ref, g_ref, wg_ref, wu_ref, wd_ref, o_ref, h_ref, acc_ref, *, nj):
    i, j = pl.program_id(0), pl.program_id(1)
    slot = i % 2

    @pl.when((i == 0) & (j == 0))
    def _():
        h_ref[0] = _rms(x_ref[...], g_ref[...]).astype(BF16)

    def step(first, last):
        h = h_ref[slot]
        a = _dot(h, wg_ref[...])
        b = _dot(h, wu_ref[...])
        act = a * (1.0 / (1.0 + jnp.exp(-a))) * b
        y = _dot(act.astype(BF16), wd_ref[...])
        if not first:
            y = acc_ref[...] + y
        if last:
            h_ref[1 - slot] = _rms(xnext_ref[...], g_ref[...]).astype(BF16)
            o_ref[...] = x_ref[...] + 0.5 * y
        else:
            acc_ref[...] = y

    for jj in range(nj):
        pl.when(j == jj)(functools.partial(step, jj == 0, jj == nj - 1))


def _ffn(x, g, wg, wu, wd, layer, half, *, tm=512, tf=1408):
    n, d = x.shape
    dff = wg.shape[-1]
    ni, nj = n // tm, dff // tf
    return pl.pallas_call(
        functools.partial(_ffn_kernel, nj=nj),
        out_shape=jax.ShapeDtypeStruct((n, d), F32),
        grid=(ni, nj),
        in_specs=[
            pl.BlockSpec((tm, d), lambda i, j: (i, 0)),
            pl.BlockSpec((tm, d), lambda i, j: (jnp.minimum(i + 1, ni - 1), 0)),
            pl.BlockSpec((1, d), lambda i, j: (0, 0)),
            pl.BlockSpec((None, None, d, tf), lambda i, j: (layer, half, 0, j)),
            pl.BlockSpec((None, None, d, tf), lambda i, j: (layer, half, 0, j)),
            pl.BlockSpec((None, None, tf, d), lambda i, j: (layer, half, j, 0)),
        ],
        out_specs=pl.BlockSpec((tm, d), lambda i, j: (i, 0)),
        scratch_shapes=[pltpu.VMEM((2, tm, d), BF16), pltpu.VMEM((tm, d), F32)],
        compiler_params=_cparams(("arbitrary", "arbitrary")),
        name="ffn",
    )(x, x, g.reshape(1, d), wg, wu, wd)


GATE_WIDTH = N_GROUPS * LANES
GATE_ROWS = 16
_C_Q = 0
_C_K = _C_Q + A_WIDTH
_C_V = _C_K + 2 * KV_WIDTH
_C_C = _C_V + 2 * KV_WIDTH
_C_GATE = _C_C + 2 * KV_WIDTH
_C_U = _C_GATE + GATE_WIDTH
_C_GB = _C_U + A_WIDTH
_C_GC = _C_GB + A_WIDTH
_C_END = _C_GC + A_WIDTH
CONV_HALO = 8


def _inproj_kernel(x_ref, pos_ref, g_ref, w_ref, qnw_ref, knw_ref, cw_ref, pq_ref, pk_ref, freq_ref,
                   qt_ref, kslc_ref, kwin_ref, vt_ref, kc_ref, vc_ref, gates_ref, ob_ref,
                   vext_ref, *, tm, tiles_per_batch):
    i = pl.program_id(0)
    chunks = range(tm // LANES)

    def transposed(a, c):
        return jnp.concatenate([a[r * LANES:(r + 1) * LANES, c * LANES:(c + 1) * LANES].T for r in chunks], axis=1)

    def per_group(k):
        low = lax.broadcasted_iota(jnp.int32, (1, LANES), 1) < HEAD_DIM
        return [jnp.where(low, k, 0.0), jnp.where(low, pltpu.roll(k, HEAD_DIM, axis=1), 0.0)]

    @pl.when(i % tiles_per_batch == 0)
    def _():
        vext_ref[0:CONV_HALO, :] = jnp.zeros((CONV_HALO, A_WIDTH), F32)

    h = _rms(x_ref[...], g_ref[...]).astype(BF16)

    def proj(c0, width):
        return _dot(h, w_ref[:, c0:c0 + width])

    q_raw = proj(_C_Q, A_WIDTH)
    k2 = proj(_C_K, 2 * KV_WIDTH)
    cos_t, sin_t = _rope_tables(pos_ref[...].astype(F32), freq_ref[...])
    v = proj(_C_GC, A_WIDTH) * proj(_C_U, A_WIDTH)
    gate_b = proj(_C_GB, A_WIDTH)

    q = _group_rms(q_raw, qnw_ref[...], pq_ref[...])
    q = _rope(q, cos_t, sin_t) * (HEAD_DIM ** -0.5)
    for c in range(A_WIDTH // LANES):
        qt_ref[c * LANES:(c + 1) * LANES, :] = transposed(q, c).astype(BF16)

    tok = (i % tiles_per_batch) * tm + lax.broadcasted_iota(jnp.int32, (tm, 1), 0)
    blk_onehot = (lax.broadcasted_iota(jnp.int32, (1, LANES), 1) == (tok >> int(math.log2(SEL_BLOCK))))
    blk_onehot = jnp.where(blk_onehot, 1.0, 0.0).astype(BF16)
    ks = _rope(_group_rms(k2[:, :KV_WIDTH], knw_ref[1:2, :], pk_ref[...]), cos_t, sin_t)
    for g, kg in enumerate(per_group(ks)):
        kslc_ref[g] = jnp.concatenate([blk_onehot, kg.astype(BF16)], axis=1)
    kw = _rope(_group_rms(k2[:, KV_WIDTH:], knw_ref[2:3, :], pk_ref[...]), cos_t, sin_t)
    for g, kg in enumerate(per_group(kw)):
        kwin_ref[g] = kg.astype(BF16)

    v2 = proj(_C_V, 2 * KV_WIDTH)
    ones_rows = jnp.where(lax.broadcasted_iota(jnp.int32, (LANES - HEAD_DIM, 1), 0) == 0, 1.0, 0.0)
    ones_rows = jnp.broadcast_to(ones_rows, (LANES - HEAD_DIM, tm)).astype(BF16)
    for branch in range(2):
        vt = transposed(v2, branch).astype(BF16)
        for g in range(N_GROUPS):
            vt_ref[branch, g, 0:HEAD_DIM, :] = vt[g * HEAD_DIM:(g + 1) * HEAD_DIM, :]
            vt_ref[branch, g, HEAD_DIM:, :] = ones_rows
    c2 = proj(_C_C, 2 * KV_WIDTH)
    kc_ref[...] = c2[:, :KV_WIDTH]
    vc_ref[...] = c2[:, KV_WIDTH:]
    gate_logits = proj(_C_GATE, GATE_WIDTH)

    vext_ref[CONV_HALO:CONV_HALO + tm, :] = v
    y = (cw_ref[2:3, :] * v
         + cw_ref[1:2, :] * vext_ref[pl.ds(CONV_HALO - 1, tm), :]
         + cw_ref[0:1, :] * vext_ref[pl.ds(CONV_HALO - 2, tm), :])
    ob_ref[...] = (gate_b * y).astype(BF16)
    vext_ref[0:CONV_HALO, :] = vext_ref[tm:tm + CONV_HALO, :]
    gates_ref[...] = 1.0 / (1.0 + jnp.exp(-gate_logits))


def _inproj(x, pos_col, g, w, qnw, knw, cw, pq, pk, freq, *, seq, tm=256):
    n, d = x.shape
    bsz, tpb = n // seq, seq // tm
    row = lambda i: (i, 0)
    fixed = lambda i: (0, 0)
    widths = [(KV_WIDTH, F32), (KV_WIDTH, F32), (GATE_WIDTH, F32), (A_WIDTH, BF16)]
    out_shape = [jax.ShapeDtypeStruct((bsz, A_WIDTH, seq), BF16),
                 jax.ShapeDtypeStruct((bsz, N_GROUPS, seq, 2 * LANES), BF16),
                 jax.ShapeDtypeStruct((bsz, N_GROUPS, seq, LANES), BF16),
                 jax.ShapeDtypeStruct((bsz, 2, N_GROUPS, LANES, seq), BF16)]
    out_specs = [pl.BlockSpec((None, A_WIDTH, tm), lambda i: (i // tpb, 0, i % tpb)),
                 pl.BlockSpec((None, N_GROUPS, tm, 2 * LANES), lambda i: (i // tpb, 0, i % tpb, 0)),
                 pl.BlockSpec((None, N_GROUPS, tm, LANES), lambda i: (i // tpb, 0, i % tpb, 0)),
                 pl.BlockSpec((None, 2, N_GROUPS, LANES, tm), lambda i: (i // tpb, 0, 0, 0, i % tpb))]
    return pl.pallas_call(
        functools.partial(_inproj_kernel, tm=tm, tiles_per_batch=tpb),
        out_shape=out_shape + [jax.ShapeDtypeStruct((n, wd), dt) for wd, dt in widths],
        grid=(n // tm,),
        in_specs=[
            pl.BlockSpec((tm, d), row),
            pl.BlockSpec((tm, 1), row),
            pl.BlockSpec((1, d), fixed),
            pl.BlockSpec(w.shape, fixed),
            pl.BlockSpec(qnw.shape, fixed),
            pl.BlockSpec(knw.shape, fixed),
            pl.BlockSpec(cw.shape, fixed),
            pl.BlockSpec(pq.shape, fixed),
            pl.BlockSpec(pk.shape, fixed),
            pl.BlockSpec(freq.shape, fixed),
        ],
        out_specs=out_specs + [pl.BlockSpec((tm, wd), row) for wd, _ in widths],
        scratch_shapes=[pltpu.VMEM((tm + CONV_HALO, A_WIDTH), F32)],
        compiler_params=_cparams(("arbitrary",)),
        name="inproj",
    )(x, pos_col, g.reshape(1, d), w, qnw, knw, cw, pq, pk, freq)


def _gelu_tanh(x):
    return 0.5 * x * (1.0 + jnp.tanh(math.sqrt(2.0 / math.pi) * (x + 0.044715 * (x * x * x))))


def _compress_kernel(xk_ref, xv_ref, pos_ref, pe_ref, wk_ref, wv_ref, w2_ref, w2vt_ref, knw_ref, pk_ref, freq_ref,
                     kc_ref, vct_ref, *, ncp):
    def hidden(x_ref, pe_row, w_ref):
        x = jnp.concatenate([x_ref[pl.ds(j, ncp, stride=CMP_STRIDE), :] for j in range(CMP_STRIDE)], axis=1)
        a = _dot((x + pe_ref[pe_row:pe_row + 1, :]).astype(BF16), w_ref[0])
        b = _dot((x + pe_ref[pe_row + 1:pe_row + 2, :]).astype(BF16), w_ref[1])
        hid = a + pltpu.roll(b, ncp - 1, axis=0)
        return _gelu_tanh(hid)

    low = lax.broadcasted_iota(jnp.int32, (1, LANES), 1) < HEAD_DIM
    kc = _dot(hidden(xk_ref, 0, wk_ref).astype(BF16), w2_ref[...])
    kc = _group_rms(kc, knw_ref[0:1, :], pk_ref[...])
    cos_t, sin_t = _rope_tables(pos_ref[...].astype(F32), freq_ref[...])
    kc = _rope(kc, cos_t, sin_t)
    kc_ref[0] = jnp.where(low, kc, 0.0).astype(BF16)
    kc_ref[1] = jnp.where(low, pltpu.roll(kc, HEAD_DIM, axis=1), 0.0).astype(BF16)
    hv = hidden(xv_ref, 2, wv_ref)
    hv_t = jnp.concatenate(
        [jnp.concatenate([hv[r * LANES:(r + 1) * LANES, c * LANES:(c + 1) * LANES].T for r in range(ncp // LANES)],
                         axis=1) for c in range(hv.shape[1] // LANES)], axis=0)
    vct = _dot(w2vt_ref[...], hv_t.astype(BF16))
    for g in range(N_GROUPS):
        vct_ref[g, 0:HEAD_DIM, :] = vct[g * HEAD_DIM:(g + 1) * HEAD_DIM, :].astype(BF16)
        vct_ref[g, HEAD_DIM:, :] = jnp.zeros((LANES - HEAD_DIM, ncp), BF16)


def _compress(xk, xv, pos_c, pe, wk, wv, w2k, w2vt, knw, pk, freq, *, seq):
    bsz, ncp = pos_c.shape[:2]
    bat = lambda b: (b, 0, 0)
    fix2 = lambda b: (0, 0)
    fix3 = lambda b: (0, 0, 0)
    return pl.pallas_call(
        functools.partial(_compress_kernel, ncp=ncp),
        out_shape=[jax.ShapeDtypeStruct((bsz, N_GROUPS, ncp, LANES), BF16),
                   jax.ShapeDtypeStruct((bsz, N_GROUPS, LANES, ncp), BF16)],
        grid=(bsz,),
        in_specs=[
            pl.BlockSpec((seq, KV_WIDTH), lambda b: (b, 0)),
            pl.BlockSpec((seq, KV_WIDTH), lambda b: (b, 0)),
            pl.BlockSpec((None, ncp, 1), bat),
            pl.BlockSpec(pe.shape, fix2),
            pl.BlockSpec(wk.shape, fix3),
            pl.BlockSpec(wv.shape, fix3),
            pl.BlockSpec(w2k.shape, fix2),
            pl.BlockSpec(w2vt.shape, fix2),
            pl.BlockSpec(knw.shape, fix2),
            pl.BlockSpec(pk.shape, fix2),
            pl.BlockSpec(freq.shape, fix2),
        ],
        out_specs=[pl.BlockSpec((None, N_GROUPS, ncp, LANES), lambda b: (b, 0, 0, 0)),
                   pl.BlockSpec((None, N_GROUPS, LANES, ncp), lambda b: (b, 0, 0, 0))],
        compiler_params=_cparams(("parallel",)),
        name="compress",
    )(xk, xv, pos_c, pe, wk, wv, w2k, w2vt, knw, pk, freq)


def _head_columns(qt_ref):
    cols = []
    for pair in range(HEADS_PER_GROUP // 2):
        both = qt_ref[pair * LANES:(pair + 1) * LANES, :]
        cols += [both, jnp.concatenate([both[HEAD_DIM:, :], both[:HEAD_DIM, :]], axis=0)]
    return cols


def _software_pipeline(n, scores, probs, finish):
    s, p = {}, {}
    for step in range(n + 2):
        if step >= 2:
            finish(step - 2, p.pop(step - 2))
        if 1 <= step <= n:
            p[step - 1] = probs(step - 1, s.pop(step - 1))
        if step < n:
            s[step] = scores(step)


def _gate_row(gt_ref, branch, hd):
    r = branch * HEADS_PER_GROUP + hd
    return gt_ref[r:r + 1, :]


def _cmp_kernel(qt_ref, kc_ref, vct_ref, ovt_ref, gt_ref, ot_ref, sbt_ref, *, tq, ncp):
    i = pl.program_id(2)
    t = i * tq + lax.broadcasted_iota(jnp.int32, (1, tq), 1)
    blk_end = lax.broadcasted_iota(jnp.int32, (ncp, 1), 0) * CMP_STRIDE + (CMP_BLOCK - 1)
    valid = blk_end <= t
    any_valid = (t >= CMP_BLOCK - 1).astype(F32)
    kc = kc_ref[...]
    vct = vct_ref[...]
    heads = _head_columns(qt_ref)
    psum_parts = []

    def scores(hd):
        return jnp.where(valid, _dot(kc, heads[hd]), NEG)

    def probs(hd, s):
        e = jnp.exp(s - jnp.max(s, axis=0, keepdims=True))
        return e * (any_valid / jnp.sum(e, axis=0, keepdims=True))

    def finish(hd, p):
        o_t = _dot(vct, p.astype(BF16))[:HEAD_DIM, :]
        ot_ref[hd * HEAD_DIM:(hd + 1) * HEAD_DIM, :] = o_t * _gate_row(gt_ref, _BR_CMP, hd)
        psum_parts.append(p)

    _software_pipeline(HEADS_PER_GROUP, scores, probs, finish)
    psum = (psum_parts[0] + psum_parts[1]) + (psum_parts[2] + psum_parts[3])

    hi = psum.astype(BF16)
    lo = (psum - hi.astype(F32)).astype(BF16)
    imp = _dot(ovt_ref[...], hi) + _dot(ovt_ref[...], lo)
    blk = lax.broadcasted_iota(jnp.int32, (LANES, 1), 0)
    sel_start = blk * SEL_BLOCK
    cur = (t >> int(math.log2(SEL_BLOCK))) << int(math.log2(SEL_BLOCK))
    imp = jnp.where(sel_start <= t, imp, -1.0)
    imp = jnp.where(sel_start == cur, 1e4, jnp.where(sel_start == 0, 1e4, imp))

    blk_f = blk.astype(F32)
    bias = jnp.full((LANES, tq), NEG, F32)
    for _ in range(SEL_TOPK):
        mx = jnp.max(imp, axis=0, keepdims=True)
        first = jnp.min(jnp.where(imp == mx, blk_f, float(LANES)), axis=0, keepdims=True)
        pick = blk_f == first
        bias = jnp.where(pick, 0.0, bias)
        imp = jnp.where(pick, -3e38, imp)
    sbt_ref[...] = bias.astype(BF16)


def _tmaps(nq):
    return (lambda b, g, i: (b, g, i)), (lambda b, g, i: (g, b * nq + i)), (lambda b, g, i: (b, g, 0, 0))


def _cmp_select(qt, kc, vct, ovt, gates_t, *, bsz, seq, tq=256):
    ncp = kc.shape[2]
    nq = seq // tq
    gw = HEADS_PER_GROUP * HEAD_DIM
    tmap, gmap, kvmap = _tmaps(nq)
    return pl.pallas_call(
        functools.partial(_cmp_kernel, tq=tq, ncp=ncp),
        out_shape=[jax.ShapeDtypeStruct((bsz, A_WIDTH, seq), F32),
                   jax.ShapeDtypeStruct((bsz, N_GROUPS, LANES, seq), BF16)],
        grid=(bsz, N_GROUPS, nq),
        in_specs=[
            pl.BlockSpec((None, gw, tq), tmap),
            pl.BlockSpec((None, None, ncp, LANES), kvmap),
            pl.BlockSpec((None, None, LANES, ncp), kvmap),
            pl.BlockSpec(ovt.shape, lambda b, g, i: (0, 0)),
            pl.BlockSpec((GATE_ROWS, tq), gmap),
        ],
        out_specs=[pl.BlockSpec((None, gw, tq), tmap),
                   pl.BlockSpec((None, None, LANES, tq), lambda b, g, i: (b, g, 0, i))],
        compiler_params=_cparams(("parallel", "parallel", "parallel")),
        name="cmp_select",
    )(qt, kc, vct, ovt, gates_t)


_SLC_UNROLL = 2


def _slc_kernel(qt_ref, sbt_ref, k_ref, vt_ref, gt_ref, prev_ref, o_ref,
                qa_ref, s_ref, p_ref, acc_ref, *, tq):
    i = pl.program_id(2)
    rows = HEADS_PER_GROUP * tq
    qa_ref[...] = jnp.concatenate([jnp.concatenate([sbt_ref[...]] * HEADS_PER_GROUP, axis=1),
                                   jnp.concatenate(_head_columns(qt_ref), axis=1)], axis=0)
    s_ref[...] = jnp.full(s_ref.shape, -jnp.inf, F32)
    p_ref[...] = jnp.zeros(p_ref.shape, BF16)
    acc_ref[...] = jnp.zeros(acc_ref.shape, F32)

    def accumulate(j, alpha):
        j = jnp.maximum(j, 0)
        pv = _dot(vt_ref[:, pl.ds(pl.multiple_of(j * tq, tq), tq)], p_ref[...])
        acc_ref[...] = alpha * acc_ref[...] + pv

    def probs(m):
        s = s_ref[...]
        m_new = jnp.maximum(m, jnp.max(s, axis=0, keepdims=True))
        p_ref[...] = jnp.exp(s - m_new).astype(BF16)
        return m_new, jnp.exp(m - m_new)

    def scores(j, masked):
        s = _dot(k_ref[pl.ds(pl.multiple_of(j * tq, tq), tq), :], qa_ref[...])
        if masked:
            causal = (lax.broadcasted_iota(jnp.int32, (tq, 1), 0)
                      <= (lax.broadcasted_iota(jnp.int32, (1, rows), 1) & (tq - 1)))
            s = jnp.where(causal, s, NEG)
        s_ref[...] = s

    def body(it, carry):
        m, alpha = carry
        accumulate(it - 2, alpha)
        m, alpha = probs(m)
        scores(it, False)
        return m, alpha

    def steps(first, count, carry):
        for k in range(count):
            carry = body(first + k, carry)
        return carry

    carry = (jnp.full((1, rows), -3e38, F32), jnp.ones((1, rows), F32))
    trips = i >> int(math.log2(_SLC_UNROLL))
    carry = lax.fori_loop(0, trips, lambda k, c: steps(k * _SLC_UNROLL, _SLC_UNROLL, c), carry)
    done = trips * _SLC_UNROLL
    size = _SLC_UNROLL // 2
    while size >= 1:
        carry = lax.cond((i & size) != 0, functools.partial(steps, done, size), lambda c: c, carry)
        done = done + (i & size)
        size //= 2
    m, alpha = carry
    accumulate(i - 2, alpha)
    m, alpha = probs(m)
    scores(i, True)
    accumulate(i - 1, alpha)
    m, alpha = probs(m)
    accumulate(i, alpha)

    gate = jnp.concatenate([_gate_row(gt_ref, _BR_SLC, hd) for hd in range(HEADS_PER_GROUP)], axis=1)
    out_t = acc_ref[0:HEAD_DIM, :] * (gate / acc_ref[HEAD_DIM:HEAD_DIM + 1, :])
    for pair in range(HEADS_PER_GROUP // 2):
        for c in range(tq // LANES):
            col = 2 * pair * tq + c * LANES
            both = jnp.concatenate([out_t[:, col:col + LANES], out_t[:, col + tq:col + tq + LANES]], axis=0)
            both = both + prev_ref[pair * LANES:(pair + 1) * LANES, c * LANES:(c + 1) * LANES]
            o_ref[c * LANES:(c + 1) * LANES, pair * LANES:(pair + 1) * LANES] = both.T.astype(BF16)


def _slc(qt, sbt, k_aug, vt_aug, gates_t, prev, *, bsz, seq, tq=256):
    nq = seq // tq
    gw = HEADS_PER_GROUP * HEAD_DIM
    rows = HEADS_PER_GROUP * tq
    tmap, gmap, kvmap = _tmaps(nq)
    return pl.pallas_call(
        functools.partial(_slc_kernel, tq=tq),
        out_shape=jax.ShapeDtypeStruct((bsz * seq, A_WIDTH), BF16),
        grid=(bsz, N_GROUPS, nq),
        in_specs=[
            pl.BlockSpec((None, gw, tq), tmap),
            pl.BlockSpec((None, None, LANES, tq), lambda b, g, i: (b, g, 0, i)),
            pl.BlockSpec((None, None, seq, 2 * LANES), kvmap),
            pl.BlockSpec((None, None, None, LANES, seq), lambda b, g, i: (b, 0, g, 0, 0)),
            pl.BlockSpec((GATE_ROWS, tq), gmap),
            pl.BlockSpec((None, gw, tq), tmap),
        ],
        out_specs=pl.BlockSpec((tq, gw), lambda b, g, i: (b * nq + i, g)),
        scratch_shapes=[pltpu.VMEM((2 * LANES, rows), BF16), pltpu.VMEM((tq, rows), F32),
                        pltpu.VMEM((tq, rows), BF16), pltpu.VMEM((LANES, rows), F32)],
        compiler_params=_cparams(("parallel", "parallel", "parallel")),
        name="slc_attn",
    )(qt, sbt, k_aug, vt_aug, gates_t, prev)


def _win_kernel(qt_ref, k_ref, vt_ref, band_ref, gt_ref, prev_ref, ot_ref, *, tq):
    i = pl.program_id(2)
    span = WINDOW + tq
    start = pl.multiple_of(jnp.maximum(i - WINDOW // tq, 0) * tq, tq)
    k = k_ref[pl.ds(start, span), :]
    vt = vt_ref[:, pl.ds(start, span)]
    heads = _head_columns(qt_ref)

    def scores(hd):
        return _dot(k, heads[hd]) + band_ref[...]

    def probs(hd, s):
        return jnp.exp(s - jnp.max(s, axis=0, keepdims=True)).astype(BF16)

    def finish(hd, p):
        acc = _dot(vt, p)
        rows = slice(hd * HEAD_DIM, (hd + 1) * HEAD_DIM)
        scale = _gate_row(gt_ref, _BR_WIN, hd) / acc[HEAD_DIM:HEAD_DIM + 1, :]
        ot_ref[rows, :] = prev_ref[rows, :] + acc[:HEAD_DIM, :] * scale

    _software_pipeline(HEADS_PER_GROUP, scores, probs, finish)


def _window_band(tq):
    edge = WINDOW // tq
    v = jnp.arange(edge + 1)[:, None, None]
    kpos = jnp.maximum(v - edge, 0) * tq + jnp.arange(WINDOW + tq)[None, :, None]
    t = v * tq + jnp.arange(tq)[None, None, :]
    ok = (t - kpos >= 0) & (t - kpos < WINDOW)
    return jnp.where(ok, 0.0, NEG).astype(F32)


def _win(qt, k, vt, gates_t, prev, *, bsz, seq, tq=256):
    nq = seq // tq
    gw = HEADS_PER_GROUP * HEAD_DIM
    band = _window_band(tq)
    last = band.shape[0] - 1
    tmap, gmap, kvmap = _tmaps(nq)
    return pl.pallas_call(
        functools.partial(_win_kernel, tq=tq),
        out_shape=jax.ShapeDtypeStruct((bsz, A_WIDTH, seq), F32),
        grid=(bsz, N_GROUPS, nq),
        in_specs=[
            pl.BlockSpec((None, gw, tq), tmap),
            pl.BlockSpec((None, None, seq, LANES), kvmap),
            pl.BlockSpec((None, None, None, LANES, seq), lambda b, g, i: (b, 1, g, 0, 0)),
            pl.BlockSpec((None,) + band.shape[1:], lambda b, g, i: (jnp.minimum(i, last), 0, 0)),
            pl.BlockSpec((GATE_ROWS, tq), gmap),
            pl.BlockSpec((None, gw, tq), tmap),
        ],
        out_specs=pl.BlockSpec((None, gw, tq), tmap),
        compiler_params=_cparams(("parallel", "parallel", "parallel")),
        name="win_attn",
    )(qt, k, vt, band, gates_t, prev)


def _outproj_kernel(x_ref, oa_ref, ob_ref, w_ref, o_ref):
    y = _dot(oa_ref[...], w_ref[0:A_WIDTH, :]) + _dot(ob_ref[...], w_ref[A_WIDTH:, :])
    o_ref[...] = x_ref[...] + y


def _outproj(x, oa, ob, w, *, tm=512):
    n, d = x.shape
    row = lambda i: (i, 0)
    return pl.pallas_call(
        _outproj_kernel,
        out_shape=jax.ShapeDtypeStruct((n, d), F32),
        grid=(n // tm,),
        in_specs=[
            pl.BlockSpec((tm, d), row),
            pl.BlockSpec((tm, A_WIDTH), row),
            pl.BlockSpec((tm, A_WIDTH), row),
            pl.BlockSpec(w.shape, lambda i: (0, 0)),
        ],
        out_specs=pl.BlockSpec((tm, d), row),
        compiler_params=_cparams(("parallel",)),
        name="outproj",
    )(x, oa, ob, w)


def _pool_kernel(x_ref, halo_ref, g_ref, w_ref, sc_ref, o_ref, ext_ref, *, tm, tiles_per_batch):
    i = pl.program_id(0)
    first = i % tiles_per_batch == 0
    x = x_ref[...]
    h = _rms(x, g_ref[...])
    hh = _rms(halo_ref[...], g_ref[...])
    ext_ref[0:POOL_HALO, :] = jnp.where(first, 0.0, hh)
    ext_ref[POOL_HALO:POOL_HALO + tm, :] = h
    t = (i % tiles_per_batch) * tm + lax.broadcasted_iota(jnp.int32, (tm, 1), 0)
    gw = h.shape[1] // len(POOL_WINDOWS)
    for gi, win in enumerate(POOL_WINDOWS):
        c0 = gi * gw
        hg = h[:, c0:c0 + gw]
        tot = hg
        for sft in range(1, win):
            tot = tot + ext_ref[pl.ds(POOL_HALO - sft, tm), c0:c0 + gw]
        cnt = jnp.minimum(t + 1, win).astype(F32)
        y = _dot((tot / cnt - hg).astype(BF16), w_ref[gi])
        o_ref[:, c0:c0 + gw] = x[:, c0:c0 + gw] + y * sc_ref[:, c0:c0 + gw]


def _pool(x, g, w, scale, *, seq, tm=512):
    n, d = x.shape
    hb = tm // POOL_HALO
    return pl.pallas_call(
        functools.partial(_pool_kernel, tm=tm, tiles_per_batch=seq // tm),
        out_shape=jax.ShapeDtypeStruct((n, d), F32),
        grid=(n // tm,),
        in_specs=[
            pl.BlockSpec((tm, d), lambda i: (i, 0)),
            pl.BlockSpec((POOL_HALO, d), lambda i: (jnp.maximum(i * hb - 1, 0), 0)),
            pl.BlockSpec((1, d), lambda i: (0, 0)),
            pl.BlockSpec(w.shape, lambda i: (0, 0, 0)),
            pl.BlockSpec((1, d), lambda i: (0, 0)),
        ],
        out_specs=pl.BlockSpec((tm, d), lambda i: (i, 0)),
        scratch_shapes=[pltpu.VMEM((tm + POOL_HALO, d), F32)],
        compiler_params=_cparams(("parallel",)),
        name="pool_mixer",
    )(x, x, g.reshape(1, d), w, scale.reshape(1, d))


def _pack_w_in(w_in):
    o = 0
    q = w_in[:, o:o + A_WIDTH]; o += A_WIDTH
    kc, vc, ks, vs, kw, vw = [w_in[:, o + k * KV_WIDTH:o + (k + 1) * KV_WIDTH] for k in range(6)]
    o += 6 * KV_WIDTH
    gates = w_in[:, o:o + N_BRANCH * N_HEADS]; o += N_BRANCH * N_HEADS
    u, gb, gc = [w_in[:, o + k * A_WIDTH:o + (k + 1) * A_WIDTH] for k in range(3)]
    gates = gates.reshape(-1, N_GROUPS, HEADS_PER_GROUP, N_BRANCH).transpose(0, 1, 3, 2)
    gates = gates.reshape(-1, N_GROUPS, N_BRANCH * HEADS_PER_GROUP)
    gates = jnp.pad(gates, ((0, 0), (0, 0), (0, LANES - N_BRANCH * HEADS_PER_GROUP))).reshape(-1, GATE_WIDTH)
    return jnp.concatenate([q, ks, kw, vs, vw, kc, vc, gates, u, gb, gc], axis=1).astype(BF16)


def _group_mean_matrix(width):
    r = jnp.arange(width) // HEAD_DIM
    return jnp.where(r[:, None] == r[None, :], 1.0 / HEAD_DIM, 0.0).astype(BF16)


def _rope_freq_row():
    lane = jnp.arange(LANES) % HEAD_DIM
    freqs = ROPE_THETA ** (-jnp.arange(0, ROPE_DIM, 2, dtype=F32) / ROPE_DIM)
    return jnp.where(lane < ROPE_DIM, freqs[lane % ROPE_HALF], 0.0).astype(F32).reshape(1, LANES)


def _per_group(a, bsz, rows):
    return a.reshape(bsz, rows, N_GROUPS, HEAD_DIM).transpose(0, 2, 1, 3)


def _hybrid_mixer(x, positions, g, w_in, q_norm, k_norm, cmp_pos, cmp_w1, cmp_w2, conv_w, w_out, *, bsz, seq):
    n = bsz * seq
    ncp = seq // CMP_STRIDE
    freq = _rope_freq_row()
    pq = _group_mean_matrix(A_WIDTH)
    pk = _group_mean_matrix(KV_WIDTH)
    qnw = jnp.tile(q_norm, N_HEADS).reshape(1, A_WIDTH)
    knw = jnp.tile(k_norm, (1, N_GROUPS))

    q_t, k_slc, k_win, v_t, kc_raw, vc_raw, gates, ob = _inproj(
        x, positions.reshape(n, 1), g, _pack_w_in(w_in), qnw, knw, conv_w, pq, pk, freq, seq=seq)

    eye_g = jnp.eye(N_GROUPS, dtype=F32)
    seg = CMP_STRIDE * KV_WIDTH

    def seg_weights(w1):
        w1r = w1.reshape(CMP_BLOCK, HEAD_DIM, CMP_HIDDEN)
        halves = [jnp.einsum('jdh,gk->jgdkh', w1r[a:a + CMP_STRIDE], eye_g).reshape(seg, N_GROUPS * CMP_HIDDEN)
                  for a in (0, CMP_STRIDE)]
        return jnp.stack(halves).astype(BF16)

    def seg_pos(pe):
        return [jnp.broadcast_to(pe[a:a + CMP_STRIDE, None, :], (CMP_STRIDE, N_GROUPS, HEAD_DIM)).reshape(seg)
                for a in (0, CMP_STRIDE)]

    pe = jnp.stack(seg_pos(cmp_pos[0]) + seg_pos(cmp_pos[1]))
    w2 = jnp.stack([jnp.einsum('hd,gk->ghkd', cmp_w2[a], eye_g).reshape(N_GROUPS * CMP_HIDDEN, KV_WIDTH)
                    for a in range(2)]).astype(BF16)
    blk_end = jnp.minimum(jnp.arange(ncp) * CMP_STRIDE + CMP_BLOCK - 1, seq - 1)
    pos_c = positions[:, blk_end].reshape(bsz, ncp, 1)
    kc_g, vc_t = _compress(kc_raw, vc_raw, pos_c, pe, seg_weights(cmp_w1[0]), seg_weights(cmp_w1[1]), w2[0], w2[1].T,
                           knw, pk, freq, seq=seq)

    gates_t = gates.reshape(n, N_GROUPS, LANES)[:, :, :GATE_ROWS].transpose(1, 2, 0).reshape(N_GROUPS * GATE_ROWS, n)
    cstart = jnp.arange(ncp) * CMP_STRIDE
    sstart = jnp.arange(LANES) * SEL_BLOCK
    ovt = ((cstart[None, :] < sstart[:, None] + SEL_BLOCK)
           & (cstart[None, :] + CMP_BLOCK > sstart[:, None])).astype(BF16)
    o_t, sel_bias = _cmp_select(q_t, kc_g, vc_t, ovt, gates_t, bsz=bsz, seq=seq)

    o_t = _win(q_t, k_win, v_t, gates_t, o_t, bsz=bsz, seq=seq)
    o_a = _slc(q_t, sel_bias, k_slc, v_t, gates_t, o_t, bsz=bsz, seq=seq)

    return _outproj(x, o_a, ob, w_out.astype(BF16))


def kernel(x, positions, ffn_norm, ffn_w_gate, ffn_w_up, ffn_w_down, mix_norm, hyb_w_in, hyb_q_norm, hyb_k_norm,
           hyb_cmp_pos, hyb_cmp_w1, hyb_cmp_w2, hyb_conv_w, hyb_w_out, pool_w, pool_scale):
    bsz, seq, d = x.shape
    depth = ffn_norm.shape[0]
    assert seq % 2048 == 0 and seq // SEL_BLOCK <= LANES and seq // SEL_BLOCK >= SEL_TOPK
    wg, wu, wd = (w.astype(BF16) for w in (ffn_w_gate, ffn_w_up, ffn_w_down))
    xf = x.reshape(bsz * seq, d)
    for layer in range(depth):
        xf = _ffn(xf, ffn_norm[layer, 0], wg, wu, wd, layer, 0)
        i = layer // 2
        if layer % 2 == 0:
            xf = _hybrid_mixer(xf, positions, mix_norm[layer], hyb_w_in[i], hyb_q_norm[i], hyb_k_norm[i],
                               hyb_cmp_pos[i], hyb_cmp_w1[i], hyb_cmp_w2[i], hyb_conv_w[i], hyb_w_out[i],
                               bsz=bsz, seq=seq)
        else:
            xf = _pool(xf, mix_norm[layer], pool_w[i].astype(BF16), pool_scale[i], seq=seq)
        xf = _ffn(xf, ffn_norm[layer, 1], wg, wu, wd, layer, 1)
    return xf.reshape(bsz, seq, d)
```

```python
import functools
import math

import jax
import jax.numpy as jnp
from jax import lax
from jax.experimental import pallas as pl
from jax.experimental.pallas import tpu as pltpu

F32 = jnp.float32
BF16 = jnp.bfloat16

HEAD_DIM = 64
N_GROUPS = 2
HEADS_PER_GROUP = 4
N_HEADS = N_GROUPS * HEADS_PER_GROUP
A_WIDTH = N_HEADS * HEAD_DIM
KV_WIDTH = N_GROUPS * HEAD_DIM
ROPE_DIM = HEAD_DIM // 4
ROPE_HALF = ROPE_DIM // 2
ROPE_THETA = 500000.0
CMP_BLOCK = 32
CMP_STRIDE = 16
CMP_HIDDEN = 2 * HEAD_DIM
SEL_BLOCK = 64
SEL_TOPK = 16
WINDOW = 512
N_BRANCH = 3
_BR_CMP, _BR_SLC, _BR_WIN = 0, 1, 2
CONV_WIDTH = 3
POOL_WINDOWS = (2, 4, 8, 16)
POOL_HALO = 16
EPS = 1e-6
NEG = -1e30

LANES = 128
VMEM_LIMIT_BYTES = 56 * 1024 * 1024


def _cparams(sem):
    return pltpu.CompilerParams(dimension_semantics=sem, vmem_limit_bytes=VMEM_LIMIT_BYTES)


def _dot(a, b):
    return jnp.dot(a, b, preferred_element_type=F32)


def _dot_nt(a, b):
    return lax.dot_general(a, b, (((1,), (1,)), ((), ())), preferred_element_type=F32)


def _dot_split(a, b):
    hi = a.astype(BF16)
    lo = (a - hi.astype(F32)).astype(BF16)
    return _dot(hi, b) + _dot(lo, b)


def _rms(x, g):
    ms = jnp.mean(x * x, axis=-1, keepdims=True)
    return x * lax.rsqrt(ms + EPS) * g


def _group_rms(x, g, pmat):
    ms = _dot_split(x * x, pmat)
    return x * lax.rsqrt(ms + EPS) * g


def _rope_tables(pos_col, freq_row):
    ang = pos_col * freq_row
    lane = lax.broadcasted_iota(jnp.int32, (1, LANES), 1) & (HEAD_DIM - 1)
    sign = jnp.where(lane < ROPE_HALF, -1.0, 1.0).astype(F32)
    return jnp.cos(ang), jnp.sin(ang) * sign


def _rope(x, cos_t, sin_t):
    w = x.shape[1]
    reps = w // LANES
    if reps > 1:
        cos_t = jnp.concatenate([cos_t] * reps, axis=1)
        sin_t = jnp.concatenate([sin_t] * reps, axis=1)
    lane = lax.broadcasted_iota(jnp.int32, (1, w), 1) & (HEAD_DIM - 1)
    partner = jnp.where(lane < ROPE_HALF,
                        pltpu.roll(x, w - ROPE_HALF, axis=1),
                        pltpu.roll(x, ROPE_HALF, axis=1))
    return x * cos_t + partner * sin_t


def _ffn_kernel(x_ref, xnext_ref, g_ref, wg_ref, wu_ref, wd_ref, o_ref, h_ref, acc_ref, *, nj):
    i, j = pl.program_id(0), pl.program_id(1)
    slot = i % 2

    @pl.when((i == 0) & (j == 0))
    def _():
        h_ref[0] = _rms(x_ref[...], g_ref[...]).astype(BF16)

    def step(first, last):
        h = h_ref[slot]
        a = _dot(h, wg_ref[...])
        b = _dot(h, wu_ref[...])
        act = a * (1.0 / (1.0 + jnp.exp(-a))) * b
        y = _dot(act.astype(BF16), wd_ref[...])
        if not first:
            y = acc_ref[...] + y
        if last:
            h_ref[1 - slot] = _rms(xnext_ref[...], g_ref[...]).astype(BF16)
            o_ref[...] = x_ref[...] + 0.5 * y
        else:
            acc_ref[...] = y

    for jj in range(nj):
        pl.when(j == jj)(functools.partial(step, jj == 0, jj == nj - 1))


def _ffn(x, g, wg, wu, wd, layer, half, *, tm=512, tf=2816):
    n, d = x.shape
    dff = wg.shape[-1]
    ni, nj = n // tm, dff // tf
    wmode = dict(pipeline_mode=pl.Buffered(1)) if nj == 1 else {}
    return pl.pallas_call(
        functools.partial(_ffn_kernel, nj=nj),
        out_shape=jax.ShapeDtypeStruct((n, d), F32),
        grid=(ni, nj),
        in_specs=[
            pl.BlockSpec((tm, d), lambda i, j: (i, 0)),
            pl.BlockSpec((tm, d), lambda i, j: (jnp.minimum(i + 1, ni - 1), 0)),
            pl.BlockSpec((1, d), lambda i, j: (0, 0)),
            pl.BlockSpec((None, None, d, tf), lambda i, j: (layer, half, 0, j), **wmode),
            pl.BlockSpec((None, None, d, tf), lambda i, j: (layer, half, 0, j), **wmode),
            pl.BlockSpec((None, None, tf, d), lambda i, j: (layer, half, j, 0), **wmode),
        ],
        out_specs=pl.BlockSpec((tm, d), lambda i, j: (i, 0)),
        scratch_shapes=[pltpu.VMEM((2, tm, d), BF16), pltpu.VMEM((tm, d), F32)],
        compiler_params=_cparams(("arbitrary", "arbitrary")),
        name="ffn",
    )(x, x, g.reshape(1, d), wg, wu, wd)


GATE_WIDTH = N_GROUPS * LANES
GATE_ROWS = 16
_C_Q = 0
_C_K = _C_Q + A_WIDTH
_C_V = _C_K + 2 * KV_WIDTH
_C_C = _C_V + 2 * KV_WIDTH
_C_GATE = _C_C + 2 * KV_WIDTH
_C_U = _C_GATE + GATE_WIDTH
_C_GB = _C_U + A_WIDTH
_C_GC = _C_GB + A_WIDTH
_C_END = _C_GC + A_WIDTH
CONV_HALO = 8


def _inproj_kernel(x_ref, pos_ref, g_ref, w_ref, qnw_ref, knw_ref, cw_ref, pq_ref, pk_ref, freq_ref,
                   qt_ref, kslc_ref, kwin_ref, vt_ref, kc_ref, vc_ref, gates_ref, ob_ref,
                   vext_ref, *, tm, tiles_per_batch):
    i = pl.program_id(0)
    chunks = range(tm // LANES)

    def transposed(a, c):
        return jnp.concatenate([a[r * LANES:(r + 1) * LANES, c * LANES:(c + 1) * LANES].T for r in chunks], axis=1)

    def per_group(k):
        low = lax.broadcasted_iota(jnp.int32, (1, LANES), 1) < HEAD_DIM
        return [jnp.where(low, k, 0.0), jnp.where(low, pltpu.roll(k, HEAD_DIM, axis=1), 0.0)]

    @pl.when(i % tiles_per_batch == 0)
    def _():
        vext_ref[0:CONV_HALO, :] = jnp.zeros((CONV_HALO, A_WIDTH), F32)

    h = _rms(x_ref[...], g_ref[...]).astype(BF16)

    def proj(c0, width):
        return _dot(h, w_ref[:, c0:c0 + width])

    q_raw = proj(_C_Q, A_WIDTH)
    k2 = proj(_C_K, 2 * KV_WIDTH)
    cos_t, sin_t = _rope_tables(pos_ref[...].astype(F32), freq_ref[...])
    v = proj(_C_GC, A_WIDTH) * proj(_C_U, A_WIDTH)
    gate_b = proj(_C_GB, A_WIDTH)

    q = _group_rms(q_raw, qnw_ref[...], pq_ref[...])
    q = _rope(q, cos_t, sin_t) * (HEAD_DIM ** -0.5)
    for c in range(A_WIDTH // LANES):
        qt_ref[c * LANES:(c + 1) * LANES, :] = transposed(q, c).astype(BF16)

    tok = (i % tiles_per_batch) * tm + lax.broadcasted_iota(jnp.int32, (tm, 1), 0)
    blk_onehot = (lax.broadcasted_iota(jnp.int32, (1, LANES), 1) == (tok >> int(math.log2(SEL_BLOCK))))
    blk_onehot = jnp.where(blk_onehot, 1.0, 0.0).astype(BF16)
    ks = _rope(_group_rms(k2[:, :KV_WIDTH], knw_ref[1:2, :], pk_ref[...]), cos_t, sin_t)
    for g, kg in enumerate(per_group(ks)):
        kslc_ref[g] = jnp.concatenate([blk_onehot, kg.astype(BF16)], axis=1)
    kw = _rope(_group_rms(k2[:, KV_WIDTH:], knw_ref[2:3, :], pk_ref[...]), cos_t, sin_t)
    for g, kg in enumerate(per_group(kw)):
        kwin_ref[g] = kg.astype(BF16)

    v2 = proj(_C_V, 2 * KV_WIDTH)
    ones_rows = jnp.where(lax.broadcasted_iota(jnp.int32, (LANES - HEAD_DIM, 1), 0) == 0, 1.0, 0.0)
    ones_rows = jnp.broadcast_to(ones_rows, (LANES - HEAD_DIM, tm)).astype(BF16)
    for branch in range(2):
        vt = transposed(v2, branch).astype(BF16)
        for g in range(N_GROUPS):
            vt_ref[branch, g, 0:HEAD_DIM, :] = vt[g * HEAD_DIM:(g + 1) * HEAD_DIM, :]
            vt_ref[branch, g, HEAD_DIM:, :] = ones_rows
    c2 = proj(_C_C, 2 * KV_WIDTH)
    kc_ref[...] = c2[:, :KV_WIDTH]
    vc_ref[...] = c2[:, KV_WIDTH:]
    gate_logits = proj(_C_GATE, GATE_WIDTH)

    vext_ref[CONV_HALO:CONV_HALO + tm, :] = v
    y = (cw_ref[2:3, :] * v
         + cw_ref[1:2, :] * vext_ref[pl.ds(CONV_HALO - 1, tm), :]
         + cw_ref[0:1, :] * vext_ref[pl.ds(CONV_HALO - 2, tm), :])
    ob_ref[...] = (gate_b * y).astype(BF16)
    vext_ref[0:CONV_HALO, :] = vext_ref[tm:tm + CONV_HALO, :]
    gates_ref[...] = 1.0 / (1.0 + jnp.exp(-gate_logits))


def _inproj(x, pos_col, g, w, qnw, knw, cw, pq, pk, freq, *, seq, tm=256):
    n, d = x.shape
    bsz, tpb = n // seq, seq // tm
    row = lambda i: (i, 0)
    fixed = lambda i: (0, 0)
    widths = [(KV_WIDTH, F32), (KV_WIDTH, F32), (GATE_WIDTH, F32), (A_WIDTH, BF16)]
    out_shape = [jax.ShapeDtypeStruct((bsz, A_WIDTH, seq), BF16),
                 jax.ShapeDtypeStruct((bsz, N_GROUPS, seq, 2 * LANES), BF16),
                 jax.ShapeDtypeStruct((bsz, N_GROUPS, seq, LANES), BF16),
                 jax.ShapeDtypeStruct((bsz, 2, N_GROUPS, LANES, seq), BF16)]
    out_specs = [pl.BlockSpec((None, A_WIDTH, tm), lambda i: (i // tpb, 0, i % tpb)),
                 pl.BlockSpec((None, N_GROUPS, tm, 2 * LANES), lambda i: (i // tpb, 0, i % tpb, 0)),
                 pl.BlockSpec((None, N_GROUPS, tm, LANES), lambda i: (i // tpb, 0, i % tpb, 0)),
                 pl.BlockSpec((None, 2, N_GROUPS, LANES, tm), lambda i: (i // tpb, 0, 0, 0, i % tpb))]
    return pl.pallas_call(
        functools.partial(_inproj_kernel, tm=tm, tiles_per_batch=tpb),
        out_shape=out_shape + [jax.ShapeDtypeStruct((n, wd), dt) for wd, dt in widths],
        grid=(n // tm,),
        in_specs=[
            pl.BlockSpec((tm, d), row),
            pl.BlockSpec((tm, 1), row),
            pl.BlockSpec((1, d), fixed),
            pl.BlockSpec(w.shape, fixed),
            pl.BlockSpec(qnw.shape, fixed),
            pl.BlockSpec(knw.shape, fixed),
            pl.BlockSpec(cw.shape, fixed),
            pl.BlockSpec(pq.shape, fixed),
            pl.BlockSpec(pk.shape, fixed),
            pl.BlockSpec(freq.shape, fixed),
        ],
        out_specs=out_specs + [pl.BlockSpec((tm, wd), row) for wd, _ in widths],
        scratch_shapes=[pltpu.VMEM((tm + CONV_HALO, A_WIDTH), F32)],
        compiler_params=_cparams(("arbitrary",)),
        name="inproj",
    )(x, pos_col, g.reshape(1, d), w, qnw, knw, cw, pq, pk, freq)


def _gelu_tanh(x):
    return 0.5 * x * (1.0 + jnp.tanh(math.sqrt(2.0 / math.pi) * (x + 0.044715 * (x * x * x))))


def _compress_kernel(xk_ref, xv_ref, pos_ref, pe_ref, wk_ref, wv_ref, w2_ref, w2vt_ref, knw_ref, pk_ref, freq_ref,
                     kc_ref, vct_ref, *, ncp):
    def hidden(x_ref, pe_row, w_ref):
        x = jnp.concatenate([x_ref[pl.ds(j, ncp, stride=CMP_STRIDE), :] for j in range(CMP_STRIDE)], axis=1)
        a = _dot((x + pe_ref[pe_row:pe_row + 1, :]).astype(BF16), w_ref[0])
        b = _dot((x + pe_ref[pe_row + 1:pe_row + 2, :]).astype(BF16), w_ref[1])
        hid = a + pltpu.roll(b, ncp - 1, axis=0)
        return _gelu_tanh(hid)

    low = lax.broadcasted_iota(jnp.int32, (1, LANES), 1) < HEAD_DIM
    kc = _dot(hidden(xk_ref, 0, wk_ref).astype(BF16), w2_ref[...])
    kc = _group_rms(kc, knw_ref[0:1, :], pk_ref[...])
    cos_t, sin_t = _rope_tables(pos_ref[...].astype(F32), freq_ref[...])
    kc = _rope(kc, cos_t, sin_t)
    kc_ref[0] = jnp.where(low, kc, 0.0).astype(BF16)
    kc_ref[1] = jnp.where(low, pltpu.roll(kc, HEAD_DIM, axis=1), 0.0).astype(BF16)
    hv = hidden(xv_ref, 2, wv_ref)
    hv_t = jnp.concatenate(
        [jnp.concatenate([hv[r * LANES:(r + 1) * LANES, c * LANES:(c + 1) * LANES].T for r in range(ncp // LANES)],
                         axis=1) for c in range(hv.shape[1] // LANES)], axis=0)
    vct = _dot(w2vt_ref[...], hv_t.astype(BF16))
    for g in range(N_GROUPS):
        vct_ref[g, 0:HEAD_DIM, :] = vct[g * HEAD_DIM:(g + 1) * HEAD_DIM, :].astype(BF16)
        vct_ref[g, HEAD_DIM:, :] = jnp.zeros((LANES - HEAD_DIM, ncp), BF16)


def _compress(xk, xv, pos_c, pe, wk, wv, w2k, w2vt, knw, pk, freq, *, seq):
    bsz, ncp = pos_c.shape[:2]
    bat = lambda b: (b, 0, 0)
    fix2 = lambda b: (0, 0)
    fix3 = lambda b: (0, 0, 0)
    return pl.pallas_call(
        functools.partial(_compress_kernel, ncp=ncp),
        out_shape=[jax.ShapeDtypeStruct((bsz, N_GROUPS, ncp, LANES), BF16),
                   jax.ShapeDtypeStruct((bsz, N_GROUPS, LANES, ncp), BF16)],
        grid=(bsz,),
        in_specs=[
            pl.BlockSpec((seq, KV_WIDTH), lambda b: (b, 0)),
            pl.BlockSpec((seq, KV_WIDTH), lambda b: (b, 0)),
            pl.BlockSpec((None, ncp, 1), bat),
            pl.BlockSpec(pe.shape, fix2),
            pl.BlockSpec(wk.shape, fix3),
            pl.BlockSpec(wv.shape, fix3),
            pl.BlockSpec(w2k.shape, fix2),
            pl.BlockSpec(w2vt.shape, fix2),
            pl.BlockSpec(knw.shape, fix2),
            pl.BlockSpec(pk.shape, fix2),
            pl.BlockSpec(freq.shape, fix2),
        ],
        out_specs=[pl.BlockSpec((None, N_GROUPS, ncp, LANES), lambda b: (b, 0, 0, 0)),
                   pl.BlockSpec((None, N_GROUPS, LANES, ncp), lambda b: (b, 0, 0, 0))],
        compiler_params=_cparams(("parallel",)),
        name="compress",
    )(xk, xv, pos_c, pe, wk, wv, w2k, w2vt, knw, pk, freq)


def _head_columns(qt_ref):
    cols = []
    for pair in range(HEADS_PER_GROUP // 2):
        both = qt_ref[pair * LANES:(pair + 1) * LANES, :]
        cols += [both, jnp.concatenate([both[HEAD_DIM:, :], both[:HEAD_DIM, :]], axis=0)]
    return cols


def _software_pipeline(n, scores, probs, finish):
    s, p = {}, {}
    for step in range(n + 2):
        if step >= 2:
            finish(step - 2, p.pop(step - 2))
        if 1 <= step <= n:
            p[step - 1] = probs(step - 1, s.pop(step - 1))
        if step < n:
            s[step] = scores(step)


def _gate_row(gt_ref, branch, hd):
    r = branch * HEADS_PER_GROUP + hd
    return gt_ref[r:r + 1, :]


def _cmp_kernel(qt_ref, kc_ref, vct_ref, ovt_ref, gt_ref, ot_ref, sbt_ref, *, tq, ncp):
    i = pl.program_id(2)
    t = i * tq + lax.broadcasted_iota(jnp.int32, (1, tq), 1)
    any_valid = (t >= CMP_BLOCK - 1).astype(F32)
    heads = _head_columns(qt_ref)
    cmp_per_sel = SEL_BLOCK // CMP_STRIDE

    def run(nrow):
        nblk = nrow // cmp_per_sel
        blk_end = lax.broadcasted_iota(jnp.int32, (nrow, 1), 0) * CMP_STRIDE + (CMP_BLOCK - 1)
        valid = blk_end <= t
        kc = kc_ref[0:nrow, :]
        vct = vct_ref[:, 0:nrow]
        psum_parts = []

        def scores(hd):
            return jnp.where(valid, _dot(kc, heads[hd]), NEG)

        def probs(hd, s):
            e = jnp.exp(s - jnp.max(s, axis=0, keepdims=True))
            return e * (any_valid / jnp.sum(e, axis=0, keepdims=True))

        def finish(hd, p):
            o_t = _dot(vct, p.astype(BF16))[:HEAD_DIM, :]
            ot_ref[hd * HEAD_DIM:(hd + 1) * HEAD_DIM, :] = o_t * _gate_row(gt_ref, _BR_CMP, hd)
            psum_parts.append(p)

        _software_pipeline(HEADS_PER_GROUP, scores, probs, finish)
        psum = (psum_parts[0] + psum_parts[1]) + (psum_parts[2] + psum_parts[3])

        hi = psum.astype(BF16)
        lo = (psum - hi.astype(F32)).astype(BF16)
        ovt = ovt_ref[0:nblk, 0:nrow]
        imp = _dot(ovt, hi) + _dot(ovt, lo)
        blk = lax.broadcasted_iota(jnp.int32, (nblk, 1), 0)
        sel_start = blk * SEL_BLOCK
        cur = (t >> int(math.log2(SEL_BLOCK))) << int(math.log2(SEL_BLOCK))
        imp = jnp.where(sel_start <= t, imp, -1.0)
        imp = jnp.where(sel_start == cur, 1e4, jnp.where(sel_start == 0, 1e4, imp))

        blk_f = blk.astype(F32)
        bias = jnp.full((nblk, tq), NEG, F32)
        for _ in range(SEL_TOPK):
            mx = jnp.max(imp, axis=0, keepdims=True)
            first = jnp.min(jnp.where(imp == mx, blk_f, float(LANES)), axis=0, keepdims=True)
            pick = blk_f == first
            bias = jnp.where(pick, 0.0, bias)
            imp = jnp.where(pick, -3e38, imp)
        sbt_ref[0:nblk, :] = bias.astype(BF16)
        if nblk < LANES:
            sbt_ref[nblk:, :] = jnp.full((LANES - nblk, tq), NEG, BF16)

    sizes = [LANES * (v + 1) for v in range(ncp // LANES)]
    assert sizes[0] // cmp_per_sel >= SEL_TOPK
    tokens_per_variant = LANES * CMP_STRIDE
    lax.switch((i * tq + tq - 1) // tokens_per_variant, [functools.partial(run, nrow) for nrow in sizes])


def _tmaps(nq):
    return (lambda b, g, i: (b, g, i)), (lambda b, g, i: (g, b * nq + i)), (lambda b, g, i: (b, g, 0, 0))


def _cmp_select(qt, kc, vct, ovt, gates_t, *, bsz, seq, tq=256):
    ncp = kc.shape[2]
    nq = seq // tq
    gw = HEADS_PER_GROUP * HEAD_DIM
    tmap, gmap, kvmap = _tmaps(nq)
    return pl.pallas_call(
        functools.partial(_cmp_kernel, tq=tq, ncp=ncp),
        out_shape=[jax.ShapeDtypeStruct((bsz, A_WIDTH, seq), F32),
                   jax.ShapeDtypeStruct((bsz, N_GROUPS, LANES, seq), BF16)],
        grid=(bsz, N_GROUPS, nq),
        in_specs=[
            pl.BlockSpec((None, gw, tq), tmap),
            pl.BlockSpec((None, None, ncp, LANES), kvmap),
            pl.BlockSpec((None, None, LANES, ncp), kvmap),
            pl.BlockSpec(ovt.shape, lambda b, g, i: (0, 0)),
            pl.BlockSpec((GATE_ROWS, tq), gmap),
        ],
        out_specs=[pl.BlockSpec((None, gw, tq), tmap),
                   pl.BlockSpec((None, None, LANES, tq), lambda b, g, i: (b, g, 0, i))],
        compiler_params=_cparams(("parallel", "parallel", "parallel")),
        name="cmp_select",
    )(qt, kc, vct, ovt, gates_t)


_SLC_UNROLL = 2


def _slc_kernel(qt_ref, sbt_ref, k_ref, vt_ref, gt_ref, prev_ref, o_ref,
                qa_ref, s_ref, p_ref, acc_ref, *, tq):
    i = pl.program_id(2)
    rows = HEADS_PER_GROUP * tq
    qa_ref[...] = jnp.concatenate([jnp.concatenate([sbt_ref[...]] * HEADS_PER_GROUP, axis=1),
                                   jnp.concatenate(_head_columns(qt_ref), axis=1)], axis=0)
    s_ref[...] = jnp.full(s_ref.shape, -jnp.inf, F32)
    p_ref[...] = jnp.zeros(p_ref.shape, BF16)
    acc_ref[...] = jnp.zeros(acc_ref.shape, F32)

    def accumulate(j, alpha):
        j = jnp.maximum(j, 0)
        pv = _dot(vt_ref[:, pl.ds(pl.multiple_of(j * tq, tq), tq)], p_ref[...])
        acc_ref[...] = alpha * acc_ref[...] + pv

    def probs(m):
        s = s_ref[...]
        m_new = jnp.maximum(m, jnp.max(s, axis=0, keepdims=True))
        p_ref[...] = jnp.exp(s - m_new).astype(BF16)
        return m_new, jnp.exp(m - m_new)

    def scores(j, masked):
        s = _dot(k_ref[pl.ds(pl.multiple_of(j * tq, tq), tq), :], qa_ref[...])
        if masked:
            causal = (lax.broadcasted_iota(jnp.int32, (tq, 1), 0)
                      <= (lax.broadcasted_iota(jnp.int32, (1, rows), 1) & (tq - 1)))
            s = jnp.where(causal, s, NEG)
        s_ref[...] = s

    def body(it, carry):
        m, alpha = carry
        accumulate(it - 2, alpha)
        m, alpha = probs(m)
        scores(it, False)
        return m, alpha

    def steps(first, count, carry):
        for k in range(count):
            carry = body(first + k, carry)
        return carry

    carry = (jnp.full((1, rows), -3e38, F32), jnp.ones((1, rows), F32))
    trips = i >> int(math.log2(_SLC_UNROLL))
    carry = lax.fori_loop(0, trips, lambda k, c: steps(k * _SLC_UNROLL, _SLC_UNROLL, c), carry)
    done = trips * _SLC_UNROLL
    size = _SLC_UNROLL // 2
    while size >= 1:
        carry = lax.cond((i & size) != 0, functools.partial(steps, done, size), lambda c: c, carry)
        done = done + (i & size)
        size //= 2
    m, alpha = carry
    accumulate(i - 2, alpha)
    m, alpha = probs(m)
    scores(i, True)
    accumulate(i - 1, alpha)
    m, alpha = probs(m)
    accumulate(i, alpha)

    gate = jnp.concatenate([_gate_row(gt_ref, _BR_SLC, hd) for hd in range(HEADS_PER_GROUP)], axis=1)
    out_t = acc_ref[0:HEAD_DIM, :] * (gate / acc_ref[HEAD_DIM:HEAD_DIM + 1, :])
    for pair in range(HEADS_PER_GROUP // 2):
        for c in range(tq // LANES):
            col = 2 * pair * tq + c * LANES
            both = jnp.concatenate([out_t[:, col:col + LANES], out_t[:, col + tq:col + tq + LANES]], axis=0)
            both = both + prev_ref[pair * LANES:(pair + 1) * LANES, c * LANES:(c + 1) * LANES]
            o_ref[c * LANES:(c + 1) * LANES, pair * LANES:(pair + 1) * LANES] = both.T.astype(BF16)


def _slc(qt, sbt, k_aug, vt_aug, gates_t, prev, *, bsz, seq, tq=256):
    nq = seq // tq
    gw = HEADS_PER_GROUP * HEAD_DIM
    rows = HEADS_PER_GROUP * tq
    tmap, gmap, kvmap = _tmaps(nq)
    return pl.pallas_call(
        functools.partial(_slc_kernel, tq=tq),
        out_shape=jax.ShapeDtypeStruct((bsz * seq, A_WIDTH), BF16),
        grid=(bsz, N_GROUPS, nq),
        in_specs=[
            pl.BlockSpec((None, gw, tq), tmap),
            pl.BlockSpec((None, None, LANES, tq), lambda b, g, i: (b, g, 0, i)),
            pl.BlockSpec((None, None, seq, 2 * LANES), kvmap),
            pl.BlockSpec((None, None, None, LANES, seq), lambda b, g, i: (b, 0, g, 0, 0)),
            pl.BlockSpec((GATE_ROWS, tq), gmap),
            pl.BlockSpec((None, gw, tq), tmap),
        ],
        out_specs=pl.BlockSpec((tq, gw), lambda b, g, i: (b * nq + i, g)),
        scratch_shapes=[pltpu.VMEM((2 * LANES, rows), BF16), pltpu.VMEM((tq, rows), F32),
                        pltpu.VMEM((tq, rows), BF16), pltpu.VMEM((LANES, rows), F32)],
        compiler_params=_cparams(("parallel", "parallel", "parallel")),
        name="slc_attn",
    )(qt, sbt, k_aug, vt_aug, gates_t, prev)


def _win_kernel(qt_ref, k_ref, vt_ref, band_ref, gt_ref, prev_ref, ot_ref, *, tq):
    i = pl.program_id(2)
    span = WINDOW + tq
    start = pl.multiple_of(jnp.maximum(i - WINDOW // tq, 0) * tq, tq)
    k = k_ref[pl.ds(start, span), :]
    vt = vt_ref[:, pl.ds(start, span)]
    heads = _head_columns(qt_ref)

    def scores(hd):
        return _dot(k, heads[hd]) + band_ref[...]

    def probs(hd, s):
        return jnp.exp(s - jnp.max(s, axis=0, keepdims=True)).astype(BF16)

    def finish(hd, p):
        acc = _dot(vt, p)
        rows = slice(hd * HEAD_DIM, (hd + 1) * HEAD_DIM)
        scale = _gate_row(gt_ref, _BR_WIN, hd) / acc[HEAD_DIM:HEAD_DIM + 1, :]
        ot_ref[rows, :] = prev_ref[rows, :] + acc[:HEAD_DIM, :] * scale

    _software_pipeline(HEADS_PER_GROUP, scores, probs, finish)


def _window_band(tq):
    edge = WINDOW // tq
    v = jnp.arange(edge + 1)[:, None, None]
    kpos = jnp.maximum(v - edge, 0) * tq + jnp.arange(WINDOW + tq)[None, :, None]
    t = v * tq + jnp.arange(tq)[None, None, :]
    ok = (t - kpos >= 0) & (t - kpos < WINDOW)
    return jnp.where(ok, 0.0, NEG).astype(F32)


def _win(qt, k, vt, gates_t, prev, *, bsz, seq, tq=256):
    nq = seq // tq
    gw = HEADS_PER_GROUP * HEAD_DIM
    band = _window_band(tq)
    last = band.shape[0] - 1
    tmap, gmap, kvmap = _tmaps(nq)
    return pl.pallas_call(
        functools.partial(_win_kernel, tq=tq),
        out_shape=jax.ShapeDtypeStruct((bsz, A_WIDTH, seq), F32),
        grid=(bsz, N_GROUPS, nq),
        in_specs=[
            pl.BlockSpec((None, gw, tq), tmap),
            pl.BlockSpec((None, None, seq, LANES), kvmap),
            pl.BlockSpec((None, None, None, LANES, seq), lambda b, g, i: (b, 1, g, 0, 0)),
            pl.BlockSpec((None,) + band.shape[1:], lambda b, g, i: (jnp.minimum(i, last), 0, 0)),
            pl.BlockSpec((GATE_ROWS, tq), gmap),
            pl.BlockSpec((None, gw, tq), tmap),
        ],
        out_specs=pl.BlockSpec((None, gw, tq), tmap),
        compiler_params=_cparams(("parallel", "parallel", "parallel")),
        name="win_attn",
    )(qt, k, vt, band, gates_t, prev)


def _outproj_kernel(x_ref, oa_ref, ob_ref, w_ref, o_ref):
    y = _dot(oa_ref[...], w_ref[0:A_WIDTH, :]) + _dot(ob_ref[...], w_ref[A_WIDTH:, :])
    o_ref[...] = x_ref[...] + y


def _outproj(x, oa, ob, w, *, tm=512):
    n, d = x.shape
    row = lambda i: (i, 0)
    return pl.pallas_call(
        _outproj_kernel,
        out_shape=jax.ShapeDtypeStruct((n, d), F32),
        grid=(n // tm,),
        in_specs=[
            pl.BlockSpec((tm, d), row),
            pl.BlockSpec((tm, A_WIDTH), row),
            pl.BlockSpec((tm, A_WIDTH), row),
            pl.BlockSpec(w.shape, lambda i: (0, 0)),
        ],
        out_specs=pl.BlockSpec((tm, d), row),
        compiler_params=_cparams(("parallel",)),
        name="outproj",
    )(x, oa, ob, w)


def _pool_kernel(x_ref, halo_ref, g_ref, w_ref, sc_ref, o_ref, ext_ref, *, tm, tiles_per_batch):
    i = pl.program_id(0)
    first = i % tiles_per_batch == 0
    x = x_ref[...]
    h = _rms(x, g_ref[...])
    hh = _rms(halo_ref[...], g_ref[...])
    ext_ref[0:POOL_HALO, :] = jnp.where(first, 0.0, hh)
    ext_ref[POOL_HALO:POOL_HALO + tm, :] = h
    t = (i % tiles_per_batch) * tm + lax.broadcasted_iota(jnp.int32, (tm, 1), 0)
    gw = h.shape[1] // len(POOL_WINDOWS)
    for gi, win in enumerate(POOL_WINDOWS):
        c0 = gi * gw
        hg = h[:, c0:c0 + gw]
        tot = hg
        for sft in range(1, win):
            tot = tot + ext_ref[pl.ds(POOL_HALO - sft, tm), c0:c0 + gw]
        cnt = jnp.minimum(t + 1, win).astype(F32)
        y = _dot((tot / cnt - hg).astype(BF16), w_ref[gi])
        o_ref[:, c0:c0 + gw] = x[:, c0:c0 + gw] + y * sc_ref[:, c0:c0 + gw]


def _pool(x, g, w, scale, *, seq, tm=512):
    n, d = x.shape
    hb = tm // POOL_HALO
    return pl.pallas_call(
        functools.partial(_pool_kernel, tm=tm, tiles_per_batch=seq // tm),
        out_shape=jax.ShapeDtypeStruct((n, d), F32),
        grid=(n // tm,),
        in_specs=[
            pl.BlockSpec((tm, d), lambda i: (i, 0)),
            pl.BlockSpec((POOL_HALO, d), lambda i: (jnp.maximum(i * hb - 1, 0), 0)),
            pl.BlockSpec((1, d), lambda i: (0, 0)),
            pl.BlockSpec(w.shape, lambda i: (0, 0, 0)),
            pl.BlockSpec((1, d), lambda i: (0, 0)),
        ],
        out_specs=pl.BlockSpec((tm, d), lambda i: (i, 0)),
        scratch_shapes=[pltpu.VMEM((tm + POOL_HALO, d), F32)],
        compiler_params=_cparams(("parallel",)),
        name="pool_mixer",
    )(x, x, g.reshape(1, d), w, scale.reshape(1, d))


def _pack_w_in(w_in):
    o = 0
    q = w_in[:, o:o + A_WIDTH]; o += A_WIDTH
    kc, vc, ks, vs, kw, vw = [w_in[:, o + k * KV_WIDTH:o + (k + 1) * KV_WIDTH] for k in range(6)]
    o += 6 * KV_WIDTH
    gates = w_in[:, o:o + N_BRANCH * N_HEADS]; o += N_BRANCH * N_HEADS
    u, gb, gc = [w_in[:, o + k * A_WIDTH:o + (k + 1) * A_WIDTH] for k in range(3)]
    gates = gates.reshape(-1, N_GROUPS, HEADS_PER_GROUP, N_BRANCH).transpose(0, 1, 3, 2)
    gates = gates.reshape(-1, N_GROUPS, N_BRANCH * HEADS_PER_GROUP)
    gates = jnp.pad(gates, ((0, 0), (0, 0), (0, LANES - N_BRANCH * HEADS_PER_GROUP))).reshape(-1, GATE_WIDTH)
    return jnp.concatenate([q, ks, kw, vs, vw, kc, vc, gates, u, gb, gc], axis=1).astype(BF16)


def _group_mean_matrix(width):
    r = jnp.arange(width) // HEAD_DIM
    return jnp.where(r[:, None] == r[None, :], 1.0 / HEAD_DIM, 0.0).astype(BF16)


def _rope_freq_row():
    lane = jnp.arange(LANES) % HEAD_DIM
    freqs = ROPE_THETA ** (-jnp.arange(0, ROPE_DIM, 2, dtype=F32) / ROPE_DIM)
    return jnp.where(lane < ROPE_DIM, freqs[lane % ROPE_HALF], 0.0).astype(F32).reshape(1, LANES)


def _per_group(a, bsz, rows):
    return a.reshape(bsz, rows, N_GROUPS, HEAD_DIM).transpose(0, 2, 1, 3)


def _hybrid_mixer(x, positions, g, w_in, q_norm, k_norm, cmp_pos, cmp_w1, cmp_w2, conv_w, w_out, *, bsz, seq):
    n = bsz * seq
    ncp = seq // CMP_STRIDE
    freq = _rope_freq_row()
    pq = _group_mean_matrix(A_WIDTH)
    pk = _group_mean_matrix(KV_WIDTH)
    qnw = jnp.tile(q_norm, N_HEADS).reshape(1, A_WIDTH)
    knw = jnp.tile(k_norm, (1, N_GROUPS))

    q_t, k_slc, k_win, v_t, kc_raw, vc_raw, gates, ob = _inproj(
        x, positions.reshape(n, 1), g, _pack_w_in(w_in), qnw, knw, conv_w, pq, pk, freq, seq=seq)

    eye_g = jnp.eye(N_GROUPS, dtype=F32)
    seg = CMP_STRIDE * KV_WIDTH

    def seg_weights(w1):
        w1r = w1.reshape(CMP_BLOCK, HEAD_DIM, CMP_HIDDEN)
        halves = [jnp.einsum('jdh,gk->jgdkh', w1r[a:a + CMP_STRIDE], eye_g).reshape(seg, N_GROUPS * CMP_HIDDEN)
                  for a in (0, CMP_STRIDE)]
        return jnp.stack(halves).astype(BF16)

    def seg_pos(pe):
        return [jnp.broadcast_to(pe[a:a + CMP_STRIDE, None, :], (CMP_STRIDE, N_GROUPS, HEAD_DIM)).reshape(seg)
                for a in (0, CMP_STRIDE)]

    pe = jnp.stack(seg_pos(cmp_pos[0]) + seg_pos(cmp_pos[1]))
    w2 = jnp.stack([jnp.einsum('hd,gk->ghkd', cmp_w2[a], eye_g).reshape(N_GROUPS * CMP_HIDDEN, KV_WIDTH)
                    for a in range(2)]).astype(BF16)
    blk_end = jnp.minimum(jnp.arange(ncp) * CMP_STRIDE + CMP_BLOCK - 1, seq - 1)
    pos_c = positions[:, blk_end].reshape(bsz, ncp, 1)
    kc_g, vc_t = _compress(kc_raw, vc_raw, pos_c, pe, seg_weights(cmp_w1[0]), seg_weights(cmp_w1[1]), w2[0], w2[1].T,
                           knw, pk, freq, seq=seq)

    gates_t = gates.reshape(n, N_GROUPS, LANES)[:, :, :GATE_ROWS].transpose(1, 2, 0).reshape(N_GROUPS * GATE_ROWS, n)
    cstart = jnp.arange(ncp) * CMP_STRIDE
    sstart = jnp.arange(LANES) * SEL_BLOCK
    ovt = ((cstart[None, :] < sstart[:, None] + SEL_BLOCK)
           & (cstart[None, :] + CMP_BLOCK > sstart[:, None])).astype(BF16)
    o_t, sel_bias = _cmp_select(q_t, kc_g, vc_t, ovt, gates_t, bsz=bsz, seq=seq)

    o_t = _win(q_t, k_win, v_t, gates_t, o_t, bsz=bsz, seq=seq)
    o_a = _slc(q_t, sel_bias, k_slc, v_t, gates_t, o_t, bsz=bsz, seq=seq)

    return _outproj(x, o_a, ob, w_out.astype(BF16))


def kernel(x, positions, ffn_norm, ffn_w_gate, ffn_w_up, ffn_w_down, mix_norm, hyb_w_in, hyb_q_norm, hyb_k_norm,
           hyb_cmp_pos, hyb_cmp_w1, hyb_cmp_w2, hyb_conv_w, hyb_w_out, pool_w, pool_scale):
    bsz, seq, d = x.shape
    depth = ffn_norm.shape[0]
    assert seq % 2048 == 0 and seq // SEL_BLOCK <= LANES and seq // SEL_BLOCK >= SEL_TOPK
    wg, wu, wd = (w.astype(BF16) for w in (ffn_w_gate, ffn_w_up, ffn_w_down))
    xf = x.reshape(bsz * seq, d)
    for layer in range(depth):
        xf = _ffn(xf, ffn_norm[layer, 0], wg, wu, wd, layer, 0)
        i = layer // 2
        if layer % 2 == 0:
            xf = _hybrid_mixer(xf, positions, mix_norm[layer], hyb_w_in[i], hyb_q_norm[i], hyb_k_norm[i],
                               hyb_cmp_pos[i], hyb_cmp_w1[i], hyb_cmp_w2[i], hyb_conv_w[i], hyb_w_out[i],
                               bsz=bsz, seq=seq)
        else:
            xf = _pool(xf, mix_norm[layer], pool_w[i].astype(BF16), pool_scale[i], seq=seq)
        xf = _ffn(xf, ffn_norm[layer, 1], wg, wu, wd, layer, 1)
    return xf.reshape(bsz, seq, d)
```

```python
import functools
import math

import jax
import jax.numpy as jnp
from jax import lax
from jax.experimental import pallas as pl
from jax.experimental.pallas import tpu as pltpu

F32 = jnp.float32
BF16 = jnp.bfloat16

HEAD_DIM = 64
N_GROUPS = 2
HEADS_PER_GROUP = 4
N_HEADS = N_GROUPS * HEADS_PER_GROUP
A_WIDTH = N_HEADS * HEAD_DIM
KV_WIDTH = N_GROUPS * HEAD_DIM
ROPE_DIM = HEAD_DIM // 4
ROPE_HALF = ROPE_DIM // 2
ROPE_THETA = 500000.0
CMP_BLOCK = 32
CMP_STRIDE = 16
CMP_HIDDEN = 2 * HEAD_DIM
SEL_BLOCK = 64
SEL_TOPK = 16
WINDOW = 512
N_BRANCH = 3
_BR_CMP, _BR_SLC, _BR_WIN = 0, 1, 2
CONV_WIDTH = 3
POOL_WINDOWS = (2, 4, 8, 16)
POOL_HALO = 16
EPS = 1e-6
NEG = -1e30

LANES = 128
VMEM_LIMIT_BYTES = 56 * 1024 * 1024


def _cparams(sem):
    return pltpu.CompilerParams(dimension_semantics=sem, vmem_limit_bytes=VMEM_LIMIT_BYTES)


def _dot(a, b):
    return jnp.dot(a, b, preferred_element_type=F32)


def _dot_nt(a, b):
    return lax.dot_general(a, b, (((1,), (1,)), ((), ())), preferred_element_type=F32)


def _dot_split(a, b):
    hi = a.astype(BF16)
    lo = (a - hi.astype(F32)).astype(BF16)
    return _dot(hi, b) + _dot(lo, b)


def _rms(x, g):
    ms = jnp.mean(x * x, axis=-1, keepdims=True)
    return x * lax.rsqrt(ms + EPS) * g


def _group_rms(x, g, pmat):
    ms = _dot_split(x * x, pmat)
    return x * lax.rsqrt(ms + EPS) * g


def _rope_tables(pos_col, freq_row):
    ang = pos_col * freq_row
    lane = lax.broadcasted_iota(jnp.int32, (1, LANES), 1) & (HEAD_DIM - 1)
    sign = jnp.where(lane < ROPE_HALF, -1.0, 1.0).astype(F32)
    return jnp.cos(ang), jnp.sin(ang) * sign


def _rope(x, cos_t, sin_t):
    w = x.shape[1]
    reps = w // LANES
    if reps > 1:
        cos_t = jnp.concatenate([cos_t] * reps, axis=1)
        sin_t = jnp.concatenate([sin_t] * reps, axis=1)
    lane = lax.broadcasted_iota(jnp.int32, (1, w), 1) & (HEAD_DIM - 1)
    partner = jnp.where(lane < ROPE_HALF,
                        pltpu.roll(x, w - ROPE_HALF, axis=1),
                        pltpu.roll(x, ROPE_HALF, axis=1))
    return x * cos_t + partner * sin_t


def _ffn_kernel(x_ref, xnext_ref, g_ref, wg_ref, wu_ref, wd_ref, o_ref, h_ref, acc_ref, *, nj):
    i, j = pl.program_id(0), pl.program_id(1)
    slot = i % 2

    @pl.when((i == 0) & (j == 0))
    def _():
        h_ref[0] = _rms(x_ref[...], g_ref[...]).astype(BF16)

    def step(first, last):
        h = h_ref[slot]
        a = _dot(h, wg_ref[...])
        b = _dot(h, wu_ref[...])
        act = a * (1.0 / (1.0 + jnp.exp(-a))) * b
        y = _dot(act.astype(BF16), wd_ref[...])
        if not first:
            y = acc_ref[...] + y
        if last:
            h_ref[1 - slot] = _rms(xnext_ref[...], g_ref[...]).astype(BF16)
            o_ref[...] = x_ref[...] + 0.5 * y
        else:
            acc_ref[...] = y

    for jj in range(nj):
        pl.when(j == jj)(functools.partial(step, jj == 0, jj == nj - 1))


def _ffn(x, g, wg, wu, wd, layer, half, *, tm=512, tf=2816):
    n, d = x.shape
    dff = wg.shape[-1]
    ni, nj = n // tm, dff // tf
    wmode = dict(pipeline_mode=pl.Buffered(1)) if nj == 1 else {}
    return pl.pallas_call(
        functools.partial(_ffn_kernel, nj=nj),
        out_shape=jax.ShapeDtypeStruct((n, d), F32),
        grid=(ni, nj),
        in_specs=[
            pl.BlockSpec((tm, d), lambda i, j: (i, 0)),
            pl.BlockSpec((tm, d), lambda i, j: (jnp.minimum(i + 1, ni - 1), 0)),
            pl.BlockSpec((1, d), lambda i, j: (0, 0)),
            pl.BlockSpec((None, None, d, tf), lambda i, j: (layer, half, 0, j), **wmode),
            pl.BlockSpec((None, None, d, tf), lambda i, j: (layer, half, 0, j), **wmode),
            pl.BlockSpec((None, None, tf, d), lambda i, j: (layer, half, j, 0), **wmode),
        ],
        out_specs=pl.BlockSpec((tm, d), lambda i, j: (i, 0)),
        scratch_shapes=[pltpu.VMEM((2, tm, d), BF16), pltpu.VMEM((tm, d), F32)],
        compiler_params=_cparams(("arbitrary", "arbitrary")),
        name="ffn",
    )(x, x, g.reshape(1, d), wg, wu, wd)


GATE_WIDTH = N_GROUPS * LANES
GATE_ROWS = 16
_C_Q = 0
_C_K = _C_Q + A_WIDTH
_C_V = _C_K + 2 * KV_WIDTH
_C_C = _C_V + 2 * KV_WIDTH
_C_GATE = _C_C + 2 * KV_WIDTH
_C_U = _C_GATE + GATE_WIDTH
_C_GB = _C_U + A_WIDTH
_C_GC = _C_GB + A_WIDTH
_C_END = _C_GC + A_WIDTH
CONV_HALO = 8


def _inproj_kernel(x_ref, pos_ref, g_ref, w_ref, qnw_ref, knw_ref, cw_ref, pq_ref, pk_ref, freq_ref,
                   qt_ref, kslc_ref, kwin_ref, vt_ref, kc_ref, vc_ref, gates_ref, ob_ref,
                   vext_ref, *, tm, tiles_per_batch):
    i = pl.program_id(0)
    chunks = range(tm // LANES)

    def transposed(a, c):
        return jnp.concatenate([a[r * LANES:(r + 1) * LANES, c * LANES:(c + 1) * LANES].T for r in chunks], axis=1)

    def per_group(k):
        low = lax.broadcasted_iota(jnp.int32, (1, LANES), 1) < HEAD_DIM
        return [jnp.where(low, k, 0.0), jnp.where(low, pltpu.roll(k, HEAD_DIM, axis=1), 0.0)]

    @pl.when(i % tiles_per_batch == 0)
    def _():
        vext_ref[0:CONV_HALO, :] = jnp.zeros((CONV_HALO, A_WIDTH), F32)

    h = _rms(x_ref[...], g_ref[...]).astype(BF16)

    def proj(c0, width):
        return _dot(h, w_ref[:, c0:c0 + width])

    q_raw = proj(_C_Q, A_WIDTH)
    k2 = proj(_C_K, 2 * KV_WIDTH)
    cos_t, sin_t = _rope_tables(pos_ref[...].astype(F32), freq_ref[...])
    v = proj(_C_GC, A_WIDTH) * proj(_C_U, A_WIDTH)
    gate_b = proj(_C_GB, A_WIDTH)

    q = _group_rms(q_raw, qnw_ref[...], pq_ref[...])
    q = _rope(q, cos_t, sin_t) * (HEAD_DIM ** -0.5)
    for c in range(A_WIDTH // LANES):
        qt_ref[c * LANES:(c + 1) * LANES, :] = transposed(q, c).astype(BF16)

    tok = (i % tiles_per_batch) * tm + lax.broadcasted_iota(jnp.int32, (tm, 1), 0)
    blk_onehot = (lax.broadcasted_iota(jnp.int32, (1, LANES), 1) == (tok >> int(math.log2(SEL_BLOCK))))
    blk_onehot = jnp.where(blk_onehot, 1.0, 0.0).astype(BF16)
    ks = _rope(_group_rms(k2[:, :KV_WIDTH], knw_ref[1:2, :], pk_ref[...]), cos_t, sin_t)
    for g, kg in enumerate(per_group(ks)):
        kslc_ref[g] = jnp.concatenate([blk_onehot, kg.astype(BF16)], axis=1)
    kw = _rope(_group_rms(k2[:, KV_WIDTH:], knw_ref[2:3, :], pk_ref[...]), cos_t, sin_t)
    for g, kg in enumerate(per_group(kw)):
        kwin_ref[g] = kg.astype(BF16)

    v2 = proj(_C_V, 2 * KV_WIDTH)
    ones_rows = jnp.where(lax.broadcasted_iota(jnp.int32, (LANES - HEAD_DIM, 1), 0) == 0, 1.0, 0.0)
    ones_rows = jnp.broadcast_to(ones_rows, (LANES - HEAD_DIM, tm)).astype(BF16)
    for branch in range(2):
        vt = transposed(v2, branch).astype(BF16)
        for g in range(N_GROUPS):
            vt_ref[branch, g, 0:HEAD_DIM, :] = vt[g * HEAD_DIM:(g + 1) * HEAD_DIM, :]
            vt_ref[branch, g, HEAD_DIM:, :] = ones_rows
    c2 = proj(_C_C, 2 * KV_WIDTH)
    kc_ref[...] = c2[:, :KV_WIDTH]
    vc_ref[...] = c2[:, KV_WIDTH:]
    gate_logits = proj(_C_GATE, GATE_WIDTH)

    vext_ref[CONV_HALO:CONV_HALO + tm, :] = v
    y = (cw_ref[2:3, :] * v
         + cw_ref[1:2, :] * vext_ref[pl.ds(CONV_HALO - 1, tm), :]
         + cw_ref[0:1, :] * vext_ref[pl.ds(CONV_HALO - 2, tm), :])
    ob_ref[...] = (gate_b * y).astype(BF16)
    vext_ref[0:CONV_HALO, :] = vext_ref[tm:tm + CONV_HALO, :]
    gates_ref[...] = 1.0 / (1.0 + jnp.exp(-gate_logits))


def _inproj(x, pos_col, g, w, qnw, knw, cw, pq, pk, freq, *, seq, tm=256):
    n, d = x.shape
    bsz, tpb = n // seq, seq // tm
    row = lambda i: (i, 0)
    fixed = lambda i: (0, 0)
    widths = [(KV_WIDTH, F32), (KV_WIDTH, F32), (GATE_WIDTH, F32), (A_WIDTH, BF16)]
    out_shape = [jax.ShapeDtypeStruct((bsz, A_WIDTH, seq), BF16),
                 jax.ShapeDtypeStruct((bsz, N_GROUPS, seq, 2 * LANES), BF16),
                 jax.ShapeDtypeStruct((bsz, N_GROUPS, seq, LANES), BF16),
                 jax.ShapeDtypeStruct((bsz, 2, N_GROUPS, LANES, seq), BF16)]
    out_specs = [pl.BlockSpec((None, A_WIDTH, tm), lambda i: (i // tpb, 0, i % tpb)),
                 pl.BlockSpec((None, N_GROUPS, tm, 2 * LANES), lambda i: (i // tpb, 0, i % tpb, 0)),
                 pl.BlockSpec((None, N_GROUPS, tm, LANES), lambda i: (i // tpb, 0, i % tpb, 0)),
                 pl.BlockSpec((None, 2, N_GROUPS, LANES, tm), lambda i: (i // tpb, 0, 0, 0, i % tpb))]
    return pl.pallas_call(
        functools.partial(_inproj_kernel, tm=tm, tiles_per_batch=tpb),
        out_shape=out_shape + [jax.ShapeDtypeStruct((n, wd), dt) for wd, dt in widths],
        grid=(n // tm,),
        in_specs=[
            pl.BlockSpec((tm, d), row),
            pl.BlockSpec((tm, 1), row),
            pl.BlockSpec((1, d), fixed),
            pl.BlockSpec(w.shape, fixed),
            pl.BlockSpec(qnw.shape, fixed),
            pl.BlockSpec(knw.shape, fixed),
            pl.BlockSpec(cw.shape, fixed),
            pl.BlockSpec(pq.shape, fixed),
            pl.BlockSpec(pk.shape, fixed),
            pl.BlockSpec(freq.shape, fixed),
        ],
        out_specs=out_specs + [pl.BlockSpec((tm, wd), row) for wd, _ in widths],
        scratch_shapes=[pltpu.VMEM((tm + CONV_HALO, A_WIDTH), F32)],
        compiler_params=_cparams(("arbitrary",)),
        name="inproj",
    )(x, pos_col, g.reshape(1, d), w, qnw, knw, cw, pq, pk, freq)


def _gelu_tanh(x):
    return 0.5 * x * (1.0 + jnp.tanh(math.sqrt(2.0 / math.pi) * (x + 0.044715 * (x * x * x))))


def _compress_kernel(xk_ref, xv_ref, pos_ref, pe_ref, wk_ref, wv_ref, w2_ref, w2vt_ref, knw_ref, pk_ref, freq_ref,
                     kc_ref, vct_ref, *, ncp):
    def hidden(x_ref, pe_row, w_ref):
        x = jnp.concatenate([x_ref[pl.ds(j, ncp, stride=CMP_STRIDE), :] for j in range(CMP_STRIDE)], axis=1)
        a = _dot((x + pe_ref[pe_row:pe_row + 1, :]).astype(BF16), w_ref[0])
        b = _dot((x + pe_ref[pe_row + 1:pe_row + 2, :]).astype(BF16), w_ref[1])
        hid = a + pltpu.roll(b, ncp - 1, axis=0)
        return _gelu_tanh(hid)

    low = lax.broadcasted_iota(jnp.int32, (1, LANES), 1) < HEAD_DIM
    kc = _dot(hidden(xk_ref, 0, wk_ref).astype(BF16), w2_ref[...])
    kc = _group_rms(kc, knw_ref[0:1, :], pk_ref[...])
    cos_t, sin_t = _rope_tables(pos_ref[...].astype(F32), freq_ref[...])
    kc = _rope(kc, cos_t, sin_t)
    kc_ref[0] = jnp.where(low, kc, 0.0).astype(BF16)
    kc_ref[1] = jnp.where(low, pltpu.roll(kc, HEAD_DIM, axis=1), 0.0).astype(BF16)
    hv = hidden(xv_ref, 2, wv_ref)
    hv_t = jnp.concatenate(
        [jnp.concatenate([hv[r * LANES:(r + 1) * LANES, c * LANES:(c + 1) * LANES].T for r in range(ncp // LANES)],
                         axis=1) for c in range(hv.shape[1] // LANES)], axis=0)
    vct = _dot(w2vt_ref[...], hv_t.astype(BF16))
    for g in range(N_GROUPS):
        vct_ref[g, 0:HEAD_DIM, :] = vct[g * HEAD_DIM:(g + 1) * HEAD_DIM, :].astype(BF16)
        vct_ref[g, HEAD_DIM:, :] = jnp.zeros((LANES - HEAD_DIM, ncp), BF16)


def _compress(xk, xv, pos_c, pe, wk, wv, w2k, w2vt, knw, pk, freq, *, seq):
    bsz, ncp = pos_c.shape[:2]
    bat = lambda b: (b, 0, 0)
    fix2 = lambda b: (0, 0)
    fix3 = lambda b: (0, 0, 0)
    return pl.pallas_call(
        functools.partial(_compress_kernel, ncp=ncp),
        out_shape=[jax.ShapeDtypeStruct((bsz, N_GROUPS, ncp, LANES), BF16),
                   jax.ShapeDtypeStruct((bsz, N_GROUPS, LANES, ncp), BF16)],
        grid=(bsz,),
        in_specs=[
            pl.BlockSpec((seq, KV_WIDTH), lambda b: (b, 0)),
            pl.BlockSpec((seq, KV_WIDTH), lambda b: (b, 0)),
            pl.BlockSpec((None, ncp, 1), bat),
            pl.BlockSpec(pe.shape, fix2),
            pl.BlockSpec(wk.shape, fix3),
            pl.BlockSpec(wv.shape, fix3),
            pl.BlockSpec(w2k.shape, fix2),
            pl.BlockSpec(w2vt.shape, fix2),
            pl.BlockSpec(knw.shape, fix2),
            pl.BlockSpec(pk.shape, fix2),
            pl.BlockSpec(freq.shape, fix2),
        ],
        out_specs=[pl.BlockSpec((None, N_GROUPS, ncp, LANES), lambda b: (b, 0, 0, 0)),
                   pl.BlockSpec((None, N_GROUPS, LANES, ncp), lambda b: (b, 0, 0, 0))],
        compiler_params=_cparams(("parallel",)),
        name="compress",
    )(xk, xv, pos_c, pe, wk, wv, w2k, w2vt, knw, pk, freq)


def _head_columns(qt_ref):
    cols = []
    for pair in range(HEADS_PER_GROUP // 2):
        both = qt_ref[pair * LANES:(pair + 1) * LANES, :]
        cols += [both, jnp.concatenate([both[HEAD_DIM:, :], both[:HEAD_DIM, :]], axis=0)]
    return cols


def _software_pipeline(n, scores, probs, finish):
    s, p = {}, {}
    for step in range(n + 2):
        if step >= 2:
            finish(step - 2, p.pop(step - 2))
        if 1 <= step <= n:
            p[step - 1] = probs(step - 1, s.pop(step - 1))
        if step < n:
            s[step] = scores(step)


def _gate_row(gt_ref, branch, hd):
    r = branch * HEADS_PER_GROUP + hd
    return gt_ref[r:r + 1, :]


def _cmp_kernel(qt_ref, kc_ref, vct_ref, ovt_ref, gt_ref, ot_ref, sbt_ref, *, tq, ncp):
    i = pl.program_id(2)
    t = i * tq + lax.broadcasted_iota(jnp.int32, (1, tq), 1)
    any_valid = (t >= CMP_BLOCK - 1).astype(F32)
    heads = _head_columns(qt_ref)
    cmp_per_sel = SEL_BLOCK // CMP_STRIDE

    def run(nrow):
        nblk = nrow // cmp_per_sel
        blk_end = lax.broadcasted_iota(jnp.int32, (nrow, 1), 0) * CMP_STRIDE + (CMP_BLOCK - 1)
        valid = blk_end <= t
        kc = kc_ref[0:nrow, :]
        vct = vct_ref[:, 0:nrow]
        psum_parts = []

        def scores(hd):
            return jnp.where(valid, _dot(kc, heads[hd]), NEG)

        def probs(hd, s):
            e = jnp.exp(s - jnp.max(s, axis=0, keepdims=True))
            return e * (any_valid / jnp.sum(e, axis=0, keepdims=True))

        def finish(hd, p):
            o_t = _dot(vct, p.astype(BF16))[:HEAD_DIM, :]
            ot_ref[hd * HEAD_DIM:(hd + 1) * HEAD_DIM, :] = o_t * _gate_row(gt_ref, _BR_CMP, hd)
            psum_parts.append(p)

        _software_pipeline(HEADS_PER_GROUP, scores, probs, finish)
        psum = (psum_parts[0] + psum_parts[1]) + (psum_parts[2] + psum_parts[3])

        hi = psum.astype(BF16)
        lo = (psum - hi.astype(F32)).astype(BF16)
        ovt = ovt_ref[0:nblk, 0:nrow]
        imp = _dot(ovt, hi) + _dot(ovt, lo)
        blk = lax.broadcasted_iota(jnp.int32, (nblk, 1), 0)
        sel_start = blk * SEL_BLOCK
        cur = (t >> int(math.log2(SEL_BLOCK))) << int(math.log2(SEL_BLOCK))
        imp = jnp.where(sel_start <= t, imp, -1.0)
        imp = jnp.where(sel_start == cur, 1e4, jnp.where(sel_start == 0, 1e4, imp))

        blk_f = blk.astype(F32)
        bias = jnp.full((nblk, tq), NEG, F32)
        for _ in range(SEL_TOPK):
            mx = jnp.max(imp, axis=0, keepdims=True)
            first = jnp.min(jnp.where(imp == mx, blk_f, float(LANES)), axis=0, keepdims=True)
            pick = blk_f == first
            bias = jnp.where(pick, 0.0, bias)
            imp = jnp.where(pick, -3e38, imp)
        sbt_ref[0:nblk, :] = bias.astype(BF16)
        if nblk < LANES:
            sbt_ref[nblk:, :] = jnp.full((LANES - nblk, tq), NEG, BF16)

    sizes = [LANES * (v + 1) for v in range(ncp // LANES)]
    assert sizes[0] // cmp_per_sel >= SEL_TOPK
    tokens_per_variant = LANES * CMP_STRIDE
    lax.switch((i * tq + tq - 1) // tokens_per_variant, [functools.partial(run, nrow) for nrow in sizes])


def _tmaps(nq):
    return (lambda b, g, i: (b, g, i)), (lambda b, g, i: (g, b * nq + i)), (lambda b, g, i: (b, g, 0, 0))


def _cmp_select(qt, kc, vct, ovt, gates_t, *, bsz, seq, tq=256):
    ncp = kc.shape[2]
    nq = seq // tq
    gw = HEADS_PER_GROUP * HEAD_DIM
    tmap, gmap, kvmap = _tmaps(nq)
    return pl.pallas_call(
        functools.partial(_cmp_kernel, tq=tq, ncp=ncp),
        out_shape=[jax.ShapeDtypeStruct((bsz, A_WIDTH, seq), F32),
                   jax.ShapeDtypeStruct((bsz, N_GROUPS, LANES, seq), BF16)],
        grid=(bsz, N_GROUPS, nq),
        in_specs=[
            pl.BlockSpec((None, gw, tq), tmap),
            pl.BlockSpec((None, None, ncp, LANES), kvmap),
            pl.BlockSpec((None, None, LANES, ncp), kvmap),
            pl.BlockSpec(ovt.shape, lambda b, g, i: (0, 0)),
            pl.BlockSpec((GATE_ROWS, tq), gmap),
        ],
        out_specs=[pl.BlockSpec((None, gw, tq), tmap),
                   pl.BlockSpec((None, None, LANES, tq), lambda b, g, i: (b, g, 0, i))],
        compiler_params=_cparams(("parallel", "parallel", "parallel")),
        name="cmp_select",
    )(qt, kc, vct, ovt, gates_t)


_SLC_UNROLL = 2


def _slc_kernel(qt_ref, sbt_ref, k_ref, vt_ref, gt_ref, prev_ref, o_ref,
                qa_ref, s_ref, p_ref, acc_ref, *, tq):
    i = pl.program_id(2)
    rows = HEADS_PER_GROUP * tq
    qa_ref[...] = jnp.concatenate([jnp.concatenate([sbt_ref[...]] * HEADS_PER_GROUP, axis=1),
                                   jnp.concatenate(_head_columns(qt_ref), axis=1)], axis=0)
    p_ref[...] = jnp.zeros(p_ref.shape, BF16)
    acc_ref[...] = jnp.zeros(acc_ref.shape, F32)

    def accumulate(unit, alpha):
        j = jnp.where(unit <= 0, i, unit - 1)
        pv = _dot(vt_ref[:, pl.ds(pl.multiple_of(j * tq, tq), tq)], p_ref[...])
        acc_ref[...] = alpha * acc_ref[...] + pv

    def probs(m):
        s = s_ref[...]
        m_new = jnp.maximum(m, jnp.max(s, axis=0, keepdims=True))
        p_ref[...] = jnp.exp(s - m_new).astype(BF16)
        return m_new, jnp.exp(m - m_new)

    def scores(j, masked):
        s = _dot(k_ref[pl.ds(pl.multiple_of(j * tq, tq), tq), :], qa_ref[...])
        if masked:
            causal = (lax.broadcasted_iota(jnp.int32, (tq, 1), 0)
                      <= (lax.broadcasted_iota(jnp.int32, (1, rows), 1) & (tq - 1)))
            s = jnp.where(causal, s, NEG)
        s_ref[...] = s

    def body(k, carry):
        m, alpha = carry
        accumulate(k - 1, alpha)
        m, alpha = probs(m)
        scores(k, False)
        return m, alpha

    def steps(first, count, carry):
        for k in range(count):
            carry = body(first + k, carry)
        return carry

    scores(i, True)
    carry = (jnp.full((1, rows), -3e38, F32), jnp.ones((1, rows), F32))
    trips = i >> int(math.log2(_SLC_UNROLL))
    carry = lax.fori_loop(0, trips, lambda k, c: steps(k * _SLC_UNROLL, _SLC_UNROLL, c), carry)
    done = trips * _SLC_UNROLL
    size = _SLC_UNROLL // 2
    while size >= 1:
        carry = lax.cond((i & size) != 0, functools.partial(steps, done, size), lambda c: c, carry)
        done = done + (i & size)
        size //= 2
    m, alpha = carry
    accumulate(i - 1, alpha)
    m, alpha = probs(m)
    accumulate(i, alpha)

    gate = jnp.concatenate([_gate_row(gt_ref, _BR_SLC, hd) for hd in range(HEADS_PER_GROUP)], axis=1)
    out_t = acc_ref[0:HEAD_DIM, :] * (gate / acc_ref[HEAD_DIM:HEAD_DIM + 1, :])
    for pair in range(HEADS_PER_GROUP // 2):
        for c in range(tq // LANES):
            col = 2 * pair * tq + c * LANES
            both = jnp.concatenate([out_t[:, col:col + LANES], out_t[:, col + tq:col + tq + LANES]], axis=0)
            both = both + prev_ref[pair * LANES:(pair + 1) * LANES, c * LANES:(c + 1) * LANES]
            o_ref[c * LANES:(c + 1) * LANES, pair * LANES:(pair + 1) * LANES] = both.T.astype(BF16)


def _slc(qt, sbt, k_aug, vt_aug, gates_t, prev, *, bsz, seq, tq=512):
    nq = seq // tq
    gw = HEADS_PER_GROUP * HEAD_DIM
    rows = HEADS_PER_GROUP * tq
    tmap, gmap, kvmap = _tmaps(nq)
    return pl.pallas_call(
        functools.partial(_slc_kernel, tq=tq),
        out_shape=jax.ShapeDtypeStruct((bsz * seq, A_WIDTH), BF16),
        grid=(bsz, N_GROUPS, nq),
        in_specs=[
            pl.BlockSpec((None, gw, tq), tmap),
            pl.BlockSpec((None, None, LANES, tq), lambda b, g, i: (b, g, 0, i)),
            pl.BlockSpec((None, None, seq, 2 * LANES), kvmap),
            pl.BlockSpec((None, None, None, LANES, seq), lambda b, g, i: (b, 0, g, 0, 0)),
            pl.BlockSpec((GATE_ROWS, tq), gmap),
            pl.BlockSpec((None, gw, tq), tmap),
        ],
        out_specs=pl.BlockSpec((tq, gw), lambda b, g, i: (b * nq + i, g)),
        scratch_shapes=[pltpu.VMEM((2 * LANES, rows), BF16), pltpu.VMEM((tq, rows), F32),
                        pltpu.VMEM((tq, rows), BF16), pltpu.VMEM((LANES, rows), F32)],
        compiler_params=_cparams(("parallel", "parallel", "parallel")),
        name="slc_attn",
    )(qt, sbt, k_aug, vt_aug, gates_t, prev)


def _win_kernel(qt_ref, k_ref, vt_ref, band_ref, gt_ref, prev_ref, ot_ref, *, tq):
    i = pl.program_id(2)
    span = WINDOW + tq
    start = pl.multiple_of(jnp.maximum(i - WINDOW // tq, 0) * tq, tq)
    k = k_ref[pl.ds(start, span), :]
    vt = vt_ref[:, pl.ds(start, span)]
    heads = _head_columns(qt_ref)

    def scores(hd):
        return _dot(k, heads[hd]) + band_ref[...]

    def probs(hd, s):
        return jnp.exp(s - jnp.max(s, axis=0, keepdims=True)).astype(BF16)

    def finish(hd, p):
        acc = _dot(vt, p)
        rows = slice(hd * HEAD_DIM, (hd + 1) * HEAD_DIM)
        scale = _gate_row(gt_ref, _BR_WIN, hd) / acc[HEAD_DIM:HEAD_DIM + 1, :]
        ot_ref[rows, :] = prev_ref[rows, :] + acc[:HEAD_DIM, :] * scale

    _software_pipeline(HEADS_PER_GROUP, scores, probs, finish)


def _window_band(tq):
    edge = WINDOW // tq
    v = jnp.arange(edge + 1)[:, None, None]
    kpos = jnp.maximum(v - edge, 0) * tq + jnp.arange(WINDOW + tq)[None, :, None]
    t = v * tq + jnp.arange(tq)[None, None, :]
    ok = (t - kpos >= 0) & (t - kpos < WINDOW)
    return jnp.where(ok, 0.0, NEG).astype(F32)


def _win(qt, k, vt, gates_t, prev, *, bsz, seq, tq=256):
    nq = seq // tq
    gw = HEADS_PER_GROUP * HEAD_DIM
    band = _window_band(tq)
    last = band.shape[0] - 1
    tmap, gmap, kvmap = _tmaps(nq)
    return pl.pallas_call(
        functools.partial(_win_kernel, tq=tq),
        out_shape=jax.ShapeDtypeStruct((bsz, A_WIDTH, seq), F32),
        grid=(bsz, N_GROUPS, nq),
        in_specs=[
            pl.BlockSpec((None, gw, tq), tmap),
            pl.BlockSpec((None, None, seq, LANES), kvmap),
            pl.BlockSpec((None, None, None, LANES, seq), lambda b, g, i: (b, 1, g, 0, 0)),
            pl.BlockSpec((None,) + band.shape[1:], lambda b, g, i: (jnp.minimum(i, last), 0, 0)),
            pl.BlockSpec((GATE_ROWS, tq), gmap),
            pl.BlockSpec((None, gw, tq), tmap),
        ],
        out_specs=pl.BlockSpec((None, gw, tq), tmap),
        compiler_params=_cparams(("parallel", "parallel", "parallel")),
        name="win_attn",
    )(qt, k, vt, band, gates_t, prev)


def _outproj_kernel(x_ref, oa_ref, ob_ref, w_ref, o_ref):
    y = _dot(oa_ref[...], w_ref[0:A_WIDTH, :]) + _dot(ob_ref[...], w_ref[A_WIDTH:, :])
    o_ref[...] = x_ref[...] + y


def _outproj(x, oa, ob, w, *, tm=512):
    n, d = x.shape
    row = lambda i: (i, 0)
    return pl.pallas_call(
        _outproj_kernel,
        out_shape=jax.ShapeDtypeStruct((n, d), F32),
        grid=(n // tm,),
        in_specs=[
            pl.BlockSpec((tm, d), row),
            pl.BlockSpec((tm, A_WIDTH), row),
            pl.BlockSpec((tm, A_WIDTH), row),
            pl.BlockSpec(w.shape, lambda i: (0, 0)),
        ],
        out_specs=pl.BlockSpec((tm, d), row),
        compiler_params=_cparams(("parallel",)),
        name="outproj",
    )(x, oa, ob, w)


def _pool_kernel(x_ref, halo_ref, g_ref, w_ref, sc_ref, o_ref, ext_ref, *, tm, tiles_per_batch):
    i = pl.program_id(0)
    first = i % tiles_per_batch == 0
    x = x_ref[...]
    h = _rms(x, g_ref[...])
    hh = _rms(halo_ref[...], g_ref[...])
    ext_ref[0:POOL_HALO, :] = jnp.where(first, 0.0, hh)
    ext_ref[POOL_HALO:POOL_HALO + tm, :] = h
    t = (i % tiles_per_batch) * tm + lax.broadcasted_iota(jnp.int32, (tm, 1), 0)
    gw = h.shape[1] // len(POOL_WINDOWS)
    for gi, win in enumerate(POOL_WINDOWS):
        c0 = gi * gw
        hg = h[:, c0:c0 + gw]
        tot = hg
        for sft in range(1, win):
            tot = tot + ext_ref[pl.ds(POOL_HALO - sft, tm), c0:c0 + gw]
        cnt = jnp.minimum(t + 1, win).astype(F32)
        y = _dot((tot / cnt - hg).astype(BF16), w_ref[gi])
        o_ref[:, c0:c0 + gw] = x[:, c0:c0 + gw] + y * sc_ref[:, c0:c0 + gw]


def _pool(x, g, w, scale, *, seq, tm=512):
    n, d = x.shape
    hb = tm // POOL_HALO
    return pl.pallas_call(
        functools.partial(_pool_kernel, tm=tm, tiles_per_batch=seq // tm),
        out_shape=jax.ShapeDtypeStruct((n, d), F32),
        grid=(n // tm,),
        in_specs=[
            pl.BlockSpec((tm, d), lambda i: (i, 0)),
            pl.BlockSpec((POOL_HALO, d), lambda i: (jnp.maximum(i * hb - 1, 0), 0)),
            pl.BlockSpec((1, d), lambda i: (0, 0)),
            pl.BlockSpec(w.shape, lambda i: (0, 0, 0)),
            pl.BlockSpec((1, d), lambda i: (0, 0)),
        ],
        out_specs=pl.BlockSpec((tm, d), lambda i: (i, 0)),
        scratch_shapes=[pltpu.VMEM((tm + POOL_HALO, d), F32)],
        compiler_params=_cparams(("parallel",)),
        name="pool_mixer",
    )(x, x, g.reshape(1, d), w, scale.reshape(1, d))


def _pack_w_in(w_in):
    o = 0
    q = w_in[:, o:o + A_WIDTH]; o += A_WIDTH
    kc, vc, ks, vs, kw, vw = [w_in[:, o + k * KV_WIDTH:o + (k + 1) * KV_WIDTH] for k in range(6)]
    o += 6 * KV_WIDTH
    gates = w_in[:, o:o + N_BRANCH * N_HEADS]; o += N_BRANCH * N_HEADS
    u, gb, gc = [w_in[:, o + k * A_WIDTH:o + (k + 1) * A_WIDTH] for k in range(3)]
    gates = gates.reshape(-1, N_GROUPS, HEADS_PER_GROUP, N_BRANCH).transpose(0, 1, 3, 2)
    gates = gates.reshape(-1, N_GROUPS, N_BRANCH * HEADS_PER_GROUP)
    gates = jnp.pad(gates, ((0, 0), (0, 0), (0, LANES - N_BRANCH * HEADS_PER_GROUP))).reshape(-1, GATE_WIDTH)
    return jnp.concatenate([q, ks, kw, vs, vw, kc, vc, gates, u, gb, gc], axis=1).astype(BF16)


def _group_mean_matrix(width):
    r = jnp.arange(width) // HEAD_DIM
    return jnp.where(r[:, None] == r[None, :], 1.0 / HEAD_DIM, 0.0).astype(BF16)


def _rope_freq_row():
    lane = jnp.arange(LANES) % HEAD_DIM
    freqs = ROPE_THETA ** (-jnp.arange(0, ROPE_DIM, 2, dtype=F32) / ROPE_DIM)
    return jnp.where(lane < ROPE_DIM, freqs[lane % ROPE_HALF], 0.0).astype(F32).reshape(1, LANES)


def _per_group(a, bsz, rows):
    return a.reshape(bsz, rows, N_GROUPS, HEAD_DIM).transpose(0, 2, 1, 3)


def _hybrid_mixer(x, positions, g, w_in, q_norm, k_norm, cmp_pos, cmp_w1, cmp_w2, conv_w, w_out, *, bsz, seq):
    n = bsz * seq
    ncp = seq // CMP_STRIDE
    freq = _rope_freq_row()
    pq = _group_mean_matrix(A_WIDTH)
    pk = _group_mean_matrix(KV_WIDTH)
    qnw = jnp.tile(q_norm, N_HEADS).reshape(1, A_WIDTH)
    knw = jnp.tile(k_norm, (1, N_GROUPS))

    q_t, k_slc, k_win, v_t, kc_raw, vc_raw, gates, ob = _inproj(
        x, positions.reshape(n, 1), g, _pack_w_in(w_in), qnw, knw, conv_w, pq, pk, freq, seq=seq)

    eye_g = jnp.eye(N_GROUPS, dtype=F32)
    seg = CMP_STRIDE * KV_WIDTH

    def seg_weights(w1):
        w1r = w1.reshape(CMP_BLOCK, HEAD_DIM, CMP_HIDDEN)
        halves = [jnp.einsum('jdh,gk->jgdkh', w1r[a:a + CMP_STRIDE], eye_g).reshape(seg, N_GROUPS * CMP_HIDDEN)
                  for a in (0, CMP_STRIDE)]
        return jnp.stack(halves).astype(BF16)

    def seg_pos(pe):
        return [jnp.broadcast_to(pe[a:a + CMP_STRIDE, None, :], (CMP_STRIDE, N_GROUPS, HEAD_DIM)).reshape(seg)
                for a in (0, CMP_STRIDE)]

    pe = jnp.stack(seg_pos(cmp_pos[0]) + seg_pos(cmp_pos[1]))
    w2 = jnp.stack([jnp.einsum('hd,gk->ghkd', cmp_w2[a], eye_g).reshape(N_GROUPS * CMP_HIDDEN, KV_WIDTH)
                    for a in range(2)]).astype(BF16)
    blk_end = jnp.minimum(jnp.arange(ncp) * CMP_STRIDE + CMP_BLOCK - 1, seq - 1)
    pos_c = positions[:, blk_end].reshape(bsz, ncp, 1)
    kc_g, vc_t = _compress(kc_raw, vc_raw, pos_c, pe, seg_weights(cmp_w1[0]), seg_weights(cmp_w1[1]), w2[0], w2[1].T,
                           knw, pk, freq, seq=seq)

    gates_t = gates.reshape(n, N_GROUPS, LANES)[:, :, :GATE_ROWS].transpose(1, 2, 0).reshape(N_GROUPS * GATE_ROWS, n)
    cstart = jnp.arange(ncp) * CMP_STRIDE
    sstart = jnp.arange(LANES) * SEL_BLOCK
    ovt = ((cstart[None, :] < sstart[:, None] + SEL_BLOCK)
           & (cstart[None, :] + CMP_BLOCK > sstart[:, None])).astype(BF16)
    o_t, sel_bias = _cmp_select(q_t, kc_g, vc_t, ovt, gates_t, bsz=bsz, seq=seq)

    o_t = _win(q_t, k_win, v_t, gates_t, o_t, bsz=bsz, seq=seq)
    o_a = _slc(q_t, sel_bias, k_slc, v_t, gates_t, o_t, bsz=bsz, seq=seq)

    return _outproj(x, o_a, ob, w_out.astype(BF16))


def kernel(x, positions, ffn_norm, ffn_w_gate, ffn_w_up, ffn_w_down, mix_norm, hyb_w_in, hyb_q_norm, hyb_k_norm,
           hyb_cmp_pos, hyb_cmp_w1, hyb_cmp_w2, hyb_conv_w, hyb_w_out, pool_w, pool_scale):
    bsz, seq, d = x.shape
    depth = ffn_norm.shape[0]
    assert seq % 2048 == 0 and seq // SEL_BLOCK <= LANES and seq // SEL_BLOCK >= SEL_TOPK
    wg, wu, wd = (w.astype(BF16) for w in (ffn_w_gate, ffn_w_up, ffn_w_down))
    xf = x.reshape(bsz * seq, d)
    for layer in range(depth):
        xf = _ffn(xf, ffn_norm[layer, 0], wg, wu, wd, layer, 0)
        i = layer // 2
        if layer % 2 == 0:
            xf = _hybrid_mixer(xf, positions, mix_norm[layer], hyb_w_in[i], hyb_q_norm[i], hyb_k_norm[i],
                               hyb_cmp_pos[i], hyb_cmp_w1[i], hyb_cmp_w2[i], hyb_conv_w[i], hyb_w_out[i],
                               bsz=bsz, seq=seq)
        else:
            xf = _pool(xf, mix_norm[layer], pool_w[i].astype(BF16), pool_scale[i], seq=seq)
        xf = _ffn(xf, ffn_norm[layer, 1], wg, wu, wd, layer, 1)
    return xf.reshape(bsz, seq, d)
```

```python
import functools
import math

import jax
import jax.numpy as jnp
from jax import lax
from jax.experimental import pallas as pl
from jax.experimental.pallas import tpu as pltpu

F32 = jnp.float32
BF16 = jnp.bfloat16

HEAD_DIM = 64
N_GROUPS = 2
HEADS_PER_GROUP = 4
N_HEADS = N_GROUPS * HEADS_PER_GROUP
A_WIDTH = N_HEADS * HEAD_DIM
KV_WIDTH = N_GROUPS * HEAD_DIM
ROPE_DIM = HEAD_DIM // 4
ROPE_HALF = ROPE_DIM // 2
ROPE_THETA = 500000.0
CMP_BLOCK = 32
CMP_STRIDE = 16
CMP_HIDDEN = 2 * HEAD_DIM
SEL_BLOCK = 64
SEL_TOPK = 16
WINDOW = 512
N_BRANCH = 3
_BR_CMP, _BR_SLC, _BR_WIN = 0, 1, 2
CONV_WIDTH = 3
POOL_WINDOWS = (2, 4, 8, 16)
POOL_HALO = 16
_POOL_PAD = 8
EPS = 1e-6
NEG = -1e30

LANES = 128
VMEM_LIMIT_BYTES = 56 * 1024 * 1024


def _cparams(sem):
    return pltpu.CompilerParams(dimension_semantics=sem, vmem_limit_bytes=VMEM_LIMIT_BYTES)


def _dot(a, b):
    return jnp.dot(a, b, preferred_element_type=F32)


def _dot_nt(a, b):
    return lax.dot_general(a, b, (((1,), (1,)), ((), ())), preferred_element_type=F32)


def _dot_split(a, b):
    hi = a.astype(BF16)
    lo = (a - hi.astype(F32)).astype(BF16)
    return _dot(hi, b) + _dot(lo, b)


def _rms(x, g):
    ms = jnp.mean(x * x, axis=-1, keepdims=True)
    return x * lax.rsqrt(ms + EPS) * g


def _group_rms(x, g, pmat):
    ms = _dot_split(x * x, pmat)
    return x * lax.rsqrt(ms + EPS) * g


def _rope_tables(pos_col, freq_row):
    ang = pos_col * freq_row
    lane = lax.broadcasted_iota(jnp.int32, (1, LANES), 1) & (HEAD_DIM - 1)
    sign = jnp.where(lane < ROPE_HALF, -1.0, 1.0).astype(F32)
    return jnp.cos(ang), jnp.sin(ang) * sign


def _rope(x, cos_t, sin_t):
    w = x.shape[1]
    reps = w // LANES
    if reps > 1:
        cos_t = jnp.concatenate([cos_t] * reps, axis=1)
        sin_t = jnp.concatenate([sin_t] * reps, axis=1)
    lane = lax.broadcasted_iota(jnp.int32, (1, w), 1) & (HEAD_DIM - 1)
    partner = jnp.where(lane < ROPE_HALF,
                        pltpu.roll(x, w - ROPE_HALF, axis=1),
                        pltpu.roll(x, ROPE_HALF, axis=1))
    return x * cos_t + partner * sin_t


def _ffn_kernel(x_ref, xnext_ref, g_ref, wg_ref, wu_ref, wd_ref, o_ref, h_ref, acc_ref, *, nj):
    i, j = pl.program_id(0), pl.program_id(1)
    slot = i % 2

    @pl.when((i == 0) & (j == 0))
    def _():
        h_ref[0] = _rms(x_ref[...], g_ref[...]).astype(BF16)

    def step(first, last):
        h = h_ref[slot]
        a = _dot(h, wg_ref[...])
        b = _dot(h, wu_ref[...])
        act = a * (1.0 / (1.0 + jnp.exp(-a))) * b
        y = _dot(act.astype(BF16), wd_ref[...])
        if not first:
            y = acc_ref[...] + y
        if last:
            h_ref[1 - slot] = _rms(xnext_ref[...], g_ref[...]).astype(BF16)
            o_ref[...] = x_ref[...] + 0.5 * y
        else:
            acc_ref[...] = y

    for jj in range(nj):
        pl.when(j == jj)(functools.partial(step, jj == 0, jj == nj - 1))


def _ffn(x, g, wg, wu, wd, layer, half, *, tm=512, tf=2816):
    n, d = x.shape
    dff = wg.shape[-1]
    ni, nj = n // tm, dff // tf
    wmode = dict(pipeline_mode=pl.Buffered(1)) if nj == 1 else {}
    return pl.pallas_call(
        functools.partial(_ffn_kernel, nj=nj),
        out_shape=jax.ShapeDtypeStruct((n, d), F32),
        grid=(ni, nj),
        in_specs=[
            pl.BlockSpec((tm, d), lambda i, j: (i, 0)),
            pl.BlockSpec((tm, d), lambda i, j: (jnp.minimum(i + 1, ni - 1), 0)),
            pl.BlockSpec((1, d), lambda i, j: (0, 0)),
            pl.BlockSpec((None, None, d, tf), lambda i, j: (layer, half, 0, j), **wmode),
            pl.BlockSpec((None, None, d, tf), lambda i, j: (layer, half, 0, j), **wmode),
            pl.BlockSpec((None, None, tf, d), lambda i, j: (layer, half, j, 0), **wmode),
        ],
        out_specs=pl.BlockSpec((tm, d), lambda i, j: (i, 0)),
        scratch_shapes=[pltpu.VMEM((2, tm, d), BF16), pltpu.VMEM((tm, d), F32)],
        compiler_params=_cparams(("arbitrary", "arbitrary")),
        name="ffn",
    )(x, x, g.reshape(1, d), wg, wu, wd)


GATE_WIDTH = N_GROUPS * LANES
GATE_ROWS = 16
_C_Q = 0
_C_K = _C_Q + A_WIDTH
_C_V = _C_K + 2 * KV_WIDTH
_C_C = _C_V + 2 * KV_WIDTH
_C_GATE = _C_C + 2 * KV_WIDTH
_C_U = _C_GATE + GATE_WIDTH
_C_GB = _C_U + A_WIDTH
_C_GC = _C_GB + A_WIDTH
_C_END = _C_GC + A_WIDTH
CONV_HALO = 8


def _inproj_kernel(x_ref, pos_ref, g_ref, w_ref, qnw_ref, knw_ref, cw_ref, pq_ref, pk_ref, freq_ref,
                   qt_ref, kslc_ref, kwin_ref, vt_ref, kc_ref, vc_ref, gates_ref, ob_ref,
                   vext_ref, *, tm, tiles_per_batch):
    i = pl.program_id(0)
    chunks = range(tm // LANES)

    def transposed(a, c):
        return jnp.concatenate([a[r * LANES:(r + 1) * LANES, c * LANES:(c + 1) * LANES].T for r in chunks], axis=1)

    def per_group(k):
        low = lax.broadcasted_iota(jnp.int32, (1, LANES), 1) < HEAD_DIM
        return [jnp.where(low, k, 0.0), jnp.where(low, pltpu.roll(k, HEAD_DIM, axis=1), 0.0)]

    @pl.when(i % tiles_per_batch == 0)
    def _():
        vext_ref[0:CONV_HALO, :] = jnp.zeros((CONV_HALO, A_WIDTH), F32)

    h = _rms(x_ref[...], g_ref[...]).astype(BF16)

    def proj(c0, width):
        return _dot(h, w_ref[:, c0:c0 + width])

    q_raw = proj(_C_Q, A_WIDTH)
    k2 = proj(_C_K, 2 * KV_WIDTH)
    cos_t, sin_t = _rope_tables(pos_ref[...].astype(F32), freq_ref[...])
    v = proj(_C_GC, A_WIDTH) * proj(_C_U, A_WIDTH)
    gate_b = proj(_C_GB, A_WIDTH)

    q = _group_rms(q_raw, qnw_ref[...], pq_ref[...])
    q = _rope(q, cos_t, sin_t) * (HEAD_DIM ** -0.5)
    for c in range(A_WIDTH // LANES):
        qt_ref[c * LANES:(c + 1) * LANES, :] = transposed(q, c).astype(BF16)

    tok = (i % tiles_per_batch) * tm + lax.broadcasted_iota(jnp.int32, (tm, 1), 0)
    blk_onehot = (lax.broadcasted_iota(jnp.int32, (1, LANES), 1) == (tok >> int(math.log2(SEL_BLOCK))))
    blk_onehot = jnp.where(blk_onehot, 1.0, 0.0).astype(BF16)
    ks = _rope(_group_rms(k2[:, :KV_WIDTH], knw_ref[1:2, :], pk_ref[...]), cos_t, sin_t)
    for g, kg in enumerate(per_group(ks)):
        kslc_ref[g] = jnp.concatenate([blk_onehot, kg.astype(BF16)], axis=1)
    kw = _rope(_group_rms(k2[:, KV_WIDTH:], knw_ref[2:3, :], pk_ref[...]), cos_t, sin_t)
    for g, kg in enumerate(per_group(kw)):
        kwin_ref[g] = kg.astype(BF16)

    v2 = proj(_C_V, 2 * KV_WIDTH)
    ones_rows = jnp.where(lax.broadcasted_iota(jnp.int32, (LANES - HEAD_DIM, 1), 0) == 0, 1.0, 0.0)
    ones_rows = jnp.broadcast_to(ones_rows, (LANES - HEAD_DIM, tm)).astype(BF16)
    for branch in range(2):
        vt = transposed(v2, branch).astype(BF16)
        for g in range(N_GROUPS):
            vt_ref[branch, g, 0:HEAD_DIM, :] = vt[g * HEAD_DIM:(g + 1) * HEAD_DIM, :]
            vt_ref[branch, g, HEAD_DIM:, :] = ones_rows
    c2 = proj(_C_C, 2 * KV_WIDTH)
    kc_ref[...] = c2[:, :KV_WIDTH]
    vc_ref[...] = c2[:, KV_WIDTH:]
    gate_logits = proj(_C_GATE, GATE_WIDTH)

    vext_ref[CONV_HALO:CONV_HALO + tm, :] = v
    y = (cw_ref[2:3, :] * v
         + cw_ref[1:2, :] * vext_ref[pl.ds(CONV_HALO - 1, tm), :]
         + cw_ref[0:1, :] * vext_ref[pl.ds(CONV_HALO - 2, tm), :])
    ob_ref[...] = (gate_b * y).astype(BF16)
    vext_ref[0:CONV_HALO, :] = vext_ref[tm:tm + CONV_HALO, :]
    gates_ref[...] = 1.0 / (1.0 + jnp.exp(-gate_logits))


def _inproj(x, pos_col, g, w, qnw, knw, cw, pq, pk, freq, *, seq, tm=512):
    n, d = x.shape
    bsz, tpb = n // seq, seq // tm
    row = lambda i: (i, 0)
    fixed = lambda i: (0, 0)
    widths = [(KV_WIDTH, F32), (KV_WIDTH, F32), (GATE_WIDTH, F32), (A_WIDTH, BF16)]
    out_shape = [jax.ShapeDtypeStruct((bsz, A_WIDTH, seq), BF16),
                 jax.ShapeDtypeStruct((bsz, N_GROUPS, seq, 2 * LANES), BF16),
                 jax.ShapeDtypeStruct((bsz, N_GROUPS, seq, LANES), BF16),
                 jax.ShapeDtypeStruct((bsz, 2, N_GROUPS, LANES, seq), BF16)]
    out_specs = [pl.BlockSpec((None, A_WIDTH, tm), lambda i: (i // tpb, 0, i % tpb)),
                 pl.BlockSpec((None, N_GROUPS, tm, 2 * LANES), lambda i: (i // tpb, 0, i % tpb, 0)),
                 pl.BlockSpec((None, N_GROUPS, tm, LANES), lambda i: (i // tpb, 0, i % tpb, 0)),
                 pl.BlockSpec((None, 2, N_GROUPS, LANES, tm), lambda i: (i // tpb, 0, 0, 0, i % tpb))]
    return pl.pallas_call(
        functools.partial(_inproj_kernel, tm=tm, tiles_per_batch=tpb),
        out_shape=out_shape + [jax.ShapeDtypeStruct((n, wd), dt) for wd, dt in widths],
        grid=(n // tm,),
        in_specs=[
            pl.BlockSpec((tm, d), row),
            pl.BlockSpec((tm, 1), row),
            pl.BlockSpec((1, d), fixed),
            pl.BlockSpec(w.shape, fixed),
            pl.BlockSpec(qnw.shape, fixed),
            pl.BlockSpec(knw.shape, fixed),
            pl.BlockSpec(cw.shape, fixed),
            pl.BlockSpec(pq.shape, fixed),
            pl.BlockSpec(pk.shape, fixed),
            pl.BlockSpec(freq.shape, fixed),
        ],
        out_specs=out_specs + [pl.BlockSpec((tm, wd), row) for wd, _ in widths],
        scratch_shapes=[pltpu.VMEM((tm + CONV_HALO, A_WIDTH), F32)],
        compiler_params=_cparams(("arbitrary",)),
        name="inproj",
    )(x, pos_col, g.reshape(1, d), w, qnw, knw, cw, pq, pk, freq)


def _gelu_tanh(x):
    return 0.5 * x * (1.0 + jnp.tanh(math.sqrt(2.0 / math.pi) * (x + 0.044715 * (x * x * x))))


def _compress_kernel(xk_ref, xv_ref, pos_ref, pe_ref, wk_ref, wv_ref, w2_ref, w2vt_ref, knw_ref, pk_ref, freq_ref,
                     kc_ref, vct_ref, *, ncp):
    def hidden(x_ref, pe_row, w_ref):
        x = jnp.concatenate([x_ref[pl.ds(j, ncp, stride=CMP_STRIDE), :] for j in range(CMP_STRIDE)], axis=1)
        a = _dot((x + pe_ref[pe_row:pe_row + 1, :]).astype(BF16), w_ref[0])
        b = _dot((x + pe_ref[pe_row + 1:pe_row + 2, :]).astype(BF16), w_ref[1])
        hid = a + pltpu.roll(b, ncp - 1, axis=0)
        return _gelu_tanh(hid)

    low = lax.broadcasted_iota(jnp.int32, (1, LANES), 1) < HEAD_DIM
    kc = _dot(hidden(xk_ref, 0, wk_ref).astype(BF16), w2_ref[...])
    kc = _group_rms(kc, knw_ref[0:1, :], pk_ref[...])
    cos_t, sin_t = _rope_tables(pos_ref[...].astype(F32), freq_ref[...])
    kc = _rope(kc, cos_t, sin_t)
    kc_ref[0] = jnp.where(low, kc, 0.0).astype(BF16)
    kc_ref[1] = jnp.where(low, pltpu.roll(kc, HEAD_DIM, axis=1), 0.0).astype(BF16)
    hv = hidden(xv_ref, 2, wv_ref)
    hv_t = jnp.concatenate(
        [jnp.concatenate([hv[r * LANES:(r + 1) * LANES, c * LANES:(c + 1) * LANES].T for r in range(ncp // LANES)],
                         axis=1) for c in range(hv.shape[1] // LANES)], axis=0)
    vct = _dot(w2vt_ref[...], hv_t.astype(BF16))
    for g in range(N_GROUPS):
        vct_ref[g, 0:HEAD_DIM, :] = vct[g * HEAD_DIM:(g + 1) * HEAD_DIM, :].astype(BF16)
        vct_ref[g, HEAD_DIM:, :] = jnp.zeros((LANES - HEAD_DIM, ncp), BF16)


def _compress(xk, xv, pos_c, pe, wk, wv, w2k, w2vt, knw, pk, freq, *, seq):
    bsz, ncp = pos_c.shape[:2]
    bat = lambda b: (b, 0, 0)
    fix2 = lambda b: (0, 0)
    fix3 = lambda b: (0, 0, 0)
    return pl.pallas_call(
        functools.partial(_compress_kernel, ncp=ncp),
        out_shape=[jax.ShapeDtypeStruct((bsz, N_GROUPS, ncp, LANES), BF16),
                   jax.ShapeDtypeStruct((bsz, N_GROUPS, LANES, ncp), BF16)],
        grid=(bsz,),
        in_specs=[
            pl.BlockSpec((seq, KV_WIDTH), lambda b: (b, 0)),
            pl.BlockSpec((seq, KV_WIDTH), lambda b: (b, 0)),
            pl.BlockSpec((None, ncp, 1), bat),
            pl.BlockSpec(pe.shape, fix2),
            pl.BlockSpec(wk.shape, fix3),
            pl.BlockSpec(wv.shape, fix3),
            pl.BlockSpec(w2k.shape, fix2),
            pl.BlockSpec(w2vt.shape, fix2),
            pl.BlockSpec(knw.shape, fix2),
            pl.BlockSpec(pk.shape, fix2),
            pl.BlockSpec(freq.shape, fix2),
        ],
        out_specs=[pl.BlockSpec((None, N_GROUPS, ncp, LANES), lambda b: (b, 0, 0, 0)),
                   pl.BlockSpec((None, N_GROUPS, LANES, ncp), lambda b: (b, 0, 0, 0))],
        compiler_params=_cparams(("parallel",)),
        name="compress",
    )(xk, xv, pos_c, pe, wk, wv, w2k, w2vt, knw, pk, freq)


def _head_columns(qt_ref):
    cols = []
    for pair in range(HEADS_PER_GROUP // 2):
        both = qt_ref[pair * LANES:(pair + 1) * LANES, :]
        cols += [both, jnp.concatenate([both[HEAD_DIM:, :], both[:HEAD_DIM, :]], axis=0)]
    return cols


def _software_pipeline(n, scores, probs, finish):
    s, p = {}, {}
    for step in range(n + 2):
        if step >= 2:
            finish(step - 2, p.pop(step - 2))
        if 1 <= step <= n:
            p[step - 1] = probs(step - 1, s.pop(step - 1))
        if step < n:
            s[step] = scores(step)


def _gate_row(gt_ref, branch, hd):
    r = branch * HEADS_PER_GROUP + hd
    return gt_ref[r:r + 1, :]


def _cmp_kernel(qt_ref, kc_ref, vct_ref, ovt_ref, gt_ref, ot_ref, sbt_ref, *, tq, ncp):
    i = pl.program_id(2)
    t = i * tq + lax.broadcasted_iota(jnp.int32, (1, tq), 1)
    any_valid = (t >= CMP_BLOCK - 1).astype(F32)
    heads = _head_columns(qt_ref)
    cmp_per_sel = SEL_BLOCK // CMP_STRIDE

    def run(nrow):
        nblk = nrow // cmp_per_sel
        blk_end = lax.broadcasted_iota(jnp.int32, (nrow, 1), 0) * CMP_STRIDE + (CMP_BLOCK - 1)
        valid = blk_end <= t
        kc = kc_ref[0:nrow, :]
        vct = vct_ref[:, 0:nrow]
        psum_parts = []

        def scores(hd):
            return jnp.where(valid, _dot(kc, heads[hd]), NEG)

        def probs(hd, s):
            e = jnp.exp(s - jnp.max(s, axis=0, keepdims=True))
            return e * (any_valid / jnp.sum(e, axis=0, keepdims=True))

        def finish(hd, p):
            o_t = _dot(vct, p.astype(BF16))[:HEAD_DIM, :]
            ot_ref[hd * HEAD_DIM:(hd + 1) * HEAD_DIM, :] = o_t * _gate_row(gt_ref, _BR_CMP, hd)
            psum_parts.append(p)

        _software_pipeline(HEADS_PER_GROUP, scores, probs, finish)
        psum = (psum_parts[0] + psum_parts[1]) + (psum_parts[2] + psum_parts[3])

        hi = psum.astype(BF16)
        lo = (psum - hi.astype(F32)).astype(BF16)
        ovt = ovt_ref[0:nblk, 0:nrow]
        imp = _dot(ovt, hi) + _dot(ovt, lo)
        blk = lax.broadcasted_iota(jnp.int32, (nblk, 1), 0)
        sel_start = blk * SEL_BLOCK
        cur = (t >> int(math.log2(SEL_BLOCK))) << int(math.log2(SEL_BLOCK))
        imp = jnp.where(sel_start <= t, imp, -1.0)
        imp = jnp.where(sel_start == cur, 1e4, jnp.where(sel_start == 0, 1e4, imp))

        blk_f = blk.astype(F32)
        bias = jnp.full((nblk, tq), NEG, F32)
        for _ in range(SEL_TOPK):
            mx = jnp.max(imp, axis=0, keepdims=True)
            first = jnp.min(jnp.where(imp == mx, blk_f, float(LANES)), axis=0, keepdims=True)
            pick = blk_f == first
            bias = jnp.where(pick, 0.0, bias)
            imp = jnp.where(pick, -3e38, imp)
        sbt_ref[0:nblk, :] = bias.astype(BF16)
        if nblk < LANES:
            sbt_ref[nblk:, :] = jnp.full((LANES - nblk, tq), NEG, BF16)

    sizes = [LANES * (v + 1) for v in range(ncp // LANES)]
    assert sizes[0] // cmp_per_sel >= SEL_TOPK
    tokens_per_variant = LANES * CMP_STRIDE
    lax.switch((i * tq + tq - 1) // tokens_per_variant, [functools.partial(run, nrow) for nrow in sizes])


def _tmaps(nq):
    return (lambda b, g, i: (b, g, i)), (lambda b, g, i: (g, b * nq + i)), (lambda b, g, i: (b, g, 0, 0))


def _cmp_select(qt, kc, vct, ovt, gates_t, *, bsz, seq, tq=512):
    ncp = kc.shape[2]
    nq = seq // tq
    gw = HEADS_PER_GROUP * HEAD_DIM
    tmap, gmap, kvmap = _tmaps(nq)
    return pl.pallas_call(
        functools.partial(_cmp_kernel, tq=tq, ncp=ncp),
        out_shape=[jax.ShapeDtypeStruct((bsz, A_WIDTH, seq), F32),
                   jax.ShapeDtypeStruct((bsz, N_GROUPS, LANES, seq), BF16)],
        grid=(bsz, N_GROUPS, nq),
        in_specs=[
            pl.BlockSpec((None, gw, tq), tmap),
            pl.BlockSpec((None, None, ncp, LANES), kvmap),
            pl.BlockSpec((None, None, LANES, ncp), kvmap),
            pl.BlockSpec(ovt.shape, lambda b, g, i: (0, 0)),
            pl.BlockSpec((GATE_ROWS, tq), gmap),
        ],
        out_specs=[pl.BlockSpec((None, gw, tq), tmap),
                   pl.BlockSpec((None, None, LANES, tq), lambda b, g, i: (b, g, 0, i))],
        compiler_params=_cparams(("parallel", "parallel", "parallel")),
        name="cmp_select",
    )(qt, kc, vct, ovt, gates_t)


_SLC_UNROLL = 2


def _slc_kernel(qt_ref, sbt_ref, k_ref, vt_ref, gt_ref, prev_ref, o_ref,
                qa_ref, s_ref, p_ref, acc_ref, *, tq):
    i = pl.program_id(2)
    rows = HEADS_PER_GROUP * tq
    qa_ref[...] = jnp.concatenate([jnp.concatenate([sbt_ref[...]] * HEADS_PER_GROUP, axis=1),
                                   jnp.concatenate(_head_columns(qt_ref), axis=1)], axis=0)
    p_ref[...] = jnp.zeros(p_ref.shape, BF16)
    acc_ref[...] = jnp.zeros(acc_ref.shape, F32)

    def accumulate(unit, alpha):
        j = jnp.where(unit <= 0, i, unit - 1)
        pv = _dot(vt_ref[:, pl.ds(pl.multiple_of(j * tq, tq), tq)], p_ref[...])
        acc_ref[...] = alpha * acc_ref[...] + pv

    def probs(m):
        s = s_ref[...]
        m_new = jnp.maximum(m, jnp.max(s, axis=0, keepdims=True))
        p_ref[...] = jnp.exp(s - m_new).astype(BF16)
        return m_new, jnp.exp(m - m_new)

    def scores(j, masked):
        s = _dot(k_ref[pl.ds(pl.multiple_of(j * tq, tq), tq), :], qa_ref[...])
        if masked:
            causal = (lax.broadcasted_iota(jnp.int32, (tq, 1), 0)
                      <= (lax.broadcasted_iota(jnp.int32, (1, rows), 1) & (tq - 1)))
            s = jnp.where(causal, s, NEG)
        s_ref[...] = s

    def body(k, carry):
        m, alpha = carry
        accumulate(k - 1, alpha)
        m, alpha = probs(m)
        scores(k, False)
        return m, alpha

    def steps(first, count, carry):
        for k in range(count):
            carry = body(first + k, carry)
        return carry

    scores(i, True)
    carry = (jnp.full((1, rows), -3e38, F32), jnp.ones((1, rows), F32))
    trips = i >> int(math.log2(_SLC_UNROLL))
    carry = lax.fori_loop(0, trips, lambda k, c: steps(k * _SLC_UNROLL, _SLC_UNROLL, c), carry)
    done = trips * _SLC_UNROLL
    size = _SLC_UNROLL // 2
    while size >= 1:
        carry = lax.cond((i & size) != 0, functools.partial(steps, done, size), lambda c: c, carry)
        done = done + (i & size)
        size //= 2
    m, alpha = carry
    accumulate(i - 1, alpha)
    m, alpha = probs(m)
    accumulate(i, alpha)

    gate = jnp.concatenate([_gate_row(gt_ref, _BR_SLC, hd) for hd in range(HEADS_PER_GROUP)], axis=1)
    out_t = acc_ref[0:HEAD_DIM, :] * (gate / acc_ref[HEAD_DIM:HEAD_DIM + 1, :])
    for pair in range(HEADS_PER_GROUP // 2):
        for c in range(tq // LANES):
            col = 2 * pair * tq + c * LANES
            both = jnp.concatenate([out_t[:, col:col + LANES], out_t[:, col + tq:col + tq + LANES]], axis=0)
            both = both + prev_ref[pair * LANES:(pair + 1) * LANES, c * LANES:(c + 1) * LANES]
            o_ref[c * LANES:(c + 1) * LANES, pair * LANES:(pair + 1) * LANES] = both.T.astype(BF16)


def _slc(qt, sbt, k_aug, vt_aug, gates_t, prev, *, bsz, seq, tq=512):
    nq = seq // tq
    gw = HEADS_PER_GROUP * HEAD_DIM
    rows = HEADS_PER_GROUP * tq
    tmap, gmap, kvmap = _tmaps(nq)
    return pl.pallas_call(
        functools.partial(_slc_kernel, tq=tq),
        out_shape=jax.ShapeDtypeStruct((bsz * seq, A_WIDTH), BF16),
        grid=(bsz, N_GROUPS, nq),
        in_specs=[
            pl.BlockSpec((None, gw, tq), tmap),
            pl.BlockSpec((None, None, LANES, tq), lambda b, g, i: (b, g, 0, i)),
            pl.BlockSpec((None, None, seq, 2 * LANES), kvmap),
            pl.BlockSpec((None, None, None, LANES, seq), lambda b, g, i: (b, 0, g, 0, 0)),
            pl.BlockSpec((GATE_ROWS, tq), gmap),
            pl.BlockSpec((None, gw, tq), tmap),
        ],
        out_specs=pl.BlockSpec((tq, gw), lambda b, g, i: (b * nq + i, g)),
        scratch_shapes=[pltpu.VMEM((2 * LANES, rows), BF16), pltpu.VMEM((tq, rows), F32),
                        pltpu.VMEM((tq, rows), BF16), pltpu.VMEM((LANES, rows), F32)],
        compiler_params=_cparams(("parallel", "parallel", "parallel")),
        name="slc_attn",
    )(qt, sbt, k_aug, vt_aug, gates_t, prev)


def _win_kernel(qt_ref, k_ref, vt_ref, band_ref, gt_ref, prev_ref, ot_ref, *, tq):
    i = pl.program_id(2)
    span = WINDOW + tq
    start = pl.multiple_of(jnp.maximum(i - WINDOW // tq, 0) * tq, tq)
    k = k_ref[pl.ds(start, span), :]
    vt = vt_ref[:, pl.ds(start, span)]
    heads = _head_columns(qt_ref)

    def scores(hd):
        return _dot(k, heads[hd]) + band_ref[...]

    def probs(hd, s):
        return jnp.exp(s - jnp.max(s, axis=0, keepdims=True)).astype(BF16)

    def finish(hd, p):
        acc = _dot(vt, p)
        rows = slice(hd * HEAD_DIM, (hd + 1) * HEAD_DIM)
        scale = _gate_row(gt_ref, _BR_WIN, hd) / acc[HEAD_DIM:HEAD_DIM + 1, :]
        ot_ref[rows, :] = prev_ref[rows, :] + acc[:HEAD_DIM, :] * scale

    _software_pipeline(HEADS_PER_GROUP, scores, probs, finish)


def _window_band(tq):
    edge = WINDOW // tq
    v = jnp.arange(edge + 1)[:, None, None]
    kpos = jnp.maximum(v - edge, 0) * tq + jnp.arange(WINDOW + tq)[None, :, None]
    t = v * tq + jnp.arange(tq)[None, None, :]
    ok = (t - kpos >= 0) & (t - kpos < WINDOW)
    return jnp.where(ok, 0.0, NEG).astype(F32)


def _win(qt, k, vt, gates_t, prev, *, bsz, seq, tq=256):
    nq = seq // tq
    gw = HEADS_PER_GROUP * HEAD_DIM
    band = _window_band(tq)
    last = band.shape[0] - 1
    tmap, gmap, kvmap = _tmaps(nq)
    return pl.pallas_call(
        functools.partial(_win_kernel, tq=tq),
        out_shape=jax.ShapeDtypeStruct((bsz, A_WIDTH, seq), F32),
        grid=(bsz, N_GROUPS, nq),
        in_specs=[
            pl.BlockSpec((None, gw, tq), tmap),
            pl.BlockSpec((None, None, seq, LANES), kvmap),
            pl.BlockSpec((None, None, None, LANES, seq), lambda b, g, i: (b, 1, g, 0, 0)),
            pl.BlockSpec((None,) + band.shape[1:], lambda b, g, i: (jnp.minimum(i, last), 0, 0)),
            pl.BlockSpec((GATE_ROWS, tq), gmap),
            pl.BlockSpec((None, gw, tq), tmap),
        ],
        out_specs=pl.BlockSpec((None, gw, tq), tmap),
        compiler_params=_cparams(("parallel", "parallel", "parallel")),
        name="win_attn",
    )(qt, k, vt, band, gates_t, prev)


def _outproj_kernel(x_ref, oa_ref, ob_ref, w_ref, o_ref):
    y = _dot(oa_ref[...], w_ref[0:A_WIDTH, :]) + _dot(ob_ref[...], w_ref[A_WIDTH:, :])
    o_ref[...] = x_ref[...] + y


def _outproj(x, oa, ob, w, *, tm=512):
    n, d = x.shape
    row = lambda i: (i, 0)
    return pl.pallas_call(
        _outproj_kernel,
        out_shape=jax.ShapeDtypeStruct((n, d), F32),
        grid=(n // tm,),
        in_specs=[
            pl.BlockSpec((tm, d), row),
            pl.BlockSpec((tm, A_WIDTH), row),
            pl.BlockSpec((tm, A_WIDTH), row),
            pl.BlockSpec(w.shape, lambda i: (0, 0)),
        ],
        out_specs=pl.BlockSpec((tm, d), row),
        compiler_params=_cparams(("parallel",)),
        name="outproj",
    )(x, oa, ob, w)


def _pool_kernel(x_ref, halo_ref, g_ref, w_ref, sc_ref, o_ref, ext_ref, tmp_ref, *, tm, tiles_per_batch):
    i = pl.program_id(0)
    first = i % tiles_per_batch == 0
    x = x_ref[...]
    h = _rms(x, g_ref[...])
    hh = _rms(halo_ref[...], g_ref[...])
    pad, body = _POOL_PAD, _POOL_PAD + POOL_HALO
    ext_ref[0:pad, :] = jnp.zeros((pad, h.shape[1]), F32)
    ext_ref[pad:body, :] = jnp.where(first, 0.0, hh)
    ext_ref[body:body + tm, :] = h
    gw = h.shape[1] // len(POOL_WINDOWS)
    for slot in range(2):
        tmp_ref[slot, 0:pad, :] = jnp.zeros((pad, gw), F32)
    t = (i % tiles_per_batch) * tm + lax.broadcasted_iota(jnp.int32, (tm, 1), 0)
    span = POOL_HALO + tm
    for gi, win in enumerate(POOL_WINDOWS):
        c0 = gi * gw
        hg = h[:, c0:c0 + gw]
        read = lambda start, c0=c0: ext_ref[pl.ds(start, span), c0:c0 + gw]
        shift, slot = 1, 0
        while shift < win:
            tmp_ref[slot, pad:pad + span, :] = read(pad) + read(pad - shift)
            read = lambda start, slot=slot: tmp_ref[slot, pl.ds(start, span), :]
            shift, slot = 2 * shift, 1 - slot
        tot = read(pad)[POOL_HALO:, :]
        cnt = jnp.minimum(t + 1, win).astype(F32)
        y = _dot((tot / cnt - hg).astype(BF16), w_ref[gi])
        o_ref[:, c0:c0 + gw] = x[:, c0:c0 + gw] + y * sc_ref[:, c0:c0 + gw]


def _pool(x, g, w, scale, *, seq, tm=512):
    n, d = x.shape
    hb = tm // POOL_HALO
    return pl.pallas_call(
        functools.partial(_pool_kernel, tm=tm, tiles_per_batch=seq // tm),
        out_shape=jax.ShapeDtypeStruct((n, d), F32),
        grid=(n // tm,),
        in_specs=[
            pl.BlockSpec((tm, d), lambda i: (i, 0)),
            pl.BlockSpec((POOL_HALO, d), lambda i: (jnp.maximum(i * hb - 1, 0), 0)),
            pl.BlockSpec((1, d), lambda i: (0, 0)),
            pl.BlockSpec(w.shape, lambda i: (0, 0, 0)),
            pl.BlockSpec((1, d), lambda i: (0, 0)),
        ],
        out_specs=pl.BlockSpec((tm, d), lambda i: (i, 0)),
        scratch_shapes=[pltpu.VMEM((_POOL_PAD + POOL_HALO + tm, d), F32),
                        pltpu.VMEM((2, _POOL_PAD + POOL_HALO + tm, d // len(POOL_WINDOWS)), F32)],
        compiler_params=_cparams(("parallel",)),
        name="pool_mixer",
    )(x, x, g.reshape(1, d), w, scale.reshape(1, d))


def _pack_w_in(w_in):
    o = 0
    q = w_in[:, o:o + A_WIDTH]; o += A_WIDTH
    kc, vc, ks, vs, kw, vw = [w_in[:, o + k * KV_WIDTH:o + (k + 1) * KV_WIDTH] for k in range(6)]
    o += 6 * KV_WIDTH
    gates = w_in[:, o:o + N_BRANCH * N_HEADS]; o += N_BRANCH * N_HEADS
    u, gb, gc = [w_in[:, o + k * A_WIDTH:o + (k + 1) * A_WIDTH] for k in range(3)]
    gates = gates.reshape(-1, N_GROUPS, HEADS_PER_GROUP, N_BRANCH).transpose(0, 1, 3, 2)
    gates = gates.reshape(-1, N_GROUPS, N_BRANCH * HEADS_PER_GROUP)
    gates = jnp.pad(gates, ((0, 0), (0, 0), (0, LANES - N_BRANCH * HEADS_PER_GROUP))).reshape(-1, GATE_WIDTH)
    return jnp.concatenate([q, ks, kw, vs, vw, kc, vc, gates, u, gb, gc], axis=1).astype(BF16)


def _group_mean_matrix(width):
    r = jnp.arange(width) // HEAD_DIM
    return jnp.where(r[:, None] == r[None, :], 1.0 / HEAD_DIM, 0.0).astype(BF16)


def _rope_freq_row():
    lane = jnp.arange(LANES) % HEAD_DIM
    freqs = ROPE_THETA ** (-jnp.arange(0, ROPE_DIM, 2, dtype=F32) / ROPE_DIM)
    return jnp.where(lane < ROPE_DIM, freqs[lane % ROPE_HALF], 0.0).astype(F32).reshape(1, LANES)


def _per_group(a, bsz, rows):
    return a.reshape(bsz, rows, N_GROUPS, HEAD_DIM).transpose(0, 2, 1, 3)


def _hybrid_mixer(x, positions, g, w_in, q_norm, k_norm, cmp_pos, cmp_w1, cmp_w2, conv_w, w_out, *, bsz, seq):
    n = bsz * seq
    ncp = seq // CMP_STRIDE
    freq = _rope_freq_row()
    pq = _group_mean_matrix(A_WIDTH)
    pk = _group_mean_matrix(KV_WIDTH)
    qnw = jnp.tile(q_norm, N_HEADS).reshape(1, A_WIDTH)
    knw = jnp.tile(k_norm, (1, N_GROUPS))

    q_t, k_slc, k_win, v_t, kc_raw, vc_raw, gates, ob = _inproj(
        x, positions.reshape(n, 1), g, _pack_w_in(w_in), qnw, knw, conv_w, pq, pk, freq, seq=seq)

    eye_g = jnp.eye(N_GROUPS, dtype=F32)
    seg = CMP_STRIDE * KV_WIDTH

    def seg_weights(w1):
        w1r = w1.reshape(CMP_BLOCK, HEAD_DIM, CMP_HIDDEN)
        halves = [jnp.einsum('jdh,gk->jgdkh', w1r[a:a + CMP_STRIDE], eye_g).reshape(seg, N_GROUPS * CMP_HIDDEN)
                  for a in (0, CMP_STRIDE)]
        return jnp.stack(halves).astype(BF16)

    def seg_pos(pe):
        return [jnp.broadcast_to(pe[a:a + CMP_STRIDE, None, :], (CMP_STRIDE, N_GROUPS, HEAD_DIM)).reshape(seg)
                for a in (0, CMP_STRIDE)]

    pe = jnp.stack(seg_pos(cmp_pos[0]) + seg_pos(cmp_pos[1]))
    w2 = jnp.stack([jnp.einsum('hd,gk->ghkd', cmp_w2[a], eye_g).reshape(N_GROUPS * CMP_HIDDEN, KV_WIDTH)
                    for a in range(2)]).astype(BF16)
    blk_end = jnp.minimum(jnp.arange(ncp) * CMP_STRIDE + CMP_BLOCK - 1, seq - 1)
    pos_c = positions[:, blk_end].reshape(bsz, ncp, 1)
    kc_g, vc_t = _compress(kc_raw, vc_raw, pos_c, pe, seg_weights(cmp_w1[0]), seg_weights(cmp_w1[1]), w2[0], w2[1].T,
                           knw, pk, freq, seq=seq)

    gates_t = gates.reshape(n, N_GROUPS, LANES)[:, :, :GATE_ROWS].transpose(1, 2, 0).reshape(N_GROUPS * GATE_ROWS, n)
    cstart = jnp.arange(ncp) * CMP_STRIDE
    sstart = jnp.arange(LANES) * SEL_BLOCK
    ovt = ((cstart[None, :] < sstart[:, None] + SEL_BLOCK)
           & (cstart[None, :] + CMP_BLOCK > sstart[:, None])).astype(BF16)
    o_t, sel_bias = _cmp_select(q_t, kc_g, vc_t, ovt, gates_t, bsz=bsz, seq=seq)

    o_t = _win(q_t, k_win, v_t, gates_t, o_t, bsz=bsz, seq=seq)
    o_a = _slc(q_t, sel_bias, k_slc, v_t, gates_t, o_t, bsz=bsz, seq=seq)

    return _outproj(x, o_a, ob, w_out.astype(BF16))


def kernel(x, positions, ffn_norm, ffn_w_gate, ffn_w_up, ffn_w_down, mix_norm, hyb_w_in, hyb_q_norm, hyb_k_norm,
           hyb_cmp_pos, hyb_cmp_w1, hyb_cmp_w2, hyb_conv_w, hyb_w_out, pool_w, pool_scale):
    bsz, seq, d = x.shape
    depth = ffn_norm.shape[0]
    assert seq % 2048 == 0 and seq // SEL_BLOCK <= LANES and seq // SEL_BLOCK >= SEL_TOPK
    wg, wu, wd = (w.astype(BF16) for w in (ffn_w_gate, ffn_w_up, ffn_w_down))
    xf = x.reshape(bsz * seq, d)
    for layer in range(depth):
        xf = _ffn(xf, ffn_norm[layer, 0], wg, wu, wd, layer, 0)
        i = layer // 2
        if layer % 2 == 0:
            xf = _hybrid_mixer(xf, positions, mix_norm[layer], hyb_w_in[i], hyb_q_norm[i], hyb_k_norm[i],
                               hyb_cmp_pos[i], hyb_cmp_w1[i], hyb_cmp_w2[i], hyb_conv_w[i], hyb_w_out[i],
                               bsz=bsz, seq=seq)
        else:
            xf = _pool(xf, mix_norm[layer], pool_w[i].astype(BF16), pool_scale[i], seq=seq)
        xf = _ffn(xf, ffn_norm[layer, 1], wg, wu, wd, layer, 1)
    return xf.reshape(bsz, seq, d)
```

```python
import functools
import math

import jax
import jax.numpy as jnp
from jax import lax
from jax.experimental import pallas as pl
from jax.experimental.pallas import tpu as pltpu

F32 = jnp.float32
BF16 = jnp.bfloat16

HEAD_DIM = 64
N_GROUPS = 2
HEADS_PER_GROUP = 4
N_HEADS = N_GROUPS * HEADS_PER_GROUP
A_WIDTH = N_HEADS * HEAD_DIM
KV_WIDTH = N_GROUPS * HEAD_DIM
ROPE_DIM = HEAD_DIM // 4
ROPE_HALF = ROPE_DIM // 2
ROPE_THETA = 500000.0
CMP_BLOCK = 32
CMP_STRIDE = 16
CMP_HIDDEN = 2 * HEAD_DIM
SEL_BLOCK = 64
SEL_TOPK = 16
WINDOW = 512
N_BRANCH = 3
_BR_CMP, _BR_SLC, _BR_WIN = 0, 1, 2
CONV_WIDTH = 3
POOL_WINDOWS = (2, 4, 8, 16)
POOL_HALO = 16
_POOL_PAD = 8
EPS = 1e-6
NEG = -1e30

LANES = 128
VMEM_LIMIT_BYTES = 56 * 1024 * 1024


def _cparams(sem):
    return pltpu.CompilerParams(dimension_semantics=sem, vmem_limit_bytes=VMEM_LIMIT_BYTES)


def _dot(a, b):
    return jnp.dot(a, b, preferred_element_type=F32)


def _dot_nt(a, b):
    return lax.dot_general(a, b, (((1,), (1,)), ((), ())), preferred_element_type=F32)


def _dot_split(a, b):
    hi = a.astype(BF16)
    lo = (a - hi.astype(F32)).astype(BF16)
    return _dot(hi, b) + _dot(lo, b)


def _rms(x, g):
    ms = jnp.mean(x * x, axis=-1, keepdims=True)
    return x * lax.rsqrt(ms + EPS) * g


def _group_rms(x, g, pmat):
    ms = _dot_split(x * x, pmat)
    return x * lax.rsqrt(ms + EPS) * g


def _rope_tables(pos_col, freq_row):
    ang = pos_col * freq_row
    lane = lax.broadcasted_iota(jnp.int32, (1, LANES), 1) & (HEAD_DIM - 1)
    sign = jnp.where(lane < ROPE_HALF, -1.0, 1.0).astype(F32)
    return jnp.cos(ang), jnp.sin(ang) * sign


def _rope(x, cos_t, sin_t):
    w = x.shape[1]
    reps = w // LANES
    if reps > 1:
        cos_t = jnp.concatenate([cos_t] * reps, axis=1)
        sin_t = jnp.concatenate([sin_t] * reps, axis=1)
    lane = lax.broadcasted_iota(jnp.int32, (1, w), 1) & (HEAD_DIM - 1)
    partner = jnp.where(lane < ROPE_HALF,
                        pltpu.roll(x, w - ROPE_HALF, axis=1),
                        pltpu.roll(x, ROPE_HALF, axis=1))
    return x * cos_t + partner * sin_t


def _ffn_kernel(x_ref, xnext_ref, g_ref, wg_ref, wu_ref, wd_ref, o_ref, h_ref, acc_ref, *, nj):
    i, j = pl.program_id(0), pl.program_id(1)
    slot = i % 2

    @pl.when((i == 0) & (j == 0))
    def _():
        h_ref[0] = _rms(x_ref[...], g_ref[...]).astype(BF16)

    def step(first, last):
        h = h_ref[slot]
        a = _dot(h, wg_ref[...])
        b = _dot(h, wu_ref[...])
        act = a * (1.0 / (1.0 + jnp.exp(-a))) * b
        y = _dot(act.astype(BF16), wd_ref[...])
        if not first:
            y = acc_ref[...] + y
        if last:
            h_ref[1 - slot] = _rms(xnext_ref[...], g_ref[...]).astype(BF16)
            o_ref[...] = x_ref[...] + 0.5 * y
        else:
            acc_ref[...] = y

    for jj in range(nj):
        pl.when(j == jj)(functools.partial(step, jj == 0, jj == nj - 1))


def _ffn(x, g, wg, wu, wd, layer, half, *, tm=512, tf=2816):
    n, d = x.shape
    dff = wg.shape[-1]
    ni, nj = n // tm, dff // tf
    wmode = dict(pipeline_mode=pl.Buffered(1)) if nj == 1 else {}
    return pl.pallas_call(
        functools.partial(_ffn_kernel, nj=nj),
        out_shape=jax.ShapeDtypeStruct((n, d), F32),
        grid=(ni, nj),
        in_specs=[
            pl.BlockSpec((tm, d), lambda i, j: (i, 0)),
            pl.BlockSpec((tm, d), lambda i, j: (jnp.minimum(i + 1, ni - 1), 0)),
            pl.BlockSpec((1, d), lambda i, j: (0, 0)),
            pl.BlockSpec((None, None, d, tf), lambda i, j: (layer, half, 0, j), **wmode),
            pl.BlockSpec((None, None, d, tf), lambda i, j: (layer, half, 0, j), **wmode),
            pl.BlockSpec((None, None, tf, d), lambda i, j: (layer, half, j, 0), **wmode),
        ],
        out_specs=pl.BlockSpec((tm, d), lambda i, j: (i, 0)),
        scratch_shapes=[pltpu.VMEM((2, tm, d), BF16), pltpu.VMEM((tm, d), F32)],
        compiler_params=_cparams(("arbitrary", "arbitrary")),
        name="ffn",
    )(x, x, g.reshape(1, d), wg, wu, wd)


GATE_WIDTH = N_GROUPS * LANES
GATE_ROWS = 16
_C_Q = 0
_C_K = _C_Q + A_WIDTH
_C_V = _C_K + 2 * KV_WIDTH
_C_C = _C_V + 2 * KV_WIDTH
_C_GATE = _C_C + 2 * KV_WIDTH
_C_U = _C_GATE + GATE_WIDTH
_C_GB = _C_U + A_WIDTH
_C_GC = _C_GB + A_WIDTH
_C_END = _C_GC + A_WIDTH
CONV_HALO = 8


def _inproj_kernel(x_ref, pos_ref, g_ref, w_ref, qnw_ref, knw_ref, cw_ref, pq_ref, pk_ref, freq_ref,
                   qt_ref, kslc_ref, kwin_ref, vt_ref, kc_ref, vc_ref, gates_ref, ob_ref,
                   vext_ref, *, tm, tiles_per_batch):
    i = pl.program_id(0)
    chunks = range(tm // LANES)

    def transposed(a, c):
        return jnp.concatenate([a[r * LANES:(r + 1) * LANES, c * LANES:(c + 1) * LANES].T for r in chunks], axis=1)

    def per_group(k):
        low = lax.broadcasted_iota(jnp.int32, (1, LANES), 1) < HEAD_DIM
        return [jnp.where(low, k, 0.0), jnp.where(low, pltpu.roll(k, HEAD_DIM, axis=1), 0.0)]

    @pl.when(i % tiles_per_batch == 0)
    def _():
        vext_ref[0:CONV_HALO, :] = jnp.zeros((CONV_HALO, A_WIDTH), F32)

    h = _rms(x_ref[...], g_ref[...]).astype(BF16)

    def proj(c0, width):
        return _dot(h, w_ref[:, c0:c0 + width])

    q_raw = proj(_C_Q, A_WIDTH)
    k2 = proj(_C_K, 2 * KV_WIDTH)
    cos_t, sin_t = _rope_tables(pos_ref[...].astype(F32), freq_ref[...])
    v = proj(_C_GC, A_WIDTH) * proj(_C_U, A_WIDTH)
    gate_b = proj(_C_GB, A_WIDTH)

    q = _group_rms(q_raw, qnw_ref[...], pq_ref[...])
    q = _rope(q, cos_t, sin_t) * (HEAD_DIM ** -0.5)
    for c in range(A_WIDTH // LANES):
        qt_ref[c * LANES:(c + 1) * LANES, :] = transposed(q, c).astype(BF16)

    tok = (i % tiles_per_batch) * tm + lax.broadcasted_iota(jnp.int32, (tm, 1), 0)
    blk_onehot = (lax.broadcasted_iota(jnp.int32, (1, LANES), 1) == (tok >> int(math.log2(SEL_BLOCK))))
    blk_onehot = jnp.where(blk_onehot, 1.0, 0.0).astype(BF16)
    ks = _rope(_group_rms(k2[:, :KV_WIDTH], knw_ref[1:2, :], pk_ref[...]), cos_t, sin_t)
    for g, kg in enumerate(per_group(ks)):
        kslc_ref[g] = jnp.concatenate([blk_onehot, kg.astype(BF16)], axis=1)
    kw = _rope(_group_rms(k2[:, KV_WIDTH:], knw_ref[2:3, :], pk_ref[...]), cos_t, sin_t)
    for g, kg in enumerate(per_group(kw)):
        kwin_ref[g] = kg.astype(BF16)

    v2 = proj(_C_V, 2 * KV_WIDTH)
    ones_rows = jnp.where(lax.broadcasted_iota(jnp.int32, (LANES - HEAD_DIM, 1), 0) == 0, 1.0, 0.0)
    ones_rows = jnp.broadcast_to(ones_rows, (LANES - HEAD_DIM, tm)).astype(BF16)
    for branch in range(2):
        vt = transposed(v2, branch).astype(BF16)
        for g in range(N_GROUPS):
            vt_ref[branch, g, 0:HEAD_DIM, :] = vt[g * HEAD_DIM:(g + 1) * HEAD_DIM, :]
            vt_ref[branch, g, HEAD_DIM:, :] = ones_rows
    c2 = proj(_C_C, 2 * KV_WIDTH)
    kc_ref[...] = c2[:, :KV_WIDTH]
    vc_ref[...] = c2[:, KV_WIDTH:]
    gate_logits = proj(_C_GATE, GATE_WIDTH)

    vext_ref[CONV_HALO:CONV_HALO + tm, :] = v
    y = (cw_ref[2:3, :] * v
         + cw_ref[1:2, :] * vext_ref[pl.ds(CONV_HALO - 1, tm), :]
         + cw_ref[0:1, :] * vext_ref[pl.ds(CONV_HALO - 2, tm), :])
    ob_ref[...] = (gate_b * y).astype(BF16)
    vext_ref[0:CONV_HALO, :] = vext_ref[tm:tm + CONV_HALO, :]
    gates_ref[...] = 1.0 / (1.0 + jnp.exp(-gate_logits))


def _inproj(x, pos_col, g, w, qnw, knw, cw, pq, pk, freq, *, seq, tm=512):
    n, d = x.shape
    bsz, tpb = n // seq, seq // tm
    row = lambda i: (i, 0)
    fixed = lambda i: (0, 0)
    widths = [(KV_WIDTH, F32), (KV_WIDTH, F32), (GATE_WIDTH, F32), (A_WIDTH, BF16)]
    out_shape = [jax.ShapeDtypeStruct((bsz, A_WIDTH, seq), BF16),
                 jax.ShapeDtypeStruct((bsz, N_GROUPS, seq, 2 * LANES), BF16),
                 jax.ShapeDtypeStruct((bsz, N_GROUPS, seq, LANES), BF16),
                 jax.ShapeDtypeStruct((bsz, 2, N_GROUPS, LANES, seq), BF16)]
    out_specs = [pl.BlockSpec((None, A_WIDTH, tm), lambda i: (i // tpb, 0, i % tpb)),
                 pl.BlockSpec((None, N_GROUPS, tm, 2 * LANES), lambda i: (i // tpb, 0, i % tpb, 0)),
                 pl.BlockSpec((None, N_GROUPS, tm, LANES), lambda i: (i // tpb, 0, i % tpb, 0)),
                 pl.BlockSpec((None, 2, N_GROUPS, LANES, tm), lambda i: (i // tpb, 0, 0, 0, i % tpb))]
    return pl.pallas_call(
        functools.partial(_inproj_kernel, tm=tm, tiles_per_batch=tpb),
        out_shape=out_shape + [jax.ShapeDtypeStruct((n, wd), dt) for wd, dt in widths],
        grid=(n // tm,),
        in_specs=[
            pl.BlockSpec((tm, d), row),
            pl.BlockSpec((tm, 1), row),
            pl.BlockSpec((1, d), fixed),
            pl.BlockSpec(w.shape, fixed),
            pl.BlockSpec(qnw.shape, fixed),
            pl.BlockSpec(knw.shape, fixed),
            pl.BlockSpec(cw.shape, fixed),
            pl.BlockSpec(pq.shape, fixed),
            pl.BlockSpec(pk.shape, fixed),
            pl.BlockSpec(freq.shape, fixed),
        ],
        out_specs=out_specs + [pl.BlockSpec((tm, wd), row) for wd, _ in widths],
        scratch_shapes=[pltpu.VMEM((tm + CONV_HALO, A_WIDTH), F32)],
        compiler_params=_cparams(("arbitrary",)),
        name="inproj",
    )(x, pos_col, g.reshape(1, d), w, qnw, knw, cw, pq, pk, freq)


def _gelu_tanh(x):
    return 0.5 * x * (1.0 + jnp.tanh(math.sqrt(2.0 / math.pi) * (x + 0.044715 * (x * x * x))))


def _compress_kernel(xk_ref, xv_ref, pos_ref, pe_ref, wk_ref, wv_ref, w2_ref, w2vt_ref, knw_ref, pk_ref, freq_ref,
                     kc_ref, vct_ref, *, ncp):
    def hidden(x_ref, pe_row, w_ref):
        x = jnp.concatenate([x_ref[pl.ds(j, ncp, stride=CMP_STRIDE), :] for j in range(CMP_STRIDE)], axis=1)
        a = _dot((x + pe_ref[pe_row:pe_row + 1, :]).astype(BF16), w_ref[0])
        b = _dot((x + pe_ref[pe_row + 1:pe_row + 2, :]).astype(BF16), w_ref[1])
        hid = a + pltpu.roll(b, ncp - 1, axis=0)
        return _gelu_tanh(hid)

    low = lax.broadcasted_iota(jnp.int32, (1, LANES), 1) < HEAD_DIM
    kc = _dot(hidden(xk_ref, 0, wk_ref).astype(BF16), w2_ref[...])
    kc = _group_rms(kc, knw_ref[0:1, :], pk_ref[...])
    cos_t, sin_t = _rope_tables(pos_ref[...].astype(F32), freq_ref[...])
    kc = _rope(kc, cos_t, sin_t)
    kc_ref[0] = jnp.where(low, kc, 0.0).astype(BF16)
    kc_ref[1] = jnp.where(low, pltpu.roll(kc, HEAD_DIM, axis=1), 0.0).astype(BF16)
    hv = hidden(xv_ref, 2, wv_ref)
    hv_t = jnp.concatenate(
        [jnp.concatenate([hv[r * LANES:(r + 1) * LANES, c * LANES:(c + 1) * LANES].T for r in range(ncp // LANES)],
                         axis=1) for c in range(hv.shape[1] // LANES)], axis=0)
    vct = _dot(w2vt_ref[...], hv_t.astype(BF16))
    for g in range(N_GROUPS):
        vct_ref[g, 0:HEAD_DIM, :] = vct[g * HEAD_DIM:(g + 1) * HEAD_DIM, :].astype(BF16)
        vct_ref[g, HEAD_DIM:, :] = jnp.zeros((LANES - HEAD_DIM, ncp), BF16)


def _compress(xk, xv, pos_c, pe, wk, wv, w2k, w2vt, knw, pk, freq, *, seq):
    bsz, ncp = pos_c.shape[:2]
    bat = lambda b: (b, 0, 0)
    fix2 = lambda b: (0, 0)
    fix3 = lambda b: (0, 0, 0)
    return pl.pallas_call(
        functools.partial(_compress_kernel, ncp=ncp),
        out_shape=[jax.ShapeDtypeStruct((bsz, N_GROUPS, ncp, LANES), BF16),
                   jax.ShapeDtypeStruct((bsz, N_GROUPS, LANES, ncp), BF16)],
        grid=(bsz,),
        in_specs=[
            pl.BlockSpec((seq, KV_WIDTH), lambda b: (b, 0)),
            pl.BlockSpec((seq, KV_WIDTH), lambda b: (b, 0)),
            pl.BlockSpec((None, ncp, 1), bat),
            pl.BlockSpec(pe.shape, fix2),
            pl.BlockSpec(wk.shape, fix3),
            pl.BlockSpec(wv.shape, fix3),
            pl.BlockSpec(w2k.shape, fix2),
            pl.BlockSpec(w2vt.shape, fix2),
            pl.BlockSpec(knw.shape, fix2),
            pl.BlockSpec(pk.shape, fix2),
            pl.BlockSpec(freq.shape, fix2),
        ],
        out_specs=[pl.BlockSpec((None, N_GROUPS, ncp, LANES), lambda b: (b, 0, 0, 0)),
                   pl.BlockSpec((None, N_GROUPS, LANES, ncp), lambda b: (b, 0, 0, 0))],
        compiler_params=_cparams(("parallel",)),
        name="compress",
    )(xk, xv, pos_c, pe, wk, wv, w2k, w2vt, knw, pk, freq)


def _head_columns(qt_ref):
    cols = []
    for pair in range(qt_ref.shape[0] // LANES):
        both = qt_ref[pair * LANES:(pair + 1) * LANES, :]
        cols += [both, jnp.concatenate([both[HEAD_DIM:, :], both[:HEAD_DIM, :]], axis=0)]
    return cols


def _software_pipeline(n, scores, probs, finish):
    s, p = {}, {}
    for step in range(n + 2):
        if step >= 2:
            finish(step - 2, p.pop(step - 2))
        if 1 <= step <= n:
            p[step - 1] = probs(step - 1, s.pop(step - 1))
        if step < n:
            s[step] = scores(step)


def _gate_row(gt_ref, branch, hd, group=0):
    r = group * GATE_ROWS + branch * HEADS_PER_GROUP + hd
    return gt_ref[r:r + 1, :]


def _cmp_kernel(qt_ref, kc_ref, vct_ref, ovt_ref, gt_ref, ot_ref, sbt_ref, *, tq, ncp):
    i = pl.program_id(1)
    t = i * tq + lax.broadcasted_iota(jnp.int32, (1, tq), 1)
    any_valid = (t >= CMP_BLOCK - 1).astype(F32)
    heads = _head_columns(qt_ref)
    cmp_per_sel = SEL_BLOCK // CMP_STRIDE
    groups = range(N_GROUPS)

    def run(nrow):
        nblk = nrow // cmp_per_sel
        blk_end = lax.broadcasted_iota(jnp.int32, (nrow, 1), 0) * CMP_STRIDE + (CMP_BLOCK - 1)
        valid = blk_end <= t
        kc = [kc_ref[g, 0:nrow, :] for g in groups]
        vct = [vct_ref[g, :, 0:nrow] for g in groups]
        psum_parts = [[] for _ in groups]

        def scores(u):
            return jnp.where(valid, _dot(kc[u // HEADS_PER_GROUP], heads[u]), NEG)

        def probs(u, s):
            e = jnp.exp(s - jnp.max(s, axis=0, keepdims=True))
            return e * (any_valid / jnp.sum(e, axis=0, keepdims=True))

        def finish(u, p):
            g, hd = divmod(u, HEADS_PER_GROUP)
            o_t = _dot(vct[g], p.astype(BF16))[:HEAD_DIM, :]
            ot_ref[u * HEAD_DIM:(u + 1) * HEAD_DIM, :] = o_t * _gate_row(gt_ref, _BR_CMP, hd, g)
            psum_parts[g].append(p)

        _software_pipeline(N_HEADS, scores, probs, finish)

        ovt = ovt_ref[0:nblk, 0:nrow]
        blk = lax.broadcasted_iota(jnp.int32, (nblk, 1), 0)
        sel_start = blk * SEL_BLOCK
        cur = (t >> int(math.log2(SEL_BLOCK))) << int(math.log2(SEL_BLOCK))
        imp = []
        for parts in psum_parts:
            psum = (parts[0] + parts[1]) + (parts[2] + parts[3])
            hi = psum.astype(BF16)
            lo = (psum - hi.astype(F32)).astype(BF16)
            x = jnp.where(sel_start <= t, _dot(ovt, hi) + _dot(ovt, lo), -1.0)
            imp.append(jnp.where(sel_start == cur, 1e4, jnp.where(sel_start == 0, 1e4, x)))

        blk_f = blk.astype(F32)
        bias = [jnp.full((nblk, tq), NEG, F32) for _ in groups]
        for _ in range(SEL_TOPK):
            for g in groups:
                mx = jnp.max(imp[g], axis=0, keepdims=True)
                first = jnp.min(jnp.where(imp[g] == mx, blk_f, float(LANES)), axis=0, keepdims=True)
                pick = blk_f == first
                bias[g] = jnp.where(pick, 0.0, bias[g])
                imp[g] = jnp.where(pick, -3e38, imp[g])
        for g in groups:
            sbt_ref[g, 0:nblk, :] = bias[g].astype(BF16)
            if nblk < LANES:
                sbt_ref[g, nblk:, :] = jnp.full((LANES - nblk, tq), NEG, BF16)

    sizes = [LANES * (v + 1) for v in range(ncp // LANES)]
    assert sizes[0] // cmp_per_sel >= SEL_TOPK
    tokens_per_variant = LANES * CMP_STRIDE
    lax.switch((i * tq + tq - 1) // tokens_per_variant, [functools.partial(run, nrow) for nrow in sizes])


def _tmaps(nq):
    return (lambda b, g, i: (b, g, i)), (lambda b, g, i: (g, b * nq + i)), (lambda b, g, i: (b, g, 0, 0))


def _cmp_select(qt, kc, vct, ovt, gates_t, *, bsz, seq, tq=256):
    ncp = kc.shape[2]
    nq = seq // tq
    whole = lambda b, i: (b, 0, 0, 0)
    return pl.pallas_call(
        functools.partial(_cmp_kernel, tq=tq, ncp=ncp),
        out_shape=[jax.ShapeDtypeStruct((bsz, A_WIDTH, seq), F32),
                   jax.ShapeDtypeStruct((bsz, N_GROUPS, LANES, seq), BF16)],
        grid=(bsz, nq),
        in_specs=[
            pl.BlockSpec((None, A_WIDTH, tq), lambda b, i: (b, 0, i)),
            pl.BlockSpec((None, N_GROUPS, ncp, LANES), whole),
            pl.BlockSpec((None, N_GROUPS, LANES, ncp), whole),
            pl.BlockSpec(ovt.shape, lambda b, i: (0, 0)),
            pl.BlockSpec((N_GROUPS * GATE_ROWS, tq), lambda b, i: (0, b * nq + i)),
        ],
        out_specs=[pl.BlockSpec((None, A_WIDTH, tq), lambda b, i: (b, 0, i)),
                   pl.BlockSpec((None, N_GROUPS, LANES, tq), lambda b, i: (b, 0, 0, i))],
        compiler_params=_cparams(("parallel", "parallel")),
        name="cmp_select",
    )(qt, kc, vct, ovt, gates_t)


_SLC_UNROLL = 2


def _slc_kernel(qt_ref, sbt_ref, k_ref, vt_ref, gt_ref, prev_ref, o_ref,
                qa_ref, s_ref, p_ref, acc_ref, *, tq):
    i = pl.program_id(2)
    rows = HEADS_PER_GROUP * tq
    qa_ref[...] = jnp.concatenate([jnp.concatenate([sbt_ref[...]] * HEADS_PER_GROUP, axis=1),
                                   jnp.concatenate(_head_columns(qt_ref), axis=1)], axis=0)
    p_ref[...] = jnp.zeros(p_ref.shape, BF16)
    acc_ref[...] = jnp.zeros(acc_ref.shape, F32)

    def accumulate(unit, alpha):
        j = jnp.where(unit <= 0, i, unit - 1)
        pv = _dot(vt_ref[:, pl.ds(pl.multiple_of(j * tq, tq), tq)], p_ref[...])
        acc_ref[...] = alpha * acc_ref[...] + pv

    def probs(m):
        s = s_ref[...]
        m_new = jnp.maximum(m, jnp.max(s, axis=0, keepdims=True))
        p_ref[...] = jnp.exp(s - m_new).astype(BF16)
        return m_new, jnp.exp(m - m_new)

    def scores(j, masked):
        s = _dot(k_ref[pl.ds(pl.multiple_of(j * tq, tq), tq), :], qa_ref[...])
        if masked:
            causal = (lax.broadcasted_iota(jnp.int32, (tq, 1), 0)
                      <= (lax.broadcasted_iota(jnp.int32, (1, rows), 1) & (tq - 1)))
            s = jnp.where(causal, s, NEG)
        s_ref[...] = s

    def body(k, carry):
        m, alpha = carry
        accumulate(k - 1, alpha)
        m, alpha = probs(m)
        scores(k, False)
        return m, alpha

    def steps(first, count, carry):
        for k in range(count):
            carry = body(first + k, carry)
        return carry

    scores(i, True)
    carry = (jnp.full((1, rows), -3e38, F32), jnp.ones((1, rows), F32))
    trips = i >> int(math.log2(_SLC_UNROLL))
    carry = lax.fori_loop(0, trips, lambda k, c: steps(k * _SLC_UNROLL, _SLC_UNROLL, c), carry)
    done = trips * _SLC_UNROLL
    size = _SLC_UNROLL // 2
    while size >= 1:
        carry = lax.cond((i & size) != 0, functools.partial(steps, done, size), lambda c: c, carry)
        done = done + (i & size)
        size //= 2
    m, alpha = carry
    accumulate(i - 1, alpha)
    m, alpha = probs(m)
    accumulate(i, alpha)

    gate = jnp.concatenate([_gate_row(gt_ref, _BR_SLC, hd) for hd in range(HEADS_PER_GROUP)], axis=1)
    out_t = acc_ref[0:HEAD_DIM, :] * (gate / acc_ref[HEAD_DIM:HEAD_DIM + 1, :])
    for pair in range(HEADS_PER_GROUP // 2):
        for c in range(tq // LANES):
            col = 2 * pair * tq + c * LANES
            both = jnp.concatenate([out_t[:, col:col + LANES], out_t[:, col + tq:col + tq + LANES]], axis=0)
            both = both + prev_ref[pair * LANES:(pair + 1) * LANES, c * LANES:(c + 1) * LANES]
            o_ref[c * LANES:(c + 1) * LANES, pair * LANES:(pair + 1) * LANES] = both.T.astype(BF16)


def _slc(qt, sbt, k_aug, vt_aug, gates_t, prev, *, bsz, seq, tq=512):
    nq = seq // tq
    gw = HEADS_PER_GROUP * HEAD_DIM
    rows = HEADS_PER_GROUP * tq
    tmap, gmap, kvmap = _tmaps(nq)
    return pl.pallas_call(
        functools.partial(_slc_kernel, tq=tq),
        out_shape=jax.ShapeDtypeStruct((bsz * seq, A_WIDTH), BF16),
        grid=(bsz, N_GROUPS, nq),
        in_specs=[
            pl.BlockSpec((None, gw, tq), tmap),
            pl.BlockSpec((None, None, LANES, tq), lambda b, g, i: (b, g, 0, i)),
            pl.BlockSpec((None, None, seq, 2 * LANES), kvmap),
            pl.BlockSpec((None, None, None, LANES, seq), lambda b, g, i: (b, 0, g, 0, 0)),
            pl.BlockSpec((GATE_ROWS, tq), gmap),
            pl.BlockSpec((None, gw, tq), tmap),
        ],
        out_specs=pl.BlockSpec((tq, gw), lambda b, g, i: (b * nq + i, g)),
        scratch_shapes=[pltpu.VMEM((2 * LANES, rows), BF16), pltpu.VMEM((tq, rows), F32),
                        pltpu.VMEM((tq, rows), BF16), pltpu.VMEM((LANES, rows), F32)],
        compiler_params=_cparams(("parallel", "parallel", "parallel")),
        name="slc_attn",
    )(qt, sbt, k_aug, vt_aug, gates_t, prev)


def _win_kernel(qt_ref, k_ref, vt_ref, band_ref, gt_ref, prev_ref, ot_ref, *, tq):
    i = pl.program_id(1)
    span = WINDOW + tq
    start = pl.multiple_of(jnp.maximum(i - WINDOW // tq, 0) * tq, tq)
    k = [k_ref[g, pl.ds(start, span), :] for g in range(N_GROUPS)]
    vt = [vt_ref[g, :, pl.ds(start, span)] for g in range(N_GROUPS)]
    heads = _head_columns(qt_ref)

    def scores(u):
        return _dot(k[u // HEADS_PER_GROUP], heads[u]) + band_ref[...]

    def probs(u, s):
        return jnp.exp(s - jnp.max(s, axis=0, keepdims=True)).astype(BF16)

    def finish(u, p):
        g, hd = divmod(u, HEADS_PER_GROUP)
        acc = _dot(vt[g], p)
        rows = slice(u * HEAD_DIM, (u + 1) * HEAD_DIM)
        scale = _gate_row(gt_ref, _BR_WIN, hd, g) / acc[HEAD_DIM:HEAD_DIM + 1, :]
        ot_ref[rows, :] = prev_ref[rows, :] + acc[:HEAD_DIM, :] * scale

    _software_pipeline(N_HEADS, scores, probs, finish)


def _window_band(tq):
    edge = WINDOW // tq
    v = jnp.arange(edge + 1)[:, None, None]
    kpos = jnp.maximum(v - edge, 0) * tq + jnp.arange(WINDOW + tq)[None, :, None]
    t = v * tq + jnp.arange(tq)[None, None, :]
    ok = (t - kpos >= 0) & (t - kpos < WINDOW)
    return jnp.where(ok, 0.0, NEG).astype(F32)


def _win(qt, k, vt, gates_t, prev, *, bsz, seq, tq=256):
    nq = seq // tq
    band = _window_band(tq)
    last = band.shape[0] - 1
    tile = lambda b, i: (b, 0, i)
    return pl.pallas_call(
        functools.partial(_win_kernel, tq=tq),
        out_shape=jax.ShapeDtypeStruct((bsz, A_WIDTH, seq), F32),
        grid=(bsz, nq),
        in_specs=[
            pl.BlockSpec((None, A_WIDTH, tq), tile),
            pl.BlockSpec((None, N_GROUPS, seq, LANES), lambda b, i: (b, 0, 0, 0)),
            pl.BlockSpec((None, None, N_GROUPS, LANES, seq), lambda b, i: (b, 1, 0, 0, 0)),
            pl.BlockSpec((None,) + band.shape[1:], lambda b, i: (jnp.minimum(i, last), 0, 0)),
            pl.BlockSpec((N_GROUPS * GATE_ROWS, tq), lambda b, i: (0, b * nq + i)),
            pl.BlockSpec((None, A_WIDTH, tq), tile),
        ],
        out_specs=pl.BlockSpec((None, A_WIDTH, tq), tile),
        compiler_params=_cparams(("parallel", "parallel")),
        name="win_attn",
    )(qt, k, vt, band, gates_t, prev)


def _outproj_kernel(x_ref, oa_ref, ob_ref, w_ref, o_ref):
    y = _dot(oa_ref[...], w_ref[0:A_WIDTH, :]) + _dot(ob_ref[...], w_ref[A_WIDTH:, :])
    o_ref[...] = x_ref[...] + y


def _outproj(x, oa, ob, w, *, tm=512):
    n, d = x.shape
    row = lambda i: (i, 0)
    return pl.pallas_call(
        _outproj_kernel,
        out_shape=jax.ShapeDtypeStruct((n, d), F32),
        grid=(n // tm,),
        in_specs=[
            pl.BlockSpec((tm, d), row),
            pl.BlockSpec((tm, A_WIDTH), row),
            pl.BlockSpec((tm, A_WIDTH), row),
            pl.BlockSpec(w.shape, lambda i: (0, 0)),
        ],
        out_specs=pl.BlockSpec((tm, d), row),
        compiler_params=_cparams(("parallel",)),
        name="outproj",
    )(x, oa, ob, w)


def _pool_kernel(x_ref, halo_ref, g_ref, w_ref, sc_ref, o_ref, ext_ref, tmp_ref, *, tm, tiles_per_batch):
    i = pl.program_id(0)
    first = i % tiles_per_batch == 0
    x = x_ref[...]
    h = _rms(x, g_ref[...])
    hh = _rms(halo_ref[...], g_ref[...])
    pad, body = _POOL_PAD, _POOL_PAD + POOL_HALO
    ext_ref[0:pad, :] = jnp.zeros((pad, h.shape[1]), F32)
    ext_ref[pad:body, :] = jnp.where(first, 0.0, hh)
    ext_ref[body:body + tm, :] = h
    gw = h.shape[1] // len(POOL_WINDOWS)
    for slot in range(2):
        tmp_ref[slot, 0:pad, :] = jnp.zeros((pad, gw), F32)
    t = (i % tiles_per_batch) * tm + lax.broadcasted_iota(jnp.int32, (tm, 1), 0)
    span = POOL_HALO + tm
    for gi, win in enumerate(POOL_WINDOWS):
        c0 = gi * gw
        hg = h[:, c0:c0 + gw]
        read = lambda start, c0=c0: ext_ref[pl.ds(start, span), c0:c0 + gw]
        shift, slot = 1, 0
        while shift < win:
            tmp_ref[slot, pad:pad + span, :] = read(pad) + read(pad - shift)
            read = lambda start, slot=slot: tmp_ref[slot, pl.ds(start, span), :]
            shift, slot = 2 * shift, 1 - slot
        tot = read(pad)[POOL_HALO:, :]
        cnt = jnp.minimum(t + 1, win).astype(F32)
        y = _dot((tot / cnt - hg).astype(BF16), w_ref[gi])
        o_ref[:, c0:c0 + gw] = x[:, c0:c0 + gw] + y * sc_ref[:, c0:c0 + gw]


def _pool(x, g, w, scale, *, seq, tm=512):
    n, d = x.shape
    hb = tm // POOL_HALO
    return pl.pallas_call(
        functools.partial(_pool_kernel, tm=tm, tiles_per_batch=seq // tm),
        out_shape=jax.ShapeDtypeStruct((n, d), F32),
        grid=(n // tm,),
        in_specs=[
            pl.BlockSpec((tm, d), lambda i: (i, 0)),
            pl.BlockSpec((POOL_HALO, d), lambda i: (jnp.maximum(i * hb - 1, 0), 0)),
            pl.BlockSpec((1, d), lambda i: (0, 0)),
            pl.BlockSpec(w.shape, lambda i: (0, 0, 0)),
            pl.BlockSpec((1, d), lambda i: (0, 0)),
        ],
        out_specs=pl.BlockSpec((tm, d), lambda i: (i, 0)),
        scratch_shapes=[pltpu.VMEM((_POOL_PAD + POOL_HALO + tm, d), F32),
                        pltpu.VMEM((2, _POOL_PAD + POOL_HALO + tm, d // len(POOL_WINDOWS)), F32)],
        compiler_params=_cparams(("parallel",)),
        name="pool_mixer",
    )(x, x, g.reshape(1, d), w, scale.reshape(1, d))


def _pack_w_in(w_in):
    o = 0
    q = w_in[:, o:o + A_WIDTH]; o += A_WIDTH
    kc, vc, ks, vs, kw, vw = [w_in[:, o + k * KV_WIDTH:o + (k + 1) * KV_WIDTH] for k in range(6)]
    o += 6 * KV_WIDTH
    gates = w_in[:, o:o + N_BRANCH * N_HEADS]; o += N_BRANCH * N_HEADS
    u, gb, gc = [w_in[:, o + k * A_WIDTH:o + (k + 1) * A_WIDTH] for k in range(3)]
    gates = gates.reshape(-1, N_GROUPS, HEADS_PER_GROUP, N_BRANCH).transpose(0, 1, 3, 2)
    gates = gates.reshape(-1, N_GROUPS, N_BRANCH * HEADS_PER_GROUP)
    gates = jnp.pad(gates, ((0, 0), (0, 0), (0, LANES - N_BRANCH * HEADS_PER_GROUP))).reshape(-1, GATE_WIDTH)
    return jnp.concatenate([q, ks, kw, vs, vw, kc, vc, gates, u, gb, gc], axis=1).astype(BF16)


def _group_mean_matrix(width):
    r = jnp.arange(width) // HEAD_DIM
    return jnp.where(r[:, None] == r[None, :], 1.0 / HEAD_DIM, 0.0).astype(BF16)


def _rope_freq_row():
    lane = jnp.arange(LANES) % HEAD_DIM
    freqs = ROPE_THETA ** (-jnp.arange(0, ROPE_DIM, 2, dtype=F32) / ROPE_DIM)
    return jnp.where(lane < ROPE_DIM, freqs[lane % ROPE_HALF], 0.0).astype(F32).reshape(1, LANES)


def _per_group(a, bsz, rows):
    return a.reshape(bsz, rows, N_GROUPS, HEAD_DIM).transpose(0, 2, 1, 3)


def _hybrid_mixer(x, positions, g, w_in, q_norm, k_norm, cmp_pos, cmp_w1, cmp_w2, conv_w, w_out, *, bsz, seq):
    n = bsz * seq
    ncp = seq // CMP_STRIDE
    freq = _rope_freq_row()
    pq = _group_mean_matrix(A_WIDTH)
    pk = _group_mean_matrix(KV_WIDTH)
    qnw = jnp.tile(q_norm, N_HEADS).reshape(1, A_WIDTH)
    knw = jnp.tile(k_norm, (1, N_GROUPS))

    q_t, k_slc, k_win, v_t, kc_raw, vc_raw, gates, ob = _inproj(
        x, positions.reshape(n, 1), g, _pack_w_in(w_in), qnw, knw, conv_w, pq, pk, freq, seq=seq)

    eye_g = jnp.eye(N_GROUPS, dtype=F32)
    seg = CMP_STRIDE * KV_WIDTH

    def seg_weights(w1):
        w1r = w1.reshape(CMP_BLOCK, HEAD_DIM, CMP_HIDDEN)
        halves = [jnp.einsum('jdh,gk->jgdkh', w1r[a:a + CMP_STRIDE], eye_g).reshape(seg, N_GROUPS * CMP_HIDDEN)
                  for a in (0, CMP_STRIDE)]
        return jnp.stack(halves).astype(BF16)

    def seg_pos(pe):
        return [jnp.broadcast_to(pe[a:a + CMP_STRIDE, None, :], (CMP_STRIDE, N_GROUPS, HEAD_DIM)).reshape(seg)
                for a in (0, CMP_STRIDE)]

    pe = jnp.stack(seg_pos(cmp_pos[0]) + seg_pos(cmp_pos[1]))
    w2 = jnp.stack([jnp.einsum('hd,gk->ghkd', cmp_w2[a], eye_g).reshape(N_GROUPS * CMP_HIDDEN, KV_WIDTH)
                    for a in range(2)]).astype(BF16)
    blk_end = jnp.minimum(jnp.arange(ncp) * CMP_STRIDE + CMP_BLOCK - 1, seq - 1)
    pos_c = positions[:, blk_end].reshape(bsz, ncp, 1)
    kc_g, vc_t = _compress(kc_raw, vc_raw, pos_c, pe, seg_weights(cmp_w1[0]), seg_weights(cmp_w1[1]), w2[0], w2[1].T,
                           knw, pk, freq, seq=seq)

    gates_t = gates.reshape(n, N_GROUPS, LANES)[:, :, :GATE_ROWS].transpose(1, 2, 0).reshape(N_GROUPS * GATE_ROWS, n)
    cstart = jnp.arange(ncp) * CMP_STRIDE
    sstart = jnp.arange(LANES) * SEL_BLOCK
    ovt = ((cstart[None, :] < sstart[:, None] + SEL_BLOCK)
           & (cstart[None, :] + CMP_BLOCK > sstart[:, None])).astype(BF16)
    o_t, sel_bias = _cmp_select(q_t, kc_g, vc_t, ovt, gates_t, bsz=bsz, seq=seq)

    o_t = _win(q_t, k_win, v_t, gates_t, o_t, bsz=bsz, seq=seq)
    o_a = _slc(q_t, sel_bias, k_slc, v_t, gates_t, o_t, bsz=bsz, seq=seq)

    return _outproj(x, o_a, ob, w_out.astype(BF16))


def kernel(x, positions, ffn_norm, ffn_w_gate, ffn_w_up, ffn_w_down, mix_norm, hyb_w_in, hyb_q_norm, hyb_k_norm,
           hyb_cmp_pos, hyb_cmp_w1, hyb_cmp_w2, hyb_conv_w, hyb_w_out, pool_w, pool_scale):
    bsz, seq, d = x.shape
    depth = ffn_norm.shape[0]
    assert seq % 2048 == 0 and seq // SEL_BLOCK <= LANES and seq // SEL_BLOCK >= SEL_TOPK
    wg, wu, wd = (w.astype(BF16) for w in (ffn_w_gate, ffn_w_up, ffn_w_down))
    xf = x.reshape(bsz * seq, d)
    for layer in range(depth):
        xf = _ffn(xf, ffn_norm[layer, 0], wg, wu, wd, layer, 0)
        i = layer // 2
        if layer % 2 == 0:
            xf = _hybrid_mixer(xf, positions, mix_norm[layer], hyb_w_in[i], hyb_q_norm[i], hyb_k_norm[i],
                               hyb_cmp_pos[i], hyb_cmp_w1[i], hyb_cmp_w2[i], hyb_conv_w[i], hyb_w_out[i],
                               bsz=bsz, seq=seq)
        else:
            xf = _pool(xf, mix_norm[layer], pool_w[i].astype(BF16), pool_scale[i], seq=seq)
        xf = _ffn(xf, ffn_norm[layer, 1], wg, wu, wd, layer, 1)
    return xf.reshape(bsz, seq, d)
```

```python
import functools
import math

import jax
import jax.numpy as jnp
from jax import lax
from jax.experimental import pallas as pl
from jax.experimental.pallas import tpu as pltpu

F32 = jnp.float32
BF16 = jnp.bfloat16

HEAD_DIM = 64
N_GROUPS = 2
HEADS_PER_GROUP = 4
N_HEADS = N_GROUPS * HEADS_PER_GROUP
A_WIDTH = N_HEADS * HEAD_DIM
KV_WIDTH = N_GROUPS * HEAD_DIM
ROPE_DIM = HEAD_DIM // 4
ROPE_HALF = ROPE_DIM // 2
ROPE_THETA = 500000.0
CMP_BLOCK = 32
CMP_STRIDE = 16
CMP_HIDDEN = 2 * HEAD_DIM
SEL_BLOCK = 64
SEL_TOPK = 16
WINDOW = 512
N_BRANCH = 3
_BR_CMP, _BR_SLC, _BR_WIN = 0, 1, 2
CONV_WIDTH = 3
POOL_WINDOWS = (2, 4, 8, 16)
POOL_HALO = 16
_POOL_PAD = 8
EPS = 1e-6
NEG = -1e30

LANES = 128
VMEM_LIMIT_BYTES = 56 * 1024 * 1024


def _cparams(sem):
    return pltpu.CompilerParams(dimension_semantics=sem, vmem_limit_bytes=VMEM_LIMIT_BYTES)


def _dot(a, b):
    return jnp.dot(a, b, preferred_element_type=F32)


def _dot_nt(a, b):
    return lax.dot_general(a, b, (((1,), (1,)), ((), ())), preferred_element_type=F32)


def _dot_split(a, b):
    hi = a.astype(BF16)
    lo = (a - hi.astype(F32)).astype(BF16)
    return _dot(hi, b) + _dot(lo, b)


def _rms(x, g):
    ms = jnp.mean(x * x, axis=-1, keepdims=True)
    return x * lax.rsqrt(ms + EPS) * g


def _group_rms(x, g, pmat):
    ms = _dot_split(x * x, pmat)
    return x * lax.rsqrt(ms + EPS) * g


def _rope_tables(pos_col, freq_row):
    ang = pos_col * freq_row
    lane = lax.broadcasted_iota(jnp.int32, (1, LANES), 1) & (HEAD_DIM - 1)
    sign = jnp.where(lane < ROPE_HALF, -1.0, 1.0).astype(F32)
    return jnp.cos(ang), jnp.sin(ang) * sign


def _rope(x, cos_t, sin_t):
    w = x.shape[1]
    reps = w // LANES
    if reps > 1:
        cos_t = jnp.concatenate([cos_t] * reps, axis=1)
        sin_t = jnp.concatenate([sin_t] * reps, axis=1)
    lane = lax.broadcasted_iota(jnp.int32, (1, w), 1) & (HEAD_DIM - 1)
    partner = jnp.where(lane < ROPE_HALF,
                        pltpu.roll(x, w - ROPE_HALF, axis=1),
                        pltpu.roll(x, ROPE_HALF, axis=1))
    return x * cos_t + partner * sin_t


def _ffn_kernel(x_ref, xnext_ref, g_ref, wg_ref, wu_ref, wd_ref, o_ref, h_ref, acc_ref, *, nj):
    i, j = pl.program_id(0), pl.program_id(1)
    slot = i % 2

    @pl.when((i == 0) & (j == 0))
    def _():
        h_ref[0] = _rms(x_ref[...], g_ref[...]).astype(BF16)

    def step(first, last):
        h = h_ref[slot]
        a = _dot(h, wg_ref[...])
        b = _dot(h, wu_ref[...])
        act = a * (1.0 / (1.0 + jnp.exp(-a))) * b
        y = _dot(act.astype(BF16), wd_ref[...])
        if not first:
            y = acc_ref[...] + y
        if last:
            h_ref[1 - slot] = _rms(xnext_ref[...], g_ref[...]).astype(BF16)
            o_ref[...] = x_ref[...] + 0.5 * y
        else:
            acc_ref[...] = y

    for jj in range(nj):
        pl.when(j == jj)(functools.partial(step, jj == 0, jj == nj - 1))


def _ffn(x, g, wg, wu, wd, layer, half, *, tm=512, tf=2816):
    n, d = x.shape
    dff = wg.shape[-1]
    ni, nj = n // tm, dff // tf
    wmode = dict(pipeline_mode=pl.Buffered(1)) if nj == 1 else {}
    return pl.pallas_call(
        functools.partial(_ffn_kernel, nj=nj),
        out_shape=jax.ShapeDtypeStruct((n, d), F32),
        grid=(ni, nj),
        in_specs=[
            pl.BlockSpec((tm, d), lambda i, j: (i, 0)),
            pl.BlockSpec((tm, d), lambda i, j: (jnp.minimum(i + 1, ni - 1), 0)),
            pl.BlockSpec((1, d), lambda i, j: (0, 0)),
            pl.BlockSpec((None, None, d, tf), lambda i, j: (layer, half, 0, j), **wmode),
            pl.BlockSpec((None, None, d, tf), lambda i, j: (layer, half, 0, j), **wmode),
            pl.BlockSpec((None, None, tf, d), lambda i, j: (layer, half, j, 0), **wmode),
        ],
        out_specs=pl.BlockSpec((tm, d), lambda i, j: (i, 0)),
        scratch_shapes=[pltpu.VMEM((2, tm, d), BF16), pltpu.VMEM((tm, d), F32)],
        compiler_params=_cparams(("arbitrary", "arbitrary")),
        name="ffn",
    )(x, x, g.reshape(1, d), wg, wu, wd)


GATE_WIDTH = N_GROUPS * LANES
GATE_ROWS = 16
_C_Q = 0
_C_K = _C_Q + A_WIDTH
_C_V = _C_K + 2 * KV_WIDTH
_C_C = _C_V + 2 * KV_WIDTH
_C_GATE = _C_C + 2 * KV_WIDTH
_C_U = _C_GATE + GATE_WIDTH
_C_GB = _C_U + A_WIDTH
_C_GC = _C_GB + A_WIDTH
_C_END = _C_GC + A_WIDTH
CONV_HALO = 8


def _inproj_kernel(x_ref, pos_ref, g_ref, w_ref, qnw_ref, knw_ref, cw_ref, pq_ref, pk_ref, freq_ref,
                   qt_ref, kslc_ref, kwin_ref, vt_ref, kc_ref, vc_ref, gates_ref, ob_ref,
                   vext_ref, *, tm, tiles_per_batch):
    i = pl.program_id(0)
    chunks = range(tm // LANES)

    def transposed(a, c):
        return jnp.concatenate([a[r * LANES:(r + 1) * LANES, c * LANES:(c + 1) * LANES].T for r in chunks], axis=1)

    def per_group(k):
        low = lax.broadcasted_iota(jnp.int32, (1, LANES), 1) < HEAD_DIM
        return [jnp.where(low, k, 0.0), jnp.where(low, pltpu.roll(k, HEAD_DIM, axis=1), 0.0)]

    @pl.when(i % tiles_per_batch == 0)
    def _():
        vext_ref[0:CONV_HALO, :] = jnp.zeros((CONV_HALO, A_WIDTH), F32)

    h = _rms(x_ref[...], g_ref[...]).astype(BF16)

    def proj(c0, width):
        return _dot(h, w_ref[:, c0:c0 + width])

    q_raw = proj(_C_Q, A_WIDTH)
    k2 = proj(_C_K, 2 * KV_WIDTH)
    cos_t, sin_t = _rope_tables(pos_ref[...].astype(F32), freq_ref[...])
    v = proj(_C_GC, A_WIDTH) * proj(_C_U, A_WIDTH)
    gate_b = proj(_C_GB, A_WIDTH)

    q = _group_rms(q_raw, qnw_ref[...], pq_ref[...])
    q = _rope(q, cos_t, sin_t) * (HEAD_DIM ** -0.5)
    for c in range(A_WIDTH // LANES):
        qt_ref[c * LANES:(c + 1) * LANES, :] = transposed(q, c).astype(BF16)

    tok = (i % tiles_per_batch) * tm + lax.broadcasted_iota(jnp.int32, (tm, 1), 0)
    blk_onehot = (lax.broadcasted_iota(jnp.int32, (1, LANES), 1) == (tok >> int(math.log2(SEL_BLOCK))))
    blk_onehot = jnp.where(blk_onehot, 1.0, 0.0).astype(BF16)
    ks = _rope(_group_rms(k2[:, :KV_WIDTH], knw_ref[1:2, :], pk_ref[...]), cos_t, sin_t)
    for g, kg in enumerate(per_group(ks)):
        kslc_ref[g] = jnp.concatenate([blk_onehot, kg.astype(BF16)], axis=1)
    kw = _rope(_group_rms(k2[:, KV_WIDTH:], knw_ref[2:3, :], pk_ref[...]), cos_t, sin_t)
    for g, kg in enumerate(per_group(kw)):
        kwin_ref[g] = kg.astype(BF16)

    v2 = proj(_C_V, 2 * KV_WIDTH)
    ones_rows = jnp.where(lax.broadcasted_iota(jnp.int32, (LANES - HEAD_DIM, 1), 0) == 0, 1.0, 0.0)
    ones_rows = jnp.broadcast_to(ones_rows, (LANES - HEAD_DIM, tm)).astype(BF16)
    for branch in range(2):
        vt = transposed(v2, branch).astype(BF16)
        for g in range(N_GROUPS):
            vt_ref[branch, g, 0:HEAD_DIM, :] = vt[g * HEAD_DIM:(g + 1) * HEAD_DIM, :]
            vt_ref[branch, g, HEAD_DIM:, :] = ones_rows
    c2 = proj(_C_C, 2 * KV_WIDTH)
    kc_ref[...] = c2[:, :KV_WIDTH]
    vc_ref[...] = c2[:, KV_WIDTH:]
    gate_logits = proj(_C_GATE, GATE_WIDTH)

    vext_ref[CONV_HALO:CONV_HALO + tm, :] = v
    y = (cw_ref[2:3, :] * v
         + cw_ref[1:2, :] * vext_ref[pl.ds(CONV_HALO - 1, tm), :]
         + cw_ref[0:1, :] * vext_ref[pl.ds(CONV_HALO - 2, tm), :])
    ob_ref[...] = (gate_b * y).astype(BF16)
    vext_ref[0:CONV_HALO, :] = vext_ref[tm:tm + CONV_HALO, :]
    gates_ref[...] = 1.0 / (1.0 + jnp.exp(-gate_logits))


def _inproj(x, pos_col, g, w, qnw, knw, cw, pq, pk, freq, *, seq, tm=512):
    n, d = x.shape
    bsz, tpb = n // seq, seq // tm
    row = lambda i: (i, 0)
    fixed = lambda i: (0, 0)
    widths = [(KV_WIDTH, F32), (KV_WIDTH, F32), (GATE_WIDTH, F32), (A_WIDTH, BF16)]
    out_shape = [jax.ShapeDtypeStruct((bsz, A_WIDTH, seq), BF16),
                 jax.ShapeDtypeStruct((bsz, N_GROUPS, seq, 2 * LANES), BF16),
                 jax.ShapeDtypeStruct((bsz, N_GROUPS, seq, LANES), BF16),
                 jax.ShapeDtypeStruct((bsz, 2, N_GROUPS, LANES, seq), BF16)]
    out_specs = [pl.BlockSpec((None, A_WIDTH, tm), lambda i: (i // tpb, 0, i % tpb)),
                 pl.BlockSpec((None, N_GROUPS, tm, 2 * LANES), lambda i: (i // tpb, 0, i % tpb, 0)),
                 pl.BlockSpec((None, N_GROUPS, tm, LANES), lambda i: (i // tpb, 0, i % tpb, 0)),
                 pl.BlockSpec((None, 2, N_GROUPS, LANES, tm), lambda i: (i // tpb, 0, 0, 0, i % tpb))]
    return pl.pallas_call(
        functools.partial(_inproj_kernel, tm=tm, tiles_per_batch=tpb),
        out_shape=out_shape + [jax.ShapeDtypeStruct((n, wd), dt) for wd, dt in widths],
        grid=(n // tm,),
        in_specs=[
            pl.BlockSpec((tm, d), row),
            pl.BlockSpec((tm, 1), row),
            pl.BlockSpec((1, d), fixed),
            pl.BlockSpec(w.shape, fixed),
            pl.BlockSpec(qnw.shape, fixed),
            pl.BlockSpec(knw.shape, fixed),
            pl.BlockSpec(cw.shape, fixed),
            pl.BlockSpec(pq.shape, fixed),
            pl.BlockSpec(pk.shape, fixed),
            pl.BlockSpec(freq.shape, fixed),
        ],
        out_specs=out_specs + [pl.BlockSpec((tm, wd), row) for wd, _ in widths],
        scratch_shapes=[pltpu.VMEM((tm + CONV_HALO, A_WIDTH), F32)],
        compiler_params=_cparams(("arbitrary",)),
        name="inproj",
    )(x, pos_col, g.reshape(1, d), w, qnw, knw, cw, pq, pk, freq)


def _gelu_tanh(x):
    return 0.5 * x * (1.0 + jnp.tanh(math.sqrt(2.0 / math.pi) * (x + 0.044715 * (x * x * x))))


def _compress_kernel(xk_ref, xv_ref, pos_ref, pe_ref, wk_ref, wv_ref, w2_ref, w2vt_ref, knw_ref, pk_ref, freq_ref,
                     kc_ref, vct_ref, *, ncp):
    def hidden(x_ref, pe_row, w_ref):
        x = jnp.concatenate([x_ref[pl.ds(j, ncp, stride=CMP_STRIDE), :] for j in range(CMP_STRIDE)], axis=1)
        a = _dot((x + pe_ref[pe_row:pe_row + 1, :]).astype(BF16), w_ref[0])
        b = _dot((x + pe_ref[pe_row + 1:pe_row + 2, :]).astype(BF16), w_ref[1])
        hid = a + pltpu.roll(b, ncp - 1, axis=0)
        return _gelu_tanh(hid)

    low = lax.broadcasted_iota(jnp.int32, (1, LANES), 1) < HEAD_DIM
    kc = _dot(hidden(xk_ref, 0, wk_ref).astype(BF16), w2_ref[...])
    kc = _group_rms(kc, knw_ref[0:1, :], pk_ref[...])
    cos_t, sin_t = _rope_tables(pos_ref[...].astype(F32), freq_ref[...])
    kc = _rope(kc, cos_t, sin_t)
    kc_ref[0] = jnp.where(low, kc, 0.0).astype(BF16)
    kc_ref[1] = jnp.where(low, pltpu.roll(kc, HEAD_DIM, axis=1), 0.0).astype(BF16)
    hv = hidden(xv_ref, 2, wv_ref)
    hv_t = jnp.concatenate(
        [jnp.concatenate([hv[r * LANES:(r + 1) * LANES, c * LANES:(c + 1) * LANES].T for r in range(ncp // LANES)],
                         axis=1) for c in range(hv.shape[1] // LANES)], axis=0)
    vct = _dot(w2vt_ref[...], hv_t.astype(BF16))
    for g in range(N_GROUPS):
        vct_ref[g, 0:HEAD_DIM, :] = vct[g * HEAD_DIM:(g + 1) * HEAD_DIM, :].astype(BF16)
        vct_ref[g, HEAD_DIM:, :] = jnp.zeros((LANES - HEAD_DIM, ncp), BF16)


def _compress(xk, xv, pos_c, pe, wk, wv, w2k, w2vt, knw, pk, freq, *, seq):
    bsz, ncp = pos_c.shape[:2]
    bat = lambda b: (b, 0, 0)
    fix2 = lambda b: (0, 0)
    fix3 = lambda b: (0, 0, 0)
    return pl.pallas_call(
        functools.partial(_compress_kernel, ncp=ncp),
        out_shape=[jax.ShapeDtypeStruct((bsz, N_GROUPS, ncp, LANES), BF16),
                   jax.ShapeDtypeStruct((bsz, N_GROUPS, LANES, ncp), BF16)],
        grid=(bsz,),
        in_specs=[
            pl.BlockSpec((seq, KV_WIDTH), lambda b: (b, 0)),
            pl.BlockSpec((seq, KV_WIDTH), lambda b: (b, 0)),
            pl.BlockSpec((None, ncp, 1), bat),
            pl.BlockSpec(pe.shape, fix2),
            pl.BlockSpec(wk.shape, fix3),
            pl.BlockSpec(wv.shape, fix3),
            pl.BlockSpec(w2k.shape, fix2),
            pl.BlockSpec(w2vt.shape, fix2),
            pl.BlockSpec(knw.shape, fix2),
            pl.BlockSpec(pk.shape, fix2),
            pl.BlockSpec(freq.shape, fix2),
        ],
        out_specs=[pl.BlockSpec((None, N_GROUPS, ncp, LANES), lambda b: (b, 0, 0, 0)),
                   pl.BlockSpec((None, N_GROUPS, LANES, ncp), lambda b: (b, 0, 0, 0))],
        compiler_params=_cparams(("parallel",)),
        name="compress",
    )(xk, xv, pos_c, pe, wk, wv, w2k, w2vt, knw, pk, freq)


def _head_columns(qt_ref):
    cols = []
    for pair in range(qt_ref.shape[0] // LANES):
        both = qt_ref[pair * LANES:(pair + 1) * LANES, :]
        cols += [both, jnp.concatenate([both[HEAD_DIM:, :], both[:HEAD_DIM, :]], axis=0)]
    return cols


def _software_pipeline(n, scores, probs, finish):
    s, p = {}, {}
    for step in range(n + 2):
        if step >= 2:
            finish(step - 2, p.pop(step - 2))
        if 1 <= step <= n:
            p[step - 1] = probs(step - 1, s.pop(step - 1))
        if step < n:
            s[step] = scores(step)


def _gate_row(gt_ref, branch, hd, group=0):
    r = group * GATE_ROWS + branch * HEADS_PER_GROUP + hd
    return gt_ref[r:r + 1, :]


def _cmp_kernel(qt_ref, kc_ref, vct_ref, ovt_ref, gt_ref, ot_ref, sbt_ref, *, tq, ncp):
    i = pl.program_id(1)
    t = i * tq + lax.broadcasted_iota(jnp.int32, (1, tq), 1)
    any_valid = (t >= CMP_BLOCK - 1).astype(F32)
    heads = _head_columns(qt_ref)
    cmp_per_sel = SEL_BLOCK // CMP_STRIDE
    groups = range(N_GROUPS)

    def run(nrow):
        nblk = nrow // cmp_per_sel
        blk_end = lax.broadcasted_iota(jnp.int32, (nrow, 1), 0) * CMP_STRIDE + (CMP_BLOCK - 1)
        valid = blk_end <= t
        kc = [kc_ref[g, 0:nrow, :] for g in groups]
        vct = [vct_ref[g, :, 0:nrow] for g in groups]
        psum_parts = [[] for _ in groups]

        def scores(u):
            return jnp.where(valid, _dot(kc[u // HEADS_PER_GROUP], heads[u]), NEG)

        def probs(u, s):
            e = jnp.exp(s - jnp.max(s, axis=0, keepdims=True))
            return e * (any_valid / jnp.sum(e, axis=0, keepdims=True))

        def finish(u, p):
            g, hd = divmod(u, HEADS_PER_GROUP)
            o_t = _dot(vct[g], p.astype(BF16))[:HEAD_DIM, :]
            ot_ref[u * HEAD_DIM:(u + 1) * HEAD_DIM, :] = o_t * _gate_row(gt_ref, _BR_CMP, hd, g)
            psum_parts[g].append(p)

        _software_pipeline(N_HEADS, scores, probs, finish)

        ovt = ovt_ref[0:nblk, 0:nrow]
        blk = lax.broadcasted_iota(jnp.int32, (nblk, 1), 0)
        sel_start = blk * SEL_BLOCK
        cur = (t >> int(math.log2(SEL_BLOCK))) << int(math.log2(SEL_BLOCK))
        imp = []
        for parts in psum_parts:
            psum = (parts[0] + parts[1]) + (parts[2] + parts[3])
            hi = psum.astype(BF16)
            lo = (psum - hi.astype(F32)).astype(BF16)
            x = jnp.where(sel_start <= t, _dot(ovt, hi) + _dot(ovt, lo), -1.0)
            imp.append(jnp.where(sel_start == cur, 1e4, jnp.where(sel_start == 0, 1e4, x)))

        blk_f = blk.astype(F32)
        bias = [jnp.full((nblk, tq), NEG, F32) for _ in groups]
        for _ in range(SEL_TOPK):
            for g in groups:
                mx = jnp.max(imp[g], axis=0, keepdims=True)
                first = jnp.min(jnp.where(imp[g] == mx, blk_f, float(LANES)), axis=0, keepdims=True)
                pick = blk_f == first
                bias[g] = jnp.where(pick, 0.0, bias[g])
                imp[g] = jnp.where(pick, -3e38, imp[g])
        for g in groups:
            sbt_ref[g, 0:nblk, :] = bias[g].astype(BF16)
            if nblk < LANES:
                sbt_ref[g, nblk:, :] = jnp.full((LANES - nblk, tq), NEG, BF16)

    sizes = [LANES * (v + 1) for v in range(ncp // LANES)]
    assert sizes[0] // cmp_per_sel >= SEL_TOPK
    tokens_per_variant = LANES * CMP_STRIDE
    lax.switch((i * tq + tq - 1) // tokens_per_variant, [functools.partial(run, nrow) for nrow in sizes])


def _cmp_select(qt, kc, vct, ovt, gates_t, *, bsz, seq, tq=256):
    ncp = kc.shape[2]
    nq = seq // tq
    whole = lambda b, i: (b, 0, 0, 0)
    return pl.pallas_call(
        functools.partial(_cmp_kernel, tq=tq, ncp=ncp),
        out_shape=[jax.ShapeDtypeStruct((bsz, A_WIDTH, seq), F32),
                   jax.ShapeDtypeStruct((bsz, N_GROUPS, LANES, seq), BF16)],
        grid=(bsz, nq),
        in_specs=[
            pl.BlockSpec((None, A_WIDTH, tq), lambda b, i: (b, 0, i)),
            pl.BlockSpec((None, N_GROUPS, ncp, LANES), whole),
            pl.BlockSpec((None, N_GROUPS, LANES, ncp), whole),
            pl.BlockSpec(ovt.shape, lambda b, i: (0, 0)),
            pl.BlockSpec((N_GROUPS * GATE_ROWS, tq), lambda b, i: (0, b * nq + i)),
        ],
        out_specs=[pl.BlockSpec((None, A_WIDTH, tq), lambda b, i: (b, 0, i)),
                   pl.BlockSpec((None, N_GROUPS, LANES, tq), lambda b, i: (b, 0, 0, i))],
        compiler_params=_cparams(("parallel", "parallel")),
        name="cmp_select",
    )(qt, kc, vct, ovt, gates_t)


_SLC_UNROLL = 2


def _slc_kernel(qt_ref, sbt_ref, k_ref, vt_ref, gt_ref, prev_ref, o_ref,
                qa_ref, s_ref, p_ref, acc_ref, *, tq):
    i = pl.program_id(1)
    rows = HEADS_PER_GROUP * tq
    groups = range(N_GROUPS)
    gw = HEADS_PER_GROUP * HEAD_DIM
    for g in groups:
        qa_ref[g] = jnp.concatenate([jnp.concatenate([sbt_ref[g]] * HEADS_PER_GROUP, axis=1),
                                     jnp.concatenate(_head_columns(qt_ref.at[g * gw:(g + 1) * gw, :]), axis=1)],
                                    axis=0)
    p_ref[...] = jnp.zeros(p_ref.shape, BF16)
    acc_ref[...] = jnp.zeros(acc_ref.shape, F32)

    def accumulate(g, unit, alpha):
        j = jnp.where(unit <= 0, i, unit - 1)
        pv = _dot(vt_ref[g, :, pl.ds(pl.multiple_of(j * tq, tq), tq)], p_ref[g])
        acc_ref[g] = alpha * acc_ref[g] + pv

    def probs(g, m):
        s = s_ref[g]
        m_new = jnp.maximum(m, jnp.max(s, axis=0, keepdims=True))
        p_ref[g] = jnp.exp(s - m_new).astype(BF16)
        return m_new, jnp.exp(m - m_new)

    def scores(g, j, masked):
        s = _dot(k_ref[g, pl.ds(pl.multiple_of(j * tq, tq), tq), :], qa_ref[g])
        if masked:
            causal = (lax.broadcasted_iota(jnp.int32, (tq, 1), 0)
                      <= (lax.broadcasted_iota(jnp.int32, (1, rows), 1) & (tq - 1)))
            s = jnp.where(causal, s, NEG)
        s_ref[g] = s

    def body(k, carry):
        out = []
        for g in groups:
            m, alpha = carry[g]
            accumulate(g, k - 1, alpha)
            out.append(probs(g, m))
            scores(g, k, False)
        return tuple(out)

    def steps(first, count, carry):
        for k in range(count):
            carry = body(first + k, carry)
        return carry

    for g in groups:
        scores(g, i, True)
    carry = tuple((jnp.full((1, rows), -3e38, F32), jnp.ones((1, rows), F32)) for _ in groups)
    trips = i >> int(math.log2(_SLC_UNROLL))
    carry = lax.fori_loop(0, trips, lambda k, c: steps(k * _SLC_UNROLL, _SLC_UNROLL, c), carry)
    done = trips * _SLC_UNROLL
    size = _SLC_UNROLL // 2
    while size >= 1:
        carry = lax.cond((i & size) != 0, functools.partial(steps, done, size), lambda c: c, carry)
        done = done + (i & size)
        size //= 2
    for g in groups:
        m, alpha = carry[g]
        accumulate(g, i - 1, alpha)
        m, alpha = probs(g, m)
        accumulate(g, i, alpha)

    for g in groups:
        gate = jnp.concatenate([_gate_row(gt_ref, _BR_SLC, hd, g) for hd in range(HEADS_PER_GROUP)], axis=1)
        out_t = acc_ref[g, 0:HEAD_DIM, :] * (gate / acc_ref[g, HEAD_DIM:HEAD_DIM + 1, :])
        for pair in range(HEADS_PER_GROUP // 2):
            lanes = slice(g * gw + pair * LANES, g * gw + (pair + 1) * LANES)
            for c in range(tq // LANES):
                col = 2 * pair * tq + c * LANES
                both = jnp.concatenate([out_t[:, col:col + LANES], out_t[:, col + tq:col + tq + LANES]], axis=0)
                both = both + prev_ref[lanes, c * LANES:(c + 1) * LANES]
                o_ref[c * LANES:(c + 1) * LANES, lanes] = both.T.astype(BF16)


def _slc(qt, sbt, k_aug, vt_aug, gates_t, prev, *, bsz, seq, tq=512):
    nq = seq // tq
    rows = HEADS_PER_GROUP * tq
    tile = lambda b, i: (b, 0, i)
    return pl.pallas_call(
        functools.partial(_slc_kernel, tq=tq),
        out_shape=jax.ShapeDtypeStruct((bsz * seq, A_WIDTH), BF16),
        grid=(bsz, nq),
        in_specs=[
            pl.BlockSpec((None, A_WIDTH, tq), tile),
            pl.BlockSpec((None, N_GROUPS, LANES, tq), lambda b, i: (b, 0, 0, i)),
            pl.BlockSpec((None, N_GROUPS, seq, 2 * LANES), lambda b, i: (b, 0, 0, 0)),
            pl.BlockSpec((None, None, N_GROUPS, LANES, seq), lambda b, i: (b, 0, 0, 0, 0)),
            pl.BlockSpec((N_GROUPS * GATE_ROWS, tq), lambda b, i: (0, b * nq + i)),
            pl.BlockSpec((None, A_WIDTH, tq), tile),
        ],
        out_specs=pl.BlockSpec((tq, A_WIDTH), lambda b, i: (b * nq + i, 0)),
        scratch_shapes=[pltpu.VMEM((N_GROUPS, 2 * LANES, rows), BF16), pltpu.VMEM((N_GROUPS, tq, rows), F32),
                        pltpu.VMEM((N_GROUPS, tq, rows), BF16), pltpu.VMEM((N_GROUPS, LANES, rows), F32)],
        compiler_params=_cparams(("parallel", "parallel")),
        name="slc_attn",
    )(qt, sbt, k_aug, vt_aug, gates_t, prev)


def _win_kernel(qt_ref, k_ref, vt_ref, band_ref, gt_ref, prev_ref, ot_ref, *, tq):
    i = pl.program_id(1)
    span = WINDOW + tq
    start = pl.multiple_of(jnp.maximum(i - WINDOW // tq, 0) * tq, tq)
    k = [k_ref[g, pl.ds(start, span), :] for g in range(N_GROUPS)]
    vt = [vt_ref[g, :, pl.ds(start, span)] for g in range(N_GROUPS)]
    heads = _head_columns(qt_ref)

    def scores(u):
        return _dot(k[u // HEADS_PER_GROUP], heads[u]) + band_ref[...]

    def probs(u, s):
        return jnp.exp(s - jnp.max(s, axis=0, keepdims=True)).astype(BF16)

    def finish(u, p):
        g, hd = divmod(u, HEADS_PER_GROUP)
        acc = _dot(vt[g], p)
        rows = slice(u * HEAD_DIM, (u + 1) * HEAD_DIM)
        scale = _gate_row(gt_ref, _BR_WIN, hd, g) / acc[HEAD_DIM:HEAD_DIM + 1, :]
        ot_ref[rows, :] = prev_ref[rows, :] + acc[:HEAD_DIM, :] * scale

    _software_pipeline(N_HEADS, scores, probs, finish)


def _window_band(tq):
    edge = WINDOW // tq
    v = jnp.arange(edge + 1)[:, None, None]
    kpos = jnp.maximum(v - edge, 0) * tq + jnp.arange(WINDOW + tq)[None, :, None]
    t = v * tq + jnp.arange(tq)[None, None, :]
    ok = (t - kpos >= 0) & (t - kpos < WINDOW)
    return jnp.where(ok, 0.0, NEG).astype(F32)


def _win(qt, k, vt, gates_t, prev, *, bsz, seq, tq=256):
    nq = seq // tq
    band = _window_band(tq)
    last = band.shape[0] - 1
    tile = lambda b, i: (b, 0, i)
    return pl.pallas_call(
        functools.partial(_win_kernel, tq=tq),
        out_shape=jax.ShapeDtypeStruct((bsz, A_WIDTH, seq), F32),
        grid=(bsz, nq),
        in_specs=[
            pl.BlockSpec((None, A_WIDTH, tq), tile),
            pl.BlockSpec((None, N_GROUPS, seq, LANES), lambda b, i: (b, 0, 0, 0)),
            pl.BlockSpec((None, None, N_GROUPS, LANES, seq), lambda b, i: (b, 1, 0, 0, 0)),
            pl.BlockSpec((None,) + band.shape[1:], lambda b, i: (jnp.minimum(i, last), 0, 0)),
            pl.BlockSpec((N_GROUPS * GATE_ROWS, tq), lambda b, i: (0, b * nq + i)),
            pl.BlockSpec((None, A_WIDTH, tq), tile),
        ],
        out_specs=pl.BlockSpec((None, A_WIDTH, tq), tile),
        compiler_params=_cparams(("parallel", "parallel")),
        name="win_attn",
    )(qt, k, vt, band, gates_t, prev)


def _outproj_kernel(x_ref, oa_ref, ob_ref, w_ref, o_ref):
    y = _dot(oa_ref[...], w_ref[0:A_WIDTH, :]) + _dot(ob_ref[...], w_ref[A_WIDTH:, :])
    o_ref[...] = x_ref[...] + y


def _outproj(x, oa, ob, w, *, tm=512):
    n, d = x.shape
    row = lambda i: (i, 0)
    return pl.pallas_call(
        _outproj_kernel,
        out_shape=jax.ShapeDtypeStruct((n, d), F32),
        grid=(n // tm,),
        in_specs=[
            pl.BlockSpec((tm, d), row),
            pl.BlockSpec((tm, A_WIDTH), row),
            pl.BlockSpec((tm, A_WIDTH), row),
            pl.BlockSpec(w.shape, lambda i: (0, 0)),
        ],
        out_specs=pl.BlockSpec((tm, d), row),
        compiler_params=_cparams(("parallel",)),
        name="outproj",
    )(x, oa, ob, w)


def _pool_kernel(x_ref, halo_ref, g_ref, w_ref, sc_ref, o_ref, ext_ref, tmp_ref, *, tm, tiles_per_batch):
    i = pl.program_id(0)
    first = i % tiles_per_batch == 0
    x = x_ref[...]
    h = _rms(x, g_ref[...])
    hh = _rms(halo_ref[...], g_ref[...])
    pad, body = _POOL_PAD, _POOL_PAD + POOL_HALO
    ext_ref[0:pad, :] = jnp.zeros((pad, h.shape[1]), F32)
    ext_ref[pad:body, :] = jnp.where(first, 0.0, hh)
    ext_ref[body:body + tm, :] = h
    gw = h.shape[1] // len(POOL_WINDOWS)
    for slot in range(2):
        tmp_ref[slot, 0:pad, :] = jnp.zeros((pad, gw), F32)
    t = (i % tiles_per_batch) * tm + lax.broadcasted_iota(jnp.int32, (tm, 1), 0)
    span = POOL_HALO + tm
    for gi, win in enumerate(POOL_WINDOWS):
        c0 = gi * gw
        hg = h[:, c0:c0 + gw]
        read = lambda start, c0=c0: ext_ref[pl.ds(start, span), c0:c0 + gw]
        shift, slot = 1, 0
        while shift < win:
            tmp_ref[slot, pad:pad + span, :] = read(pad) + read(pad - shift)
            read = lambda start, slot=slot: tmp_ref[slot, pl.ds(start, span), :]
            shift, slot = 2 * shift, 1 - slot
        tot = read(pad)[POOL_HALO:, :]
        cnt = jnp.minimum(t + 1, win).astype(F32)
        y = _dot((tot / cnt - hg).astype(BF16), w_ref[gi])
        o_ref[:, c0:c0 + gw] = x[:, c0:c0 + gw] + y * sc_ref[:, c0:c0 + gw]


def _pool(x, g, w, scale, *, seq, tm=512):
    n, d = x.shape
    hb = tm // POOL_HALO
    return pl.pallas_call(
        functools.partial(_pool_kernel, tm=tm, tiles_per_batch=seq // tm),
        out_shape=jax.ShapeDtypeStruct((n, d), F32),
        grid=(n // tm,),
        in_specs=[
            pl.BlockSpec((tm, d), lambda i: (i, 0)),
            pl.BlockSpec((POOL_HALO, d), lambda i: (jnp.maximum(i * hb - 1, 0), 0)),
            pl.BlockSpec((1, d), lambda i: (0, 0)),
            pl.BlockSpec(w.shape, lambda i: (0, 0, 0)),
            pl.BlockSpec((1, d), lambda i: (0, 0)),
        ],
        out_specs=pl.BlockSpec((tm, d), lambda i: (i, 0)),
        scratch_shapes=[pltpu.VMEM((_POOL_PAD + POOL_HALO + tm, d), F32),
                        pltpu.VMEM((2, _POOL_PAD + POOL_HALO + tm, d // len(POOL_WINDOWS)), F32)],
        compiler_params=_cparams(("parallel",)),
        name="pool_mixer",
    )(x, x, g.reshape(1, d), w, scale.reshape(1, d))


def _pack_w_in(w_in):
    o = 0
    q = w_in[:, o:o + A_WIDTH]; o += A_WIDTH
    kc, vc, ks, vs, kw, vw = [w_in[:, o + k * KV_WIDTH:o + (k + 1) * KV_WIDTH] for k in range(6)]
    o += 6 * KV_WIDTH
    gates = w_in[:, o:o + N_BRANCH * N_HEADS]; o += N_BRANCH * N_HEADS
    u, gb, gc = [w_in[:, o + k * A_WIDTH:o + (k + 1) * A_WIDTH] for k in range(3)]
    gates = gates.reshape(-1, N_GROUPS, HEADS_PER_GROUP, N_BRANCH).transpose(0, 1, 3, 2)
    gates = gates.reshape(-1, N_GROUPS, N_BRANCH * HEADS_PER_GROUP)
    gates = jnp.pad(gates, ((0, 0), (0, 0), (0, LANES - N_BRANCH * HEADS_PER_GROUP))).reshape(-1, GATE_WIDTH)
    return jnp.concatenate([q, ks, kw, vs, vw, kc, vc, gates, u, gb, gc], axis=1).astype(BF16)


def _group_mean_matrix(width):
    r = jnp.arange(width) // HEAD_DIM
    return jnp.where(r[:, None] == r[None, :], 1.0 / HEAD_DIM, 0.0).astype(BF16)


def _rope_freq_row():
    lane = jnp.arange(LANES) % HEAD_DIM
    freqs = ROPE_THETA ** (-jnp.arange(0, ROPE_DIM, 2, dtype=F32) / ROPE_DIM)
    return jnp.where(lane < ROPE_DIM, freqs[lane % ROPE_HALF], 0.0).astype(F32).reshape(1, LANES)


def _per_group(a, bsz, rows):
    return a.reshape(bsz, rows, N_GROUPS, HEAD_DIM).transpose(0, 2, 1, 3)


def _hybrid_mixer(x, positions, g, w_in, q_norm, k_norm, cmp_pos, cmp_w1, cmp_w2, conv_w, w_out, *, bsz, seq):
    n = bsz * seq
    ncp = seq // CMP_STRIDE
    freq = _rope_freq_row()
    pq = _group_mean_matrix(A_WIDTH)
    pk = _group_mean_matrix(KV_WIDTH)
    qnw = jnp.tile(q_norm, N_HEADS).reshape(1, A_WIDTH)
    knw = jnp.tile(k_norm, (1, N_GROUPS))

    q_t, k_slc, k_win, v_t, kc_raw, vc_raw, gates, ob = _inproj(
        x, positions.reshape(n, 1), g, _pack_w_in(w_in), qnw, knw, conv_w, pq, pk, freq, seq=seq)

    eye_g = jnp.eye(N_GROUPS, dtype=F32)
    seg = CMP_STRIDE * KV_WIDTH

    def seg_weights(w1):
        w1r = w1.reshape(CMP_BLOCK, HEAD_DIM, CMP_HIDDEN)
        halves = [jnp.einsum('jdh,gk->jgdkh', w1r[a:a + CMP_STRIDE], eye_g).reshape(seg, N_GROUPS * CMP_HIDDEN)
                  for a in (0, CMP_STRIDE)]
        return jnp.stack(halves).astype(BF16)

    def seg_pos(pe):
        return [jnp.broadcast_to(pe[a:a + CMP_STRIDE, None, :], (CMP_STRIDE, N_GROUPS, HEAD_DIM)).reshape(seg)
                for a in (0, CMP_STRIDE)]

    pe = jnp.stack(seg_pos(cmp_pos[0]) + seg_pos(cmp_pos[1]))
    w2 = jnp.stack([jnp.einsum('hd,gk->ghkd', cmp_w2[a], eye_g).reshape(N_GROUPS * CMP_HIDDEN, KV_WIDTH)
                    for a in range(2)]).astype(BF16)
    blk_end = jnp.minimum(jnp.arange(ncp) * CMP_STRIDE + CMP_BLOCK - 1, seq - 1)
    pos_c = positions[:, blk_end].reshape(bsz, ncp, 1)
    kc_g, vc_t = _compress(kc_raw, vc_raw, pos_c, pe, seg_weights(cmp_w1[0]), seg_weights(cmp_w1[1]), w2[0], w2[1].T,
                           knw, pk, freq, seq=seq)

    gates_t = gates.reshape(n, N_GROUPS, LANES)[:, :, :GATE_ROWS].transpose(1, 2, 0).reshape(N_GROUPS * GATE_ROWS, n)
    cstart = jnp.arange(ncp) * CMP_STRIDE
    sstart = jnp.arange(LANES) * SEL_BLOCK
    ovt = ((cstart[None, :] < sstart[:, None] + SEL_BLOCK)
           & (cstart[None, :] + CMP_BLOCK > sstart[:, None])).astype(BF16)
    o_t, sel_bias = _cmp_select(q_t, kc_g, vc_t, ovt, gates_t, bsz=bsz, seq=seq)

    o_t = _win(q_t, k_win, v_t, gates_t, o_t, bsz=bsz, seq=seq)
    o_a = _slc(q_t, sel_bias, k_slc, v_t, gates_t, o_t, bsz=bsz, seq=seq)

    return _outproj(x, o_a, ob, w_out.astype(BF16))


def kernel(x, positions, ffn_norm, ffn_w_gate, ffn_w_up, ffn_w_down, mix_norm, hyb_w_in, hyb_q_norm, hyb_k_norm,
           hyb_cmp_pos, hyb_cmp_w1, hyb_cmp_w2, hyb_conv_w, hyb_w_out, pool_w, pool_scale):
    bsz, seq, d = x.shape
    depth = ffn_norm.shape[0]
    assert seq % 2048 == 0 and seq // SEL_BLOCK <= LANES and seq // SEL_BLOCK >= SEL_TOPK
    wg, wu, wd = (w.astype(BF16) for w in (ffn_w_gate, ffn_w_up, ffn_w_down))
    xf = x.reshape(bsz * seq, d)
    for layer in range(depth):
        xf = _ffn(xf, ffn_norm[layer, 0], wg, wu, wd, layer, 0)
        i = layer // 2
        if layer % 2 == 0:
            xf = _hybrid_mixer(xf, positions, mix_norm[layer], hyb_w_in[i], hyb_q_norm[i], hyb_k_norm[i],
                               hyb_cmp_pos[i], hyb_cmp_w1[i], hyb_cmp_w2[i], hyb_conv_w[i], hyb_w_out[i],
                               bsz=bsz, seq=seq)
        else:
            xf = _pool(xf, mix_norm[layer], pool_w[i].astype(BF16), pool_scale[i], seq=seq)
        xf = _ffn(xf, ffn_norm[layer, 1], wg, wu, wd, layer, 1)
    return xf.reshape(bsz, seq, d)
```

```python
import functools
import math

import jax
import jax.numpy as jnp
from jax import lax
from jax.experimental import pallas as pl
from jax.experimental.pallas import tpu as pltpu

F32 = jnp.float32
BF16 = jnp.bfloat16

HEAD_DIM = 64
N_GROUPS = 2
HEADS_PER_GROUP = 4
N_HEADS = N_GROUPS * HEADS_PER_GROUP
A_WIDTH = N_HEADS * HEAD_DIM
KV_WIDTH = N_GROUPS * HEAD_DIM
ROPE_DIM = HEAD_DIM // 4
ROPE_HALF = ROPE_DIM // 2
ROPE_THETA = 500000.0
CMP_BLOCK = 32
CMP_STRIDE = 16
CMP_HIDDEN = 2 * HEAD_DIM
SEL_BLOCK = 64
SEL_TOPK = 16
WINDOW = 512
N_BRANCH = 3
_BR_CMP, _BR_SLC, _BR_WIN = 0, 1, 2
CONV_WIDTH = 3
POOL_WINDOWS = (2, 4, 8, 16)
POOL_HALO = 16
_POOL_PAD = 8
EPS = 1e-6
NEG = -1e30

LANES = 128
VMEM_LIMIT_BYTES = 56 * 1024 * 1024


def _cparams(sem):
    return pltpu.CompilerParams(dimension_semantics=sem, vmem_limit_bytes=VMEM_LIMIT_BYTES)


def _dot(a, b):
    return jnp.dot(a, b, preferred_element_type=F32)


def _dot_split(a, b):
    hi = a.astype(BF16)
    lo = (a - hi.astype(F32)).astype(BF16)
    return _dot(hi, b) + _dot(lo, b)


def _rms(x, g):
    ms = jnp.mean(x * x, axis=-1, keepdims=True)
    return x * lax.rsqrt(ms + EPS) * g


def _group_rms(x, g, pmat):
    ms = _dot_split(x * x, pmat)
    return x * lax.rsqrt(ms + EPS) * g


def _rope_tables(pos_col, freq_row):
    ang = pos_col * freq_row
    lane = lax.broadcasted_iota(jnp.int32, (1, LANES), 1) & (HEAD_DIM - 1)
    sign = jnp.where(lane < ROPE_HALF, -1.0, 1.0).astype(F32)
    return jnp.cos(ang), jnp.sin(ang) * sign


def _rope(x, cos_t, sin_t):
    w = x.shape[1]
    reps = w // LANES
    if reps > 1:
        cos_t = jnp.concatenate([cos_t] * reps, axis=1)
        sin_t = jnp.concatenate([sin_t] * reps, axis=1)
    lane = lax.broadcasted_iota(jnp.int32, (1, w), 1) & (HEAD_DIM - 1)
    partner = jnp.where(lane < ROPE_HALF,
                        pltpu.roll(x, w - ROPE_HALF, axis=1),
                        pltpu.roll(x, ROPE_HALF, axis=1))
    return x * cos_t + partner * sin_t


def _ffn_kernel(x_ref, xnext_ref, g_ref, wg_ref, wu_ref, wd_ref, o_ref, h_ref, acc_ref, *, nj):
    i, j = pl.program_id(0), pl.program_id(1)
    slot = i % 2

    @pl.when((i == 0) & (j == 0))
    def _():
        h_ref[0] = _rms(x_ref[...], g_ref[...]).astype(BF16)

    def step(first, last):
        h = h_ref[slot]
        a = _dot(h, wg_ref[...])
        b = _dot(h, wu_ref[...])
        act = a * (1.0 / (1.0 + jnp.exp(-a))) * b
        y = _dot(act.astype(BF16), wd_ref[...])
        if not first:
            y = acc_ref[...] + y
        if last:
            h_ref[1 - slot] = _rms(xnext_ref[...], g_ref[...]).astype(BF16)
            o_ref[...] = x_ref[...] + 0.5 * y
        else:
            acc_ref[...] = y

    for jj in range(nj):
        pl.when(j == jj)(functools.partial(step, jj == 0, jj == nj - 1))


def _ffn(x, g, wg, wu, wd, layer, half, *, tm=512, tf=2816):
    n, d = x.shape
    dff = wg.shape[-1]
    ni, nj = n // tm, dff // tf
    wmode = dict(pipeline_mode=pl.Buffered(1)) if nj == 1 else {}
    return pl.pallas_call(
        functools.partial(_ffn_kernel, nj=nj),
        out_shape=jax.ShapeDtypeStruct((n, d), F32),
        grid=(ni, nj),
        in_specs=[
            pl.BlockSpec((tm, d), lambda i, j: (i, 0)),
            pl.BlockSpec((tm, d), lambda i, j: (jnp.minimum(i + 1, ni - 1), 0)),
            pl.BlockSpec((1, d), lambda i, j: (0, 0)),
            pl.BlockSpec((None, None, d, tf), lambda i, j: (layer, half, 0, j), **wmode),
            pl.BlockSpec((None, None, d, tf), lambda i, j: (layer, half, 0, j), **wmode),
            pl.BlockSpec((None, None, tf, d), lambda i, j: (layer, half, j, 0), **wmode),
        ],
        out_specs=pl.BlockSpec((tm, d), lambda i, j: (i, 0)),
        scratch_shapes=[pltpu.VMEM((2, tm, d), BF16), pltpu.VMEM((tm, d), F32)],
        compiler_params=_cparams(("arbitrary", "arbitrary")),
        name="ffn",
    )(x, x, g.reshape(1, d), wg, wu, wd)


GATE_WIDTH = N_GROUPS * LANES
GATE_ROWS = 16
_C_Q = 0
_C_K = _C_Q + A_WIDTH
_C_V = _C_K + 2 * KV_WIDTH
_C_C = _C_V + 2 * KV_WIDTH
_C_GATE = _C_C + 2 * KV_WIDTH
_C_U = _C_GATE + GATE_WIDTH
_C_GB = _C_U + A_WIDTH
_C_GC = _C_GB + A_WIDTH
CONV_HALO = 8
V_ROWS = HEAD_DIM + 16


def _inproj_kernel(x_ref, pos_ref, g_ref, w_ref, qnw_ref, knw_ref, cw_ref, pq_ref, pk_ref, freq_ref,
                   qt_ref, kslc_ref, kwin_ref, vt_ref, kc_ref, vc_ref, gates_ref, ob_ref,
                   vext_ref, *, tm, tiles_per_batch):
    i = pl.program_id(0)
    chunks = range(tm // LANES)

    def transposed(a, c):
        return jnp.concatenate([a[r * LANES:(r + 1) * LANES, c * LANES:(c + 1) * LANES].T for r in chunks], axis=1)

    def per_group(k):
        low = lax.broadcasted_iota(jnp.int32, (1, LANES), 1) < HEAD_DIM
        return [jnp.where(low, k, 0.0), jnp.where(low, pltpu.roll(k, HEAD_DIM, axis=1), 0.0)]

    @pl.when(i % tiles_per_batch == 0)
    def _():
        vext_ref[0:CONV_HALO, :] = jnp.zeros((CONV_HALO, A_WIDTH), F32)

    h = _rms(x_ref[...], g_ref[...]).astype(BF16)

    def proj(c0, width):
        return _dot(h, w_ref[:, c0:c0 + width])

    q_raw = proj(_C_Q, A_WIDTH)
    k2 = proj(_C_K, 2 * KV_WIDTH)
    cos_t, sin_t = _rope_tables(pos_ref[...].astype(F32), freq_ref[...])
    v = proj(_C_GC, A_WIDTH) * proj(_C_U, A_WIDTH)
    gate_b = proj(_C_GB, A_WIDTH)

    q = _group_rms(q_raw, qnw_ref[...], pq_ref[...])
    q = _rope(q, cos_t, sin_t) * (HEAD_DIM ** -0.5)
    for c in range(A_WIDTH // LANES):
        qt_ref[c * LANES:(c + 1) * LANES, :] = transposed(q, c).astype(BF16)

    tok = (i % tiles_per_batch) * tm + lax.broadcasted_iota(jnp.int32, (tm, 1), 0)
    blk_onehot = (lax.broadcasted_iota(jnp.int32, (1, LANES), 1) == (tok >> int(math.log2(SEL_BLOCK))))
    blk_onehot = jnp.where(blk_onehot, 1.0, 0.0).astype(BF16)
    ks = _rope(_group_rms(k2[:, :KV_WIDTH], knw_ref[1:2, :], pk_ref[...]), cos_t, sin_t)
    for g, kg in enumerate(per_group(ks)):
        kslc_ref[g] = jnp.concatenate([blk_onehot, kg.astype(BF16)], axis=1)
    kw = _rope(_group_rms(k2[:, KV_WIDTH:], knw_ref[2:3, :], pk_ref[...]), cos_t, sin_t)
    for g, kg in enumerate(per_group(kw)):
        kwin_ref[g] = kg.astype(BF16)

    v2 = proj(_C_V, 2 * KV_WIDTH)
    ones_rows = jnp.where(lax.broadcasted_iota(jnp.int32, (V_ROWS - HEAD_DIM, 1), 0) == 0, 1.0, 0.0)
    ones_rows = jnp.broadcast_to(ones_rows, (V_ROWS - HEAD_DIM, tm)).astype(BF16)
    for branch in range(2):
        vt = transposed(v2, branch).astype(BF16)
        for g in range(N_GROUPS):
            vt_ref[branch, g, 0:HEAD_DIM, :] = vt[g * HEAD_DIM:(g + 1) * HEAD_DIM, :]
            vt_ref[branch, g, HEAD_DIM:, :] = ones_rows
    c2 = proj(_C_C, 2 * KV_WIDTH)
    kc_ref[...] = c2[:, :KV_WIDTH]
    vc_ref[...] = c2[:, KV_WIDTH:]
    gate_logits = proj(_C_GATE, GATE_WIDTH)

    vext_ref[CONV_HALO:CONV_HALO + tm, :] = v
    y = (cw_ref[2:3, :] * v
         + cw_ref[1:2, :] * vext_ref[pl.ds(CONV_HALO - 1, tm), :]
         + cw_ref[0:1, :] * vext_ref[pl.ds(CONV_HALO - 2, tm), :])
    ob_ref[...] = (gate_b * y).astype(BF16)
    vext_ref[0:CONV_HALO, :] = vext_ref[tm:tm + CONV_HALO, :]
    gates_ref[...] = 1.0 / (1.0 + jnp.exp(-gate_logits))


def _inproj(x, pos_col, g, w, qnw, knw, cw, pq, pk, freq, *, seq, tm=512):
    n, d = x.shape
    bsz, tpb = n // seq, seq // tm
    row = lambda i: (i, 0)
    fixed = lambda i: (0, 0)
    widths = [(KV_WIDTH, F32), (KV_WIDTH, F32), (GATE_WIDTH, F32), (A_WIDTH, BF16)]
    out_shape = [jax.ShapeDtypeStruct((bsz, A_WIDTH, seq), BF16),
                 jax.ShapeDtypeStruct((bsz, N_GROUPS, seq, 2 * LANES), BF16),
                 jax.ShapeDtypeStruct((bsz, N_GROUPS, seq, LANES), BF16),
                 jax.ShapeDtypeStruct((bsz, 2, N_GROUPS, V_ROWS, seq), BF16)]
    out_specs = [pl.BlockSpec((None, A_WIDTH, tm), lambda i: (i // tpb, 0, i % tpb)),
                 pl.BlockSpec((None, N_GROUPS, tm, 2 * LANES), lambda i: (i // tpb, 0, i % tpb, 0)),
                 pl.BlockSpec((None, N_GROUPS, tm, LANES), lambda i: (i // tpb, 0, i % tpb, 0)),
                 pl.BlockSpec((None, 2, N_GROUPS, V_ROWS, tm), lambda i: (i // tpb, 0, 0, 0, i % tpb))]
    return pl.pallas_call(
        functools.partial(_inproj_kernel, tm=tm, tiles_per_batch=tpb),
        out_shape=out_shape + [jax.ShapeDtypeStruct((n, wd), dt) for wd, dt in widths],
        grid=(n // tm,),
        in_specs=[
            pl.BlockSpec((tm, d), row),
            pl.BlockSpec((tm, 1), row),
            pl.BlockSpec((1, d), fixed),
            pl.BlockSpec(w.shape, fixed),
            pl.BlockSpec(qnw.shape, fixed),
            pl.BlockSpec(knw.shape, fixed),
            pl.BlockSpec(cw.shape, fixed),
            pl.BlockSpec(pq.shape, fixed),
            pl.BlockSpec(pk.shape, fixed),
            pl.BlockSpec(freq.shape, fixed),
        ],
        out_specs=out_specs + [pl.BlockSpec((tm, wd), row) for wd, _ in widths],
        scratch_shapes=[pltpu.VMEM((tm + CONV_HALO, A_WIDTH), F32)],
        compiler_params=_cparams(("arbitrary",)),
        name="inproj",
    )(x, pos_col, g.reshape(1, d), w, qnw, knw, cw, pq, pk, freq)


def _gelu_tanh(x):
    return 0.5 * x * (1.0 + jnp.tanh(math.sqrt(2.0 / math.pi) * (x + 0.044715 * (x * x * x))))


def _compress_kernel(xk_ref, xv_ref, pos_ref, pe_ref, wk_ref, wv_ref, w2_ref, w2vt_ref, knw_ref, pk_ref, freq_ref,
                     kc_ref, vct_ref, *, ncp):
    def hidden(x_ref, pe_row, w_ref):
        x = jnp.concatenate([x_ref[pl.ds(j, ncp, stride=CMP_STRIDE), :] for j in range(CMP_STRIDE)], axis=1)
        a = _dot((x + pe_ref[pe_row:pe_row + 1, :]).astype(BF16), w_ref[0])
        b = _dot((x + pe_ref[pe_row + 1:pe_row + 2, :]).astype(BF16), w_ref[1])
        hid = a + pltpu.roll(b, ncp - 1, axis=0)
        return _gelu_tanh(hid)

    low = lax.broadcasted_iota(jnp.int32, (1, LANES), 1) < HEAD_DIM
    kc = _dot(hidden(xk_ref, 0, wk_ref).astype(BF16), w2_ref[...])
    kc = _group_rms(kc, knw_ref[0:1, :], pk_ref[...])
    cos_t, sin_t = _rope_tables(pos_ref[...].astype(F32), freq_ref[...])
    kc = _rope(kc, cos_t, sin_t)
    kc_ref[0] = jnp.where(low, kc, 0.0).astype(BF16)
    kc_ref[1] = jnp.where(low, pltpu.roll(kc, HEAD_DIM, axis=1), 0.0).astype(BF16)
    hv = hidden(xv_ref, 2, wv_ref)
    hv_t = jnp.concatenate(
        [jnp.concatenate([hv[r * LANES:(r + 1) * LANES, c * LANES:(c + 1) * LANES].T for r in range(ncp // LANES)],
                         axis=1) for c in range(hv.shape[1] // LANES)], axis=0)
    vct = _dot(w2vt_ref[...], hv_t.astype(BF16))
    for g in range(N_GROUPS):
        vct_ref[g] = vct[g * HEAD_DIM:(g + 1) * HEAD_DIM, :].astype(BF16)


def _compress(xk, xv, pos_c, pe, wk, wv, w2k, w2vt, knw, pk, freq, *, seq):
    bsz, ncp = pos_c.shape[:2]
    bat = lambda b: (b, 0, 0)
    fix2 = lambda b: (0, 0)
    fix3 = lambda b: (0, 0, 0)
    return pl.pallas_call(
        functools.partial(_compress_kernel, ncp=ncp),
        out_shape=[jax.ShapeDtypeStruct((bsz, N_GROUPS, ncp, LANES), BF16),
                   jax.ShapeDtypeStruct((bsz, N_GROUPS, HEAD_DIM, ncp), BF16)],
        grid=(bsz,),
        in_specs=[
            pl.BlockSpec((seq, KV_WIDTH), lambda b: (b, 0)),
            pl.BlockSpec((seq, KV_WIDTH), lambda b: (b, 0)),
            pl.BlockSpec((None, ncp, 1), bat),
            pl.BlockSpec(pe.shape, fix2),
            pl.BlockSpec(wk.shape, fix3),
            pl.BlockSpec(wv.shape, fix3),
            pl.BlockSpec(w2k.shape, fix2),
            pl.BlockSpec(w2vt.shape, fix2),
            pl.BlockSpec(knw.shape, fix2),
            pl.BlockSpec(pk.shape, fix2),
            pl.BlockSpec(freq.shape, fix2),
        ],
        out_specs=[pl.BlockSpec((None, N_GROUPS, ncp, LANES), lambda b: (b, 0, 0, 0)),
                   pl.BlockSpec((None, N_GROUPS, HEAD_DIM, ncp), lambda b: (b, 0, 0, 0))],
        compiler_params=_cparams(("parallel",)),
        name="compress",
    )(xk, xv, pos_c, pe, wk, wv, w2k, w2vt, knw, pk, freq)


def _head_columns(qt_ref):
    cols = []
    for pair in range(qt_ref.shape[0] // LANES):
        both = qt_ref[pair * LANES:(pair + 1) * LANES, :]
        cols += [both, jnp.concatenate([both[HEAD_DIM:, :], both[:HEAD_DIM, :]], axis=0)]
    return cols


def _software_pipeline(n, scores, probs, finish):
    s, p = {}, {}
    for step in range(n + 2):
        if step >= 2:
            finish(step - 2, p.pop(step - 2))
        if 1 <= step <= n:
            p[step - 1] = probs(step - 1, s.pop(step - 1))
        if step < n:
            s[step] = scores(step)


def _gate_row(gt_ref, branch, hd, group=0):
    r = group * GATE_ROWS + branch * HEADS_PER_GROUP + hd
    return gt_ref[r:r + 1, :]


def _cmp_kernel(qt_ref, kc_ref, vct_ref, ovt_ref, gt_ref, ot_ref, sbt_ref, *, tq, ncp):
    i = pl.program_id(1)
    t = i * tq + lax.broadcasted_iota(jnp.int32, (1, tq), 1)
    any_valid = (t >= CMP_BLOCK - 1).astype(F32)
    heads = _head_columns(qt_ref)
    cmp_per_sel = SEL_BLOCK // CMP_STRIDE
    groups = range(N_GROUPS)

    def run(nrow):
        nblk = nrow // cmp_per_sel
        blk_end = lax.broadcasted_iota(jnp.int32, (nrow, 1), 0) * CMP_STRIDE + (CMP_BLOCK - 1)
        valid = blk_end <= t
        kc = [kc_ref[g, 0:nrow, :] for g in groups]
        vct = [vct_ref[g, :, 0:nrow] for g in groups]
        psum_parts = [[] for _ in groups]

        def scores(u):
            return jnp.where(valid, _dot(kc[u // HEADS_PER_GROUP], heads[u]), NEG)

        def probs(u, s):
            e = jnp.exp(s - jnp.max(s, axis=0, keepdims=True))
            return e * (any_valid / jnp.sum(e, axis=0, keepdims=True))

        def finish(u, p):
            g, hd = divmod(u, HEADS_PER_GROUP)
            o_t = _dot(vct[g], p.astype(BF16))
            ot_ref[u * HEAD_DIM:(u + 1) * HEAD_DIM, :] = o_t * _gate_row(gt_ref, _BR_CMP, hd, g)
            psum_parts[g].append(p)

        _software_pipeline(N_HEADS, scores, probs, finish)

        ovt = ovt_ref[0:nblk, 0:nrow]
        blk = lax.broadcasted_iota(jnp.int32, (nblk, 1), 0)
        sel_start = blk * SEL_BLOCK
        cur = (t >> int(math.log2(SEL_BLOCK))) << int(math.log2(SEL_BLOCK))
        imp = []
        for parts in psum_parts:
            psum = (parts[0] + parts[1]) + (parts[2] + parts[3])
            hi = psum.astype(BF16)
            lo = (psum - hi.astype(F32)).astype(BF16)
            x = jnp.where(sel_start <= t, _dot(ovt, hi) + _dot(ovt, lo), -1.0)
            imp.append(jnp.where(sel_start == cur, 1e4, jnp.where(sel_start == 0, 1e4, x)))

        blk_f = blk.astype(F32)
        bias = [jnp.full((nblk, tq), NEG, F32) for _ in groups]
        for _ in range(SEL_TOPK):
            for g in groups:
                mx = jnp.max(imp[g], axis=0, keepdims=True)
                first = jnp.min(jnp.where(imp[g] == mx, blk_f, float(LANES)), axis=0, keepdims=True)
                pick = blk_f == first
                bias[g] = jnp.where(pick, 0.0, bias[g])
                imp[g] = jnp.where(pick, -3e38, imp[g])
        for g in groups:
            sbt_ref[g, 0:nblk, :] = bias[g].astype(BF16)
            if nblk < LANES:
                sbt_ref[g, nblk:, :] = jnp.full((LANES - nblk, tq), NEG, BF16)

    sizes = [LANES * (v + 1) for v in range(ncp // LANES)]
    assert sizes[0] // cmp_per_sel >= SEL_TOPK
    tokens_per_variant = LANES * CMP_STRIDE
    lax.switch((i * tq + tq - 1) // tokens_per_variant, [functools.partial(run, nrow) for nrow in sizes])


def _cmp_select(qt, kc, vct, ovt, gates_t, *, bsz, seq, tq=256):
    ncp = kc.shape[2]
    nq = seq // tq
    whole = lambda b, i: (b, 0, 0, 0)
    return pl.pallas_call(
        functools.partial(_cmp_kernel, tq=tq, ncp=ncp),
        out_shape=[jax.ShapeDtypeStruct((bsz, A_WIDTH, seq), F32),
                   jax.ShapeDtypeStruct((bsz, N_GROUPS, LANES, seq), BF16)],
        grid=(bsz, nq),
        in_specs=[
            pl.BlockSpec((None, A_WIDTH, tq), lambda b, i: (b, 0, i)),
            pl.BlockSpec((None, N_GROUPS, ncp, LANES), whole),
            pl.BlockSpec((None, N_GROUPS, HEAD_DIM, ncp), whole),
            pl.BlockSpec(ovt.shape, lambda b, i: (0, 0)),
            pl.BlockSpec((N_GROUPS * GATE_ROWS, tq), lambda b, i: (0, b * nq + i)),
        ],
        out_specs=[pl.BlockSpec((None, A_WIDTH, tq), lambda b, i: (b, 0, i)),
                   pl.BlockSpec((None, N_GROUPS, LANES, tq), lambda b, i: (b, 0, 0, i))],
        compiler_params=_cparams(("parallel", "parallel")),
        name="cmp_select",
    )(qt, kc, vct, ovt, gates_t)


_SLC_UNROLL = 2


def _slc_kernel(qt_ref, sbt_ref, k_ref, vt_ref, gt_ref, prev_ref, o_ref,
                qa_ref, s_ref, p_ref, acc_ref, *, tq):
    i = pl.program_id(2)
    rows = HEADS_PER_GROUP * tq
    qa_ref[...] = jnp.concatenate([jnp.concatenate([sbt_ref[...]] * HEADS_PER_GROUP, axis=1),
                                   jnp.concatenate(_head_columns(qt_ref), axis=1)], axis=0)
    p_ref[...] = jnp.zeros(p_ref.shape, BF16)
    acc_ref[...] = jnp.zeros(acc_ref.shape, F32)

    def accumulate(unit, alpha):
        j = jnp.where(unit <= 0, i, unit - 1)
        pv = _dot(vt_ref[:, pl.ds(pl.multiple_of(j * tq, tq), tq)], p_ref[...])
        acc_ref[...] = alpha * acc_ref[...] + pv

    def probs(m):
        s = s_ref[...]
        m_new = jnp.maximum(m, jnp.max(s, axis=0, keepdims=True))
        p_ref[...] = jnp.exp(s - m_new).astype(BF16)
        return m_new, jnp.exp(m - m_new)

    def scores(j, masked):
        s = _dot(k_ref[pl.ds(pl.multiple_of(j * tq, tq), tq), :], qa_ref[...])
        if masked:
            causal = (lax.broadcasted_iota(jnp.int32, (tq, 1), 0)
                      <= (lax.broadcasted_iota(jnp.int32, (1, rows), 1) & (tq - 1)))
            s = jnp.where(causal, s, NEG)
        s_ref[...] = s

    def body(k, carry):
        m, alpha = carry
        accumulate(k - 1, alpha)
        m, alpha = probs(m)
        scores(k, False)
        return m, alpha

    def steps(first, count, carry):
        for k in range(count):
            carry = body(first + k, carry)
        return carry

    scores(i, True)
    carry = (jnp.full((1, rows), -3e38, F32), jnp.ones((1, rows), F32))
    trips = i >> int(math.log2(_SLC_UNROLL))
    carry = lax.fori_loop(0, trips, lambda k, c: steps(k * _SLC_UNROLL, _SLC_UNROLL, c), carry)
    done = trips * _SLC_UNROLL
    size = _SLC_UNROLL // 2
    while size >= 1:
        carry = lax.cond((i & size) != 0, functools.partial(steps, done, size), lambda c: c, carry)
        done = done + (i & size)
        size //= 2
    m, alpha = carry
    accumulate(i - 1, alpha)
    m, alpha = probs(m)
    accumulate(i, alpha)

    gate = jnp.concatenate([_gate_row(gt_ref, _BR_SLC, hd) for hd in range(HEADS_PER_GROUP)], axis=1)
    out_t = acc_ref[0:HEAD_DIM, :] * (gate / acc_ref[HEAD_DIM:HEAD_DIM + 1, :])
    for pair in range(HEADS_PER_GROUP // 2):
        for c in range(tq // LANES):
            col = 2 * pair * tq + c * LANES
            both = jnp.concatenate([out_t[:, col:col + LANES], out_t[:, col + tq:col + tq + LANES]], axis=0)
            both = both + prev_ref[pair * LANES:(pair + 1) * LANES, c * LANES:(c + 1) * LANES]
            o_ref[c * LANES:(c + 1) * LANES, pair * LANES:(pair + 1) * LANES] = both.T.astype(BF16)


def _slc(qt, sbt, k_aug, vt_aug, gates_t, prev, *, bsz, seq, tq=512):
    nq = seq // tq
    gw = HEADS_PER_GROUP * HEAD_DIM
    rows = HEADS_PER_GROUP * tq
    tmap = lambda b, g, i: (b, g, i)
    return pl.pallas_call(
        functools.partial(_slc_kernel, tq=tq),
        out_shape=jax.ShapeDtypeStruct((bsz * seq, A_WIDTH), BF16),
        grid=(bsz, N_GROUPS, nq),
        in_specs=[
            pl.BlockSpec((None, gw, tq), tmap),
            pl.BlockSpec((None, None, LANES, tq), lambda b, g, i: (b, g, 0, i)),
            pl.BlockSpec((None, None, seq, 2 * LANES), lambda b, g, i: (b, g, 0, 0)),
            pl.BlockSpec((None, None, None, V_ROWS, seq), lambda b, g, i: (b, 0, g, 0, 0)),
            pl.BlockSpec((GATE_ROWS, tq), lambda b, g, i: (g, b * nq + i)),
            pl.BlockSpec((None, gw, tq), tmap),
        ],
        out_specs=pl.BlockSpec((tq, gw), lambda b, g, i: (b * nq + i, g)),
        scratch_shapes=[pltpu.VMEM((2 * LANES, rows), BF16), pltpu.VMEM((tq, rows), F32),
                        pltpu.VMEM((tq, rows), BF16), pltpu.VMEM((V_ROWS, rows), F32)],
        compiler_params=_cparams(("parallel", "parallel", "parallel")),
        name="slc_attn",
    )(qt, sbt, k_aug, vt_aug, gates_t, prev)


def _win_kernel(qt_ref, k_ref, vt_ref, band_ref, gt_ref, prev_ref, ot_ref, *, tq):
    i = pl.program_id(1)
    span = WINDOW + tq
    start = pl.multiple_of(jnp.maximum(i - WINDOW // tq, 0) * tq, tq)
    k = [k_ref[g, pl.ds(start, span), :] for g in range(N_GROUPS)]
    vt = [vt_ref[g, :, pl.ds(start, span)] for g in range(N_GROUPS)]
    heads = _head_columns(qt_ref)

    def scores(u):
        return _dot(k[u // HEADS_PER_GROUP], heads[u]) + band_ref[...]

    def probs(u, s):
        return jnp.exp(s - jnp.max(s, axis=0, keepdims=True)).astype(BF16)

    def finish(u, p):
        g, hd = divmod(u, HEADS_PER_GROUP)
        acc = _dot(vt[g], p)
        rows = slice(u * HEAD_DIM, (u + 1) * HEAD_DIM)
        scale = _gate_row(gt_ref, _BR_WIN, hd, g) / acc[HEAD_DIM:HEAD_DIM + 1, :]
        ot_ref[rows, :] = prev_ref[rows, :] + acc[:HEAD_DIM, :] * scale

    _software_pipeline(N_HEADS, scores, probs, finish)


def _window_band(tq):
    edge = WINDOW // tq
    v = jnp.arange(edge + 1)[:, None, None]
    kpos = jnp.maximum(v - edge, 0) * tq + jnp.arange(WINDOW + tq)[None, :, None]
    t = v * tq + jnp.arange(tq)[None, None, :]
    ok = (t - kpos >= 0) & (t - kpos < WINDOW)
    return jnp.where(ok, 0.0, NEG).astype(F32)


def _win(qt, k, vt, gates_t, prev, *, bsz, seq, tq=256):
    nq = seq // tq
    band = _window_band(tq)
    last = band.shape[0] - 1
    tile = lambda b, i: (b, 0, i)
    return pl.pallas_call(
        functools.partial(_win_kernel, tq=tq),
        out_shape=jax.ShapeDtypeStruct((bsz, A_WIDTH, seq), F32),
        grid=(bsz, nq),
        in_specs=[
            pl.BlockSpec((None, A_WIDTH, tq), tile),
            pl.BlockSpec((None, N_GROUPS, seq, LANES), lambda b, i: (b, 0, 0, 0)),
            pl.BlockSpec((None, None, N_GROUPS, V_ROWS, seq), lambda b, i: (b, 1, 0, 0, 0)),
            pl.BlockSpec((None,) + band.shape[1:], lambda b, i: (jnp.minimum(i, last), 0, 0)),
            pl.BlockSpec((N_GROUPS * GATE_ROWS, tq), lambda b, i: (0, b * nq + i)),
            pl.BlockSpec((None, A_WIDTH, tq), tile),
        ],
        out_specs=pl.BlockSpec((None, A_WIDTH, tq), tile),
        compiler_params=_cparams(("parallel", "parallel")),
        name="win_attn",
    )(qt, k, vt, band, gates_t, prev)


def _outproj_kernel(x_ref, oa_ref, ob_ref, w_ref, o_ref):
    y = _dot(oa_ref[...], w_ref[0:A_WIDTH, :]) + _dot(ob_ref[...], w_ref[A_WIDTH:, :])
    o_ref[...] = x_ref[...] + y


def _outproj(x, oa, ob, w, *, tm=1024):
    n, d = x.shape
    row = lambda i: (i, 0)
    return pl.pallas_call(
        _outproj_kernel,
        out_shape=jax.ShapeDtypeStruct((n, d), F32),
        grid=(n // tm,),
        in_specs=[
            pl.BlockSpec((tm, d), row),
            pl.BlockSpec((tm, A_WIDTH), row),
            pl.BlockSpec((tm, A_WIDTH), row),
            pl.BlockSpec(w.shape, lambda i: (0, 0)),
        ],
        out_specs=pl.BlockSpec((tm, d), row),
        compiler_params=_cparams(("parallel",)),
        name="outproj",
    )(x, oa, ob, w)


def _pool_kernel(x_ref, halo_ref, g_ref, w_ref, sc_ref, o_ref, ext_ref, tmp_ref, *, tm, tiles_per_batch):
    i = pl.program_id(0)
    first = i % tiles_per_batch == 0
    x = x_ref[...]
    h = _rms(x, g_ref[...])
    hh = _rms(halo_ref[...], g_ref[...])
    pad, body = _POOL_PAD, _POOL_PAD + POOL_HALO
    ext_ref[0:pad, :] = jnp.zeros((pad, h.shape[1]), F32)
    ext_ref[pad:body, :] = jnp.where(first, 0.0, hh)
    ext_ref[body:body + tm, :] = h
    gw = h.shape[1] // len(POOL_WINDOWS)
    for slot in range(2):
        tmp_ref[slot, 0:pad, :] = jnp.zeros((pad, gw), F32)
    t = (i % tiles_per_batch) * tm + lax.broadcasted_iota(jnp.int32, (tm, 1), 0)
    span = POOL_HALO + tm
    for gi, win in enumerate(POOL_WINDOWS):
        c0 = gi * gw
        hg = h[:, c0:c0 + gw]
        read = lambda start, c0=c0: ext_ref[pl.ds(start, span), c0:c0 + gw]
        shift, slot = 1, 0
        while shift < win:
            tmp_ref[slot, pad:pad + span, :] = read(pad) + read(pad - shift)
            read = lambda start, slot=slot: tmp_ref[slot, pl.ds(start, span), :]
            shift, slot = 2 * shift, 1 - slot
        tot = read(pad)[POOL_HALO:, :]
        cnt = jnp.minimum(t + 1, win).astype(F32)
        y = _dot((tot / cnt - hg).astype(BF16), w_ref[gi])
        o_ref[:, c0:c0 + gw] = x[:, c0:c0 + gw] + y * sc_ref[:, c0:c0 + gw]


def _pool(x, g, w, scale, *, seq, tm=512):
    n, d = x.shape
    hb = tm // POOL_HALO
    return pl.pallas_call(
        functools.partial(_pool_kernel, tm=tm, tiles_per_batch=seq // tm),
        out_shape=jax.ShapeDtypeStruct((n, d), F32),
        grid=(n // tm,),
        in_specs=[
            pl.BlockSpec((tm, d), lambda i: (i, 0)),
            pl.BlockSpec((POOL_HALO, d), lambda i: (jnp.maximum(i * hb - 1, 0), 0)),
            pl.BlockSpec((1, d), lambda i: (0, 0)),
            pl.BlockSpec(w.shape, lambda i: (0, 0, 0)),
            pl.BlockSpec((1, d), lambda i: (0, 0)),
        ],
        out_specs=pl.BlockSpec((tm, d), lambda i: (i, 0)),
        scratch_shapes=[pltpu.VMEM((_POOL_PAD + POOL_HALO + tm, d), F32),
                        pltpu.VMEM((2, _POOL_PAD + POOL_HALO + tm, d // len(POOL_WINDOWS)), F32)],
        compiler_params=_cparams(("parallel",)),
        name="pool_mixer",
    )(x, x, g.reshape(1, d), w, scale.reshape(1, d))


def _pack_w_in(w_in):
    o = 0
    q = w_in[:, o:o + A_WIDTH]; o += A_WIDTH
    kc, vc, ks, vs, kw, vw = [w_in[:, o + k * KV_WIDTH:o + (k + 1) * KV_WIDTH] for k in range(6)]
    o += 6 * KV_WIDTH
    gates = w_in[:, o:o + N_BRANCH * N_HEADS]; o += N_BRANCH * N_HEADS
    u, gb, gc = [w_in[:, o + k * A_WIDTH:o + (k + 1) * A_WIDTH] for k in range(3)]
    gates = gates.reshape(-1, N_GROUPS, HEADS_PER_GROUP, N_BRANCH).transpose(0, 1, 3, 2)
    gates = gates.reshape(-1, N_GROUPS, N_BRANCH * HEADS_PER_GROUP)
    gates = jnp.pad(gates, ((0, 0), (0, 0), (0, LANES - N_BRANCH * HEADS_PER_GROUP))).reshape(-1, GATE_WIDTH)
    return jnp.concatenate([q, ks, kw, vs, vw, kc, vc, gates, u, gb, gc], axis=1).astype(BF16)


def _group_mean_matrix(width):
    r = jnp.arange(width) // HEAD_DIM
    return jnp.where(r[:, None] == r[None, :], 1.0 / HEAD_DIM, 0.0).astype(BF16)


def _rope_freq_row():
    lane = jnp.arange(LANES) % HEAD_DIM
    freqs = ROPE_THETA ** (-jnp.arange(0, ROPE_DIM, 2, dtype=F32) / ROPE_DIM)
    return jnp.where(lane < ROPE_DIM, freqs[lane % ROPE_HALF], 0.0).astype(F32).reshape(1, LANES)


def _hybrid_mixer(x, positions, g, w_in, q_norm, k_norm, cmp_pos, cmp_w1, cmp_w2, conv_w, w_out, *, bsz, seq):
    n = bsz * seq
    ncp = seq // CMP_STRIDE
    freq = _rope_freq_row()
    pq = _group_mean_matrix(A_WIDTH)
    pk = _group_mean_matrix(KV_WIDTH)
    qnw = jnp.tile(q_norm, N_HEADS).reshape(1, A_WIDTH)
    knw = jnp.tile(k_norm, (1, N_GROUPS))

    q_t, k_slc, k_win, v_t, kc_raw, vc_raw, gates, ob = _inproj(
        x, positions.reshape(n, 1), g, _pack_w_in(w_in), qnw, knw, conv_w, pq, pk, freq, seq=seq)

    eye_g = jnp.eye(N_GROUPS, dtype=F32)
    seg = CMP_STRIDE * KV_WIDTH

    def seg_weights(w1):
        w1r = w1.reshape(CMP_BLOCK, HEAD_DIM, CMP_HIDDEN)
        halves = [jnp.einsum('jdh,gk->jgdkh', w1r[a:a + CMP_STRIDE], eye_g).reshape(seg, N_GROUPS * CMP_HIDDEN)
                  for a in (0, CMP_STRIDE)]
        return jnp.stack(halves).astype(BF16)

    def seg_pos(pe):
        return [jnp.broadcast_to(pe[a:a + CMP_STRIDE, None, :], (CMP_STRIDE, N_GROUPS, HEAD_DIM)).reshape(seg)
                for a in (0, CMP_STRIDE)]

    pe = jnp.stack(seg_pos(cmp_pos[0]) + seg_pos(cmp_pos[1]))
    w2 = jnp.stack([jnp.einsum('hd,gk->ghkd', cmp_w2[a], eye_g).reshape(N_GROUPS * CMP_HIDDEN, KV_WIDTH)
                    for a in range(2)]).astype(BF16)
    blk_end = jnp.minimum(jnp.arange(ncp) * CMP_STRIDE + CMP_BLOCK - 1, seq - 1)
    pos_c = positions[:, blk_end].reshape(bsz, ncp, 1)
    kc_g, vc_t = _compress(kc_raw, vc_raw, pos_c, pe, seg_weights(cmp_w1[0]), seg_weights(cmp_w1[1]), w2[0], w2[1].T,
                           knw, pk, freq, seq=seq)

    gates_t = gates.reshape(n, N_GROUPS, LANES)[:, :, :GATE_ROWS].transpose(1, 2, 0).reshape(N_GROUPS * GATE_ROWS, n)
    cstart = jnp.arange(ncp) * CMP_STRIDE
    sstart = jnp.arange(LANES) * SEL_BLOCK
    ovt = ((cstart[None, :] < sstart[:, None] + SEL_BLOCK)
           & (cstart[None, :] + CMP_BLOCK > sstart[:, None])).astype(BF16)
    o_t, sel_bias = _cmp_select(q_t, kc_g, vc_t, ovt, gates_t, bsz=bsz, seq=seq)

    o_t = _win(q_t, k_win, v_t, gates_t, o_t, bsz=bsz, seq=seq)
    o_a = _slc(q_t, sel_bias, k_slc, v_t, gates_t, o_t, bsz=bsz, seq=seq)

    return _outproj(x, o_a, ob, w_out.astype(BF16))


def kernel(x, positions, ffn_norm, ffn_w_gate, ffn_w_up, ffn_w_down, mix_norm, hyb_w_in, hyb_q_norm, hyb_k_norm,
           hyb_cmp_pos, hyb_cmp_w1, hyb_cmp_w2, hyb_conv_w, hyb_w_out, pool_w, pool_scale):
    bsz, seq, d = x.shape
    depth = ffn_norm.shape[0]
    assert seq % 2048 == 0 and seq // SEL_BLOCK <= LANES and seq // SEL_BLOCK >= SEL_TOPK
    wg, wu, wd = (w.astype(BF16) for w in (ffn_w_gate, ffn_w_up, ffn_w_down))
    xf = x.reshape(bsz * seq, d)
    for layer in range(depth):
        xf = _ffn(xf, ffn_norm[layer, 0], wg, wu, wd, layer, 0)
        i = layer // 2
        if layer % 2 == 0:
            xf = _hybrid_mixer(xf, positions, mix_norm[layer], hyb_w_in[i], hyb_q_norm[i], hyb_k_norm[i],
                               hyb_cmp_pos[i], hyb_cmp_w1[i], hyb_cmp_w2[i], hyb_conv_w[i], hyb_w_out[i],
                               bsz=bsz, seq=seq)
        else:
            xf = _pool(xf, mix_norm[layer], pool_w[i].astype(BF16), pool_scale[i], seq=seq)
        xf = _ffn(xf, ffn_norm[layer, 1], wg, wu, wd, layer, 1)
    return xf.reshape(bsz, seq, d)
```

```python
import functools
import math

import jax
import jax.numpy as jnp
from jax import lax
from jax.experimental import pallas as pl
from jax.experimental.pallas import tpu as pltpu

F32 = jnp.float32
BF16 = jnp.bfloat16

HEAD_DIM = 64
N_GROUPS = 2
HEADS_PER_GROUP = 4
N_HEADS = N_GROUPS * HEADS_PER_GROUP
A_WIDTH = N_HEADS * HEAD_DIM
KV_WIDTH = N_GROUPS * HEAD_DIM
ROPE_DIM = HEAD_DIM // 4
ROPE_HALF = ROPE_DIM // 2
ROPE_THETA = 500000.0
CMP_BLOCK = 32
CMP_STRIDE = 16
CMP_HIDDEN = 2 * HEAD_DIM
SEL_BLOCK = 64
SEL_TOPK = 16
WINDOW = 512
N_BRANCH = 3
_BR_CMP, _BR_SLC, _BR_WIN = 0, 1, 2
CONV_WIDTH = 3
POOL_WINDOWS = (2, 4, 8, 16)
POOL_HALO = 16
_POOL_PAD = 8
EPS = 1e-6
NEG = -1e30

LANES = 128
VMEM_LIMIT_BYTES = 56 * 1024 * 1024


def _cparams(sem):
    return pltpu.CompilerParams(dimension_semantics=sem, vmem_limit_bytes=VMEM_LIMIT_BYTES)


def _dot(a, b):
    return jnp.dot(a, b, preferred_element_type=F32)


def _dot_split(a, b):
    hi = a.astype(BF16)
    lo = (a - hi.astype(F32)).astype(BF16)
    return _dot(hi, b) + _dot(lo, b)


def _rms(x, g):
    ms = jnp.mean(x * x, axis=-1, keepdims=True)
    return x * lax.rsqrt(ms + EPS) * g


def _group_rms(x, g, pmat):
    ms = _dot_split(x * x, pmat)
    return x * lax.rsqrt(ms + EPS) * g


def _rope_tables(pos_col, freq_row):
    ang = pos_col * freq_row
    lane = lax.broadcasted_iota(jnp.int32, (1, LANES), 1) & (HEAD_DIM - 1)
    sign = jnp.where(lane < ROPE_HALF, -1.0, 1.0).astype(F32)
    return jnp.cos(ang), jnp.sin(ang) * sign


def _rope(x, cos_t, sin_t):
    w = x.shape[1]
    reps = w // LANES
    if reps > 1:
        cos_t = jnp.concatenate([cos_t] * reps, axis=1)
        sin_t = jnp.concatenate([sin_t] * reps, axis=1)
    lane = lax.broadcasted_iota(jnp.int32, (1, w), 1) & (HEAD_DIM - 1)
    partner = jnp.where(lane < ROPE_HALF,
                        pltpu.roll(x, w - ROPE_HALF, axis=1),
                        pltpu.roll(x, ROPE_HALF, axis=1))
    return x * cos_t + partner * sin_t


def _ffn_kernel(xprev_ref, xnext_ref, g_ref, wg_ref, wu_ref, wd_ref, o_ref, h_ref, act_ref, *, ni):
    i = pl.program_id(0)
    slot = i % 2

    def front():
        h = h_ref[slot]
        a = _dot(h, wg_ref[...])
        b = _dot(h, wu_ref[...])
        act_ref[slot] = (a * (1.0 / (1.0 + jnp.exp(-a))) * b).astype(BF16)
        h_ref[1 - slot] = _rms(xnext_ref[...], g_ref[...]).astype(BF16)

    def back():
        o_ref[...] = xprev_ref[...] + 0.5 * _dot(act_ref[1 - slot], wd_ref[...])

    @pl.when(i == 0)
    def _():
        h_ref[0] = _rms(xprev_ref[...], g_ref[...]).astype(BF16)
        front()

    @pl.when((i > 0) & (i < ni))
    def _():
        back()
        front()

    pl.when(i == ni)(back)


def _ffn(x, g, wg, wu, wd, layer, half, *, tm=512):
    n, d = x.shape
    dff = wg.shape[-1]
    ni = n // tm
    once = dict(pipeline_mode=pl.Buffered(1))
    prev = lambda i: (jnp.maximum(i - 1, 0), 0)
    return pl.pallas_call(
        functools.partial(_ffn_kernel, ni=ni),
        out_shape=jax.ShapeDtypeStruct((n, d), F32),
        grid=(ni + 1,),
        in_specs=[
            pl.BlockSpec((tm, d), prev),
            pl.BlockSpec((tm, d), lambda i: (jnp.minimum(i + 1, ni - 1), 0)),
            pl.BlockSpec((1, d), lambda i: (0, 0)),
            pl.BlockSpec((None, None, d, dff), lambda i: (layer, half, 0, 0), **once),
            pl.BlockSpec((None, None, d, dff), lambda i: (layer, half, 0, 0), **once),
            pl.BlockSpec((None, None, dff, d), lambda i: (layer, half, 0, 0), **once),
        ],
        out_specs=pl.BlockSpec((tm, d), prev),
        scratch_shapes=[pltpu.VMEM((2, tm, d), BF16), pltpu.VMEM((2, tm, dff), BF16)],
        compiler_params=_cparams(("arbitrary",)),
        name="ffn",
    )(x, x, g.reshape(1, d), wg, wu, wd)


GATE_WIDTH = N_GROUPS * LANES
GATE_ROWS = 16
_C_Q = 0
_C_K = _C_Q + A_WIDTH
_C_V = _C_K + 2 * KV_WIDTH
_C_C = _C_V + 2 * KV_WIDTH
_C_GATE = _C_C + 2 * KV_WIDTH
_C_U = _C_GATE + GATE_WIDTH
_C_GB = _C_U + A_WIDTH
_C_GC = _C_GB + A_WIDTH
CONV_HALO = 8
V_ROWS = HEAD_DIM + 16


def _inproj_kernel(x_ref, pos_ref, g_ref, w_ref, qnw_ref, knw_ref, cw_ref, pq_ref, pk_ref, freq_ref,
                   qt_ref, kslc_ref, kwin_ref, vt_ref, kc_ref, vc_ref, gates_ref, ob_ref,
                   vext_ref, *, tm, tiles_per_batch):
    i = pl.program_id(0)
    chunks = range(tm // LANES)

    def transposed(a, c):
        return jnp.concatenate([a[r * LANES:(r + 1) * LANES, c * LANES:(c + 1) * LANES].T for r in chunks], axis=1)

    def per_group(k):
        low = lax.broadcasted_iota(jnp.int32, (1, LANES), 1) < HEAD_DIM
        return [jnp.where(low, k, 0.0), jnp.where(low, pltpu.roll(k, HEAD_DIM, axis=1), 0.0)]

    @pl.when(i % tiles_per_batch == 0)
    def _():
        vext_ref[0:CONV_HALO, :] = jnp.zeros((CONV_HALO, A_WIDTH), F32)

    h = _rms(x_ref[...], g_ref[...]).astype(BF16)

    def proj(c0, width):
        return _dot(h, w_ref[:, c0:c0 + width])

    q_raw = proj(_C_Q, A_WIDTH)
    k2 = proj(_C_K, 2 * KV_WIDTH)
    cos_t, sin_t = _rope_tables(pos_ref[...].astype(F32), freq_ref[...])
    v = proj(_C_GC, A_WIDTH) * proj(_C_U, A_WIDTH)
    gate_b = proj(_C_GB, A_WIDTH)

    q = _group_rms(q_raw, qnw_ref[...], pq_ref[...])
    q = _rope(q, cos_t, sin_t) * (HEAD_DIM ** -0.5)
    for c in range(A_WIDTH // LANES):
        qt_ref[c * LANES:(c + 1) * LANES, :] = transposed(q, c).astype(BF16)

    tok = (i % tiles_per_batch) * tm + lax.broadcasted_iota(jnp.int32, (tm, 1), 0)
    blk_onehot = (lax.broadcasted_iota(jnp.int32, (1, LANES), 1) == (tok >> int(math.log2(SEL_BLOCK))))
    blk_onehot = jnp.where(blk_onehot, 1.0, 0.0).astype(BF16)
    ks = _rope(_group_rms(k2[:, :KV_WIDTH], knw_ref[1:2, :], pk_ref[...]), cos_t, sin_t)
    for g, kg in enumerate(per_group(ks)):
        kslc_ref[g] = jnp.concatenate([blk_onehot, kg.astype(BF16)], axis=1)
    kw = _rope(_group_rms(k2[:, KV_WIDTH:], knw_ref[2:3, :], pk_ref[...]), cos_t, sin_t)
    for g, kg in enumerate(per_group(kw)):
        kwin_ref[g] = kg.astype(BF16)

    v2 = proj(_C_V, 2 * KV_WIDTH)
    ones_rows = jnp.where(lax.broadcasted_iota(jnp.int32, (V_ROWS - HEAD_DIM, 1), 0) == 0, 1.0, 0.0)
    ones_rows = jnp.broadcast_to(ones_rows, (V_ROWS - HEAD_DIM, tm)).astype(BF16)
    for branch in range(2):
        vt = transposed(v2, branch).astype(BF16)
        for g in range(N_GROUPS):
            vt_ref[branch, g, 0:HEAD_DIM, :] = vt[g * HEAD_DIM:(g + 1) * HEAD_DIM, :]
            vt_ref[branch, g, HEAD_DIM:, :] = ones_rows
    c2 = proj(_C_C, 2 * KV_WIDTH)
    kc_ref[...] = c2[:, :KV_WIDTH]
    vc_ref[...] = c2[:, KV_WIDTH:]
    gate_logits = proj(_C_GATE, GATE_WIDTH)

    vext_ref[CONV_HALO:CONV_HALO + tm, :] = v
    y = (cw_ref[2:3, :] * v
         + cw_ref[1:2, :] * vext_ref[pl.ds(CONV_HALO - 1, tm), :]
         + cw_ref[0:1, :] * vext_ref[pl.ds(CONV_HALO - 2, tm), :])
    ob_ref[...] = (gate_b * y).astype(BF16)
    vext_ref[0:CONV_HALO, :] = vext_ref[tm:tm + CONV_HALO, :]
    gates_ref[...] = 1.0 / (1.0 + jnp.exp(-gate_logits))


def _inproj(x, pos_col, g, w, qnw, knw, cw, pq, pk, freq, *, seq, tm=512):
    n, d = x.shape
    bsz, tpb = n // seq, seq // tm
    row = lambda i: (i, 0)
    fixed = lambda i: (0, 0)
    widths = [(KV_WIDTH, F32), (KV_WIDTH, F32), (GATE_WIDTH, F32), (A_WIDTH, BF16)]
    out_shape = [jax.ShapeDtypeStruct((bsz, A_WIDTH, seq), BF16),
                 jax.ShapeDtypeStruct((bsz, N_GROUPS, seq, 2 * LANES), BF16),
                 jax.ShapeDtypeStruct((bsz, N_GROUPS, seq, LANES), BF16),
                 jax.ShapeDtypeStruct((bsz, 2, N_GROUPS, V_ROWS, seq), BF16)]
    out_specs = [pl.BlockSpec((None, A_WIDTH, tm), lambda i: (i // tpb, 0, i % tpb)),
                 pl.BlockSpec((None, N_GROUPS, tm, 2 * LANES), lambda i: (i // tpb, 0, i % tpb, 0)),
                 pl.BlockSpec((None, N_GROUPS, tm, LANES), lambda i: (i // tpb, 0, i % tpb, 0)),
                 pl.BlockSpec((None, 2, N_GROUPS, V_ROWS, tm), lambda i: (i // tpb, 0, 0, 0, i % tpb))]
    return pl.pallas_call(
        functools.partial(_inproj_kernel, tm=tm, tiles_per_batch=tpb),
        out_shape=out_shape + [jax.ShapeDtypeStruct((n, wd), dt) for wd, dt in widths],
        grid=(n // tm,),
        in_specs=[
            pl.BlockSpec((tm, d), row),
            pl.BlockSpec((tm, 1), row),
            pl.BlockSpec((1, d), fixed),
            pl.BlockSpec(w.shape, fixed),
            pl.BlockSpec(qnw.shape, fixed),
            pl.BlockSpec(knw.shape, fixed),
            pl.BlockSpec(cw.shape, fixed),
            pl.BlockSpec(pq.shape, fixed),
            pl.BlockSpec(pk.shape, fixed),
            pl.BlockSpec(freq.shape, fixed),
        ],
        out_specs=out_specs + [pl.BlockSpec((tm, wd), row) for wd, _ in widths],
        scratch_shapes=[pltpu.VMEM((tm + CONV_HALO, A_WIDTH), F32)],
        compiler_params=_cparams(("arbitrary",)),
        name="inproj",
    )(x, pos_col, g.reshape(1, d), w, qnw, knw, cw, pq, pk, freq)


def _gelu_tanh(x):
    return 0.5 * x * (1.0 + jnp.tanh(math.sqrt(2.0 / math.pi) * (x + 0.044715 * (x * x * x))))


def _compress_kernel(xk_ref, xv_ref, pos_ref, pe_ref, wk_ref, wv_ref, w2_ref, w2vt_ref, knw_ref, pk_ref, freq_ref,
                     kc_ref, vct_ref, *, ncp):
    def hidden(x_ref, pe_row, w_ref):
        x = jnp.concatenate([x_ref[pl.ds(j, ncp, stride=CMP_STRIDE), :] for j in range(CMP_STRIDE)], axis=1)
        a = _dot((x + pe_ref[pe_row:pe_row + 1, :]).astype(BF16), w_ref[0])
        b = _dot((x + pe_ref[pe_row + 1:pe_row + 2, :]).astype(BF16), w_ref[1])
        hid = a + pltpu.roll(b, ncp - 1, axis=0)
        return _gelu_tanh(hid)

    low = lax.broadcasted_iota(jnp.int32, (1, LANES), 1) < HEAD_DIM
    kc = _dot(hidden(xk_ref, 0, wk_ref).astype(BF16), w2_ref[...])
    kc = _group_rms(kc, knw_ref[0:1, :], pk_ref[...])
    cos_t, sin_t = _rope_tables(pos_ref[...].astype(F32), freq_ref[...])
    kc = _rope(kc, cos_t, sin_t)
    kc_ref[0] = jnp.where(low, kc, 0.0).astype(BF16)
    kc_ref[1] = jnp.where(low, pltpu.roll(kc, HEAD_DIM, axis=1), 0.0).astype(BF16)
    hv = hidden(xv_ref, 2, wv_ref)
    hv_t = jnp.concatenate(
        [jnp.concatenate([hv[r * LANES:(r + 1) * LANES, c * LANES:(c + 1) * LANES].T for r in range(ncp // LANES)],
                         axis=1) for c in range(hv.shape[1] // LANES)], axis=0)
    vct = _dot(w2vt_ref[...], hv_t.astype(BF16))
    for g in range(N_GROUPS):
        vct_ref[g] = vct[g * HEAD_DIM:(g + 1) * HEAD_DIM, :].astype(BF16)


def _compress(xk, xv, pos_c, pe, wk, wv, w2k, w2vt, knw, pk, freq, *, seq):
    bsz, ncp = pos_c.shape[:2]
    bat = lambda b: (b, 0, 0)
    fix2 = lambda b: (0, 0)
    fix3 = lambda b: (0, 0, 0)
    return pl.pallas_call(
        functools.partial(_compress_kernel, ncp=ncp),
        out_shape=[jax.ShapeDtypeStruct((bsz, N_GROUPS, ncp, LANES), BF16),
                   jax.ShapeDtypeStruct((bsz, N_GROUPS, HEAD_DIM, ncp), BF16)],
        grid=(bsz,),
        in_specs=[
            pl.BlockSpec((seq, KV_WIDTH), lambda b: (b, 0)),
            pl.BlockSpec((seq, KV_WIDTH), lambda b: (b, 0)),
            pl.BlockSpec((None, ncp, 1), bat),
            pl.BlockSpec(pe.shape, fix2),
            pl.BlockSpec(wk.shape, fix3),
            pl.BlockSpec(wv.shape, fix3),
            pl.BlockSpec(w2k.shape, fix2),
            pl.BlockSpec(w2vt.shape, fix2),
            pl.BlockSpec(knw.shape, fix2),
            pl.BlockSpec(pk.shape, fix2),
            pl.BlockSpec(freq.shape, fix2),
        ],
        out_specs=[pl.BlockSpec((None, N_GROUPS, ncp, LANES), lambda b: (b, 0, 0, 0)),
                   pl.BlockSpec((None, N_GROUPS, HEAD_DIM, ncp), lambda b: (b, 0, 0, 0))],
        compiler_params=_cparams(("parallel",)),
        name="compress",
    )(xk, xv, pos_c, pe, wk, wv, w2k, w2vt, knw, pk, freq)


def _head_columns(qt_ref):
    cols = []
    for pair in range(qt_ref.shape[0] // LANES):
        both = qt_ref[pair * LANES:(pair + 1) * LANES, :]
        cols += [both, jnp.concatenate([both[HEAD_DIM:, :], both[:HEAD_DIM, :]], axis=0)]
    return cols


def _software_pipeline(n, scores, probs, finish):
    s, p = {}, {}
    for step in range(n + 2):
        if step >= 2:
            finish(step - 2, p.pop(step - 2))
        if 1 <= step <= n:
            p[step - 1] = probs(step - 1, s.pop(step - 1))
        if step < n:
            s[step] = scores(step)


def _gate_row(gt_ref, branch, hd, group=0):
    r = group * GATE_ROWS + branch * HEADS_PER_GROUP + hd
    return gt_ref[r:r + 1, :]


def _cmp_kernel(qt_ref, kc_ref, vct_ref, ovt_ref, gt_ref, ot_ref, sbt_ref, *, tq, ncp):
    i = pl.program_id(1)
    t = i * tq + lax.broadcasted_iota(jnp.int32, (1, tq), 1)
    any_valid = (t >= CMP_BLOCK - 1).astype(F32)
    heads = _head_columns(qt_ref)
    cmp_per_sel = SEL_BLOCK // CMP_STRIDE
    groups = range(N_GROUPS)

    def run(nrow):
        nblk = nrow // cmp_per_sel
        blk_end = lax.broadcasted_iota(jnp.int32, (nrow, 1), 0) * CMP_STRIDE + (CMP_BLOCK - 1)
        valid = blk_end <= t
        kc = [kc_ref[g, 0:nrow, :] for g in groups]
        vct = [vct_ref[g, :, 0:nrow] for g in groups]
        psum_parts = [[] for _ in groups]

        def scores(u):
            return jnp.where(valid, _dot(kc[u // HEADS_PER_GROUP], heads[u]), NEG)

        def probs(u, s):
            e = jnp.exp(s - jnp.max(s, axis=0, keepdims=True))
            return e * (any_valid / jnp.sum(e, axis=0, keepdims=True))

        def finish(u, p):
            g, hd = divmod(u, HEADS_PER_GROUP)
            o_t = _dot(vct[g], p.astype(BF16))
            ot_ref[u * HEAD_DIM:(u + 1) * HEAD_DIM, :] = o_t * _gate_row(gt_ref, _BR_CMP, hd, g)
            psum_parts[g].append(p)

        _software_pipeline(N_HEADS, scores, probs, finish)

        ovt = ovt_ref[0:nblk, 0:nrow]
        blk = lax.broadcasted_iota(jnp.int32, (nblk, 1), 0)
        sel_start = blk * SEL_BLOCK
        cur = (t >> int(math.log2(SEL_BLOCK))) << int(math.log2(SEL_BLOCK))
        imp = []
        for parts in psum_parts:
            psum = (parts[0] + parts[1]) + (parts[2] + parts[3])
            hi = psum.astype(BF16)
            lo = (psum - hi.astype(F32)).astype(BF16)
            x = jnp.where(sel_start <= t, _dot(ovt, hi) + _dot(ovt, lo), -1.0)
            imp.append(jnp.where(sel_start == cur, 1e4, jnp.where(sel_start == 0, 1e4, x)))

        blk_f = blk.astype(F32)
        bias = [jnp.full((nblk, tq), NEG, F32) for _ in groups]
        for _ in range(SEL_TOPK):
            for g in groups:
                mx = jnp.max(imp[g], axis=0, keepdims=True)
                first = jnp.min(jnp.where(imp[g] == mx, blk_f, float(LANES)), axis=0, keepdims=True)
                pick = blk_f == first
                bias[g] = jnp.where(pick, 0.0, bias[g])
                imp[g] = jnp.where(pick, -3e38, imp[g])
        for g in groups:
            sbt_ref[g, 0:nblk, :] = bias[g].astype(BF16)
            if nblk < LANES:
                sbt_ref[g, nblk:, :] = jnp.full((LANES - nblk, tq), NEG, BF16)

    sizes = [LANES * (v + 1) for v in range(ncp // LANES)]
    assert sizes[0] // cmp_per_sel >= SEL_TOPK
    tokens_per_variant = LANES * CMP_STRIDE
    lax.switch((i * tq + tq - 1) // tokens_per_variant, [functools.partial(run, nrow) for nrow in sizes])


def _cmp_select(qt, kc, vct, ovt, gates_t, *, bsz, seq, tq=256):
    ncp = kc.shape[2]
    nq = seq // tq
    whole = lambda b, i: (b, 0, 0, 0)
    return pl.pallas_call(
        functools.partial(_cmp_kernel, tq=tq, ncp=ncp),
        out_shape=[jax.ShapeDtypeStruct((bsz, A_WIDTH, seq), F32),
                   jax.ShapeDtypeStruct((bsz, N_GROUPS, LANES, seq), BF16)],
        grid=(bsz, nq),
        in_specs=[
            pl.BlockSpec((None, A_WIDTH, tq), lambda b, i: (b, 0, i)),
            pl.BlockSpec((None, N_GROUPS, ncp, LANES), whole),
            pl.BlockSpec((None, N_GROUPS, HEAD_DIM, ncp), whole),
            pl.BlockSpec(ovt.shape, lambda b, i: (0, 0)),
            pl.BlockSpec((N_GROUPS * GATE_ROWS, tq), lambda b, i: (0, b * nq + i)),
        ],
        out_specs=[pl.BlockSpec((None, A_WIDTH, tq), lambda b, i: (b, 0, i)),
                   pl.BlockSpec((None, N_GROUPS, LANES, tq), lambda b, i: (b, 0, 0, i))],
        compiler_params=_cparams(("parallel", "parallel")),
        name="cmp_select",
    )(qt, kc, vct, ovt, gates_t)


_SLC_UNROLL = 2


def _slc_kernel(qt_ref, sbt_ref, k_ref, vt_ref, gt_ref, prev_ref, o_ref,
                qa_ref, s_ref, p_ref, acc_ref, *, tq):
    i = pl.program_id(2)
    rows = HEADS_PER_GROUP * tq
    qa_ref[...] = jnp.concatenate([jnp.concatenate([sbt_ref[...]] * HEADS_PER_GROUP, axis=1),
                                   jnp.concatenate(_head_columns(qt_ref), axis=1)], axis=0)
    p_ref[...] = jnp.zeros(p_ref.shape, BF16)
    acc_ref[...] = jnp.zeros(acc_ref.shape, F32)

    def accumulate(unit, alpha):
        j = jnp.where(unit <= 0, i, unit - 1)
        pv = _dot(vt_ref[:, pl.ds(pl.multiple_of(j * tq, tq), tq)], p_ref[...])
        acc_ref[...] = alpha * acc_ref[...] + pv

    def probs(m):
        s = s_ref[...]
        m_new = jnp.maximum(m, jnp.max(s, axis=0, keepdims=True))
        p_ref[...] = jnp.exp(s - m_new).astype(BF16)
        return m_new, jnp.exp(m - m_new)

    def scores(j, masked):
        s = _dot(k_ref[pl.ds(pl.multiple_of(j * tq, tq), tq), :], qa_ref[...])
        if masked:
            causal = (lax.broadcasted_iota(jnp.int32, (tq, 1), 0)
                      <= (lax.broadcasted_iota(jnp.int32, (1, rows), 1) & (tq - 1)))
            s = jnp.where(causal, s, NEG)
        s_ref[...] = s

    def body(k, carry):
        m, alpha = carry
        accumulate(k - 1, alpha)
        m, alpha = probs(m)
        scores(k, False)
        return m, alpha

    def steps(first, count, carry):
        for k in range(count):
            carry = body(first + k, carry)
        return carry

    scores(i, True)
    carry = (jnp.full((1, rows), -3e38, F32), jnp.ones((1, rows), F32))
    trips = i >> int(math.log2(_SLC_UNROLL))
    carry = lax.fori_loop(0, trips, lambda k, c: steps(k * _SLC_UNROLL, _SLC_UNROLL, c), carry)
    done = trips * _SLC_UNROLL
    size = _SLC_UNROLL // 2
    while size >= 1:
        carry = lax.cond((i & size) != 0, functools.partial(steps, done, size), lambda c: c, carry)
        done = done + (i & size)
        size //= 2
    m, alpha = carry
    accumulate(i - 1, alpha)
    m, alpha = probs(m)
    accumulate(i, alpha)

    gate = jnp.concatenate([_gate_row(gt_ref, _BR_SLC, hd) for hd in range(HEADS_PER_GROUP)], axis=1)
    out_t = acc_ref[0:HEAD_DIM, :] * (gate / acc_ref[HEAD_DIM:HEAD_DIM + 1, :])
    for pair in range(HEADS_PER_GROUP // 2):
        for c in range(tq // LANES):
            col = 2 * pair * tq + c * LANES
            both = jnp.concatenate([out_t[:, col:col + LANES], out_t[:, col + tq:col + tq + LANES]], axis=0)
            both = both + prev_ref[pair * LANES:(pair + 1) * LANES, c * LANES:(c + 1) * LANES]
            o_ref[c * LANES:(c + 1) * LANES, pair * LANES:(pair + 1) * LANES] = both.T.astype(BF16)


def _slc(qt, sbt, k_aug, vt_aug, gates_t, prev, *, bsz, seq, tq=512):
    nq = seq // tq
    gw = HEADS_PER_GROUP * HEAD_DIM
    rows = HEADS_PER_GROUP * tq
    tmap = lambda b, g, i: (b, g, i)
    return pl.pallas_call(
        functools.partial(_slc_kernel, tq=tq),
        out_shape=jax.ShapeDtypeStruct((bsz * seq, A_WIDTH), BF16),
        grid=(bsz, N_GROUPS, nq),
        in_specs=[
            pl.BlockSpec((None, gw, tq), tmap),
            pl.BlockSpec((None, None, LANES, tq), lambda b, g, i: (b, g, 0, i)),
            pl.BlockSpec((None, None, seq, 2 * LANES), lambda b, g, i: (b, g, 0, 0)),
            pl.BlockSpec((None, None, None, V_ROWS, seq), lambda b, g, i: (b, 0, g, 0, 0)),
            pl.BlockSpec((GATE_ROWS, tq), lambda b, g, i: (g, b * nq + i)),
            pl.BlockSpec((None, gw, tq), tmap),
        ],
        out_specs=pl.BlockSpec((tq, gw), lambda b, g, i: (b * nq + i, g)),
        scratch_shapes=[pltpu.VMEM((2 * LANES, rows), BF16), pltpu.VMEM((tq, rows), F32),
                        pltpu.VMEM((tq, rows), BF16), pltpu.VMEM((V_ROWS, rows), F32)],
        compiler_params=_cparams(("parallel", "parallel", "parallel")),
        name="slc_attn",
    )(qt, sbt, k_aug, vt_aug, gates_t, prev)


def _win_kernel(qt_ref, k_ref, vt_ref, band_ref, gt_ref, prev_ref, ot_ref, *, tq):
    i = pl.program_id(1)
    span = WINDOW + tq
    start = pl.multiple_of(jnp.maximum(i - WINDOW // tq, 0) * tq, tq)
    k = [k_ref[g, pl.ds(start, span), :] for g in range(N_GROUPS)]
    vt = [vt_ref[g, :, pl.ds(start, span)] for g in range(N_GROUPS)]
    heads = _head_columns(qt_ref)

    def scores(u):
        return _dot(k[u // HEADS_PER_GROUP], heads[u]) + band_ref[...]

    def probs(u, s):
        return jnp.exp(s - jnp.max(s, axis=0, keepdims=True)).astype(BF16)

    def finish(u, p):
        g, hd = divmod(u, HEADS_PER_GROUP)
        acc = _dot(vt[g], p)
        rows = slice(u * HEAD_DIM, (u + 1) * HEAD_DIM)
        scale = _gate_row(gt_ref, _BR_WIN, hd, g) / acc[HEAD_DIM:HEAD_DIM + 1, :]
        ot_ref[rows, :] = prev_ref[rows, :] + acc[:HEAD_DIM, :] * scale

    _software_pipeline(N_HEADS, scores, probs, finish)


def _window_band(tq):
    edge = WINDOW // tq
    v = jnp.arange(edge + 1)[:, None, None]
    kpos = jnp.maximum(v - edge, 0) * tq + jnp.arange(WINDOW + tq)[None, :, None]
    t = v * tq + jnp.arange(tq)[None, None, :]
    ok = (t - kpos >= 0) & (t - kpos < WINDOW)
    return jnp.where(ok, 0.0, NEG).astype(F32)


def _win(qt, k, vt, gates_t, prev, *, bsz, seq, tq=256):
    nq = seq // tq
    band = _window_band(tq)
    last = band.shape[0] - 1
    tile = lambda b, i: (b, 0, i)
    return pl.pallas_call(
        functools.partial(_win_kernel, tq=tq),
        out_shape=jax.ShapeDtypeStruct((bsz, A_WIDTH, seq), F32),
        grid=(bsz, nq),
        in_specs=[
            pl.BlockSpec((None, A_WIDTH, tq), tile),
            pl.BlockSpec((None, N_GROUPS, seq, LANES), lambda b, i: (b, 0, 0, 0)),
            pl.BlockSpec((None, None, N_GROUPS, V_ROWS, seq), lambda b, i: (b, 1, 0, 0, 0)),
            pl.BlockSpec((None,) + band.shape[1:], lambda b, i: (jnp.minimum(i, last), 0, 0)),
            pl.BlockSpec((N_GROUPS * GATE_ROWS, tq), lambda b, i: (0, b * nq + i)),
            pl.BlockSpec((None, A_WIDTH, tq), tile),
        ],
        out_specs=pl.BlockSpec((None, A_WIDTH, tq), tile),
        compiler_params=_cparams(("parallel", "parallel")),
        name="win_attn",
    )(qt, k, vt, band, gates_t, prev)


def _outproj_kernel(x_ref, oa_ref, ob_ref, w_ref, o_ref):
    y = _dot(oa_ref[...], w_ref[0:A_WIDTH, :]) + _dot(ob_ref[...], w_ref[A_WIDTH:, :])
    o_ref[...] = x_ref[...] + y


def _outproj(x, oa, ob, w, *, tm=1024):
    n, d = x.shape
    row = lambda i: (i, 0)
    return pl.pallas_call(
        _outproj_kernel,
        out_shape=jax.ShapeDtypeStruct((n, d), F32),
        grid=(n // tm,),
        in_specs=[
            pl.BlockSpec((tm, d), row),
            pl.BlockSpec((tm, A_WIDTH), row),
            pl.BlockSpec((tm, A_WIDTH), row),
            pl.BlockSpec(w.shape, lambda i: (0, 0)),
        ],
        out_specs=pl.BlockSpec((tm, d), row),
        compiler_params=_cparams(("parallel",)),
        name="outproj",
    )(x, oa, ob, w)


def _pool_kernel(x_ref, halo_ref, g_ref, w_ref, sc_ref, o_ref, ext_ref, tmp_ref, *, tm, tiles_per_batch):
    i = pl.program_id(0)
    first = i % tiles_per_batch == 0
    x = x_ref[...]
    h = _rms(x, g_ref[...])
    hh = _rms(halo_ref[...], g_ref[...])
    pad, body = _POOL_PAD, _POOL_PAD + POOL_HALO
    ext_ref[0:pad, :] = jnp.zeros((pad, h.shape[1]), F32)
    ext_ref[pad:body, :] = jnp.where(first, 0.0, hh)
    ext_ref[body:body + tm, :] = h
    gw = h.shape[1] // len(POOL_WINDOWS)
    for slot in range(2):
        tmp_ref[slot, 0:pad, :] = jnp.zeros((pad, gw), F32)
    t = (i % tiles_per_batch) * tm + lax.broadcasted_iota(jnp.int32, (tm, 1), 0)
    span = POOL_HALO + tm
    for gi, win in enumerate(POOL_WINDOWS):
        c0 = gi * gw
        hg = h[:, c0:c0 + gw]
        read = lambda start, c0=c0: ext_ref[pl.ds(start, span), c0:c0 + gw]
        shift, slot = 1, 0
        while shift < win:
            tmp_ref[slot, pad:pad + span, :] = read(pad) + read(pad - shift)
            read = lambda start, slot=slot: tmp_ref[slot, pl.ds(start, span), :]
            shift, slot = 2 * shift, 1 - slot
        tot = read(pad)[POOL_HALO:, :]
        cnt = jnp.minimum(t + 1, win).astype(F32)
        y = _dot((tot / cnt - hg).astype(BF16), w_ref[gi])
        o_ref[:, c0:c0 + gw] = x[:, c0:c0 + gw] + y * sc_ref[:, c0:c0 + gw]


def _pool(x, g, w, scale, *, seq, tm=512):
    n, d = x.shape
    hb = tm // POOL_HALO
    return pl.pallas_call(
        functools.partial(_pool_kernel, tm=tm, tiles_per_batch=seq // tm),
        out_shape=jax.ShapeDtypeStruct((n, d), F32),
        grid=(n // tm,),
        in_specs=[
            pl.BlockSpec((tm, d), lambda i: (i, 0)),
            pl.BlockSpec((POOL_HALO, d), lambda i: (jnp.maximum(i * hb - 1, 0), 0)),
            pl.BlockSpec((1, d), lambda i: (0, 0)),
            pl.BlockSpec(w.shape, lambda i: (0, 0, 0)),
            pl.BlockSpec((1, d), lambda i: (0, 0)),
        ],
        out_specs=pl.BlockSpec((tm, d), lambda i: (i, 0)),
        scratch_shapes=[pltpu.VMEM((_POOL_PAD + POOL_HALO + tm, d), F32),
                        pltpu.VMEM((2, _POOL_PAD + POOL_HALO + tm, d // len(POOL_WINDOWS)), F32)],
        compiler_params=_cparams(("parallel",)),
        name="pool_mixer",
    )(x, x, g.reshape(1, d), w, scale.reshape(1, d))


def _pack_w_in(w_in):
    o = 0
    q = w_in[:, o:o + A_WIDTH]; o += A_WIDTH
    kc, vc, ks, vs, kw, vw = [w_in[:, o + k * KV_WIDTH:o + (k + 1) * KV_WIDTH] for k in range(6)]
    o += 6 * KV_WIDTH
    gates = w_in[:, o:o + N_BRANCH * N_HEADS]; o += N_BRANCH * N_HEADS
    u, gb, gc = [w_in[:, o + k * A_WIDTH:o + (k + 1) * A_WIDTH] for k in range(3)]
    gates = gates.reshape(-1, N_GROUPS, HEADS_PER_GROUP, N_BRANCH).transpose(0, 1, 3, 2)
    gates = gates.reshape(-1, N_GROUPS, N_BRANCH * HEADS_PER_GROUP)
    gates = jnp.pad(gates, ((0, 0), (0, 0), (0, LANES - N_BRANCH * HEADS_PER_GROUP))).reshape(-1, GATE_WIDTH)
    return jnp.concatenate([q, ks, kw, vs, vw, kc, vc, gates, u, gb, gc], axis=1).astype(BF16)


def _group_mean_matrix(width):
    r = jnp.arange(width) // HEAD_DIM
    return jnp.where(r[:, None] == r[None, :], 1.0 / HEAD_DIM, 0.0).astype(BF16)


def _rope_freq_row():
    lane = jnp.arange(LANES) % HEAD_DIM
    freqs = ROPE_THETA ** (-jnp.arange(0, ROPE_DIM, 2, dtype=F32) / ROPE_DIM)
    return jnp.where(lane < ROPE_DIM, freqs[lane % ROPE_HALF], 0.0).astype(F32).reshape(1, LANES)


def _hybrid_mixer(x, positions, g, w_in, q_norm, k_norm, cmp_pos, cmp_w1, cmp_w2, conv_w, w_out, *, bsz, seq):
    n = bsz * seq
    ncp = seq // CMP_STRIDE
    freq = _rope_freq_row()
    pq = _group_mean_matrix(A_WIDTH)
    pk = _group_mean_matrix(KV_WIDTH)
    qnw = jnp.tile(q_norm, N_HEADS).reshape(1, A_WIDTH)
    knw = jnp.tile(k_norm, (1, N_GROUPS))

    q_t, k_slc, k_win, v_t, kc_raw, vc_raw, gates, ob = _inproj(
        x, positions.reshape(n, 1), g, _pack_w_in(w_in), qnw, knw, conv_w, pq, pk, freq, seq=seq)

    eye_g = jnp.eye(N_GROUPS, dtype=F32)
    seg = CMP_STRIDE * KV_WIDTH

    def seg_weights(w1):
        w1r = w1.reshape(CMP_BLOCK, HEAD_DIM, CMP_HIDDEN)
        halves = [jnp.einsum('jdh,gk->jgdkh', w1r[a:a + CMP_STRIDE], eye_g).reshape(seg, N_GROUPS * CMP_HIDDEN)
                  for a in (0, CMP_STRIDE)]
        return jnp.stack(halves).astype(BF16)

    def seg_pos(pe):
        return [jnp.broadcast_to(pe[a:a + CMP_STRIDE, None, :], (CMP_STRIDE, N_GROUPS, HEAD_DIM)).reshape(seg)
                for a in (0, CMP_STRIDE)]

    pe = jnp.stack(seg_pos(cmp_pos[0]) + seg_pos(cmp_pos[1]))
    w2 = jnp.stack([jnp.einsum('hd,gk->ghkd', cmp_w2[a], eye_g).reshape(N_GROUPS * CMP_HIDDEN, KV_WIDTH)
                    for a in range(2)]).astype(BF16)
    blk_end = jnp.minimum(jnp.arange(ncp) * CMP_STRIDE + CMP_BLOCK - 1, seq - 1)
    pos_c = positions[:, blk_end].reshape(bsz, ncp, 1)
    kc_g, vc_t = _compress(kc_raw, vc_raw, pos_c, pe, seg_weights(cmp_w1[0]), seg_weights(cmp_w1[1]), w2[0], w2[1].T,
                           knw, pk, freq, seq=seq)

    gates_t = gates.reshape(n, N_GROUPS, LANES)[:, :, :GATE_ROWS].transpose(1, 2, 0).reshape(N_GROUPS * GATE_ROWS, n)
    cstart = jnp.arange(ncp) * CMP_STRIDE
    sstart = jnp.arange(LANES) * SEL_BLOCK
    ovt = ((cstart[None, :] < sstart[:, None] + SEL_BLOCK)
           & (cstart[None, :] + CMP_BLOCK > sstart[:, None])).astype(BF16)
    o_t, sel_bias = _cmp_select(q_t, kc_g, vc_t, ovt, gates_t, bsz=bsz, seq=seq)

    o_t = _win(q_t, k_win, v_t, gates_t, o_t, bsz=bsz, seq=seq)
    o_a = _slc(q_t, sel_bias, k_slc, v_t, gates_t, o_t, bsz=bsz, seq=seq)

    return _outproj(x, o_a, ob, w_out.astype(BF16))


def kernel(x, positions, ffn_norm, ffn_w_gate, ffn_w_up, ffn_w_down, mix_norm, hyb_w_in, hyb_q_norm, hyb_k_norm,
           hyb_cmp_pos, hyb_cmp_w1, hyb_cmp_w2, hyb_conv_w, hyb_w_out, pool_w, pool_scale):
    bsz, seq, d = x.shape
    depth = ffn_norm.shape[0]
    assert seq % 2048 == 0 and seq // SEL_BLOCK <= LANES and seq // SEL_BLOCK >= SEL_TOPK
    wg, wu, wd = (w.astype(BF16) for w in (ffn_w_gate, ffn_w_up, ffn_w_down))
    xf = x.reshape(bsz * seq, d)
    for layer in range(depth):
        xf = _ffn(xf, ffn_norm[layer, 0], wg, wu, wd, layer, 0)
        i = layer // 2
        if layer % 2 == 0:
            xf = _hybrid_mixer(xf, positions, mix_norm[layer], hyb_w_in[i], hyb_q_norm[i], hyb_k_norm[i],
                               hyb_cmp_pos[i], hyb_cmp_w1[i], hyb_cmp_w2[i], hyb_conv_w[i], hyb_w_out[i],
                               bsz=bsz, seq=seq)
        else:
            xf = _pool(xf, mix_norm[layer], pool_w[i].astype(BF16), pool_scale[i], seq=seq)
        xf = _ffn(xf, ffn_norm[layer, 1], wg, wu, wd, layer, 1)
    return xf.reshape(bsz, seq, d)
```

```python
import functools
import math

import jax
import jax.numpy as jnp
import numpy as np
from jax import lax
from jax.experimental import pallas as pl
from jax.experimental.pallas import tpu as pltpu

F32 = jnp.float32
BF16 = jnp.bfloat16

HEAD_DIM = 64
N_GROUPS = 2
HEADS_PER_GROUP = 4
N_HEADS = N_GROUPS * HEADS_PER_GROUP
A_WIDTH = N_HEADS * HEAD_DIM
KV_WIDTH = N_GROUPS * HEAD_DIM
ROPE_DIM = HEAD_DIM // 4
ROPE_HALF = ROPE_DIM // 2
ROPE_THETA = 500000.0
CMP_BLOCK = 32
CMP_STRIDE = 16
CMP_HIDDEN = 2 * HEAD_DIM
SEL_BLOCK = 64
SEL_TOPK = 16
WINDOW = 512
N_BRANCH = 3
_BR_CMP, _BR_SLC, _BR_WIN = 0, 1, 2
CONV_WIDTH = 3
POOL_WINDOWS = (2, 4, 8, 16)
POOL_HALO = 16
_POOL_PAD = 8
EPS = 1e-6
NEG = -1e30

LANES = 128
VMEM_LIMIT_BYTES = 56 * 1024 * 1024


def _cparams(sem):
    return pltpu.CompilerParams(dimension_semantics=sem, vmem_limit_bytes=VMEM_LIMIT_BYTES)


def _dot(a, b):
    return jnp.dot(a, b, preferred_element_type=F32)


def _dot_split(a, b):
    hi = a.astype(BF16)
    lo = (a - hi.astype(F32)).astype(BF16)
    return _dot(hi, b) + _dot(lo, b)


def _rms(x, g):
    ms = jnp.mean(x * x, axis=-1, keepdims=True)
    return x * lax.rsqrt(ms + EPS) * g


def _group_rms(x, g, pmat):
    ms = _dot_split(x * x, pmat)
    return x * lax.rsqrt(ms + EPS) * g


def _rope_tables(pos_col, freq_row):
    ang = pos_col * freq_row
    lane = lax.broadcasted_iota(jnp.int32, (1, LANES), 1) & (HEAD_DIM - 1)
    sign = jnp.where(lane < ROPE_HALF, -1.0, 1.0).astype(F32)
    return jnp.cos(ang), jnp.sin(ang) * sign


def _rope(x, cos_t, sin_t):
    w = x.shape[1]
    reps = w // LANES
    if reps > 1:
        cos_t = jnp.concatenate([cos_t] * reps, axis=1)
        sin_t = jnp.concatenate([sin_t] * reps, axis=1)
    lane = lax.broadcasted_iota(jnp.int32, (1, w), 1) & (HEAD_DIM - 1)
    partner = jnp.where(lane < ROPE_HALF,
                        pltpu.roll(x, w - ROPE_HALF, axis=1),
                        pltpu.roll(x, ROPE_HALF, axis=1))
    return x * cos_t + partner * sin_t


def _ffn_kernel(x_ref, xnext_ref, g_ref, wg_ref, wu_ref, wd_ref, o_ref, h_ref, acc_ref, *, nj):
    i, j = pl.program_id(0), pl.program_id(1)
    slot = i % 2

    @pl.when((i == 0) & (j == 0))
    def _():
        h_ref[0] = _rms(x_ref[...], g_ref[...]).astype(BF16)

    def step(first, last):
        h = h_ref[slot]
        a = _dot(h, wg_ref[...])
        b = _dot(h, wu_ref[...])
        act = a * (1.0 / (1.0 + jnp.exp(-a))) * b
        y = _dot(act.astype(BF16), wd_ref[...])
        if not first:
            y = acc_ref[...] + y
        if last:
            h_ref[1 - slot] = _rms(xnext_ref[...], g_ref[...]).astype(BF16)
            o_ref[...] = x_ref[...] + 0.5 * y
        else:
            acc_ref[...] = y

    for jj in range(nj):
        pl.when(j == jj)(functools.partial(step, jj == 0, jj == nj - 1))


def _ffn(x, g, wg, wu, wd, layer, half, *, tm=512, tf=2816):
    n, d = x.shape
    dff = wg.shape[-1]
    ni, nj = n // tm, dff // tf
    wmode = dict(pipeline_mode=pl.Buffered(1)) if nj == 1 else {}
    return pl.pallas_call(
        functools.partial(_ffn_kernel, nj=nj),
        out_shape=jax.ShapeDtypeStruct((n, d), F32),
        grid=(ni, nj),
        in_specs=[
            pl.BlockSpec((tm, d), lambda i, j: (i, 0)),
            pl.BlockSpec((tm, d), lambda i, j: (jnp.minimum(i + 1, ni - 1), 0)),
            pl.BlockSpec((1, d), lambda i, j: (0, 0)),
            pl.BlockSpec((None, None, d, tf), lambda i, j: (layer, half, 0, j), **wmode),
            pl.BlockSpec((None, None, d, tf), lambda i, j: (layer, half, 0, j), **wmode),
            pl.BlockSpec((None, None, tf, d), lambda i, j: (layer, half, j, 0), **wmode),
        ],
        out_specs=pl.BlockSpec((tm, d), lambda i, j: (i, 0)),
        scratch_shapes=[pltpu.VMEM((2, tm, d), BF16), pltpu.VMEM((tm, d), F32)],
        compiler_params=_cparams(("arbitrary", "arbitrary")),
        name="ffn",
    )(x, x, g.reshape(1, d), wg, wu, wd)


GATE_WIDTH = N_GROUPS * LANES
GATE_ROWS = 16
_C_Q = 0
_C_K = _C_Q + A_WIDTH
_C_V = _C_K + 2 * KV_WIDTH
_C_C = _C_V + 2 * KV_WIDTH
_C_GATE = _C_C + 2 * KV_WIDTH
_C_U = _C_GATE + GATE_WIDTH
_C_GB = _C_U + A_WIDTH
_C_GC = _C_GB + A_WIDTH
CONV_HALO = 8
V_ROWS = HEAD_DIM + 16


def _inproj_kernel(x_ref, pos_ref, g_ref, w_ref, qnw_ref, knw_ref, cw_ref, pq_ref, pk_ref, freq_ref,
                   qt_ref, kslc_ref, kwin_ref, vt_ref, kc_ref, vc_ref, gates_ref, ob_ref,
                   vext_ref, *, tm, tiles_per_batch):
    i = pl.program_id(0)
    chunks = range(tm // LANES)

    def transposed(a, c):
        return jnp.concatenate([a[r * LANES:(r + 1) * LANES, c * LANES:(c + 1) * LANES].T for r in chunks], axis=1)

    def per_group(k):
        low = lax.broadcasted_iota(jnp.int32, (1, LANES), 1) < HEAD_DIM
        return [jnp.where(low, k, 0.0), jnp.where(low, pltpu.roll(k, HEAD_DIM, axis=1), 0.0)]

    @pl.when(i % tiles_per_batch == 0)
    def _():
        vext_ref[0:CONV_HALO, :] = jnp.zeros((CONV_HALO, A_WIDTH), F32)

    h = _rms(x_ref[...], g_ref[...]).astype(BF16)

    def proj(c0, width):
        return _dot(h, w_ref[:, c0:c0 + width])

    q_raw = proj(_C_Q, A_WIDTH)
    k2 = proj(_C_K, 2 * KV_WIDTH)
    cos_t, sin_t = _rope_tables(pos_ref[...].astype(F32), freq_ref[...])
    v = proj(_C_GC, A_WIDTH) * proj(_C_U, A_WIDTH)
    gate_b = proj(_C_GB, A_WIDTH)

    q = _group_rms(q_raw, qnw_ref[...], pq_ref[...])
    q = _rope(q, cos_t, sin_t) * (HEAD_DIM ** -0.5)
    for c in range(A_WIDTH // LANES):
        qt_ref[c * LANES:(c + 1) * LANES, :] = transposed(q, c).astype(BF16)

    tok = (i % tiles_per_batch) * tm + lax.broadcasted_iota(jnp.int32, (tm, 1), 0)
    blk_onehot = (lax.broadcasted_iota(jnp.int32, (1, LANES), 1) == (tok >> int(math.log2(SEL_BLOCK))))
    blk_onehot = jnp.where(blk_onehot, 1.0, 0.0).astype(BF16)
    ks = _rope(_group_rms(k2[:, :KV_WIDTH], knw_ref[1:2, :], pk_ref[...]), cos_t, sin_t)
    for g, kg in enumerate(per_group(ks)):
        kslc_ref[g] = jnp.concatenate([blk_onehot, kg.astype(BF16)], axis=1)
    kw = _rope(_group_rms(k2[:, KV_WIDTH:], knw_ref[2:3, :], pk_ref[...]), cos_t, sin_t)
    for g, kg in enumerate(per_group(kw)):
        kwin_ref[g] = kg.astype(BF16)

    v2 = proj(_C_V, 2 * KV_WIDTH)
    ones_rows = jnp.where(lax.broadcasted_iota(jnp.int32, (V_ROWS - HEAD_DIM, 1), 0) == 0, 1.0, 0.0)
    ones_rows = jnp.broadcast_to(ones_rows, (V_ROWS - HEAD_DIM, tm)).astype(BF16)
    for branch in range(2):
        vt = transposed(v2, branch).astype(BF16)
        for g in range(N_GROUPS):
            vt_ref[branch, g, 0:HEAD_DIM, :] = vt[g * HEAD_DIM:(g + 1) * HEAD_DIM, :]
            vt_ref[branch, g, HEAD_DIM:, :] = ones_rows
    c2 = proj(_C_C, 2 * KV_WIDTH)
    kc_ref[...] = c2[:, :KV_WIDTH]
    vc_ref[...] = c2[:, KV_WIDTH:]
    gate_logits = proj(_C_GATE, GATE_WIDTH)

    vext_ref[CONV_HALO:CONV_HALO + tm, :] = v
    y = (cw_ref[2:3, :] * v
         + cw_ref[1:2, :] * vext_ref[pl.ds(CONV_HALO - 1, tm), :]
         + cw_ref[0:1, :] * vext_ref[pl.ds(CONV_HALO - 2, tm), :])
    ob_ref[...] = (gate_b * y).astype(BF16)
    vext_ref[0:CONV_HALO, :] = vext_ref[tm:tm + CONV_HALO, :]
    gates_ref[...] = 1.0 / (1.0 + jnp.exp(-gate_logits))


def _inproj(x, pos_col, g, w, qnw, knw, cw, pq, pk, freq, *, seq, tm=512):
    n, d = x.shape
    bsz, tpb = n // seq, seq // tm
    row = lambda i: (i, 0)
    fixed = lambda i: (0, 0)
    widths = [(KV_WIDTH, F32), (KV_WIDTH, F32), (GATE_WIDTH, F32), (A_WIDTH, BF16)]
    out_shape = [jax.ShapeDtypeStruct((bsz, A_WIDTH, seq), BF16),
                 jax.ShapeDtypeStruct((bsz, N_GROUPS, seq, 2 * LANES), BF16),
                 jax.ShapeDtypeStruct((bsz, N_GROUPS, seq, LANES), BF16),
                 jax.ShapeDtypeStruct((bsz, 2, N_GROUPS, V_ROWS, seq), BF16)]
    out_specs = [pl.BlockSpec((None, A_WIDTH, tm), lambda i: (i // tpb, 0, i % tpb)),
                 pl.BlockSpec((None, N_GROUPS, tm, 2 * LANES), lambda i: (i // tpb, 0, i % tpb, 0)),
                 pl.BlockSpec((None, N_GROUPS, tm, LANES), lambda i: (i // tpb, 0, i % tpb, 0)),
                 pl.BlockSpec((None, 2, N_GROUPS, V_ROWS, tm), lambda i: (i // tpb, 0, 0, 0, i % tpb))]
    return pl.pallas_call(
        functools.partial(_inproj_kernel, tm=tm, tiles_per_batch=tpb),
        out_shape=out_shape + [jax.ShapeDtypeStruct((n, wd), dt) for wd, dt in widths],
        grid=(n // tm,),
        in_specs=[
            pl.BlockSpec((tm, d), row),
            pl.BlockSpec((tm, 1), row),
            pl.BlockSpec((1, d), fixed),
            pl.BlockSpec(w.shape, fixed),
            pl.BlockSpec(qnw.shape, fixed),
            pl.BlockSpec(knw.shape, fixed),
            pl.BlockSpec(cw.shape, fixed),
            pl.BlockSpec(pq.shape, fixed),
            pl.BlockSpec(pk.shape, fixed),
            pl.BlockSpec(freq.shape, fixed),
        ],
        out_specs=out_specs + [pl.BlockSpec((tm, wd), row) for wd, _ in widths],
        scratch_shapes=[pltpu.VMEM((tm + CONV_HALO, A_WIDTH), F32)],
        compiler_params=_cparams(("arbitrary",)),
        name="inproj",
    )(x, pos_col, g.reshape(1, d), w, qnw, knw, cw, pq, pk, freq)


def _gelu_tanh(x):
    return 0.5 * x * (1.0 + jnp.tanh(math.sqrt(2.0 / math.pi) * (x + 0.044715 * (x * x * x))))


def _compress_kernel(xk_ref, xv_ref, pos_ref, pe_ref, wk_ref, wv_ref, w2_ref, w2vt_ref, knw_ref, pk_ref, freq_ref,
                     kc_ref, vct_ref, *, ncp):
    def hidden(x_ref, pe_row, w_ref):
        x = jnp.concatenate([x_ref[pl.ds(j, ncp, stride=CMP_STRIDE), :] for j in range(CMP_STRIDE)], axis=1)
        a = _dot((x + pe_ref[pe_row:pe_row + 1, :]).astype(BF16), w_ref[0])
        b = _dot((x + pe_ref[pe_row + 1:pe_row + 2, :]).astype(BF16), w_ref[1])
        hid = a + pltpu.roll(b, ncp - 1, axis=0)
        return _gelu_tanh(hid)

    low = lax.broadcasted_iota(jnp.int32, (1, LANES), 1) < HEAD_DIM
    kc = _dot(hidden(xk_ref, 0, wk_ref).astype(BF16), w2_ref[...])
    kc = _group_rms(kc, knw_ref[0:1, :], pk_ref[...])
    cos_t, sin_t = _rope_tables(pos_ref[...].astype(F32), freq_ref[...])
    kc = _rope(kc, cos_t, sin_t)
    kc_ref[0] = jnp.where(low, kc, 0.0).astype(BF16)
    kc_ref[1] = jnp.where(low, pltpu.roll(kc, HEAD_DIM, axis=1), 0.0).astype(BF16)
    hv = hidden(xv_ref, 2, wv_ref)
    hv_t = jnp.concatenate(
        [jnp.concatenate([hv[r * LANES:(r + 1) * LANES, c * LANES:(c + 1) * LANES].T for r in range(ncp // LANES)],
                         axis=1) for c in range(hv.shape[1] // LANES)], axis=0)
    vct = _dot(w2vt_ref[...], hv_t.astype(BF16))
    for g in range(N_GROUPS):
        vct_ref[g] = vct[g * HEAD_DIM:(g + 1) * HEAD_DIM, :].astype(BF16)


def _compress(xk, xv, pos_c, pe, wk, wv, w2k, w2vt, knw, pk, freq, *, seq):
    bsz, ncp = pos_c.shape[:2]
    bat = lambda b: (b, 0, 0)
    fix2 = lambda b: (0, 0)
    fix3 = lambda b: (0, 0, 0)
    return pl.pallas_call(
        functools.partial(_compress_kernel, ncp=ncp),
        out_shape=[jax.ShapeDtypeStruct((bsz, N_GROUPS, ncp, LANES), BF16),
                   jax.ShapeDtypeStruct((bsz, N_GROUPS, HEAD_DIM, ncp), BF16)],
        grid=(bsz,),
        in_specs=[
            pl.BlockSpec((seq, KV_WIDTH), lambda b: (b, 0)),
            pl.BlockSpec((seq, KV_WIDTH), lambda b: (b, 0)),
            pl.BlockSpec((None, ncp, 1), bat),
            pl.BlockSpec(pe.shape, fix2),
            pl.BlockSpec(wk.shape, fix3),
            pl.BlockSpec(wv.shape, fix3),
            pl.BlockSpec(w2k.shape, fix2),
            pl.BlockSpec(w2vt.shape, fix2),
            pl.BlockSpec(knw.shape, fix2),
            pl.BlockSpec(pk.shape, fix2),
            pl.BlockSpec(freq.shape, fix2),
        ],
        out_specs=[pl.BlockSpec((None, N_GROUPS, ncp, LANES), lambda b: (b, 0, 0, 0)),
                   pl.BlockSpec((None, N_GROUPS, HEAD_DIM, ncp), lambda b: (b, 0, 0, 0))],
        compiler_params=_cparams(("parallel",)),
        name="compress",
    )(xk, xv, pos_c, pe, wk, wv, w2k, w2vt, knw, pk, freq)


def _head_columns(qt_ref):
    cols = []
    for pair in range(qt_ref.shape[0] // LANES):
        both = qt_ref[pair * LANES:(pair + 1) * LANES, :]
        cols += [both, jnp.concatenate([both[HEAD_DIM:, :], both[:HEAD_DIM, :]], axis=0)]
    return cols


def _software_pipeline(n, scores, probs, finish):
    s, p = {}, {}
    for step in range(n + 2):
        if step >= 2:
            finish(step - 2, p.pop(step - 2))
        if 1 <= step <= n:
            p[step - 1] = probs(step - 1, s.pop(step - 1))
        if step < n:
            s[step] = scores(step)


def _gate_row(gt_ref, branch, hd, group=0):
    r = group * GATE_ROWS + branch * HEADS_PER_GROUP + hd
    return gt_ref[r:r + 1, :]


def _cmp_kernel(qt_ref, kc_ref, vct_ref, ovt_ref, gt_ref, ot_ref, sbt_ref, *, tq, ncp):
    i = pl.program_id(1)
    t = i * tq + lax.broadcasted_iota(jnp.int32, (1, tq), 1)
    any_valid = (t >= CMP_BLOCK - 1).astype(F32)
    heads = _head_columns(qt_ref)
    cmp_per_sel = SEL_BLOCK // CMP_STRIDE
    groups = range(N_GROUPS)

    def run(nrow):
        nblk = nrow // cmp_per_sel
        blk_end = lax.broadcasted_iota(jnp.int32, (nrow, 1), 0) * CMP_STRIDE + (CMP_BLOCK - 1)
        valid = blk_end <= t
        kc = [kc_ref[g, 0:nrow, :] for g in groups]
        vct = [vct_ref[g, :, 0:nrow] for g in groups]
        psum_parts = [[] for _ in groups]

        def scores(u):
            return jnp.where(valid, _dot(kc[u // HEADS_PER_GROUP], heads[u]), NEG)

        def probs(u, s):
            e = jnp.exp(s - jnp.max(s, axis=0, keepdims=True))
            return e * (any_valid / jnp.sum(e, axis=0, keepdims=True))

        def finish(u, p):
            g, hd = divmod(u, HEADS_PER_GROUP)
            o_t = _dot(vct[g], p.astype(BF16))
            ot_ref[u * HEAD_DIM:(u + 1) * HEAD_DIM, :] = o_t * _gate_row(gt_ref, _BR_CMP, hd, g)
            psum_parts[g].append(p)

        _software_pipeline(N_HEADS, scores, probs, finish)

        ovt = ovt_ref[0:nblk, 0:nrow]
        blk = lax.broadcasted_iota(jnp.int32, (nblk, 1), 0)
        sel_start = blk * SEL_BLOCK
        cur = (t >> int(math.log2(SEL_BLOCK))) << int(math.log2(SEL_BLOCK))
        imp = []
        for parts in psum_parts:
            psum = (parts[0] + parts[1]) + (parts[2] + parts[3])
            hi = psum.astype(BF16)
            lo = (psum - hi.astype(F32)).astype(BF16)
            x = jnp.where(sel_start <= t, _dot(ovt, hi) + _dot(ovt, lo), -1.0)
            imp.append(jnp.where(sel_start == cur, 1e4, jnp.where(sel_start == 0, 1e4, x)))

        blk_f = blk.astype(F32)
        bias = [jnp.full((nblk, tq), NEG, F32) for _ in groups]
        for _ in range(SEL_TOPK):
            for g in groups:
                mx = jnp.max(imp[g], axis=0, keepdims=True)
                first = jnp.min(jnp.where(imp[g] == mx, blk_f, float(LANES)), axis=0, keepdims=True)
                pick = blk_f == first
                bias[g] = jnp.where(pick, 0.0, bias[g])
                imp[g] = jnp.where(pick, -3e38, imp[g])
        for g in groups:
            sbt_ref[g, 0:nblk, :] = bias[g].astype(BF16)
            if nblk < LANES:
                sbt_ref[g, nblk:, :] = jnp.full((LANES - nblk, tq), NEG, BF16)

    sizes = [LANES * (v + 1) for v in range(ncp // LANES)]
    assert sizes[0] // cmp_per_sel >= SEL_TOPK
    tokens_per_variant = LANES * CMP_STRIDE
    lax.switch((i * tq + tq - 1) // tokens_per_variant, [functools.partial(run, nrow) for nrow in sizes])


def _cmp_select(qt, kc, vct, ovt, gates_t, *, bsz, seq, tq=256):
    ncp = kc.shape[2]
    nq = seq // tq
    whole = lambda b, i: (b, 0, 0, 0)
    return pl.pallas_call(
        functools.partial(_cmp_kernel, tq=tq, ncp=ncp),
        out_shape=[jax.ShapeDtypeStruct((bsz, A_WIDTH, seq), F32),
                   jax.ShapeDtypeStruct((bsz, N_GROUPS, LANES, seq), BF16)],
        grid=(bsz, nq),
        in_specs=[
            pl.BlockSpec((None, A_WIDTH, tq), lambda b, i: (b, 0, i)),
            pl.BlockSpec((None, N_GROUPS, ncp, LANES), whole),
            pl.BlockSpec((None, N_GROUPS, HEAD_DIM, ncp), whole),
            pl.BlockSpec(ovt.shape, lambda b, i: (0, 0)),
            pl.BlockSpec((N_GROUPS * GATE_ROWS, tq), lambda b, i: (0, b * nq + i)),
        ],
        out_specs=[pl.BlockSpec((None, A_WIDTH, tq), lambda b, i: (b, 0, i)),
                   pl.BlockSpec((None, N_GROUPS, LANES, tq), lambda b, i: (b, 0, 0, i))],
        compiler_params=_cparams(("parallel", "parallel")),
        name="cmp_select",
    )(qt, kc, vct, ovt, gates_t)


_SLC_UNROLL = 2


def _slc_kernel(qt_ref, sbt_ref, k_ref, vt_ref, gt_ref, prev_ref, o_ref,
                qa_ref, s_ref, p_ref, acc_ref, *, tq):
    i = pl.program_id(2)
    rows = HEADS_PER_GROUP * tq
    qa_ref[...] = jnp.concatenate([jnp.concatenate([sbt_ref[...]] * HEADS_PER_GROUP, axis=1),
                                   jnp.concatenate(_head_columns(qt_ref), axis=1)], axis=0)
    p_ref[...] = jnp.zeros(p_ref.shape, BF16)
    acc_ref[...] = jnp.zeros(acc_ref.shape, F32)

    def accumulate(unit, alpha):
        j = jnp.where(unit <= 0, i, unit - 1)
        pv = _dot(vt_ref[:, pl.ds(pl.multiple_of(j * tq, tq), tq)], p_ref[...])
        acc_ref[...] = alpha * acc_ref[...] + pv

    def probs(m):
        s = s_ref[...]
        m_new = jnp.maximum(m, jnp.max(s, axis=0, keepdims=True))
        p_ref[...] = jnp.exp(s - m_new).astype(BF16)
        return m_new, jnp.exp(m - m_new)

    def scores(j, masked):
        s = _dot(k_ref[pl.ds(pl.multiple_of(j * tq, tq), tq), :], qa_ref[...])
        if masked:
            causal = (lax.broadcasted_iota(jnp.int32, (tq, 1), 0)
                      <= (lax.broadcasted_iota(jnp.int32, (1, rows), 1) & (tq - 1)))
            s = jnp.where(causal, s, NEG)
        s_ref[...] = s

    def body(k, carry):
        m, alpha = carry
        accumulate(k - 1, alpha)
        m, alpha = probs(m)
        scores(k, False)
        return m, alpha

    def steps(first, count, carry):
        for k in range(count):
            carry = body(first + k, carry)
        return carry

    scores(i, True)
    carry = (jnp.full((1, rows), -3e38, F32), jnp.ones((1, rows), F32))
    trips = i >> int(math.log2(_SLC_UNROLL))
    carry = lax.fori_loop(0, trips, lambda k, c: steps(k * _SLC_UNROLL, _SLC_UNROLL, c), carry)
    done = trips * _SLC_UNROLL
    size = _SLC_UNROLL // 2
    while size >= 1:
        carry = lax.cond((i & size) != 0, functools.partial(steps, done, size), lambda c: c, carry)
        done = done + (i & size)
        size //= 2
    m, alpha = carry
    accumulate(i - 1, alpha)
    m, alpha = probs(m)
    accumulate(i, alpha)

    gate = jnp.concatenate([_gate_row(gt_ref, _BR_SLC, hd) for hd in range(HEADS_PER_GROUP)], axis=1)
    out_t = acc_ref[0:HEAD_DIM, :] * (gate / acc_ref[HEAD_DIM:HEAD_DIM + 1, :])
    for pair in range(HEADS_PER_GROUP // 2):
        for c in range(tq // LANES):
            col = 2 * pair * tq + c * LANES
            both = jnp.concatenate([out_t[:, col:col + LANES], out_t[:, col + tq:col + tq + LANES]], axis=0)
            both = both + prev_ref[pair * LANES:(pair + 1) * LANES, c * LANES:(c + 1) * LANES]
            o_ref[c * LANES:(c + 1) * LANES, pair * LANES:(pair + 1) * LANES] = both.T.astype(BF16)


def _slc(qt, sbt, k_aug, vt_aug, gates_t, prev, *, bsz, seq, tq=512):
    nq = seq // tq
    gw = HEADS_PER_GROUP * HEAD_DIM
    rows = HEADS_PER_GROUP * tq
    tmap = lambda b, g, i: (b, g, i)
    return pl.pallas_call(
        functools.partial(_slc_kernel, tq=tq),
        out_shape=jax.ShapeDtypeStruct((bsz * seq, A_WIDTH), BF16),
        grid=(bsz, N_GROUPS, nq),
        in_specs=[
            pl.BlockSpec((None, gw, tq), tmap),
            pl.BlockSpec((None, None, LANES, tq), lambda b, g, i: (b, g, 0, i)),
            pl.BlockSpec((None, None, seq, 2 * LANES), lambda b, g, i: (b, g, 0, 0)),
            pl.BlockSpec((None, None, None, V_ROWS, seq), lambda b, g, i: (b, 0, g, 0, 0)),
            pl.BlockSpec((GATE_ROWS, tq), lambda b, g, i: (g, b * nq + i)),
            pl.BlockSpec((None, gw, tq), tmap),
        ],
        out_specs=pl.BlockSpec((tq, gw), lambda b, g, i: (b * nq + i, g)),
        scratch_shapes=[pltpu.VMEM((2 * LANES, rows), BF16), pltpu.VMEM((tq, rows), F32),
                        pltpu.VMEM((tq, rows), BF16), pltpu.VMEM((V_ROWS, rows), F32)],
        compiler_params=_cparams(("parallel", "parallel", "parallel")),
        name="slc_attn",
    )(qt, sbt, k_aug, vt_aug, gates_t, prev)


def _win_kernel(qt_ref, k_ref, vt_ref, band_ref, gt_ref, prev_ref, ot_ref, *, tq):
    i = pl.program_id(1)
    span = WINDOW + tq
    start = pl.multiple_of(jnp.maximum(i - WINDOW // tq, 0) * tq, tq)
    k = [k_ref[g, pl.ds(start, span), :] for g in range(N_GROUPS)]
    vt = [vt_ref[g, :, pl.ds(start, span)] for g in range(N_GROUPS)]
    heads = _head_columns(qt_ref)

    def scores(u):
        return _dot(k[u // HEADS_PER_GROUP], heads[u]) + band_ref[...]

    def probs(u, s):
        return jnp.exp(s - jnp.max(s, axis=0, keepdims=True)).astype(BF16)

    def finish(u, p):
        g, hd = divmod(u, HEADS_PER_GROUP)
        acc = _dot(vt[g], p)
        rows = slice(u * HEAD_DIM, (u + 1) * HEAD_DIM)
        scale = _gate_row(gt_ref, _BR_WIN, hd, g) / acc[HEAD_DIM:HEAD_DIM + 1, :]
        ot_ref[rows, :] = prev_ref[rows, :] + acc[:HEAD_DIM, :] * scale

    _software_pipeline(N_HEADS, scores, probs, finish)


def _window_band(tq):
    edge = WINDOW // tq
    v = np.arange(edge + 1)[:, None, None]
    kpos = np.maximum(v - edge, 0) * tq + np.arange(WINDOW + tq)[None, :, None]
    t = v * tq + np.arange(tq)[None, None, :]
    ok = (t - kpos >= 0) & (t - kpos < WINDOW)
    return np.where(ok, 0.0, NEG).astype(np.float32)


def _win(qt, k, vt, gates_t, prev, *, bsz, seq, tq=256):
    nq = seq // tq
    band = _window_band(tq)
    last = band.shape[0] - 1
    tile = lambda b, i: (b, 0, i)
    return pl.pallas_call(
        functools.partial(_win_kernel, tq=tq),
        out_shape=jax.ShapeDtypeStruct((bsz, A_WIDTH, seq), F32),
        grid=(bsz, nq),
        in_specs=[
            pl.BlockSpec((None, A_WIDTH, tq), tile),
            pl.BlockSpec((None, N_GROUPS, seq, LANES), lambda b, i: (b, 0, 0, 0)),
            pl.BlockSpec((None, None, N_GROUPS, V_ROWS, seq), lambda b, i: (b, 1, 0, 0, 0)),
            pl.BlockSpec((None,) + band.shape[1:], lambda b, i: (jnp.minimum(i, last), 0, 0)),
            pl.BlockSpec((N_GROUPS * GATE_ROWS, tq), lambda b, i: (0, b * nq + i)),
            pl.BlockSpec((None, A_WIDTH, tq), tile),
        ],
        out_specs=pl.BlockSpec((None, A_WIDTH, tq), tile),
        compiler_params=_cparams(("parallel", "parallel")),
        name="win_attn",
    )(qt, k, vt, band, gates_t, prev)


def _outproj_kernel(x_ref, oa_ref, ob_ref, w_ref, o_ref):
    y = _dot(oa_ref[...], w_ref[0:A_WIDTH, :]) + _dot(ob_ref[...], w_ref[A_WIDTH:, :])
    o_ref[...] = x_ref[...] + y


def _outproj(x, oa, ob, w, *, tm=1024):
    n, d = x.shape
    row = lambda i: (i, 0)
    return pl.pallas_call(
        _outproj_kernel,
        out_shape=jax.ShapeDtypeStruct((n, d), F32),
        grid=(n // tm,),
        in_specs=[
            pl.BlockSpec((tm, d), row),
            pl.BlockSpec((tm, A_WIDTH), row),
            pl.BlockSpec((tm, A_WIDTH), row),
            pl.BlockSpec(w.shape, lambda i: (0, 0)),
        ],
        out_specs=pl.BlockSpec((tm, d), row),
        compiler_params=_cparams(("parallel",)),
        name="outproj",
    )(x, oa, ob, w)


def _pool_kernel(x_ref, halo_ref, g_ref, w_ref, sc_ref, o_ref, ext_ref, tmp_ref, *, tm, tiles_per_batch):
    i = pl.program_id(0)
    first = i % tiles_per_batch == 0
    x = x_ref[...]
    h = _rms(x, g_ref[...])
    hh = _rms(halo_ref[...], g_ref[...])
    pad, body = _POOL_PAD, _POOL_PAD + POOL_HALO
    ext_ref[0:pad, :] = jnp.zeros((pad, h.shape[1]), F32)
    ext_ref[pad:body, :] = jnp.where(first, 0.0, hh)
    ext_ref[body:body + tm, :] = h
    gw = h.shape[1] // len(POOL_WINDOWS)
    for slot in range(2):
        tmp_ref[slot, 0:pad, :] = jnp.zeros((pad, gw), F32)
    t = (i % tiles_per_batch) * tm + lax.broadcasted_iota(jnp.int32, (tm, 1), 0)
    span = POOL_HALO + tm
    for gi, win in enumerate(POOL_WINDOWS):
        c0 = gi * gw
        hg = h[:, c0:c0 + gw]
        read = lambda start, c0=c0: ext_ref[pl.ds(start, span), c0:c0 + gw]
        shift, slot = 1, 0
        while shift < win:
            tmp_ref[slot, pad:pad + span, :] = read(pad) + read(pad - shift)
            read = lambda start, slot=slot: tmp_ref[slot, pl.ds(start, span), :]
            shift, slot = 2 * shift, 1 - slot
        tot = read(pad)[POOL_HALO:, :]
        cnt = jnp.minimum(t + 1, win).astype(F32)
        y = _dot((tot / cnt - hg).astype(BF16), w_ref[gi])
        o_ref[:, c0:c0 + gw] = x[:, c0:c0 + gw] + y * sc_ref[:, c0:c0 + gw]


def _pool(x, g, w, scale, *, seq, tm=512):
    n, d = x.shape
    hb = tm // POOL_HALO
    return pl.pallas_call(
        functools.partial(_pool_kernel, tm=tm, tiles_per_batch=seq // tm),
        out_shape=jax.ShapeDtypeStruct((n, d), F32),
        grid=(n // tm,),
        in_specs=[
            pl.BlockSpec((tm, d), lambda i: (i, 0)),
            pl.BlockSpec((POOL_HALO, d), lambda i: (jnp.maximum(i * hb - 1, 0), 0)),
            pl.BlockSpec((1, d), lambda i: (0, 0)),
            pl.BlockSpec(w.shape, lambda i: (0, 0, 0)),
            pl.BlockSpec((1, d), lambda i: (0, 0)),
        ],
        out_specs=pl.BlockSpec((tm, d), lambda i: (i, 0)),
        scratch_shapes=[pltpu.VMEM((_POOL_PAD + POOL_HALO + tm, d), F32),
                        pltpu.VMEM((2, _POOL_PAD + POOL_HALO + tm, d // len(POOL_WINDOWS)), F32)],
        compiler_params=_cparams(("parallel",)),
        name="pool_mixer",
    )(x, x, g.reshape(1, d), w, scale.reshape(1, d))


def _pack_w_in(w_in):
    o = 0
    q = w_in[:, o:o + A_WIDTH]; o += A_WIDTH
    kc, vc, ks, vs, kw, vw = [w_in[:, o + k * KV_WIDTH:o + (k + 1) * KV_WIDTH] for k in range(6)]
    o += 6 * KV_WIDTH
    gates = w_in[:, o:o + N_BRANCH * N_HEADS]; o += N_BRANCH * N_HEADS
    u, gb, gc = [w_in[:, o + k * A_WIDTH:o + (k + 1) * A_WIDTH] for k in range(3)]
    gates = gates.reshape(-1, N_GROUPS, HEADS_PER_GROUP, N_BRANCH).transpose(0, 1, 3, 2)
    gates = gates.reshape(-1, N_GROUPS, N_BRANCH * HEADS_PER_GROUP)
    gates = jnp.pad(gates, ((0, 0), (0, 0), (0, LANES - N_BRANCH * HEADS_PER_GROUP))).reshape(-1, GATE_WIDTH)
    return jnp.concatenate([q, ks, kw, vs, vw, kc, vc, gates, u, gb, gc], axis=1).astype(BF16)


def _group_mean_matrix(width):
    r = np.arange(width) // HEAD_DIM
    return jnp.asarray(np.where(r[:, None] == r[None, :], 1.0 / HEAD_DIM, 0.0), BF16)


def _rope_freq_row():
    lane = jnp.arange(LANES) % HEAD_DIM
    freqs = ROPE_THETA ** (-jnp.arange(0, ROPE_DIM, 2, dtype=F32) / ROPE_DIM)
    return jnp.where(lane < ROPE_DIM, freqs[lane % ROPE_HALF], 0.0).astype(F32).reshape(1, LANES)


def _hybrid_mixer(x, positions, g, w_in, q_norm, k_norm, cmp_pos, cmp_w1, cmp_w2, conv_w, w_out, *, bsz, seq):
    n = bsz * seq
    ncp = seq // CMP_STRIDE
    freq = _rope_freq_row()
    pq = _group_mean_matrix(A_WIDTH)
    pk = _group_mean_matrix(KV_WIDTH)
    qnw = jnp.tile(q_norm, N_HEADS).reshape(1, A_WIDTH)
    knw = jnp.tile(k_norm, (1, N_GROUPS))

    q_t, k_slc, k_win, v_t, kc_raw, vc_raw, gates, ob = _inproj(
        x, positions.reshape(n, 1), g, _pack_w_in(w_in), qnw, knw, conv_w, pq, pk, freq, seq=seq)

    eye_g = jnp.eye(N_GROUPS, dtype=F32)
    seg = CMP_STRIDE * KV_WIDTH

    def seg_weights(w1):
        w1r = w1.reshape(CMP_BLOCK, HEAD_DIM, CMP_HIDDEN)
        halves = [jnp.einsum('jdh,gk->jgdkh', w1r[a:a + CMP_STRIDE], eye_g).reshape(seg, N_GROUPS * CMP_HIDDEN)
                  for a in (0, CMP_STRIDE)]
        return jnp.stack(halves).astype(BF16)

    def seg_pos(pe):
        return [jnp.broadcast_to(pe[a:a + CMP_STRIDE, None, :], (CMP_STRIDE, N_GROUPS, HEAD_DIM)).reshape(seg)
                for a in (0, CMP_STRIDE)]

    pe = jnp.stack(seg_pos(cmp_pos[0]) + seg_pos(cmp_pos[1]))
    w2 = jnp.stack([jnp.einsum('hd,gk->ghkd', cmp_w2[a], eye_g).reshape(N_GROUPS * CMP_HIDDEN, KV_WIDTH)
                    for a in range(2)]).astype(BF16)
    pos_c = jnp.concatenate([positions[:, CMP_BLOCK - 1::CMP_STRIDE], positions[:, -1:]], axis=1).reshape(bsz, ncp, 1)
    kc_g, vc_t = _compress(kc_raw, vc_raw, pos_c, pe, seg_weights(cmp_w1[0]), seg_weights(cmp_w1[1]), w2[0], w2[1].T,
                           knw, pk, freq, seq=seq)

    gates_t = gates.reshape(n, N_GROUPS, LANES)[:, :, :GATE_ROWS].transpose(1, 2, 0).reshape(N_GROUPS * GATE_ROWS, n)
    cstart = np.arange(ncp) * CMP_STRIDE
    sstart = np.arange(LANES) * SEL_BLOCK
    ovt = jnp.asarray((cstart[None, :] < sstart[:, None] + SEL_BLOCK)
                      & (cstart[None, :] + CMP_BLOCK > sstart[:, None]), BF16)
    o_t, sel_bias = _cmp_select(q_t, kc_g, vc_t, ovt, gates_t, bsz=bsz, seq=seq)

    o_t = _win(q_t, k_win, v_t, gates_t, o_t, bsz=bsz, seq=seq)
    o_a = _slc(q_t, sel_bias, k_slc, v_t, gates_t, o_t, bsz=bsz, seq=seq)

    return _outproj(x, o_a, ob, w_out.astype(BF16))


def kernel(x, positions, ffn_norm, ffn_w_gate, ffn_w_up, ffn_w_down, mix_norm, hyb_w_in, hyb_q_norm, hyb_k_norm,
           hyb_cmp_pos, hyb_cmp_w1, hyb_cmp_w2, hyb_conv_w, hyb_w_out, pool_w, pool_scale):
    bsz, seq, d = x.shape
    depth = ffn_norm.shape[0]
    assert seq % 2048 == 0 and seq // SEL_BLOCK <= LANES and seq // SEL_BLOCK >= SEL_TOPK
    wg, wu, wd = (w.astype(BF16) for w in (ffn_w_gate, ffn_w_up, ffn_w_down))
    xf = x.reshape(bsz * seq, d)
    for layer in range(depth):
        xf = _ffn(xf, ffn_norm[layer, 0], wg, wu, wd, layer, 0)
        i = layer // 2
        if layer % 2 == 0:
            xf = _hybrid_mixer(xf, positions, mix_norm[layer], hyb_w_in[i], hyb_q_norm[i], hyb_k_norm[i],
                               hyb_cmp_pos[i], hyb_cmp_w1[i], hyb_cmp_w2[i], hyb_conv_w[i], hyb_w_out[i],
                               bsz=bsz, seq=seq)
        else:
            xf = _pool(xf, mix_norm[layer], pool_w[i].astype(BF16), pool_scale[i], seq=seq)
        xf = _ffn(xf, ffn_norm[layer, 1], wg, wu, wd, layer, 1)
    return xf.reshape(bsz, seq, d)
```

```python
import functools
import math

import jax
import jax.numpy as jnp
from jax import lax
from jax.experimental import pallas as pl
from jax.experimental.pallas import tpu as pltpu

F32 = jnp.float32
BF16 = jnp.bfloat16

HEAD_DIM = 64
N_GROUPS = 2
HEADS_PER_GROUP = 4
N_HEADS = N_GROUPS * HEADS_PER_GROUP
A_WIDTH = N_HEADS * HEAD_DIM
KV_WIDTH = N_GROUPS * HEAD_DIM
ROPE_DIM = HEAD_DIM // 4
ROPE_HALF = ROPE_DIM // 2
ROPE_THETA = 500000.0
CMP_BLOCK = 32
CMP_STRIDE = 16
CMP_HIDDEN = 2 * HEAD_DIM
SEL_BLOCK = 64
SEL_TOPK = 16
WINDOW = 512
N_BRANCH = 3
_BR_CMP, _BR_SLC, _BR_WIN = 0, 1, 2
CONV_WIDTH = 3
POOL_WINDOWS = (2, 4, 8, 16)
POOL_HALO = 16
_POOL_PAD = 8
EPS = 1e-6
NEG = -1e30

LANES = 128
VMEM_LIMIT_BYTES = 56 * 1024 * 1024


def _cparams(sem):
    return pltpu.CompilerParams(dimension_semantics=sem, vmem_limit_bytes=VMEM_LIMIT_BYTES)


def _dot(a, b):
    return jnp.dot(a, b, preferred_element_type=F32)


def _dot_split(a, b):
    hi = a.astype(BF16)
    lo = (a - hi.astype(F32)).astype(BF16)
    return _dot(hi, b) + _dot(lo, b)


def _rms(x, g):
    ms = jnp.mean(x * x, axis=-1, keepdims=True)
    return x * lax.rsqrt(ms + EPS) * g


def _group_rms(x, g, pmat):
    ms = _dot_split(x * x, pmat)
    return x * lax.rsqrt(ms + EPS) * g


def _rope_tables(pos_col, freq_row):
    ang = pos_col * freq_row
    lane = lax.broadcasted_iota(jnp.int32, (1, LANES), 1) & (HEAD_DIM - 1)
    sign = jnp.where(lane < ROPE_HALF, -1.0, 1.0).astype(F32)
    return jnp.cos(ang), jnp.sin(ang) * sign


def _rope(x, cos_t, sin_t):
    w = x.shape[1]
    reps = w // LANES
    if reps > 1:
        cos_t = jnp.concatenate([cos_t] * reps, axis=1)
        sin_t = jnp.concatenate([sin_t] * reps, axis=1)
    lane = lax.broadcasted_iota(jnp.int32, (1, w), 1) & (HEAD_DIM - 1)
    partner = jnp.where(lane < ROPE_HALF,
                        pltpu.roll(x, w - ROPE_HALF, axis=1),
                        pltpu.roll(x, ROPE_HALF, axis=1))
    return x * cos_t + partner * sin_t


def _ffn_kernel(x_ref, xnext_ref, g_ref, wg_ref, wu_ref, wd_ref, o_ref, h_ref, acc_ref, *, nj):
    i, j = pl.program_id(0), pl.program_id(1)
    slot = i % 2

    @pl.when((i == 0) & (j == 0))
    def _():
        h_ref[0] = _rms(x_ref[...], g_ref[...]).astype(BF16)

    def step(first, last):
        h = h_ref[slot]
        a = _dot(h, wg_ref[...])
        b = _dot(h, wu_ref[...])
        act = a * (1.0 / (1.0 + jnp.exp(-a))) * b
        y = _dot(act.astype(BF16), wd_ref[...])
        if not first:
            y = acc_ref[...] + y
        if last:
            h_ref[1 - slot] = _rms(xnext_ref[...], g_ref[...]).astype(BF16)
            o_ref[...] = x_ref[...] + 0.5 * y
        else:
            acc_ref[...] = y

    for jj in range(nj):
        pl.when(j == jj)(functools.partial(step, jj == 0, jj == nj - 1))


def _ffn(x, g, wg, wu, wd, layer, half, *, tm=512, tf=2816):
    n, d = x.shape
    dff = wg.shape[-1]
    ni, nj = n // tm, dff // tf
    wmode = dict(pipeline_mode=pl.Buffered(1)) if nj == 1 else {}
    return pl.pallas_call(
        functools.partial(_ffn_kernel, nj=nj),
        out_shape=jax.ShapeDtypeStruct((n, d), F32),
        grid=(ni, nj),
        in_specs=[
            pl.BlockSpec((tm, d), lambda i, j: (i, 0)),
            pl.BlockSpec((tm, d), lambda i, j: (jnp.minimum(i + 1, ni - 1), 0)),
            pl.BlockSpec((1, d), lambda i, j: (0, 0)),
            pl.BlockSpec((None, None, d, tf), lambda i, j: (layer, half, 0, j), **wmode),
            pl.BlockSpec((None, None, d, tf), lambda i, j: (layer, half, 0, j), **wmode),
            pl.BlockSpec((None, None, tf, d), lambda i, j: (layer, half, j, 0), **wmode),
        ],
        out_specs=pl.BlockSpec((tm, d), lambda i, j: (i, 0)),
        scratch_shapes=[pltpu.VMEM((2, tm, d), BF16), pltpu.VMEM((tm, d), F32)],
        compiler_params=_cparams(("arbitrary", "arbitrary")),
        name="ffn",
    )(x, x, g.reshape(1, d), wg, wu, wd)


GATE_WIDTH = N_GROUPS * LANES
GATE_ROWS = 16
_C_Q = 0
_C_K = _C_Q + A_WIDTH
_C_V = _C_K + 2 * KV_WIDTH
_C_C = _C_V + 2 * KV_WIDTH
_C_GATE = _C_C + 2 * KV_WIDTH
_C_U = _C_GATE + GATE_WIDTH
_C_GB = _C_U + A_WIDTH
_C_GC = _C_GB + A_WIDTH
CONV_HALO = 8
V_ROWS = HEAD_DIM + 16


def _inproj_kernel(x_ref, pos_ref, g_ref, w_ref, qnw_ref, knw_ref, cw_ref, pq_ref, pk_ref, freq_ref,
                   qt_ref, kslc_ref, kwin_ref, vt_ref, kc_ref, vc_ref, gates_ref, ob_ref,
                   vext_ref, *, tm, tiles_per_batch):
    i = pl.program_id(0)
    chunks = range(tm // LANES)

    def transposed(a, c):
        return jnp.concatenate([a[r * LANES:(r + 1) * LANES, c * LANES:(c + 1) * LANES].T for r in chunks], axis=1)

    def per_group(k):
        low = lax.broadcasted_iota(jnp.int32, (1, LANES), 1) < HEAD_DIM
        return [jnp.where(low, k, 0.0), jnp.where(low, pltpu.roll(k, HEAD_DIM, axis=1), 0.0)]

    @pl.when(i % tiles_per_batch == 0)
    def _():
        vext_ref[0:CONV_HALO, :] = jnp.zeros((CONV_HALO, A_WIDTH), F32)

    h = _rms(x_ref[...], g_ref[...]).astype(BF16)

    def proj(c0, width):
        return _dot(h, w_ref[:, c0:c0 + width])

    q_raw = proj(_C_Q, A_WIDTH)
    k2 = proj(_C_K, 2 * KV_WIDTH)
    cos_t, sin_t = _rope_tables(pos_ref[...].astype(F32), freq_ref[...])
    v = proj(_C_GC, A_WIDTH) * proj(_C_U, A_WIDTH)
    gate_b = proj(_C_GB, A_WIDTH)

    q = _group_rms(q_raw, qnw_ref[...], pq_ref[...])
    q = _rope(q, cos_t, sin_t) * (HEAD_DIM ** -0.5)
    for c in range(A_WIDTH // LANES):
        qt_ref[c * LANES:(c + 1) * LANES, :] = transposed(q, c).astype(BF16)

    tok = (i % tiles_per_batch) * tm + lax.broadcasted_iota(jnp.int32, (tm, 1), 0)
    blk_onehot = (lax.broadcasted_iota(jnp.int32, (1, LANES), 1) == (tok >> int(math.log2(SEL_BLOCK))))
    blk_onehot = jnp.where(blk_onehot, 1.0, 0.0).astype(BF16)
    ks = _rope(_group_rms(k2[:, :KV_WIDTH], knw_ref[1:2, :], pk_ref[...]), cos_t, sin_t)
    for g, kg in enumerate(per_group(ks)):
        kslc_ref[g] = jnp.concatenate([blk_onehot, kg.astype(BF16)], axis=1)
    kw = _rope(_group_rms(k2[:, KV_WIDTH:], knw_ref[2:3, :], pk_ref[...]), cos_t, sin_t)
    for g, kg in enumerate(per_group(kw)):
        kwin_ref[g] = kg.astype(BF16)

    v2 = proj(_C_V, 2 * KV_WIDTH)
    ones_rows = jnp.where(lax.broadcasted_iota(jnp.int32, (V_ROWS - HEAD_DIM, 1), 0) == 0, 1.0, 0.0)
    ones_rows = jnp.broadcast_to(ones_rows, (V_ROWS - HEAD_DIM, tm)).astype(BF16)
    for branch in range(2):
        vt = transposed(v2, branch).astype(BF16)
        for g in range(N_GROUPS):
            vt_ref[branch, g, 0:HEAD_DIM, :] = vt[g * HEAD_DIM:(g + 1) * HEAD_DIM, :]
            vt_ref[branch, g, HEAD_DIM:, :] = ones_rows
    c2 = proj(_C_C, 2 * KV_WIDTH)
    kc_ref[...] = c2[:, :KV_WIDTH]
    vc_ref[...] = c2[:, KV_WIDTH:]
    gate_logits = proj(_C_GATE, GATE_WIDTH)

    vext_ref[CONV_HALO:CONV_HALO + tm, :] = v
    y = (cw_ref[2:3, :] * v
         + cw_ref[1:2, :] * vext_ref[pl.ds(CONV_HALO - 1, tm), :]
         + cw_ref[0:1, :] * vext_ref[pl.ds(CONV_HALO - 2, tm), :])
    ob_ref[...] = (gate_b * y).astype(BF16)
    vext_ref[0:CONV_HALO, :] = vext_ref[tm:tm + CONV_HALO, :]
    gates_ref[...] = 1.0 / (1.0 + jnp.exp(-gate_logits))


def _inproj(x, pos_col, g, w, qnw, knw, cw, pq, pk, freq, *, seq, tm=512):
    n, d = x.shape
    bsz, tpb = n // seq, seq // tm
    row = lambda i: (i, 0)
    fixed = lambda i: (0, 0)
    widths = [(KV_WIDTH, F32), (KV_WIDTH, F32), (GATE_WIDTH, F32), (A_WIDTH, BF16)]
    out_shape = [jax.ShapeDtypeStruct((bsz, A_WIDTH, seq), BF16),
                 jax.ShapeDtypeStruct((bsz, N_GROUPS, seq, 2 * LANES), BF16),
                 jax.ShapeDtypeStruct((bsz, N_GROUPS, seq, LANES), BF16),
                 jax.ShapeDtypeStruct((bsz, 2, N_GROUPS, V_ROWS, seq), BF16)]
    out_specs = [pl.BlockSpec((None, A_WIDTH, tm), lambda i: (i // tpb, 0, i % tpb)),
                 pl.BlockSpec((None, N_GROUPS, tm, 2 * LANES), lambda i: (i // tpb, 0, i % tpb, 0)),
                 pl.BlockSpec((None, N_GROUPS, tm, LANES), lambda i: (i // tpb, 0, i % tpb, 0)),
                 pl.BlockSpec((None, 2, N_GROUPS, V_ROWS, tm), lambda i: (i // tpb, 0, 0, 0, i % tpb))]
    return pl.pallas_call(
        functools.partial(_inproj_kernel, tm=tm, tiles_per_batch=tpb),
        out_shape=out_shape + [jax.ShapeDtypeStruct((n, wd), dt) for wd, dt in widths],
        grid=(n // tm,),
        in_specs=[
            pl.BlockSpec((tm, d), row),
            pl.BlockSpec((tm, 1), row),
            pl.BlockSpec((1, d), fixed),
            pl.BlockSpec(w.shape, fixed),
            pl.BlockSpec(qnw.shape, fixed),
            pl.BlockSpec(knw.shape, fixed),
            pl.BlockSpec(cw.shape, fixed),
            pl.BlockSpec(pq.shape, fixed),
            pl.BlockSpec(pk.shape, fixed),
            pl.BlockSpec(freq.shape, fixed),
        ],
        out_specs=out_specs + [pl.BlockSpec((tm, wd), row) for wd, _ in widths],
        scratch_shapes=[pltpu.VMEM((tm + CONV_HALO, A_WIDTH), F32)],
        compiler_params=_cparams(("arbitrary",)),
        name="inproj",
    )(x, pos_col, g.reshape(1, d), w, qnw, knw, cw, pq, pk, freq)


def _gelu_tanh(x):
    return 0.5 * x * (1.0 + jnp.tanh(math.sqrt(2.0 / math.pi) * (x + 0.044715 * (x * x * x))))


def _compress_kernel(xk_ref, xv_ref, pos_ref, pe_ref, wk_ref, wv_ref, w2_ref, w2vt_ref, knw_ref, pk_ref, freq_ref,
                     kc_ref, vct_ref, *, ncp):
    def hidden(x_ref, pe_row, w_ref):
        x = jnp.concatenate([x_ref[pl.ds(j, ncp, stride=CMP_STRIDE), :] for j in range(CMP_STRIDE)], axis=1)
        a = _dot((x + pe_ref[pe_row:pe_row + 1, :]).astype(BF16), w_ref[0])
        b = _dot((x + pe_ref[pe_row + 1:pe_row + 2, :]).astype(BF16), w_ref[1])
        hid = a + pltpu.roll(b, ncp - 1, axis=0)
        return _gelu_tanh(hid)

    low = lax.broadcasted_iota(jnp.int32, (1, LANES), 1) < HEAD_DIM
    kc = _dot(hidden(xk_ref, 0, wk_ref).astype(BF16), w2_ref[...])
    kc = _group_rms(kc, knw_ref[0:1, :], pk_ref[...])
    cos_t, sin_t = _rope_tables(pos_ref[...].astype(F32), freq_ref[...])
    kc = _rope(kc, cos_t, sin_t)
    kc_ref[0] = jnp.where(low, kc, 0.0).astype(BF16)
    kc_ref[1] = jnp.where(low, pltpu.roll(kc, HEAD_DIM, axis=1), 0.0).astype(BF16)
    hv = hidden(xv_ref, 2, wv_ref)
    hv_t = jnp.concatenate(
        [jnp.concatenate([hv[r * LANES:(r + 1) * LANES, c * LANES:(c + 1) * LANES].T for r in range(ncp // LANES)],
                         axis=1) for c in range(hv.shape[1] // LANES)], axis=0)
    vct = _dot(w2vt_ref[...], hv_t.astype(BF16))
    for g in range(N_GROUPS):
        vct_ref[g] = vct[g * HEAD_DIM:(g + 1) * HEAD_DIM, :].astype(BF16)


def _compress(xk, xv, pos_c, pe, wk, wv, w2k, w2vt, knw, pk, freq, *, seq):
    bsz, ncp = pos_c.shape[:2]
    bat = lambda b: (b, 0, 0)
    fix2 = lambda b: (0, 0)
    fix3 = lambda b: (0, 0, 0)
    return pl.pallas_call(
        functools.partial(_compress_kernel, ncp=ncp),
        out_shape=[jax.ShapeDtypeStruct((bsz, N_GROUPS, ncp, LANES), BF16),
                   jax.ShapeDtypeStruct((bsz, N_GROUPS, HEAD_DIM, ncp), BF16)],
        grid=(bsz,),
        in_specs=[
            pl.BlockSpec((seq, KV_WIDTH), lambda b: (b, 0)),
            pl.BlockSpec((seq, KV_WIDTH), lambda b: (b, 0)),
            pl.BlockSpec((None, ncp, 1), bat),
            pl.BlockSpec(pe.shape, fix2),
            pl.BlockSpec(wk.shape, fix3),
            pl.BlockSpec(wv.shape, fix3),
            pl.BlockSpec(w2k.shape, fix2),
            pl.BlockSpec(w2vt.shape, fix2),
            pl.BlockSpec(knw.shape, fix2),
            pl.BlockSpec(pk.shape, fix2),
            pl.BlockSpec(freq.shape, fix2),
        ],
        out_specs=[pl.BlockSpec((None, N_GROUPS, ncp, LANES), lambda b: (b, 0, 0, 0)),
                   pl.BlockSpec((None, N_GROUPS, HEAD_DIM, ncp), lambda b: (b, 0, 0, 0))],
        compiler_params=_cparams(("parallel",)),
        name="compress",
    )(xk, xv, pos_c, pe, wk, wv, w2k, w2vt, knw, pk, freq)


def _head_columns(qt_ref):
    cols = []
    for pair in range(qt_ref.shape[0] // LANES):
        both = qt_ref[pair * LANES:(pair + 1) * LANES, :]
        cols += [both, jnp.concatenate([both[HEAD_DIM:, :], both[:HEAD_DIM, :]], axis=0)]
    return cols


def _software_pipeline(n, scores, probs, finish, between=()):
    s, p = {}, {}
    between = list(between)
    for step in range(n + 2):
        if step >= 2:
            finish(step - 2, p.pop(step - 2))
        if 1 <= step <= n:
            p[step - 1] = probs(step - 1, s.pop(step - 1))
        if step < n:
            s[step] = scores(step)
        for extra in between[len(between) * step // (n + 2):len(between) * (step + 1) // (n + 2)]:
            extra()


def _gate_row(gt_ref, branch, hd, group=0):
    r = group * GATE_ROWS + branch * HEADS_PER_GROUP + hd
    return gt_ref[r:r + 1, :]


def _cmp_kernel(qt_ref, kc_ref, vct_ref, ovt_ref, kwin_ref, vtwin_ref, band_ref, gt_ref, ot_ref, sbt_ref, *, tq, ncp):
    i = pl.program_id(1)
    span = WINDOW + tq
    wstart = pl.multiple_of(jnp.maximum(i - WINDOW // tq, 0) * tq, tq)
    t = i * tq + lax.broadcasted_iota(jnp.int32, (1, tq), 1)
    any_valid = (t >= CMP_BLOCK - 1).astype(F32)
    heads = _head_columns(qt_ref)
    cmp_per_sel = SEL_BLOCK // CMP_STRIDE
    groups = range(N_GROUPS)

    def run(nrow):
        nblk = nrow // cmp_per_sel
        blk_end = lax.broadcasted_iota(jnp.int32, (nrow, 1), 0) * CMP_STRIDE + (CMP_BLOCK - 1)
        valid = blk_end <= t
        kc = [kc_ref[g, 0:nrow, :] for g in groups]
        vct = [vct_ref[g, :, 0:nrow] for g in groups]
        psum_parts = [[] for _ in groups]

        def scores(u):
            return jnp.where(valid, _dot(kc[u // HEADS_PER_GROUP], heads[u]), NEG)

        def probs(u, s):
            e = jnp.exp(s - jnp.max(s, axis=0, keepdims=True))
            return e * (any_valid / jnp.sum(e, axis=0, keepdims=True))

        def finish(u, p):
            g, hd = divmod(u, HEADS_PER_GROUP)
            o_t = _dot(vct[g], p.astype(BF16))
            ot_ref[u * HEAD_DIM:(u + 1) * HEAD_DIM, :] = o_t * _gate_row(gt_ref, _BR_CMP, hd, g)
            psum_parts[g].append(p)

        _software_pipeline(N_HEADS, scores, probs, finish)

        ovt = ovt_ref[0:nblk, 0:nrow]
        blk = lax.broadcasted_iota(jnp.int32, (nblk, 1), 0)
        sel_start = blk * SEL_BLOCK
        cur = (t >> int(math.log2(SEL_BLOCK))) << int(math.log2(SEL_BLOCK))
        imp = []
        for parts in psum_parts:
            psum = (parts[0] + parts[1]) + (parts[2] + parts[3])
            hi = psum.astype(BF16)
            lo = (psum - hi.astype(F32)).astype(BF16)
            x = jnp.where(sel_start <= t, _dot(ovt, hi) + _dot(ovt, lo), -1.0)
            imp.append(jnp.where(sel_start == cur, 1e4, jnp.where(sel_start == 0, 1e4, x)))

        blk_f = blk.astype(F32)
        bias = [jnp.full((nblk, tq), NEG, F32) for _ in groups]

        def topk_round():
            for g in groups:
                mx = jnp.max(imp[g], axis=0, keepdims=True)
                first = jnp.min(jnp.where(imp[g] == mx, blk_f, float(LANES)), axis=0, keepdims=True)
                pick = blk_f == first
                bias[g] = jnp.where(pick, 0.0, bias[g])
                imp[g] = jnp.where(pick, -3e38, imp[g])

        def win_scores(u):
            k = kwin_ref[u // HEADS_PER_GROUP, pl.ds(wstart, span), :]
            return _dot(k, heads[u]) + band_ref[...]

        def win_probs(u, s):
            return jnp.exp(s - jnp.max(s, axis=0, keepdims=True)).astype(BF16)

        def win_finish(u, p):
            g, hd = divmod(u, HEADS_PER_GROUP)
            acc = _dot(vtwin_ref[g, :, pl.ds(wstart, span)], p)
            rows = slice(u * HEAD_DIM, (u + 1) * HEAD_DIM)
            scale = _gate_row(gt_ref, _BR_WIN, hd, g) / acc[HEAD_DIM:HEAD_DIM + 1, :]
            ot_ref[rows, :] = ot_ref[rows, :] + acc[:HEAD_DIM, :] * scale

        _software_pipeline(N_HEADS, win_scores, win_probs, win_finish, between=[topk_round] * SEL_TOPK)
        for g in groups:
            sbt_ref[g, 0:nblk, :] = bias[g].astype(BF16)
            if nblk < LANES:
                sbt_ref[g, nblk:, :] = jnp.full((LANES - nblk, tq), NEG, BF16)

    sizes = [LANES * (v + 1) for v in range(ncp // LANES)]
    assert sizes[0] // cmp_per_sel >= SEL_TOPK
    tokens_per_variant = LANES * CMP_STRIDE
    lax.switch((i * tq + tq - 1) // tokens_per_variant, [functools.partial(run, nrow) for nrow in sizes])


def _window_band(tq):
    edge = WINDOW // tq
    v = jnp.arange(edge + 1)[:, None, None]
    kpos = jnp.maximum(v - edge, 0) * tq + jnp.arange(WINDOW + tq)[None, :, None]
    t = v * tq + jnp.arange(tq)[None, None, :]
    ok = (t - kpos >= 0) & (t - kpos < WINDOW)
    return jnp.where(ok, 0.0, NEG).astype(F32)


def _cmp_win_select(qt, kc, vct, ovt, k_win, vt, gates_t, *, bsz, seq, tq=256):
    ncp = kc.shape[2]
    nq = seq // tq
    band = _window_band(tq)
    last = band.shape[0] - 1
    whole = lambda b, i: (b, 0, 0, 0)
    return pl.pallas_call(
        functools.partial(_cmp_kernel, tq=tq, ncp=ncp),
        out_shape=[jax.ShapeDtypeStruct((bsz, A_WIDTH, seq), F32),
                   jax.ShapeDtypeStruct((bsz, N_GROUPS, LANES, seq), BF16)],
        grid=(bsz, nq),
        in_specs=[
            pl.BlockSpec((None, A_WIDTH, tq), lambda b, i: (b, 0, i)),
            pl.BlockSpec((None, N_GROUPS, ncp, LANES), whole),
            pl.BlockSpec((None, N_GROUPS, HEAD_DIM, ncp), whole),
            pl.BlockSpec(ovt.shape, lambda b, i: (0, 0)),
            pl.BlockSpec((None, N_GROUPS, seq, LANES), whole),
            pl.BlockSpec((None, None, N_GROUPS, V_ROWS, seq), lambda b, i: (b, 1, 0, 0, 0)),
            pl.BlockSpec((None,) + band.shape[1:], lambda b, i: (jnp.minimum(i, last), 0, 0)),
            pl.BlockSpec((N_GROUPS * GATE_ROWS, tq), lambda b, i: (0, b * nq + i)),
        ],
        out_specs=[pl.BlockSpec((None, A_WIDTH, tq), lambda b, i: (b, 0, i)),
                   pl.BlockSpec((None, N_GROUPS, LANES, tq), lambda b, i: (b, 0, 0, i))],
        compiler_params=_cparams(("parallel", "parallel")),
        name="cmp_win_select",
    )(qt, kc, vct, ovt, k_win, vt, band, gates_t)


_SLC_UNROLL = 2


def _slc_kernel(qt_ref, sbt_ref, k_ref, vt_ref, gt_ref, prev_ref, o_ref,
                qa_ref, s_ref, p_ref, acc_ref, *, tq):
    i = pl.program_id(2)
    rows = HEADS_PER_GROUP * tq
    qa_ref[...] = jnp.concatenate([jnp.concatenate([sbt_ref[...]] * HEADS_PER_GROUP, axis=1),
                                   jnp.concatenate(_head_columns(qt_ref), axis=1)], axis=0)
    p_ref[...] = jnp.zeros(p_ref.shape, BF16)
    acc_ref[...] = jnp.zeros(acc_ref.shape, F32)

    def accumulate(unit, alpha):
        j = jnp.where(unit <= 0, i, unit - 1)
        pv = _dot(vt_ref[:, pl.ds(pl.multiple_of(j * tq, tq), tq)], p_ref[...])
        acc_ref[...] = alpha * acc_ref[...] + pv

    def probs(m):
        s = s_ref[...]
        m_new = jnp.maximum(m, jnp.max(s, axis=0, keepdims=True))
        p_ref[...] = jnp.exp(s - m_new).astype(BF16)
        return m_new, jnp.exp(m - m_new)

    def scores(j, masked):
        s = _dot(k_ref[pl.ds(pl.multiple_of(j * tq, tq), tq), :], qa_ref[...])
        if masked:
            causal = (lax.broadcasted_iota(jnp.int32, (tq, 1), 0)
                      <= (lax.broadcasted_iota(jnp.int32, (1, rows), 1) & (tq - 1)))
            s = jnp.where(causal, s, NEG)
        s_ref[...] = s

    def body(k, carry):
        m, alpha = carry
        accumulate(k - 1, alpha)
        m, alpha = probs(m)
        scores(k, False)
        return m, alpha

    def steps(first, count, carry):
        for k in range(count):
            carry = body(first + k, carry)
        return carry

    scores(i, True)
    carry = (jnp.full((1, rows), -3e38, F32), jnp.ones((1, rows), F32))
    trips = i >> int(math.log2(_SLC_UNROLL))
    carry = lax.fori_loop(0, trips, lambda k, c: steps(k * _SLC_UNROLL, _SLC_UNROLL, c), carry)
    done = trips * _SLC_UNROLL
    size = _SLC_UNROLL // 2
    while size >= 1:
        carry = lax.cond((i & size) != 0, functools.partial(steps, done, size), lambda c: c, carry)
        done = done + (i & size)
        size //= 2
    m, alpha = carry
    accumulate(i - 1, alpha)
    m, alpha = probs(m)
    accumulate(i, alpha)

    gate = jnp.concatenate([_gate_row(gt_ref, _BR_SLC, hd) for hd in range(HEADS_PER_GROUP)], axis=1)
    out_t = acc_ref[0:HEAD_DIM, :] * (gate / acc_ref[HEAD_DIM:HEAD_DIM + 1, :])
    for pair in range(HEADS_PER_GROUP // 2):
        for c in range(tq // LANES):
            col = 2 * pair * tq + c * LANES
            both = jnp.concatenate([out_t[:, col:col + LANES], out_t[:, col + tq:col + tq + LANES]], axis=0)
            both = both + prev_ref[pair * LANES:(pair + 1) * LANES, c * LANES:(c + 1) * LANES]
            o_ref[c * LANES:(c + 1) * LANES, pair * LANES:(pair + 1) * LANES] = both.T.astype(BF16)


def _slc(qt, sbt, k_aug, vt_aug, gates_t, prev, *, bsz, seq, tq=512):
    nq = seq // tq
    gw = HEADS_PER_GROUP * HEAD_DIM
    rows = HEADS_PER_GROUP * tq
    tmap = lambda b, g, i: (b, g, i)
    return pl.pallas_call(
        functools.partial(_slc_kernel, tq=tq),
        out_shape=jax.ShapeDtypeStruct((bsz * seq, A_WIDTH), BF16),
        grid=(bsz, N_GROUPS, nq),
        in_specs=[
            pl.BlockSpec((None, gw, tq), tmap),
            pl.BlockSpec((None, None, LANES, tq), lambda b, g, i: (b, g, 0, i)),
            pl.BlockSpec((None, None, seq, 2 * LANES), lambda b, g, i: (b, g, 0, 0)),
            pl.BlockSpec((None, None, None, V_ROWS, seq), lambda b, g, i: (b, 0, g, 0, 0)),
            pl.BlockSpec((GATE_ROWS, tq), lambda b, g, i: (g, b * nq + i)),
            pl.BlockSpec((None, gw, tq), tmap),
        ],
        out_specs=pl.BlockSpec((tq, gw), lambda b, g, i: (b * nq + i, g)),
        scratch_shapes=[pltpu.VMEM((2 * LANES, rows), BF16), pltpu.VMEM((tq, rows), F32),
                        pltpu.VMEM((tq, rows), BF16), pltpu.VMEM((V_ROWS, rows), F32)],
        compiler_params=_cparams(("parallel", "parallel", "parallel")),
        name="slc_attn",
    )(qt, sbt, k_aug, vt_aug, gates_t, prev)


def _outproj_kernel(x_ref, oa_ref, ob_ref, w_ref, o_ref):
    y = _dot(oa_ref[...], w_ref[0:A_WIDTH, :]) + _dot(ob_ref[...], w_ref[A_WIDTH:, :])
    o_ref[...] = x_ref[...] + y


def _outproj(x, oa, ob, w, *, tm=1024):
    n, d = x.shape
    row = lambda i: (i, 0)
    return pl.pallas_call(
        _outproj_kernel,
        out_shape=jax.ShapeDtypeStruct((n, d), F32),
        grid=(n // tm,),
        in_specs=[
            pl.BlockSpec((tm, d), row),
            pl.BlockSpec((tm, A_WIDTH), row),
            pl.BlockSpec((tm, A_WIDTH), row),
            pl.BlockSpec(w.shape, lambda i: (0, 0)),
        ],
        out_specs=pl.BlockSpec((tm, d), row),
        compiler_params=_cparams(("parallel",)),
        name="outproj",
    )(x, oa, ob, w)


def _pool_kernel(x_ref, halo_ref, g_ref, w_ref, sc_ref, o_ref, ext_ref, tmp_ref, *, tm, tiles_per_batch):
    i = pl.program_id(0)
    first = i % tiles_per_batch == 0
    x = x_ref[...]
    h = _rms(x, g_ref[...])
    hh = _rms(halo_ref[...], g_ref[...])
    pad, body = _POOL_PAD, _POOL_PAD + POOL_HALO
    ext_ref[0:pad, :] = jnp.zeros((pad, h.shape[1]), F32)
    ext_ref[pad:body, :] = jnp.where(first, 0.0, hh)
    ext_ref[body:body + tm, :] = h
    gw = h.shape[1] // len(POOL_WINDOWS)
    for slot in range(2):
        tmp_ref[slot, 0:pad, :] = jnp.zeros((pad, gw), F32)
    t = (i % tiles_per_batch) * tm + lax.broadcasted_iota(jnp.int32, (tm, 1), 0)
    span = POOL_HALO + tm
    for gi, win in enumerate(POOL_WINDOWS):
        c0 = gi * gw
        hg = h[:, c0:c0 + gw]
        read = lambda start, c0=c0: ext_ref[pl.ds(start, span), c0:c0 + gw]
        shift, slot = 1, 0
        while shift < win:
            tmp_ref[slot, pad:pad + span, :] = read(pad) + read(pad - shift)
            read = lambda start, slot=slot: tmp_ref[slot, pl.ds(start, span), :]
            shift, slot = 2 * shift, 1 - slot
        tot = read(pad)[POOL_HALO:, :]
        cnt = jnp.minimum(t + 1, win).astype(F32)
        y = _dot((tot / cnt - hg).astype(BF16), w_ref[gi])
        o_ref[:, c0:c0 + gw] = x[:, c0:c0 + gw] + y * sc_ref[:, c0:c0 + gw]


def _pool(x, g, w, scale, *, seq, tm=512):
    n, d = x.shape
    hb = tm // POOL_HALO
    return pl.pallas_call(
        functools.partial(_pool_kernel, tm=tm, tiles_per_batch=seq // tm),
        out_shape=jax.ShapeDtypeStruct((n, d), F32),
        grid=(n // tm,),
        in_specs=[
            pl.BlockSpec((tm, d), lambda i: (i, 0)),
            pl.BlockSpec((POOL_HALO, d), lambda i: (jnp.maximum(i * hb - 1, 0), 0)),
            pl.BlockSpec((1, d), lambda i: (0, 0)),
            pl.BlockSpec(w.shape, lambda i: (0, 0, 0)),
            pl.BlockSpec((1, d), lambda i: (0, 0)),
        ],
        out_specs=pl.BlockSpec((tm, d), lambda i: (i, 0)),
        scratch_shapes=[pltpu.VMEM((_POOL_PAD + POOL_HALO + tm, d), F32),
                        pltpu.VMEM((2, _POOL_PAD + POOL_HALO + tm, d // len(POOL_WINDOWS)), F32)],
        compiler_params=_cparams(("parallel",)),
        name="pool_mixer",
    )(x, x, g.reshape(1, d), w, scale.reshape(1, d))


def _pack_w_in(w_in):
    o = 0
    q = w_in[:, o:o + A_WIDTH]; o += A_WIDTH
    kc, vc, ks, vs, kw, vw = [w_in[:, o + k * KV_WIDTH:o + (k + 1) * KV_WIDTH] for k in range(6)]
    o += 6 * KV_WIDTH
    gates = w_in[:, o:o + N_BRANCH * N_HEADS]; o += N_BRANCH * N_HEADS
    u, gb, gc = [w_in[:, o + k * A_WIDTH:o + (k + 1) * A_WIDTH] for k in range(3)]
    gates = gates.reshape(-1, N_GROUPS, HEADS_PER_GROUP, N_BRANCH).transpose(0, 1, 3, 2)
    gates = gates.reshape(-1, N_GROUPS, N_BRANCH * HEADS_PER_GROUP)
    gates = jnp.pad(gates, ((0, 0), (0, 0), (0, LANES - N_BRANCH * HEADS_PER_GROUP))).reshape(-1, GATE_WIDTH)
    return jnp.concatenate([q, ks, kw, vs, vw, kc, vc, gates, u, gb, gc], axis=1).astype(BF16)


def _group_mean_matrix(width):
    r = jnp.arange(width) // HEAD_DIM
    return jnp.where(r[:, None] == r[None, :], 1.0 / HEAD_DIM, 0.0).astype(BF16)


def _rope_freq_row():
    lane = jnp.arange(LANES) % HEAD_DIM
    freqs = ROPE_THETA ** (-jnp.arange(0, ROPE_DIM, 2, dtype=F32) / ROPE_DIM)
    return jnp.where(lane < ROPE_DIM, freqs[lane % ROPE_HALF], 0.0).astype(F32).reshape(1, LANES)


def _hybrid_mixer(x, positions, g, w_in, q_norm, k_norm, cmp_pos, cmp_w1, cmp_w2, conv_w, w_out, *, bsz, seq):
    n = bsz * seq
    ncp = seq // CMP_STRIDE
    freq = _rope_freq_row()
    pq = _group_mean_matrix(A_WIDTH)
    pk = _group_mean_matrix(KV_WIDTH)
    qnw = jnp.tile(q_norm, N_HEADS).reshape(1, A_WIDTH)
    knw = jnp.tile(k_norm, (1, N_GROUPS))

    q_t, k_slc, k_win, v_t, kc_raw, vc_raw, gates, ob = _inproj(
        x, positions.reshape(n, 1), g, _pack_w_in(w_in), qnw, knw, conv_w, pq, pk, freq, seq=seq)

    eye_g = jnp.eye(N_GROUPS, dtype=F32)
    seg = CMP_STRIDE * KV_WIDTH

    def seg_weights(w1):
        w1r = w1.reshape(CMP_BLOCK, HEAD_DIM, CMP_HIDDEN)
        halves = [jnp.einsum('jdh,gk->jgdkh', w1r[a:a + CMP_STRIDE], eye_g).reshape(seg, N_GROUPS * CMP_HIDDEN)
                  for a in (0, CMP_STRIDE)]
        return jnp.stack(halves).astype(BF16)

    def seg_pos(pe):
        return [jnp.broadcast_to(pe[a:a + CMP_STRIDE, None, :], (CMP_STRIDE, N_GROUPS, HEAD_DIM)).reshape(seg)
                for a in (0, CMP_STRIDE)]

    pe = jnp.stack(seg_pos(cmp_pos[0]) + seg_pos(cmp_pos[1]))
    w2 = jnp.stack([jnp.einsum('hd,gk->ghkd', cmp_w2[a], eye_g).reshape(N_GROUPS * CMP_HIDDEN, KV_WIDTH)
                    for a in range(2)]).astype(BF16)
    blk_end = jnp.minimum(jnp.arange(ncp) * CMP_STRIDE + CMP_BLOCK - 1, seq - 1)
    pos_c = positions[:, blk_end].reshape(bsz, ncp, 1)
    kc_g, vc_t = _compress(kc_raw, vc_raw, pos_c, pe, seg_weights(cmp_w1[0]), seg_weights(cmp_w1[1]), w2[0], w2[1].T,
                           knw, pk, freq, seq=seq)

    gates_t = gates.reshape(n, N_GROUPS, LANES)[:, :, :GATE_ROWS].transpose(1, 2, 0).reshape(N_GROUPS * GATE_ROWS, n)
    cstart = jnp.arange(ncp) * CMP_STRIDE
    sstart = jnp.arange(LANES) * SEL_BLOCK
    ovt = ((cstart[None, :] < sstart[:, None] + SEL_BLOCK)
           & (cstart[None, :] + CMP_BLOCK > sstart[:, None])).astype(BF16)
    o_t, sel_bias = _cmp_win_select(q_t, kc_g, vc_t, ovt, k_win, v_t, gates_t, bsz=bsz, seq=seq)
    o_a = _slc(q_t, sel_bias, k_slc, v_t, gates_t, o_t, bsz=bsz, seq=seq)

    return _outproj(x, o_a, ob, w_out.astype(BF16))


def kernel(x, positions, ffn_norm, ffn_w_gate, ffn_w_up, ffn_w_down, mix_norm, hyb_w_in, hyb_q_norm, hyb_k_norm,
           hyb_cmp_pos, hyb_cmp_w1, hyb_cmp_w2, hyb_conv_w, hyb_w_out, pool_w, pool_scale):
    bsz, seq, d = x.shape
    depth = ffn_norm.shape[0]
    assert seq % 2048 == 0 and seq // SEL_BLOCK <= LANES and seq // SEL_BLOCK >= SEL_TOPK
    wg, wu, wd = (w.astype(BF16) for w in (ffn_w_gate, ffn_w_up, ffn_w_down))
    xf = x.reshape(bsz * seq, d)
    for layer in range(depth):
        xf = _ffn(xf, ffn_norm[layer, 0], wg, wu, wd, layer, 0)
        i = layer // 2
        if layer % 2 == 0:
            xf = _hybrid_mixer(xf, positions, mix_norm[layer], hyb_w_in[i], hyb_q_norm[i], hyb_k_norm[i],
                               hyb_cmp_pos[i], hyb_cmp_w1[i], hyb_cmp_w2[i], hyb_conv_w[i], hyb_w_out[i],
                               bsz=bsz, seq=seq)
        else:
            xf = _pool(xf, mix_norm[layer], pool_w[i].astype(BF16), pool_scale[i], seq=seq)
        xf = _ffn(xf, ffn_norm[layer, 1], wg, wu, wd, layer, 1)
    return xf.reshape(bsz, seq, d)
```

```python
import functools
import math

import jax
import jax.numpy as jnp
from jax import lax
from jax.experimental import pallas as pl
from jax.experimental.pallas import tpu as pltpu

F32 = jnp.float32
BF16 = jnp.bfloat16

HEAD_DIM = 64
N_GROUPS = 2
HEADS_PER_GROUP = 4
N_HEADS = N_GROUPS * HEADS_PER_GROUP
A_WIDTH = N_HEADS * HEAD_DIM
KV_WIDTH = N_GROUPS * HEAD_DIM
ROPE_DIM = HEAD_DIM // 4
ROPE_HALF = ROPE_DIM // 2
ROPE_THETA = 500000.0
CMP_BLOCK = 32
CMP_STRIDE = 16
CMP_HIDDEN = 2 * HEAD_DIM
SEL_BLOCK = 64
SEL_TOPK = 16
WINDOW = 512
N_BRANCH = 3
_BR_CMP, _BR_SLC, _BR_WIN = 0, 1, 2
CONV_WIDTH = 3
POOL_WINDOWS = (2, 4, 8, 16)
POOL_HALO = 16
_POOL_PAD = 8
EPS = 1e-6
NEG = -1e30

LANES = 128
VMEM_LIMIT_BYTES = 56 * 1024 * 1024


def _cparams(sem):
    return pltpu.CompilerParams(dimension_semantics=sem, vmem_limit_bytes=VMEM_LIMIT_BYTES)


def _dot(a, b):
    return jnp.dot(a, b, preferred_element_type=F32)


def _dot_split(a, b):
    hi = a.astype(BF16)
    lo = (a - hi.astype(F32)).astype(BF16)
    return _dot(hi, b) + _dot(lo, b)


def _rms(x, g):
    ms = jnp.mean(x * x, axis=-1, keepdims=True)
    return x * lax.rsqrt(ms + EPS) * g


def _group_rms(x, g, pmat):
    ms = _dot_split(x * x, pmat)
    return x * lax.rsqrt(ms + EPS) * g


def _rope_tables(pos_col, freq_row):
    ang = pos_col * freq_row
    lane = lax.broadcasted_iota(jnp.int32, (1, LANES), 1) & (HEAD_DIM - 1)
    sign = jnp.where(lane < ROPE_HALF, -1.0, 1.0).astype(F32)
    return jnp.cos(ang), jnp.sin(ang) * sign


def _rope(x, cos_t, sin_t):
    w = x.shape[1]
    reps = w // LANES
    if reps > 1:
        cos_t = jnp.concatenate([cos_t] * reps, axis=1)
        sin_t = jnp.concatenate([sin_t] * reps, axis=1)
    lane = lax.broadcasted_iota(jnp.int32, (1, w), 1) & (HEAD_DIM - 1)
    partner = jnp.where(lane < ROPE_HALF,
                        pltpu.roll(x, w - ROPE_HALF, axis=1),
                        pltpu.roll(x, ROPE_HALF, axis=1))
    return x * cos_t + partner * sin_t


def _ffn_kernel(x_ref, xnext_ref, g_ref, wg_ref, wu_ref, wd_ref, o_ref, h_ref, acc_ref, *, nj):
    i, j = pl.program_id(0), pl.program_id(1)
    slot = i % 2

    @pl.when((i == 0) & (j == 0))
    def _():
        h_ref[0] = _rms(x_ref[...], g_ref[...]).astype(BF16)

    def step(first, last):
        h = h_ref[slot]
        a = _dot(h, wg_ref[...])
        b = _dot(h, wu_ref[...])
        act = a * (1.0 / (1.0 + jnp.exp(-a))) * b
        y = _dot(act.astype(BF16), wd_ref[...])
        if not first:
            y = acc_ref[...] + y
        if last:
            h_ref[1 - slot] = _rms(xnext_ref[...], g_ref[...]).astype(BF16)
            o_ref[...] = x_ref[...] + 0.5 * y
        else:
            acc_ref[...] = y

    for jj in range(nj):
        pl.when(j == jj)(functools.partial(step, jj == 0, jj == nj - 1))


def _ffn(x, g, wg, wu, wd, layer, half, *, tm=512, tf=2816):
    n, d = x.shape
    dff = wg.shape[-1]
    ni, nj = n // tm, dff // tf
    wmode = dict(pipeline_mode=pl.Buffered(1)) if nj == 1 else {}
    return pl.pallas_call(
        functools.partial(_ffn_kernel, nj=nj),
        out_shape=jax.ShapeDtypeStruct((n, d), F32),
        grid=(ni, nj),
        in_specs=[
            pl.BlockSpec((tm, d), lambda i, j: (i, 0)),
            pl.BlockSpec((tm, d), lambda i, j: (jnp.minimum(i + 1, ni - 1), 0)),
            pl.BlockSpec((1, d), lambda i, j: (0, 0)),
            pl.BlockSpec((None, None, d, tf), lambda i, j: (layer, half, 0, j), **wmode),
            pl.BlockSpec((None, None, d, tf), lambda i, j: (layer, half, 0, j), **wmode),
            pl.BlockSpec((None, None, tf, d), lambda i, j: (layer, half, j, 0), **wmode),
        ],
        out_specs=pl.BlockSpec((tm, d), lambda i, j: (i, 0)),
        scratch_shapes=[pltpu.VMEM((2, tm, d), BF16), pltpu.VMEM((tm, d), F32)],
        compiler_params=_cparams(("arbitrary", "arbitrary")),
        name="ffn",
    )(x, x, g.reshape(1, d), wg, wu, wd)


GATE_WIDTH = N_GROUPS * LANES
GATE_ROWS = 16
_C_Q = 0
_C_K = _C_Q + A_WIDTH
_C_V = _C_K + 2 * KV_WIDTH
_C_C = _C_V + 2 * KV_WIDTH
_C_GATE = _C_C + 2 * KV_WIDTH
_C_U = _C_GATE + GATE_WIDTH
_C_GB = _C_U + A_WIDTH
_C_GC = _C_GB + A_WIDTH
CONV_HALO = 8
V_ROWS = HEAD_DIM + 16


def _inproj_kernel(x_ref, pos_ref, g_ref, w_ref, qnw_ref, knw_ref, cw_ref, pq_ref, pk_ref, freq_ref,
                   qt_ref, kslc_ref, kwin_ref, vt_ref, kc_ref, vc_ref, gates_ref, ob_ref,
                   vext_ref, *, tm, tiles_per_batch):
    i = pl.program_id(0)
    chunks = range(tm // LANES)

    def transposed(a, c):
        return jnp.concatenate([a[r * LANES:(r + 1) * LANES, c * LANES:(c + 1) * LANES].T for r in chunks], axis=1)

    def per_group(k):
        low = lax.broadcasted_iota(jnp.int32, (1, LANES), 1) < HEAD_DIM
        return [jnp.where(low, k, 0.0), jnp.where(low, pltpu.roll(k, HEAD_DIM, axis=1), 0.0)]

    @pl.when(i % tiles_per_batch == 0)
    def _():
        vext_ref[0:CONV_HALO, :] = jnp.zeros((CONV_HALO, A_WIDTH), F32)

    h = _rms(x_ref[...], g_ref[...]).astype(BF16)

    def proj(c0, width):
        return _dot(h, w_ref[:, c0:c0 + width])

    q_raw = proj(_C_Q, A_WIDTH)
    k2 = proj(_C_K, 2 * KV_WIDTH)
    cos_t, sin_t = _rope_tables(pos_ref[...].astype(F32), freq_ref[...])
    v = proj(_C_GC, A_WIDTH) * proj(_C_U, A_WIDTH)
    gate_b = proj(_C_GB, A_WIDTH)

    q = _group_rms(q_raw, qnw_ref[...], pq_ref[...])
    q = _rope(q, cos_t, sin_t) * (HEAD_DIM ** -0.5)
    for c in range(A_WIDTH // LANES):
        qt_ref[c * LANES:(c + 1) * LANES, :] = transposed(q, c).astype(BF16)

    tok = (i % tiles_per_batch) * tm + lax.broadcasted_iota(jnp.int32, (tm, 1), 0)
    blk_onehot = (lax.broadcasted_iota(jnp.int32, (1, LANES), 1) == (tok >> int(math.log2(SEL_BLOCK))))
    blk_onehot = jnp.where(blk_onehot, 1.0, 0.0).astype(BF16)
    ks = _rope(_group_rms(k2[:, :KV_WIDTH], knw_ref[1:2, :], pk_ref[...]), cos_t, sin_t)
    for g, kg in enumerate(per_group(ks)):
        kslc_ref[g] = jnp.concatenate([blk_onehot, kg.astype(BF16)], axis=1)
    kw = _rope(_group_rms(k2[:, KV_WIDTH:], knw_ref[2:3, :], pk_ref[...]), cos_t, sin_t)
    for g, kg in enumerate(per_group(kw)):
        kwin_ref[g] = kg.astype(BF16)

    v2 = proj(_C_V, 2 * KV_WIDTH)
    ones_rows = jnp.where(lax.broadcasted_iota(jnp.int32, (V_ROWS - HEAD_DIM, 1), 0) == 0, 1.0, 0.0)
    ones_rows = jnp.broadcast_to(ones_rows, (V_ROWS - HEAD_DIM, tm)).astype(BF16)
    for branch in range(2):
        vt = transposed(v2, branch).astype(BF16)
        for g in range(N_GROUPS):
            vt_ref[branch, g, 0:HEAD_DIM, :] = vt[g * HEAD_DIM:(g + 1) * HEAD_DIM, :]
            vt_ref[branch, g, HEAD_DIM:, :] = ones_rows
    c2 = proj(_C_C, 2 * KV_WIDTH)
    kc_ref[...] = c2[:, :KV_WIDTH]
    vc_ref[...] = c2[:, KV_WIDTH:]
    gate_logits = proj(_C_GATE, GATE_WIDTH)

    vext_ref[CONV_HALO:CONV_HALO + tm, :] = v
    y = (cw_ref[2:3, :] * v
         + cw_ref[1:2, :] * vext_ref[pl.ds(CONV_HALO - 1, tm), :]
         + cw_ref[0:1, :] * vext_ref[pl.ds(CONV_HALO - 2, tm), :])
    ob_ref[...] = (gate_b * y).astype(BF16)
    vext_ref[0:CONV_HALO, :] = vext_ref[tm:tm + CONV_HALO, :]
    gates_ref[...] = 1.0 / (1.0 + jnp.exp(-gate_logits))


def _inproj(x, pos_col, g, w, qnw, knw, cw, pq, pk, freq, *, seq, tm=512):
    n, d = x.shape
    bsz, tpb = n // seq, seq // tm
    row = lambda i: (i, 0)
    fixed = lambda i: (0, 0)
    widths = [(KV_WIDTH, F32), (KV_WIDTH, F32), (GATE_WIDTH, F32), (A_WIDTH, BF16)]
    out_shape = [jax.ShapeDtypeStruct((bsz, A_WIDTH, seq), BF16),
                 jax.ShapeDtypeStruct((bsz, N_GROUPS, seq, 2 * LANES), BF16),
                 jax.ShapeDtypeStruct((bsz, N_GROUPS, seq, LANES), BF16),
                 jax.ShapeDtypeStruct((bsz, 2, N_GROUPS, V_ROWS, seq), BF16)]
    out_specs = [pl.BlockSpec((None, A_WIDTH, tm), lambda i: (i // tpb, 0, i % tpb)),
                 pl.BlockSpec((None, N_GROUPS, tm, 2 * LANES), lambda i: (i // tpb, 0, i % tpb, 0)),
                 pl.BlockSpec((None, N_GROUPS, tm, LANES), lambda i: (i // tpb, 0, i % tpb, 0)),
                 pl.BlockSpec((None, 2, N_GROUPS, V_ROWS, tm), lambda i: (i // tpb, 0, 0, 0, i % tpb))]
    return pl.pallas_call(
        functools.partial(_inproj_kernel, tm=tm, tiles_per_batch=tpb),
        out_shape=out_shape + [jax.ShapeDtypeStruct((n, wd), dt) for wd, dt in widths],
        grid=(n // tm,),
        in_specs=[
            pl.BlockSpec((tm, d), row),
            pl.BlockSpec((tm, 1), row),
            pl.BlockSpec((1, d), fixed),
            pl.BlockSpec(w.shape, fixed),
            pl.BlockSpec(qnw.shape, fixed),
            pl.BlockSpec(knw.shape, fixed),
            pl.BlockSpec(cw.shape, fixed),
            pl.BlockSpec(pq.shape, fixed),
            pl.BlockSpec(pk.shape, fixed),
            pl.BlockSpec(freq.shape, fixed),
        ],
        out_specs=out_specs + [pl.BlockSpec((tm, wd), row) for wd, _ in widths],
        scratch_shapes=[pltpu.VMEM((tm + CONV_HALO, A_WIDTH), F32)],
        compiler_params=_cparams(("arbitrary",)),
        name="inproj",
    )(x, pos_col, g.reshape(1, d), w, qnw, knw, cw, pq, pk, freq)


def _gelu_tanh(x):
    return 0.5 * x * (1.0 + jnp.tanh(math.sqrt(2.0 / math.pi) * (x + 0.044715 * (x * x * x))))


def _compress_kernel(xk_ref, xv_ref, pos_ref, pe_ref, wk_ref, wv_ref, w2_ref, w2vt_ref, knw_ref, pk_ref, freq_ref,
                     kc_ref, vct_ref, *, ncp):
    def hidden(x_ref, pe_row, w_ref):
        x = jnp.concatenate([x_ref[pl.ds(j, ncp, stride=CMP_STRIDE), :] for j in range(CMP_STRIDE)], axis=1)
        a = _dot((x + pe_ref[pe_row:pe_row + 1, :]).astype(BF16), w_ref[0])
        b = _dot((x + pe_ref[pe_row + 1:pe_row + 2, :]).astype(BF16), w_ref[1])
        hid = a + pltpu.roll(b, ncp - 1, axis=0)
        return _gelu_tanh(hid)

    low = lax.broadcasted_iota(jnp.int32, (1, LANES), 1) < HEAD_DIM
    kc = _dot(hidden(xk_ref, 0, wk_ref).astype(BF16), w2_ref[...])
    kc = _group_rms(kc, knw_ref[0:1, :], pk_ref[...])
    cos_t, sin_t = _rope_tables(pos_ref[...].astype(F32), freq_ref[...])
    kc = _rope(kc, cos_t, sin_t)
    kc_ref[0] = jnp.where(low, kc, 0.0).astype(BF16)
    kc_ref[1] = jnp.where(low, pltpu.roll(kc, HEAD_DIM, axis=1), 0.0).astype(BF16)
    hv = hidden(xv_ref, 2, wv_ref)
    hv_t = jnp.concatenate(
        [jnp.concatenate([hv[r * LANES:(r + 1) * LANES, c * LANES:(c + 1) * LANES].T for r in range(ncp // LANES)],
                         axis=1) for c in range(hv.shape[1] // LANES)], axis=0)
    vct = _dot(w2vt_ref[...], hv_t.astype(BF16))
    for g in range(N_GROUPS):
        vct_ref[g] = vct[g * HEAD_DIM:(g + 1) * HEAD_DIM, :].astype(BF16)


def _compress(xk, xv, pos_c, pe, wk, wv, w2k, w2vt, knw, pk, freq, *, seq):
    bsz, ncp = pos_c.shape[:2]
    bat = lambda b: (b, 0, 0)
    fix2 = lambda b: (0, 0)
    fix3 = lambda b: (0, 0, 0)
    return pl.pallas_call(
        functools.partial(_compress_kernel, ncp=ncp),
        out_shape=[jax.ShapeDtypeStruct((bsz, N_GROUPS, ncp, LANES), BF16),
                   jax.ShapeDtypeStruct((bsz, N_GROUPS, HEAD_DIM, ncp), BF16)],
        grid=(bsz,),
        in_specs=[
            pl.BlockSpec((seq, KV_WIDTH), lambda b: (b, 0)),
            pl.BlockSpec((seq, KV_WIDTH), lambda b: (b, 0)),
            pl.BlockSpec((None, ncp, 1), bat),
            pl.BlockSpec(pe.shape, fix2),
            pl.BlockSpec(wk.shape, fix3),
            pl.BlockSpec(wv.shape, fix3),
            pl.BlockSpec(w2k.shape, fix2),
            pl.BlockSpec(w2vt.shape, fix2),
            pl.BlockSpec(knw.shape, fix2),
            pl.BlockSpec(pk.shape, fix2),
            pl.BlockSpec(freq.shape, fix2),
        ],
        out_specs=[pl.BlockSpec((None, N_GROUPS, ncp, LANES), lambda b: (b, 0, 0, 0)),
                   pl.BlockSpec((None, N_GROUPS, HEAD_DIM, ncp), lambda b: (b, 0, 0, 0))],
        compiler_params=_cparams(("parallel",)),
        name="compress",
    )(xk, xv, pos_c, pe, wk, wv, w2k, w2vt, knw, pk, freq)


def _head_columns(qt_ref):
    cols = []
    for pair in range(qt_ref.shape[0] // LANES):
        both = qt_ref[pair * LANES:(pair + 1) * LANES, :]
        cols += [both, jnp.concatenate([both[HEAD_DIM:, :], both[:HEAD_DIM, :]], axis=0)]
    return cols


def _software_pipeline(n, scores, probs, finish, after=None):
    s, p = {}, {}
    for step in range(n + 2):
        if step >= 2:
            finish(step - 2, p.pop(step - 2))
        if 1 <= step <= n:
            p[step - 1] = probs(step - 1, s.pop(step - 1))
        if step < n:
            s[step] = scores(step)
        for extra in (after or {}).get(step, ()):
            extra()


def _gate_row(gt_ref, branch, hd, group=0):
    r = group * GATE_ROWS + branch * HEADS_PER_GROUP + hd
    return gt_ref[r:r + 1, :]


def _cmp_kernel(qt_ref, kc_ref, vct_ref, ovt_ref, kwin_ref, vtwin_ref, band_ref, gt_ref, ot_ref, sbt_ref, *, tq, ncp):
    i = pl.program_id(1)
    span = WINDOW + tq
    wstart = pl.multiple_of(jnp.maximum(i - WINDOW // tq, 0) * tq, tq)
    t = i * tq + lax.broadcasted_iota(jnp.int32, (1, tq), 1)
    any_valid = (t >= CMP_BLOCK - 1).astype(F32)
    heads = _head_columns(qt_ref)
    cmp_per_sel = SEL_BLOCK // CMP_STRIDE
    groups = range(N_GROUPS)

    def run(nrow):
        nblk = nrow // cmp_per_sel
        blk_end = lax.broadcasted_iota(jnp.int32, (nrow, 1), 0) * CMP_STRIDE + (CMP_BLOCK - 1)
        valid = blk_end <= t
        kc = [kc_ref[g, 0:nrow, :] for g in groups]
        vct = [vct_ref[g, :, 0:nrow] for g in groups]
        psum_parts = [[] for _ in groups]

        def scores(unit):
            u = unit % N_HEADS
            if unit < N_HEADS:
                return jnp.where(valid, _dot(kc[u // HEADS_PER_GROUP], heads[u]), NEG)
            k = kwin_ref[u // HEADS_PER_GROUP, pl.ds(wstart, span), :]
            return _dot(k, heads[u]) + band_ref[...]

        def probs(unit, s):
            e = jnp.exp(s - jnp.max(s, axis=0, keepdims=True))
            if unit < N_HEADS:
                return e * (any_valid / jnp.sum(e, axis=0, keepdims=True))
            return e.astype(BF16)

        def finish(unit, p):
            u = unit % N_HEADS
            g, hd = divmod(u, HEADS_PER_GROUP)
            rows = slice(u * HEAD_DIM, (u + 1) * HEAD_DIM)
            if unit < N_HEADS:
                ot_ref[rows, :] = _dot(vct[g], p.astype(BF16)) * _gate_row(gt_ref, _BR_CMP, hd, g)
                psum_parts[g].append(p)
            else:
                acc = _dot(vtwin_ref[g, :, pl.ds(wstart, span)], p)
                scale = _gate_row(gt_ref, _BR_WIN, hd, g) / acc[HEAD_DIM:HEAD_DIM + 1, :]
                ot_ref[rows, :] = ot_ref[rows, :] + acc[:HEAD_DIM, :] * scale

        blk = lax.broadcasted_iota(jnp.int32, (nblk, 1), 0)
        blk_f = blk.astype(F32)
        imp = []
        bias = [jnp.full((nblk, tq), NEG, F32) for _ in groups]

        def importance():
            ovt = ovt_ref[0:nblk, 0:nrow]
            sel_start = blk * SEL_BLOCK
            cur = (t >> int(math.log2(SEL_BLOCK))) << int(math.log2(SEL_BLOCK))
            for parts in psum_parts:
                psum = (parts[0] + parts[1]) + (parts[2] + parts[3])
                hi = psum.astype(BF16)
                lo = (psum - hi.astype(F32)).astype(BF16)
                x = jnp.where(sel_start <= t, _dot(ovt, hi) + _dot(ovt, lo), -1.0)
                imp.append(jnp.where(sel_start == cur, 1e4, jnp.where(sel_start == 0, 1e4, x)))

        def topk_round():
            for g in groups:
                mx = jnp.max(imp[g], axis=0, keepdims=True)
                first = jnp.min(jnp.where(imp[g] == mx, blk_f, float(LANES)), axis=0, keepdims=True)
                pick = blk_f == first
                bias[g] = jnp.where(pick, 0.0, bias[g])
                imp[g] = jnp.where(pick, -3e38, imp[g])

        last_cmp = N_HEADS + 1
        rest = range(last_cmp + 1, 2 * N_HEADS + 2)
        after = {last_cmp: [importance]}
        for r in range(SEL_TOPK):
            after.setdefault(rest[r * len(rest) // SEL_TOPK], []).append(topk_round)
        _software_pipeline(2 * N_HEADS, scores, probs, finish, after)
        for g in groups:
            sbt_ref[g, 0:nblk, :] = bias[g].astype(BF16)
            if nblk < LANES:
                sbt_ref[g, nblk:, :] = jnp.full((LANES - nblk, tq), NEG, BF16)

    sizes = [LANES * (v + 1) for v in range(ncp // LANES)]
    assert sizes[0] // cmp_per_sel >= SEL_TOPK
    tokens_per_variant = LANES * CMP_STRIDE
    lax.switch((i * tq + tq - 1) // tokens_per_variant, [functools.partial(run, nrow) for nrow in sizes])


def _window_band(tq):
    edge = WINDOW // tq
    v = jnp.arange(edge + 1)[:, None, None]
    kpos = jnp.maximum(v - edge, 0) * tq + jnp.arange(WINDOW + tq)[None, :, None]
    t = v * tq + jnp.arange(tq)[None, None, :]
    ok = (t - kpos >= 0) & (t - kpos < WINDOW)
    return jnp.where(ok, 0.0, NEG).astype(F32)


def _cmp_win_select(qt, kc, vct, ovt, k_win, vt, gates_t, *, bsz, seq, tq=256):
    ncp = kc.shape[2]
    nq = seq // tq
    band = _window_band(tq)
    last = band.shape[0] - 1
    whole = lambda b, i: (b, 0, 0, 0)
    return pl.pallas_call(
        functools.partial(_cmp_kernel, tq=tq, ncp=ncp),
        out_shape=[jax.ShapeDtypeStruct((bsz, A_WIDTH, seq), F32),
                   jax.ShapeDtypeStruct((bsz, N_GROUPS, LANES, seq), BF16)],
        grid=(bsz, nq),
        in_specs=[
            pl.BlockSpec((None, A_WIDTH, tq), lambda b, i: (b, 0, i)),
            pl.BlockSpec((None, N_GROUPS, ncp, LANES), whole),
            pl.BlockSpec((None, N_GROUPS, HEAD_DIM, ncp), whole),
            pl.BlockSpec(ovt.shape, lambda b, i: (0, 0)),
            pl.BlockSpec((None, N_GROUPS, seq, LANES), whole),
            pl.BlockSpec((None, None, N_GROUPS, V_ROWS, seq), lambda b, i: (b, 1, 0, 0, 0)),
            pl.BlockSpec((None,) + band.shape[1:], lambda b, i: (jnp.minimum(i, last), 0, 0)),
            pl.BlockSpec((N_GROUPS * GATE_ROWS, tq), lambda b, i: (0, b * nq + i)),
        ],
        out_specs=[pl.BlockSpec((None, A_WIDTH, tq), lambda b, i: (b, 0, i)),
                   pl.BlockSpec((None, N_GROUPS, LANES, tq), lambda b, i: (b, 0, 0, i))],
        compiler_params=_cparams(("parallel", "parallel")),
        name="cmp_win_select",
    )(qt, kc, vct, ovt, k_win, vt, band, gates_t)


_SLC_UNROLL = 2


def _slc_kernel(qt_ref, sbt_ref, k_ref, vt_ref, gt_ref, prev_ref, o_ref,
                qa_ref, s_ref, p_ref, acc_ref, *, tq):
    i = pl.program_id(2)
    rows = HEADS_PER_GROUP * tq
    qa_ref[...] = jnp.concatenate([jnp.concatenate([sbt_ref[...]] * HEADS_PER_GROUP, axis=1),
                                   jnp.concatenate(_head_columns(qt_ref), axis=1)], axis=0)
    p_ref[...] = jnp.zeros(p_ref.shape, BF16)
    acc_ref[...] = jnp.zeros(acc_ref.shape, F32)

    def accumulate(unit, alpha):
        j = jnp.where(unit <= 0, i, unit - 1)
        pv = _dot(vt_ref[:, pl.ds(pl.multiple_of(j * tq, tq), tq)], p_ref[...])
        acc_ref[...] = alpha * acc_ref[...] + pv

    def probs(m):
        s = s_ref[...]
        m_new = jnp.maximum(m, jnp.max(s, axis=0, keepdims=True))
        p_ref[...] = jnp.exp(s - m_new).astype(BF16)
        return m_new, jnp.exp(m - m_new)

    def scores(j, masked):
        s = _dot(k_ref[pl.ds(pl.multiple_of(j * tq, tq), tq), :], qa_ref[...])
        if masked:
            causal = (lax.broadcasted_iota(jnp.int32, (tq, 1), 0)
                      <= (lax.broadcasted_iota(jnp.int32, (1, rows), 1) & (tq - 1)))
            s = jnp.where(causal, s, NEG)
        s_ref[...] = s

    def body(k, carry):
        m, alpha = carry
        accumulate(k - 1, alpha)
        m, alpha = probs(m)
        scores(k, False)
        return m, alpha

    def steps(first, count, carry):
        for k in range(count):
            carry = body(first + k, carry)
        return carry

    scores(i, True)
    carry = (jnp.full((1, rows), -3e38, F32), jnp.ones((1, rows), F32))
    trips = i >> int(math.log2(_SLC_UNROLL))
    carry = lax.fori_loop(0, trips, lambda k, c: steps(k * _SLC_UNROLL, _SLC_UNROLL, c), carry)
    done = trips * _SLC_UNROLL
    size = _SLC_UNROLL // 2
    while size >= 1:
        carry = lax.cond((i & size) != 0, functools.partial(steps, done, size), lambda c: c, carry)
        done = done + (i & size)
        size //= 2
    m, alpha = carry
    accumulate(i - 1, alpha)
    m, alpha = probs(m)
    accumulate(i, alpha)

    gate = jnp.concatenate([_gate_row(gt_ref, _BR_SLC, hd) for hd in range(HEADS_PER_GROUP)], axis=1)
    out_t = acc_ref[0:HEAD_DIM, :] * (gate / acc_ref[HEAD_DIM:HEAD_DIM + 1, :])
    for pair in range(HEADS_PER_GROUP // 2):
        for c in range(tq // LANES):
            col = 2 * pair * tq + c * LANES
            both = jnp.concatenate([out_t[:, col:col + LANES], out_t[:, col + tq:col + tq + LANES]], axis=0)
            both = both + prev_ref[pair * LANES:(pair + 1) * LANES, c * LANES:(c + 1) * LANES]
            o_ref[c * LANES:(c + 1) * LANES, pair * LANES:(pair + 1) * LANES] = both.T.astype(BF16)


def _slc(qt, sbt, k_aug, vt_aug, gates_t, prev, *, bsz, seq, tq=512):
    nq = seq // tq
    gw = HEADS_PER_GROUP * HEAD_DIM
    rows = HEADS_PER_GROUP * tq
    tmap = lambda b, g, i: (b, g, i)
    return pl.pallas_call(
        functools.partial(_slc_kernel, tq=tq),
        out_shape=jax.ShapeDtypeStruct((bsz * seq, A_WIDTH), BF16),
        grid=(bsz, N_GROUPS, nq),
        in_specs=[
            pl.BlockSpec((None, gw, tq), tmap),
            pl.BlockSpec((None, None, LANES, tq), lambda b, g, i: (b, g, 0, i)),
            pl.BlockSpec((None, None, seq, 2 * LANES), lambda b, g, i: (b, g, 0, 0)),
            pl.BlockSpec((None, None, None, V_ROWS, seq), lambda b, g, i: (b, 0, g, 0, 0)),
            pl.BlockSpec((GATE_ROWS, tq), lambda b, g, i: (g, b * nq + i)),
            pl.BlockSpec((None, gw, tq), tmap),
        ],
        out_specs=pl.BlockSpec((tq, gw), lambda b, g, i: (b * nq + i, g)),
        scratch_shapes=[pltpu.VMEM((2 * LANES, rows), BF16), pltpu.VMEM((tq, rows), F32),
                        pltpu.VMEM((tq, rows), BF16), pltpu.VMEM((V_ROWS, rows), F32)],
        compiler_params=_cparams(("parallel", "parallel", "parallel")),
        name="slc_attn",
    )(qt, sbt, k_aug, vt_aug, gates_t, prev)


def _outproj_kernel(x_ref, oa_ref, ob_ref, w_ref, o_ref):
    y = _dot(oa_ref[...], w_ref[0:A_WIDTH, :]) + _dot(ob_ref[...], w_ref[A_WIDTH:, :])
    o_ref[...] = x_ref[...] + y


def _outproj(x, oa, ob, w, *, tm=1024):
    n, d = x.shape
    row = lambda i: (i, 0)
    return pl.pallas_call(
        _outproj_kernel,
        out_shape=jax.ShapeDtypeStruct((n, d), F32),
        grid=(n // tm,),
        in_specs=[
            pl.BlockSpec((tm, d), row),
            pl.BlockSpec((tm, A_WIDTH), row),
            pl.BlockSpec((tm, A_WIDTH), row),
            pl.BlockSpec(w.shape, lambda i: (0, 0)),
        ],
        out_specs=pl.BlockSpec((tm, d), row),
        compiler_params=_cparams(("parallel",)),
        name="outproj",
    )(x, oa, ob, w)


def _pool_kernel(x_ref, halo_ref, g_ref, w_ref, sc_ref, o_ref, ext_ref, tmp_ref, *, tm, tiles_per_batch):
    i = pl.program_id(0)
    first = i % tiles_per_batch == 0
    x = x_ref[...]
    h = _rms(x, g_ref[...])
    hh = _rms(halo_ref[...], g_ref[...])
    pad, body = _POOL_PAD, _POOL_PAD + POOL_HALO
    ext_ref[0:pad, :] = jnp.zeros((pad, h.shape[1]), F32)
    ext_ref[pad:body, :] = jnp.where(first, 0.0, hh)
    ext_ref[body:body + tm, :] = h
    gw = h.shape[1] // len(POOL_WINDOWS)
    for slot in range(2):
        tmp_ref[slot, 0:pad, :] = jnp.zeros((pad, gw), F32)
    t = (i % tiles_per_batch) * tm + lax.broadcasted_iota(jnp.int32, (tm, 1), 0)
    span = POOL_HALO + tm
    for gi, win in enumerate(POOL_WINDOWS):
        c0 = gi * gw
        hg = h[:, c0:c0 + gw]
        read = lambda start, c0=c0: ext_ref[pl.ds(start, span), c0:c0 + gw]
        shift, slot = 1, 0
        while shift < win:
            tmp_ref[slot, pad:pad + span, :] = read(pad) + read(pad - shift)
            read = lambda start, slot=slot: tmp_ref[slot, pl.ds(start, span), :]
            shift, slot = 2 * shift, 1 - slot
        tot = read(pad)[POOL_HALO:, :]
        cnt = jnp.minimum(t + 1, win).astype(F32)
        y = _dot((tot / cnt - hg).astype(BF16), w_ref[gi])
        o_ref[:, c0:c0 + gw] = x[:, c0:c0 + gw] + y * sc_ref[:, c0:c0 + gw]


def _pool(x, g, w, scale, *, seq, tm=512):
    n, d = x.shape
    hb = tm // POOL_HALO
    return pl.pallas_call(
        functools.partial(_pool_kernel, tm=tm, tiles_per_batch=seq // tm),
        out_shape=jax.ShapeDtypeStruct((n, d), F32),
        grid=(n // tm,),
        in_specs=[
            pl.BlockSpec((tm, d), lambda i: (i, 0)),
            pl.BlockSpec((POOL_HALO, d), lambda i: (jnp.maximum(i * hb - 1, 0), 0)),
            pl.BlockSpec((1, d), lambda i: (0, 0)),
            pl.BlockSpec(w.shape, lambda i: (0, 0, 0)),
            pl.BlockSpec((1, d), lambda i: (0, 0)),
        ],
        out_specs=pl.BlockSpec((tm, d), lambda i: (i, 0)),
        scratch_shapes=[pltpu.VMEM((_POOL_PAD + POOL_HALO + tm, d), F32),
                        pltpu.VMEM((2, _POOL_PAD + POOL_HALO + tm, d // len(POOL_WINDOWS)), F32)],
        compiler_params=_cparams(("parallel",)),
        name="pool_mixer",
    )(x, x, g.reshape(1, d), w, scale.reshape(1, d))


def _pack_w_in(w_in):
    o = 0
    q = w_in[:, o:o + A_WIDTH]; o += A_WIDTH
    kc, vc, ks, vs, kw, vw = [w_in[:, o + k * KV_WIDTH:o + (k + 1) * KV_WIDTH] for k in range(6)]
    o += 6 * KV_WIDTH
    gates = w_in[:, o:o + N_BRANCH * N_HEADS]; o += N_BRANCH * N_HEADS
    u, gb, gc = [w_in[:, o + k * A_WIDTH:o + (k + 1) * A_WIDTH] for k in range(3)]
    gates = gates.reshape(-1, N_GROUPS, HEADS_PER_GROUP, N_BRANCH).transpose(0, 1, 3, 2)
    gates = gates.reshape(-1, N_GROUPS, N_BRANCH * HEADS_PER_GROUP)
    gates = jnp.pad(gates, ((0, 0), (0, 0), (0, LANES - N_BRANCH * HEADS_PER_GROUP))).reshape(-1, GATE_WIDTH)
    return jnp.concatenate([q, ks, kw, vs, vw, kc, vc, gates, u, gb, gc], axis=1).astype(BF16)


def _group_mean_matrix(width):
    r = jnp.arange(width) // HEAD_DIM
    return jnp.where(r[:, None] == r[None, :], 1.0 / HEAD_DIM, 0.0).astype(BF16)


def _rope_freq_row():
    lane = jnp.arange(LANES) % HEAD_DIM
    freqs = ROPE_THETA ** (-jnp.arange(0, ROPE_DIM, 2, dtype=F32) / ROPE_DIM)
    return jnp.where(lane < ROPE_DIM, freqs[lane % ROPE_HALF], 0.0).astype(F32).reshape(1, LANES)


def _hybrid_mixer(x, positions, g, w_in, q_norm, k_norm, cmp_pos, cmp_w1, cmp_w2, conv_w, w_out, *, bsz, seq):
    n = bsz * seq
    ncp = seq // CMP_STRIDE
    freq = _rope_freq_row()
    pq = _group_mean_matrix(A_WIDTH)
    pk = _group_mean_matrix(KV_WIDTH)
    qnw = jnp.tile(q_norm, N_HEADS).reshape(1, A_WIDTH)
    knw = jnp.tile(k_norm, (1, N_GROUPS))

    q_t, k_slc, k_win, v_t, kc_raw, vc_raw, gates, ob = _inproj(
        x, positions.reshape(n, 1), g, _pack_w_in(w_in), qnw, knw, conv_w, pq, pk, freq, seq=seq)

    eye_g = jnp.eye(N_GROUPS, dtype=F32)
    seg = CMP_STRIDE * KV_WIDTH

    def seg_weights(w1):
        w1r = w1.reshape(CMP_BLOCK, HEAD_DIM, CMP_HIDDEN)
        halves = [jnp.einsum('jdh,gk->jgdkh', w1r[a:a + CMP_STRIDE], eye_g).reshape(seg, N_GROUPS * CMP_HIDDEN)
                  for a in (0, CMP_STRIDE)]
        return jnp.stack(halves).astype(BF16)

    def seg_pos(pe):
        return [jnp.broadcast_to(pe[a:a + CMP_STRIDE, None, :], (CMP_STRIDE, N_GROUPS, HEAD_DIM)).reshape(seg)
                for a in (0, CMP_STRIDE)]

    pe = jnp.stack(seg_pos(cmp_pos[0]) + seg_pos(cmp_pos[1]))
    w2 = jnp.stack([jnp.einsum('hd,gk->ghkd', cmp_w2[a], eye_g).reshape(N_GROUPS * CMP_HIDDEN, KV_WIDTH)
                    for a in range(2)]).astype(BF16)
    blk_end = jnp.minimum(jnp.arange(ncp) * CMP_STRIDE + CMP_BLOCK - 1, seq - 1)
    pos_c = positions[:, blk_end].reshape(bsz, ncp, 1)
    kc_g, vc_t = _compress(kc_raw, vc_raw, pos_c, pe, seg_weights(cmp_w1[0]), seg_weights(cmp_w1[1]), w2[0], w2[1].T,
                           knw, pk, freq, seq=seq)

    gates_t = gates.reshape(n, N_GROUPS, LANES)[:, :, :GATE_ROWS].transpose(1, 2, 0).reshape(N_GROUPS * GATE_ROWS, n)
    cstart = jnp.arange(ncp) * CMP_STRIDE
    sstart = jnp.arange(LANES) * SEL_BLOCK
    ovt = ((cstart[None, :] < sstart[:, None] + SEL_BLOCK)
           & (cstart[None, :] + CMP_BLOCK > sstart[:, None])).astype(BF16)
    o_t, sel_bias = _cmp_win_select(q_t, kc_g, vc_t, ovt, k_win, v_t, gates_t, bsz=bsz, seq=seq)
    o_a = _slc(q_t, sel_bias, k_slc, v_t, gates_t, o_t, bsz=bsz, seq=seq)

    return _outproj(x, o_a, ob, w_out.astype(BF16))


def kernel(x, positions, ffn_norm, ffn_w_gate, ffn_w_up, ffn_w_down, mix_norm, hyb_w_in, hyb_q_norm, hyb_k_norm,
           hyb_cmp_pos, hyb_cmp_w1, hyb_cmp_w2, hyb_conv_w, hyb_w_out, pool_w, pool_scale):
    bsz, seq, d = x.shape
    depth = ffn_norm.shape[0]
    assert seq % 2048 == 0 and seq // SEL_BLOCK <= LANES and seq // SEL_BLOCK >= SEL_TOPK
    wg, wu, wd = (w.astype(BF16) for w in (ffn_w_gate, ffn_w_up, ffn_w_down))
    xf = x.reshape(bsz * seq, d)
    for layer in range(depth):
        xf = _ffn(xf, ffn_norm[layer, 0], wg, wu, wd, layer, 0)
        i = layer // 2
        if layer % 2 == 0:
            xf = _hybrid_mixer(xf, positions, mix_norm[layer], hyb_w_in[i], hyb_q_norm[i], hyb_k_norm[i],
                               hyb_cmp_pos[i], hyb_cmp_w1[i], hyb_cmp_w2[i], hyb_conv_w[i], hyb_w_out[i],
                               bsz=bsz, seq=seq)
        else:
            xf = _pool(xf, mix_norm[layer], pool_w[i].astype(BF16), pool_scale[i], seq=seq)
        xf = _ffn(xf, ffn_norm[layer, 1], wg, wu, wd, layer, 1)
    return xf.reshape(bsz, seq, d)
```

```python
import functools
import math

import jax
import jax.numpy as jnp
from jax import lax
from jax.experimental import pallas as pl
from jax.experimental.pallas import tpu as pltpu

F32 = jnp.float32
BF16 = jnp.bfloat16

HEAD_DIM = 64
N_GROUPS = 2
HEADS_PER_GROUP = 4
N_HEADS = N_GROUPS * HEADS_PER_GROUP
A_WIDTH = N_HEADS * HEAD_DIM
KV_WIDTH = N_GROUPS * HEAD_DIM
ROPE_DIM = HEAD_DIM // 4
ROPE_HALF = ROPE_DIM // 2
ROPE_THETA = 500000.0
CMP_BLOCK = 32
CMP_STRIDE = 16
CMP_HIDDEN = 2 * HEAD_DIM
SEL_BLOCK = 64
SEL_TOPK = 16
WINDOW = 512
N_BRANCH = 3
_BR_CMP, _BR_SLC, _BR_WIN = 0, 1, 2
CONV_WIDTH = 3
POOL_WINDOWS = (2, 4, 8, 16)
POOL_HALO = 16
_POOL_PAD = 8
EPS = 1e-6
NEG = -1e30

LANES = 128
VMEM_LIMIT_BYTES = 56 * 1024 * 1024


def _cparams(sem):
    return pltpu.CompilerParams(dimension_semantics=sem, vmem_limit_bytes=VMEM_LIMIT_BYTES)


def _dot(a, b):
    return jnp.dot(a, b, preferred_element_type=F32)


def _dot_split(a, b):
    hi = a.astype(BF16)
    lo = (a - hi.astype(F32)).astype(BF16)
    return _dot(hi, b) + _dot(lo, b)


def _rms(x, g):
    ms = jnp.mean(x * x, axis=-1, keepdims=True)
    return x * lax.rsqrt(ms + EPS) * g


def _group_rms(x, g, pmat):
    ms = _dot_split(x * x, pmat)
    return x * lax.rsqrt(ms + EPS) * g


def _rope_tables(pos_col, freq_row):
    ang = pos_col * freq_row
    lane = lax.broadcasted_iota(jnp.int32, (1, LANES), 1) & (HEAD_DIM - 1)
    sign = jnp.where(lane < ROPE_HALF, -1.0, 1.0).astype(F32)
    return jnp.cos(ang), jnp.sin(ang) * sign


def _rope(x, cos_t, sin_t):
    w = x.shape[1]
    reps = w // LANES
    if reps > 1:
        cos_t = jnp.concatenate([cos_t] * reps, axis=1)
        sin_t = jnp.concatenate([sin_t] * reps, axis=1)
    lane = lax.broadcasted_iota(jnp.int32, (1, w), 1) & (HEAD_DIM - 1)
    partner = jnp.where(lane < ROPE_HALF,
                        pltpu.roll(x, w - ROPE_HALF, axis=1),
                        pltpu.roll(x, ROPE_HALF, axis=1))
    return x * cos_t + partner * sin_t


def _ffn_kernel(x_ref, xnext_ref, g_ref, wg_ref, wu_ref, wd_ref, o_ref, h_ref, acc_ref, *, nj):
    i, j = pl.program_id(0), pl.program_id(1)
    slot = i % 2

    @pl.when((i == 0) & (j == 0))
    def _():
        h_ref[0] = _rms(x_ref[...], g_ref[...]).astype(BF16)

    def step(first, last):
        h = h_ref[slot]
        a = _dot(h, wg_ref[...])
        b = _dot(h, wu_ref[...])
        act = a * (1.0 / (1.0 + jnp.exp(-a))) * b
        y = _dot(act.astype(BF16), wd_ref[...])
        if not first:
            y = acc_ref[...] + y
        if last:
            h_ref[1 - slot] = _rms(xnext_ref[...], g_ref[...]).astype(BF16)
            o_ref[...] = x_ref[...] + 0.5 * y
        else:
            acc_ref[...] = y

    for jj in range(nj):
        pl.when(j == jj)(functools.partial(step, jj == 0, jj == nj - 1))


def _ffn(x, g, wg, wu, wd, layer, half, *, tm=512, tf=2816):
    n, d = x.shape
    dff = wg.shape[-1]
    ni, nj = n // tm, dff // tf
    wmode = dict(pipeline_mode=pl.Buffered(1)) if nj == 1 else {}
    return pl.pallas_call(
        functools.partial(_ffn_kernel, nj=nj),
        out_shape=jax.ShapeDtypeStruct((n, d), F32),
        grid=(ni, nj),
        in_specs=[
            pl.BlockSpec((tm, d), lambda i, j: (i, 0)),
            pl.BlockSpec((tm, d), lambda i, j: (jnp.minimum(i + 1, ni - 1), 0)),
            pl.BlockSpec((1, d), lambda i, j: (0, 0)),
            pl.BlockSpec((None, None, d, tf), lambda i, j: (layer, half, 0, j), **wmode),
            pl.BlockSpec((None, None, d, tf), lambda i, j: (layer, half, 0, j), **wmode),
            pl.BlockSpec((None, None, tf, d), lambda i, j: (layer, half, j, 0), **wmode),
        ],
        out_specs=pl.BlockSpec((tm, d), lambda i, j: (i, 0)),
        scratch_shapes=[pltpu.VMEM((2, tm, d), BF16), pltpu.VMEM((tm, d), F32)],
        compiler_params=_cparams(("arbitrary", "arbitrary")),
        name="ffn",
    )(x, x, g.reshape(1, d), wg, wu, wd)


GATE_WIDTH = N_GROUPS * LANES
GATE_ROWS = 16
_C_Q = 0
_C_K = _C_Q + A_WIDTH
_C_V = _C_K + 2 * KV_WIDTH
_C_C = _C_V + 2 * KV_WIDTH
_C_GATE = _C_C + 2 * KV_WIDTH
_C_U = _C_GATE + GATE_WIDTH
_C_GB = _C_U + A_WIDTH
_C_GC = _C_GB + A_WIDTH
CONV_HALO = 8
V_ROWS = HEAD_DIM + 16


def _inproj_kernel(x_ref, pos_ref, g_ref, w_ref, qnw_ref, knw_ref, cw_ref, pq_ref, pk_ref, freq_ref,
                   qt_ref, kslc_ref, kwin_ref, vt_ref, kc_ref, vc_ref, gates_ref, ob_ref,
                   vext_ref, *, tm, tiles_per_batch):
    i = pl.program_id(0)
    chunks = range(tm // LANES)

    def transposed(a, c):
        return jnp.concatenate([a[r * LANES:(r + 1) * LANES, c * LANES:(c + 1) * LANES].T for r in chunks], axis=1)

    def per_group(k):
        low = lax.broadcasted_iota(jnp.int32, (1, LANES), 1) < HEAD_DIM
        return [jnp.where(low, k, 0.0), jnp.where(low, pltpu.roll(k, HEAD_DIM, axis=1), 0.0)]

    @pl.when(i % tiles_per_batch == 0)
    def _():
        vext_ref[0:CONV_HALO, :] = jnp.zeros((CONV_HALO, A_WIDTH), F32)

    h = _rms(x_ref[...], g_ref[...]).astype(BF16)

    def proj(c0, width):
        return _dot(h, w_ref[:, c0:c0 + width])

    q_raw = proj(_C_Q, A_WIDTH)
    k2 = proj(_C_K, 2 * KV_WIDTH)
    cos_t, sin_t = _rope_tables(pos_ref[...].astype(F32), freq_ref[...])
    v = proj(_C_GC, A_WIDTH) * proj(_C_U, A_WIDTH)
    gate_b = proj(_C_GB, A_WIDTH)

    q = _group_rms(q_raw, qnw_ref[...], pq_ref[...])
    q = _rope(q, cos_t, sin_t) * (HEAD_DIM ** -0.5)
    for c in range(A_WIDTH // LANES):
        qt_ref[c * LANES:(c + 1) * LANES, :] = transposed(q, c).astype(BF16)

    tok = (i % tiles_per_batch) * tm + lax.broadcasted_iota(jnp.int32, (tm, 1), 0)
    blk_onehot = (lax.broadcasted_iota(jnp.int32, (1, LANES), 1) == (tok >> int(math.log2(SEL_BLOCK))))
    blk_onehot = jnp.where(blk_onehot, 1.0, 0.0).astype(BF16)
    ks = _rope(_group_rms(k2[:, :KV_WIDTH], knw_ref[1:2, :], pk_ref[...]), cos_t, sin_t)
    for g, kg in enumerate(per_group(ks)):
        kslc_ref[g] = jnp.concatenate([blk_onehot, kg.astype(BF16)], axis=1)
    kw = _rope(_group_rms(k2[:, KV_WIDTH:], knw_ref[2:3, :], pk_ref[...]), cos_t, sin_t)
    for g, kg in enumerate(per_group(kw)):
        kwin_ref[g] = kg.astype(BF16)

    v2 = proj(_C_V, 2 * KV_WIDTH)
    ones_rows = jnp.where(lax.broadcasted_iota(jnp.int32, (V_ROWS - HEAD_DIM, 1), 0) == 0, 1.0, 0.0)
    ones_rows = jnp.broadcast_to(ones_rows, (V_ROWS - HEAD_DIM, tm)).astype(BF16)
    for branch in range(2):
        vt = transposed(v2, branch).astype(BF16)
        for g in range(N_GROUPS):
            vt_ref[branch, g, 0:HEAD_DIM, :] = vt[g * HEAD_DIM:(g + 1) * HEAD_DIM, :]
            vt_ref[branch, g, HEAD_DIM:, :] = ones_rows
    c2 = proj(_C_C, 2 * KV_WIDTH)
    kc_ref[...] = c2[:, :KV_WIDTH]
    vc_ref[...] = c2[:, KV_WIDTH:]
    gate_logits = proj(_C_GATE, GATE_WIDTH)

    vext_ref[CONV_HALO:CONV_HALO + tm, :] = v
    y = (cw_ref[2:3, :] * v
         + cw_ref[1:2, :] * vext_ref[pl.ds(CONV_HALO - 1, tm), :]
         + cw_ref[0:1, :] * vext_ref[pl.ds(CONV_HALO - 2, tm), :])
    ob_ref[...] = (gate_b * y).astype(BF16)
    vext_ref[0:CONV_HALO, :] = vext_ref[tm:tm + CONV_HALO, :]
    gates_ref[...] = 1.0 / (1.0 + jnp.exp(-gate_logits))


def _inproj(x, pos_col, g, w, qnw, knw, cw, pq, pk, freq, *, seq, tm=512):
    n, d = x.shape
    bsz, tpb = n // seq, seq // tm
    row = lambda i: (i, 0)
    fixed = lambda i: (0, 0)
    widths = [(KV_WIDTH, F32), (KV_WIDTH, F32), (GATE_WIDTH, F32), (A_WIDTH, BF16)]
    out_shape = [jax.ShapeDtypeStruct((bsz, A_WIDTH, seq), BF16),
                 jax.ShapeDtypeStruct((bsz, N_GROUPS, seq, 2 * LANES), BF16),
                 jax.ShapeDtypeStruct((bsz, N_GROUPS, seq, LANES), BF16),
                 jax.ShapeDtypeStruct((bsz, 2, N_GROUPS, V_ROWS, seq), BF16)]
    out_specs = [pl.BlockSpec((None, A_WIDTH, tm), lambda i: (i // tpb, 0, i % tpb)),
                 pl.BlockSpec((None, N_GROUPS, tm, 2 * LANES), lambda i: (i // tpb, 0, i % tpb, 0)),
                 pl.BlockSpec((None, N_GROUPS, tm, LANES), lambda i: (i // tpb, 0, i % tpb, 0)),
                 pl.BlockSpec((None, 2, N_GROUPS, V_ROWS, tm), lambda i: (i // tpb, 0, 0, 0, i % tpb))]
    return pl.pallas_call(
        functools.partial(_inproj_kernel, tm=tm, tiles_per_batch=tpb),
        out_shape=out_shape + [jax.ShapeDtypeStruct((n, wd), dt) for wd, dt in widths],
        grid=(n // tm,),
        in_specs=[
            pl.BlockSpec((tm, d), row),
            pl.BlockSpec((tm, 1), row),
            pl.BlockSpec((1, d), fixed),
            pl.BlockSpec(w.shape, fixed),
            pl.BlockSpec(qnw.shape, fixed),
            pl.BlockSpec(knw.shape, fixed),
            pl.BlockSpec(cw.shape, fixed),
            pl.BlockSpec(pq.shape, fixed),
            pl.BlockSpec(pk.shape, fixed),
            pl.BlockSpec(freq.shape, fixed),
        ],
        out_specs=out_specs + [pl.BlockSpec((tm, wd), row) for wd, _ in widths],
        scratch_shapes=[pltpu.VMEM((tm + CONV_HALO, A_WIDTH), F32)],
        compiler_params=_cparams(("arbitrary",)),
        name="inproj",
    )(x, pos_col, g.reshape(1, d), w, qnw, knw, cw, pq, pk, freq)


def _gelu_tanh(x):
    return 0.5 * x * (1.0 + jnp.tanh(math.sqrt(2.0 / math.pi) * (x + 0.044715 * (x * x * x))))


def _compress_kernel(xk_ref, xv_ref, pos_ref, pe_ref, wk_ref, wv_ref, w2_ref, w2vt_ref, knw_ref, pk_ref, freq_ref,
                     kc_ref, vct_ref, *, ncp):
    def hidden(x_ref, pe_row, w_ref):
        x = jnp.concatenate([x_ref[pl.ds(j, ncp, stride=CMP_STRIDE), :] for j in range(CMP_STRIDE)], axis=1)
        a = _dot((x + pe_ref[pe_row:pe_row + 1, :]).astype(BF16), w_ref[0])
        b = _dot((x + pe_ref[pe_row + 1:pe_row + 2, :]).astype(BF16), w_ref[1])
        hid = a + pltpu.roll(b, ncp - 1, axis=0)
        return _gelu_tanh(hid)

    low = lax.broadcasted_iota(jnp.int32, (1, LANES), 1) < HEAD_DIM
    kc = _dot(hidden(xk_ref, 0, wk_ref).astype(BF16), w2_ref[...])
    kc = _group_rms(kc, knw_ref[0:1, :], pk_ref[...])
    cos_t, sin_t = _rope_tables(pos_ref[...].astype(F32), freq_ref[...])
    kc = _rope(kc, cos_t, sin_t)
    kc_ref[0] = jnp.where(low, kc, 0.0).astype(BF16)
    kc_ref[1] = jnp.where(low, pltpu.roll(kc, HEAD_DIM, axis=1), 0.0).astype(BF16)
    hv = hidden(xv_ref, 2, wv_ref)
    hv_t = jnp.concatenate(
        [jnp.concatenate([hv[r * LANES:(r + 1) * LANES, c * LANES:(c + 1) * LANES].T for r in range(ncp // LANES)],
                         axis=1) for c in range(hv.shape[1] // LANES)], axis=0)
    vct = _dot(w2vt_ref[...], hv_t.astype(BF16))
    for g in range(N_GROUPS):
        vct_ref[g] = vct[g * HEAD_DIM:(g + 1) * HEAD_DIM, :].astype(BF16)


def _compress(xk, xv, pos_c, pe, wk, wv, w2k, w2vt, knw, pk, freq, *, seq):
    bsz, ncp = pos_c.shape[:2]
    bat = lambda b: (b, 0, 0)
    fix2 = lambda b: (0, 0)
    fix3 = lambda b: (0, 0, 0)
    return pl.pallas_call(
        functools.partial(_compress_kernel, ncp=ncp),
        out_shape=[jax.ShapeDtypeStruct((bsz, N_GROUPS, ncp, LANES), BF16),
                   jax.ShapeDtypeStruct((bsz, N_GROUPS, HEAD_DIM, ncp), BF16)],
        grid=(bsz,),
        in_specs=[
            pl.BlockSpec((seq, KV_WIDTH), lambda b: (b, 0)),
            pl.BlockSpec((seq, KV_WIDTH), lambda b: (b, 0)),
            pl.BlockSpec((None, ncp, 1), bat),
            pl.BlockSpec(pe.shape, fix2),
            pl.BlockSpec(wk.shape, fix3),
            pl.BlockSpec(wv.shape, fix3),
            pl.BlockSpec(w2k.shape, fix2),
            pl.BlockSpec(w2vt.shape, fix2),
            pl.BlockSpec(knw.shape, fix2),
            pl.BlockSpec(pk.shape, fix2),
            pl.BlockSpec(freq.shape, fix2),
        ],
        out_specs=[pl.BlockSpec((None, N_GROUPS, ncp, LANES), lambda b: (b, 0, 0, 0)),
                   pl.BlockSpec((None, N_GROUPS, HEAD_DIM, ncp), lambda b: (b, 0, 0, 0))],
        compiler_params=_cparams(("parallel",)),
        name="compress",
    )(xk, xv, pos_c, pe, wk, wv, w2k, w2vt, knw, pk, freq)


def _head_columns(qt_ref):
    cols = []
    for pair in range(qt_ref.shape[0] // LANES):
        both = qt_ref[pair * LANES:(pair + 1) * LANES, :]
        cols += [both, jnp.concatenate([both[HEAD_DIM:, :], both[:HEAD_DIM, :]], axis=0)]
    return cols


def _software_pipeline(n, scores, probs, finish, after=None):
    s, p = {}, {}
    for step in range(n + 2):
        if step >= 2:
            finish(step - 2, p.pop(step - 2))
        if 1 <= step <= n:
            p[step - 1] = probs(step - 1, s.pop(step - 1))
        if step < n:
            s[step] = scores(step)
        for extra in (after or {}).get(step, ()):
            extra()


def _gate_row(gt_ref, branch, hd, group=0):
    r = group * GATE_ROWS + branch * HEADS_PER_GROUP + hd
    return gt_ref[r:r + 1, :]


def _cmp_kernel(qt_ref, kc_ref, vct_ref, ovt_ref, kwin_ref, vtwin_ref, band_ref, gt_ref, ot_ref, sbt_ref, *, tq, ncp):
    i = pl.program_id(1)
    span = WINDOW + tq
    wstart = pl.multiple_of(jnp.maximum(i - WINDOW // tq, 0) * tq, tq)
    t = i * tq + lax.broadcasted_iota(jnp.int32, (1, tq), 1)
    any_valid = (t >= CMP_BLOCK - 1).astype(F32)
    heads = _head_columns(qt_ref)
    cmp_per_sel = SEL_BLOCK // CMP_STRIDE
    groups = range(N_GROUPS)

    def run(nrow):
        nblk = nrow // cmp_per_sel
        blk_end = lax.broadcasted_iota(jnp.int32, (nrow, 1), 0) * CMP_STRIDE + (CMP_BLOCK - 1)
        valid = blk_end <= t
        kc = [kc_ref[g, 0:nrow, :] for g in groups]
        vct = [vct_ref[g, :, 0:nrow] for g in groups]
        psum_parts = [[] for _ in groups]

        def scores(unit):
            u = unit % N_HEADS
            if unit < N_HEADS:
                return jnp.where(valid, _dot(kc[u // HEADS_PER_GROUP], heads[u]), NEG)
            k = kwin_ref[u // HEADS_PER_GROUP, pl.ds(wstart, span), :]
            return _dot(k, heads[u]) + band_ref[...]

        def probs(unit, s):
            e = jnp.exp(s - jnp.max(s, axis=0, keepdims=True))
            if unit < N_HEADS:
                return e * (any_valid / jnp.sum(e, axis=0, keepdims=True))
            return e.astype(BF16)

        def finish(unit, p):
            u = unit % N_HEADS
            g, hd = divmod(u, HEADS_PER_GROUP)
            rows = slice(u * HEAD_DIM, (u + 1) * HEAD_DIM)
            if unit < N_HEADS:
                ot_ref[rows, :] = _dot(vct[g], p.astype(BF16)) * _gate_row(gt_ref, _BR_CMP, hd, g)
                psum_parts[g].append(p)
            else:
                acc = _dot(vtwin_ref[g, :, pl.ds(wstart, span)], p)
                scale = _gate_row(gt_ref, _BR_WIN, hd, g) / acc[HEAD_DIM:HEAD_DIM + 1, :]
                ot_ref[rows, :] = ot_ref[rows, :] + acc[:HEAD_DIM, :] * scale

        blk = lax.broadcasted_iota(jnp.int32, (nblk, 1), 0)
        blk_f = blk.astype(F32)
        imp = []
        bias = [jnp.full((nblk, tq), NEG, F32) for _ in groups]

        def importance():
            ovt = ovt_ref[0:nblk, 0:nrow]
            sel_start = blk * SEL_BLOCK
            cur = (t >> int(math.log2(SEL_BLOCK))) << int(math.log2(SEL_BLOCK))
            for parts in psum_parts:
                psum = (parts[0] + parts[1]) + (parts[2] + parts[3])
                hi = psum.astype(BF16)
                lo = (psum - hi.astype(F32)).astype(BF16)
                x = jnp.where(sel_start <= t, _dot(ovt, hi) + _dot(ovt, lo), -1.0)
                imp.append(jnp.where(sel_start == cur, 1e4, jnp.where(sel_start == 0, 1e4, x)))

        def topk_round():
            for g in groups:
                mx = jnp.max(imp[g], axis=0, keepdims=True)
                first = jnp.min(jnp.where(imp[g] == mx, blk_f, float(LANES)), axis=0, keepdims=True)
                pick = blk_f == first
                bias[g] = jnp.where(pick, 0.0, bias[g])
                imp[g] = jnp.where(pick, -3e38, imp[g])

        last_cmp = N_HEADS + 1
        rest = range(last_cmp + 1, 2 * N_HEADS + 2)
        after = {last_cmp: [importance]}
        for r in range(SEL_TOPK):
            after.setdefault(rest[r * len(rest) // SEL_TOPK], []).append(topk_round)
        _software_pipeline(2 * N_HEADS, scores, probs, finish, after)
        for g in groups:
            sbt_ref[g, 0:nblk, :] = bias[g].astype(BF16)
            if nblk < LANES:
                sbt_ref[g, nblk:, :] = jnp.full((LANES - nblk, tq), NEG, BF16)

    sizes = [LANES * (v + 1) for v in range(ncp // LANES)]
    assert sizes[0] // cmp_per_sel >= SEL_TOPK
    tokens_per_variant = LANES * CMP_STRIDE
    lax.switch((i * tq + tq - 1) // tokens_per_variant, [functools.partial(run, nrow) for nrow in sizes])


def _window_band(tq):
    edge = WINDOW // tq
    v = jnp.arange(edge + 1)[:, None, None]
    kpos = jnp.maximum(v - edge, 0) * tq + jnp.arange(WINDOW + tq)[None, :, None]
    t = v * tq + jnp.arange(tq)[None, None, :]
    ok = (t - kpos >= 0) & (t - kpos < WINDOW)
    return jnp.where(ok, 0.0, NEG).astype(F32)


def _cmp_win_select(qt, kc, vct, ovt, k_win, vt, gates_t, *, bsz, seq, tq=256):
    ncp = kc.shape[2]
    nq = seq // tq
    band = _window_band(tq)
    last = band.shape[0] - 1
    whole = lambda b, i: (b, 0, 0, 0)
    return pl.pallas_call(
        functools.partial(_cmp_kernel, tq=tq, ncp=ncp),
        out_shape=[jax.ShapeDtypeStruct((bsz, A_WIDTH, seq), F32),
                   jax.ShapeDtypeStruct((bsz, N_GROUPS, LANES, seq), BF16)],
        grid=(bsz, nq),
        in_specs=[
            pl.BlockSpec((None, A_WIDTH, tq), lambda b, i: (b, 0, i)),
            pl.BlockSpec((None, N_GROUPS, ncp, LANES), whole),
            pl.BlockSpec((None, N_GROUPS, HEAD_DIM, ncp), whole),
            pl.BlockSpec(ovt.shape, lambda b, i: (0, 0)),
            pl.BlockSpec((None, N_GROUPS, seq, LANES), whole),
            pl.BlockSpec((None, None, N_GROUPS, V_ROWS, seq), lambda b, i: (b, 1, 0, 0, 0)),
            pl.BlockSpec((None,) + band.shape[1:], lambda b, i: (jnp.minimum(i, last), 0, 0)),
            pl.BlockSpec((N_GROUPS * GATE_ROWS, tq), lambda b, i: (0, b * nq + i)),
        ],
        out_specs=[pl.BlockSpec((None, A_WIDTH, tq), lambda b, i: (b, 0, i)),
                   pl.BlockSpec((None, N_GROUPS, LANES, tq), lambda b, i: (b, 0, 0, i))],
        compiler_params=_cparams(("parallel", "parallel")),
        name="cmp_win_select",
    )(qt, kc, vct, ovt, k_win, vt, band, gates_t)


_SLC_UNROLL = 2


def _slc_kernel(qt_ref, sbt_ref, k_ref, vt_ref, gt_ref, prev_ref, o_ref,
                qa_ref, s_ref, p_ref, acc_ref, *, tq):
    i = pl.program_id(2)
    rows = HEADS_PER_GROUP * tq
    qa_ref[...] = jnp.concatenate([jnp.concatenate([sbt_ref[...]] * HEADS_PER_GROUP, axis=1),
                                   jnp.concatenate(_head_columns(qt_ref), axis=1)], axis=0)
    @pl.when((pl.program_id(0) == 0) & (pl.program_id(1) == 0) & (i == 0))
    def _():
        p_ref[...] = jnp.zeros(p_ref.shape, BF16)
        acc_ref[...] = jnp.zeros(acc_ref.shape, F32)

    def accumulate(unit, alpha):
        j = jnp.where(unit <= 0, i, unit - 1)
        pv = _dot(vt_ref[:, pl.ds(pl.multiple_of(j * tq, tq), tq)], p_ref[...])
        acc_ref[...] = alpha * acc_ref[...] + pv

    def probs(m):
        s = s_ref[...]
        m_new = jnp.maximum(m, jnp.max(s, axis=0, keepdims=True))
        p_ref[...] = jnp.exp(s - m_new).astype(BF16)
        return m_new, jnp.exp(m - m_new)

    def scores(j, masked):
        s = _dot(k_ref[pl.ds(pl.multiple_of(j * tq, tq), tq), :], qa_ref[...])
        if masked:
            causal = (lax.broadcasted_iota(jnp.int32, (tq, 1), 0)
                      <= (lax.broadcasted_iota(jnp.int32, (1, rows), 1) & (tq - 1)))
            s = jnp.where(causal, s, NEG)
        s_ref[...] = s

    def body(k, carry):
        m, alpha = carry
        accumulate(k - 1, alpha)
        m, alpha = probs(m)
        scores(k, False)
        return m, alpha

    def steps(first, count, carry):
        for k in range(count):
            carry = body(first + k, carry)
        return carry

    scores(i, True)
    carry = (jnp.full((1, rows), -3e38, F32), jnp.ones((1, rows), F32))
    trips = i >> int(math.log2(_SLC_UNROLL))
    carry = lax.fori_loop(0, trips, lambda k, c: steps(k * _SLC_UNROLL, _SLC_UNROLL, c), carry)
    done = trips * _SLC_UNROLL
    size = _SLC_UNROLL // 2
    while size >= 1:
        carry = lax.cond((i & size) != 0, functools.partial(steps, done, size), lambda c: c, carry)
        done = done + (i & size)
        size //= 2
    m, alpha = carry
    accumulate(i - 1, alpha)
    m, alpha = probs(m)
    accumulate(i, alpha)

    gate = jnp.concatenate([_gate_row(gt_ref, _BR_SLC, hd) for hd in range(HEADS_PER_GROUP)], axis=1)
    out_t = acc_ref[0:HEAD_DIM, :] * (gate / acc_ref[HEAD_DIM:HEAD_DIM + 1, :])
    for pair in range(HEADS_PER_GROUP // 2):
        for c in range(tq // LANES):
            col = 2 * pair * tq + c * LANES
            both = jnp.concatenate([out_t[:, col:col + LANES], out_t[:, col + tq:col + tq + LANES]], axis=0)
            both = both + prev_ref[pair * LANES:(pair + 1) * LANES, c * LANES:(c + 1) * LANES]
            o_ref[c * LANES:(c + 1) * LANES, pair * LANES:(pair + 1) * LANES] = both.T.astype(BF16)


def _slc(qt, sbt, k_aug, vt_aug, gates_t, prev, *, bsz, seq, tq=512):
    nq = seq // tq
    gw = HEADS_PER_GROUP * HEAD_DIM
    rows = HEADS_PER_GROUP * tq
    tmap = lambda b, g, i: (b, g, i)
    return pl.pallas_call(
        functools.partial(_slc_kernel, tq=tq),
        out_shape=jax.ShapeDtypeStruct((bsz * seq, A_WIDTH), BF16),
        grid=(bsz, N_GROUPS, nq),
        in_specs=[
            pl.BlockSpec((None, gw, tq), tmap),
            pl.BlockSpec((None, None, LANES, tq), lambda b, g, i: (b, g, 0, i)),
            pl.BlockSpec((None, None, seq, 2 * LANES), lambda b, g, i: (b, g, 0, 0)),
            pl.BlockSpec((None, None, None, V_ROWS, seq), lambda b, g, i: (b, 0, g, 0, 0)),
            pl.BlockSpec((GATE_ROWS, tq), lambda b, g, i: (g, b * nq + i)),
            pl.BlockSpec((None, gw, tq), tmap),
        ],
        out_specs=pl.BlockSpec((tq, gw), lambda b, g, i: (b * nq + i, g)),
        scratch_shapes=[pltpu.VMEM((2 * LANES, rows), BF16), pltpu.VMEM((tq, rows), F32),
                        pltpu.VMEM((tq, rows), BF16), pltpu.VMEM((V_ROWS, rows), F32)],
        compiler_params=_cparams(("arbitrary", "arbitrary", "arbitrary")),
        name="slc_attn",
    )(qt, sbt, k_aug, vt_aug, gates_t, prev)


def _outproj_kernel(x_ref, oa_ref, ob_ref, w_ref, o_ref):
    y = _dot(oa_ref[...], w_ref[0:A_WIDTH, :]) + _dot(ob_ref[...], w_ref[A_WIDTH:, :])
    o_ref[...] = x_ref[...] + y


def _outproj(x, oa, ob, w, *, tm=1024):
    n, d = x.shape
    row = lambda i: (i, 0)
    return pl.pallas_call(
        _outproj_kernel,
        out_shape=jax.ShapeDtypeStruct((n, d), F32),
        grid=(n // tm,),
        in_specs=[
            pl.BlockSpec((tm, d), row),
            pl.BlockSpec((tm, A_WIDTH), row),
            pl.BlockSpec((tm, A_WIDTH), row),
            pl.BlockSpec(w.shape, lambda i: (0, 0)),
        ],
        out_specs=pl.BlockSpec((tm, d), row),
        compiler_params=_cparams(("parallel",)),
        name="outproj",
    )(x, oa, ob, w)


def _pool_kernel(x_ref, halo_ref, g_ref, w_ref, sc_ref, o_ref, ext_ref, tmp_ref, *, tm, tiles_per_batch):
    i = pl.program_id(0)
    first = i % tiles_per_batch == 0
    x = x_ref[...]
    h = _rms(x, g_ref[...])
    hh = _rms(halo_ref[...], g_ref[...])
    pad, body = _POOL_PAD, _POOL_PAD + POOL_HALO
    ext_ref[0:pad, :] = jnp.zeros((pad, h.shape[1]), F32)
    ext_ref[pad:body, :] = jnp.where(first, 0.0, hh)
    ext_ref[body:body + tm, :] = h
    gw = h.shape[1] // len(POOL_WINDOWS)
    for slot in range(2):
        tmp_ref[slot, 0:pad, :] = jnp.zeros((pad, gw), F32)
    t = (i % tiles_per_batch) * tm + lax.broadcasted_iota(jnp.int32, (tm, 1), 0)
    span = POOL_HALO + tm
    for gi, win in enumerate(POOL_WINDOWS):
        c0 = gi * gw
        hg = h[:, c0:c0 + gw]
        read = lambda start, c0=c0: ext_ref[pl.ds(start, span), c0:c0 + gw]
        shift, slot = 1, 0
        while shift < win:
            tmp_ref[slot, pad:pad + span, :] = read(pad) + read(pad - shift)
            read = lambda start, slot=slot: tmp_ref[slot, pl.ds(start, span), :]
            shift, slot = 2 * shift, 1 - slot
        tot = read(pad)[POOL_HALO:, :]
        cnt = jnp.minimum(t + 1, win).astype(F32)
        y = _dot((tot / cnt - hg).astype(BF16), w_ref[gi])
        o_ref[:, c0:c0 + gw] = x[:, c0:c0 + gw] + y * sc_ref[:, c0:c0 + gw]


def _pool(x, g, w, scale, *, seq, tm=512):
    n, d = x.shape
    hb = tm // POOL_HALO
    return pl.pallas_call(
        functools.partial(_pool_kernel, tm=tm, tiles_per_batch=seq // tm),
        out_shape=jax.ShapeDtypeStruct((n, d), F32),
        grid=(n // tm,),
        in_specs=[
            pl.BlockSpec((tm, d), lambda i: (i, 0)),
            pl.BlockSpec((POOL_HALO, d), lambda i: (jnp.maximum(i * hb - 1, 0), 0)),
            pl.BlockSpec((1, d), lambda i: (0, 0)),
            pl.BlockSpec(w.shape, lambda i: (0, 0, 0)),
            pl.BlockSpec((1, d), lambda i: (0, 0)),
        ],
        out_specs=pl.BlockSpec((tm, d), lambda i: (i, 0)),
        scratch_shapes=[pltpu.VMEM((_POOL_PAD + POOL_HALO + tm, d), F32),
                        pltpu.VMEM((2, _POOL_PAD + POOL_HALO + tm, d // len(POOL_WINDOWS)), F32)],
        compiler_params=_cparams(("parallel",)),
        name="pool_mixer",
    )(x, x, g.reshape(1, d), w, scale.reshape(1, d))


def _pack_w_in(w_in):
    o = 0
    q = w_in[:, o:o + A_WIDTH]; o += A_WIDTH
    kc, vc, ks, vs, kw, vw = [w_in[:, o + k * KV_WIDTH:o + (k + 1) * KV_WIDTH] for k in range(6)]
    o += 6 * KV_WIDTH
    gates = w_in[:, o:o + N_BRANCH * N_HEADS]; o += N_BRANCH * N_HEADS
    u, gb, gc = [w_in[:, o + k * A_WIDTH:o + (k + 1) * A_WIDTH] for k in range(3)]
    gates = gates.reshape(-1, N_GROUPS, HEADS_PER_GROUP, N_BRANCH).transpose(0, 1, 3, 2)
    gates = gates.reshape(-1, N_GROUPS, N_BRANCH * HEADS_PER_GROUP)
    gates = jnp.pad(gates, ((0, 0), (0, 0), (0, LANES - N_BRANCH * HEADS_PER_GROUP))).reshape(-1, GATE_WIDTH)
    return jnp.concatenate([q, ks, kw, vs, vw, kc, vc, gates, u, gb, gc], axis=1).astype(BF16)


def _group_mean_matrix(width):
    r = jnp.arange(width) // HEAD_DIM
    return jnp.where(r[:, None] == r[None, :], 1.0 / HEAD_DIM, 0.0).astype(BF16)


def _rope_freq_row():
    lane = jnp.arange(LANES) % HEAD_DIM
    freqs = ROPE_THETA ** (-jnp.arange(0, ROPE_DIM, 2, dtype=F32) / ROPE_DIM)
    return jnp.where(lane < ROPE_DIM, freqs[lane % ROPE_HALF], 0.0).astype(F32).reshape(1, LANES)


def _hybrid_mixer(x, positions, g, w_in, q_norm, k_norm, cmp_pos, cmp_w1, cmp_w2, conv_w, w_out, *, bsz, seq):
    n = bsz * seq
    ncp = seq // CMP_STRIDE
    freq = _rope_freq_row()
    pq = _group_mean_matrix(A_WIDTH)
    pk = _group_mean_matrix(KV_WIDTH)
    qnw = jnp.tile(q_norm, N_HEADS).reshape(1, A_WIDTH)
    knw = jnp.tile(k_norm, (1, N_GROUPS))

    q_t, k_slc, k_win, v_t, kc_raw, vc_raw, gates, ob = _inproj(
        x, positions.reshape(n, 1), g, _pack_w_in(w_in), qnw, knw, conv_w, pq, pk, freq, seq=seq)

    eye_g = jnp.eye(N_GROUPS, dtype=F32)
    seg = CMP_STRIDE * KV_WIDTH

    def seg_weights(w1):
        w1r = w1.reshape(CMP_BLOCK, HEAD_DIM, CMP_HIDDEN)
        halves = [jnp.einsum('jdh,gk->jgdkh', w1r[a:a + CMP_STRIDE], eye_g).reshape(seg, N_GROUPS * CMP_HIDDEN)
                  for a in (0, CMP_STRIDE)]
        return jnp.stack(halves).astype(BF16)

    def seg_pos(pe):
        return [jnp.broadcast_to(pe[a:a + CMP_STRIDE, None, :], (CMP_STRIDE, N_GROUPS, HEAD_DIM)).reshape(seg)
                for a in (0, CMP_STRIDE)]

    pe = jnp.stack(seg_pos(cmp_pos[0]) + seg_pos(cmp_pos[1]))
    w2 = jnp.stack([jnp.einsum('hd,gk->ghkd', cmp_w2[a], eye_g).reshape(N_GROUPS * CMP_HIDDEN, KV_WIDTH)
                    for a in range(2)]).astype(BF16)
    blk_end = jnp.minimum(jnp.arange(ncp) * CMP_STRIDE + CMP_BLOCK - 1, seq - 1)
    pos_c = positions[:, blk_end].reshape(bsz, ncp, 1)
    kc_g, vc_t = _compress(kc_raw, vc_raw, pos_c, pe, seg_weights(cmp_w1[0]), seg_weights(cmp_w1[1]), w2[0], w2[1].T,
                           knw, pk, freq, seq=seq)

    gates_t = gates.reshape(n, N_GROUPS, LANES)[:, :, :GATE_ROWS].transpose(1, 2, 0).reshape(N_GROUPS * GATE_ROWS, n)
    cstart = jnp.arange(ncp) * CMP_STRIDE
    sstart = jnp.arange(LANES) * SEL_BLOCK
    ovt = ((cstart[None, :] < sstart[:, None] + SEL_BLOCK)
           & (cstart[None, :] + CMP_BLOCK > sstart[:, None])).astype(BF16)
    o_t, sel_bias = _cmp_win_select(q_t, kc_g, vc_t, ovt, k_win, v_t, gates_t, bsz=bsz, seq=seq)
    o_a = _slc(q_t, sel_bias, k_slc, v_t, gates_t, o_t, bsz=bsz, seq=seq)

    return _outproj(x, o_a, ob, w_out.astype(BF16))


def kernel(x, positions, ffn_norm, ffn_w_gate, ffn_w_up, ffn_w_down, mix_norm, hyb_w_in, hyb_q_norm, hyb_k_norm,
           hyb_cmp_pos, hyb_cmp_w1, hyb_cmp_w2, hyb_conv_w, hyb_w_out, pool_w, pool_scale):
    bsz, seq, d = x.shape
    depth = ffn_norm.shape[0]
    assert seq % 2048 == 0 and seq // SEL_BLOCK <= LANES and seq // SEL_BLOCK >= SEL_TOPK
    wg, wu, wd = (w.astype(BF16) for w in (ffn_w_gate, ffn_w_up, ffn_w_down))
    xf = x.reshape(bsz * seq, d)
    for layer in range(depth):
        xf = _ffn(xf, ffn_norm[layer, 0], wg, wu, wd, layer, 0)
        i = layer // 2
        if layer % 2 == 0:
            xf = _hybrid_mixer(xf, positions, mix_norm[layer], hyb_w_in[i], hyb_q_norm[i], hyb_k_norm[i],
                               hyb_cmp_pos[i], hyb_cmp_w1[i], hyb_cmp_w2[i], hyb_conv_w[i], hyb_w_out[i],
                               bsz=bsz, seq=seq)
        else:
            xf = _pool(xf, mix_norm[layer], pool_w[i].astype(BF16), pool_scale[i], seq=seq)
        xf = _ffn(xf, ffn_norm[layer, 1], wg, wu, wd, layer, 1)
    return xf.reshape(bsz, seq, d)
```

```python
import functools
import math

import jax
import jax.numpy as jnp
from jax import lax
from jax.experimental import pallas as pl
from jax.experimental.pallas import tpu as pltpu

F32 = jnp.float32
BF16 = jnp.bfloat16

HEAD_DIM = 64
N_GROUPS = 2
HEADS_PER_GROUP = 4
N_HEADS = N_GROUPS * HEADS_PER_GROUP
A_WIDTH = N_HEADS * HEAD_DIM
KV_WIDTH = N_GROUPS * HEAD_DIM
ROPE_DIM = HEAD_DIM // 4
ROPE_HALF = ROPE_DIM // 2
ROPE_THETA = 500000.0
CMP_BLOCK = 32
CMP_STRIDE = 16
CMP_HIDDEN = 2 * HEAD_DIM
SEL_BLOCK = 64
SEL_TOPK = 16
WINDOW = 512
N_BRANCH = 3
_BR_CMP, _BR_SLC, _BR_WIN = 0, 1, 2
CONV_WIDTH = 3
POOL_WINDOWS = (2, 4, 8, 16)
POOL_HALO = 16
_POOL_PAD = 8
EPS = 1e-6
NEG = -1e30

LANES = 128
VMEM_LIMIT_BYTES = 56 * 1024 * 1024


def _cparams(sem):
    return pltpu.CompilerParams(dimension_semantics=sem, vmem_limit_bytes=VMEM_LIMIT_BYTES)


def _dot(a, b):
    return jnp.dot(a, b, preferred_element_type=F32)


def _dot_split(a, b):
    hi = a.astype(BF16)
    lo = (a - hi.astype(F32)).astype(BF16)
    return _dot(hi, b) + _dot(lo, b)


def _rms(x, g):
    ms = jnp.mean(x * x, axis=-1, keepdims=True)
    return x * lax.rsqrt(ms + EPS) * g


def _group_rms(x, g, pmat):
    ms = _dot_split(x * x, pmat)
    return x * lax.rsqrt(ms + EPS) * g


def _rope_tables(pos_col, freq_row):
    ang = pos_col * freq_row
    lane = lax.broadcasted_iota(jnp.int32, (1, LANES), 1) & (HEAD_DIM - 1)
    sign = jnp.where(lane < ROPE_HALF, -1.0, 1.0).astype(F32)
    return jnp.cos(ang), jnp.sin(ang) * sign


def _rope(x, cos_t, sin_t):
    w = x.shape[1]
    reps = w // LANES
    if reps > 1:
        cos_t = jnp.concatenate([cos_t] * reps, axis=1)
        sin_t = jnp.concatenate([sin_t] * reps, axis=1)
    lane = lax.broadcasted_iota(jnp.int32, (1, w), 1) & (HEAD_DIM - 1)
    partner = jnp.where(lane < ROPE_HALF,
                        pltpu.roll(x, w - ROPE_HALF, axis=1),
                        pltpu.roll(x, ROPE_HALF, axis=1))
    return x * cos_t + partner * sin_t


def _ffn_kernel(x_ref, xnext_ref, g_ref, wg_ref, wu_ref, wd_ref, o_ref, h_ref, acc_ref, *, nj):
    i, j = pl.program_id(0), pl.program_id(1)
    slot = i % 2

    @pl.when((i == 0) & (j == 0))
    def _():
        h_ref[0] = _rms(x_ref[...], g_ref[...]).astype(BF16)

    def step(first, last):
        h = h_ref[slot]
        a = _dot(h, wg_ref[...])
        b = _dot(h, wu_ref[...])
        act = a * (1.0 / (1.0 + jnp.exp(-a))) * b
        y = _dot(act.astype(BF16), wd_ref[...])
        if not first:
            y = acc_ref[...] + y
        if last:
            h_ref[1 - slot] = _rms(xnext_ref[...], g_ref[...]).astype(BF16)
            o_ref[...] = x_ref[...] + 0.5 * y
        else:
            acc_ref[...] = y

    for jj in range(nj):
        pl.when(j == jj)(functools.partial(step, jj == 0, jj == nj - 1))


def _ffn(x, g, wg, wu, wd, layer, half, *, tm=512, tf=2816):
    n, d = x.shape
    dff = wg.shape[-1]
    ni, nj = n // tm, dff // tf
    wmode = dict(pipeline_mode=pl.Buffered(1)) if nj == 1 else {}
    return pl.pallas_call(
        functools.partial(_ffn_kernel, nj=nj),
        out_shape=jax.ShapeDtypeStruct((n, d), F32),
        grid=(ni, nj),
        in_specs=[
            pl.BlockSpec((tm, d), lambda i, j: (i, 0)),
            pl.BlockSpec((tm, d), lambda i, j: (jnp.minimum(i + 1, ni - 1), 0)),
            pl.BlockSpec((1, d), lambda i, j: (0, 0)),
            pl.BlockSpec((None, None, d, tf), lambda i, j: (layer, half, 0, j), **wmode),
            pl.BlockSpec((None, None, d, tf), lambda i, j: (layer, half, 0, j), **wmode),
            pl.BlockSpec((None, None, tf, d), lambda i, j: (layer, half, j, 0), **wmode),
        ],
        out_specs=pl.BlockSpec((tm, d), lambda i, j: (i, 0)),
        scratch_shapes=[pltpu.VMEM((2, tm, d), BF16), pltpu.VMEM((tm, d), F32)],
        compiler_params=_cparams(("arbitrary", "arbitrary")),
        name="ffn",
    )(x, x, g.reshape(1, d), wg, wu, wd)


GATE_WIDTH = N_GROUPS * LANES
GATE_ROWS = 16
_C_Q = 0
_C_K = _C_Q + A_WIDTH
_C_V = _C_K + 2 * KV_WIDTH
_C_C = _C_V + 2 * KV_WIDTH
_C_GATE = _C_C + 2 * KV_WIDTH
_C_U = _C_GATE + GATE_WIDTH
_C_GB = _C_U + A_WIDTH
_C_GC = _C_GB + A_WIDTH
CONV_HALO = 8
V_ROWS = HEAD_DIM + 16


def _inproj_kernel(x_ref, pos_ref, g_ref, w_ref, qnw_ref, knw_ref, cw_ref, pq_ref, pk_ref, freq_ref,
                   qt_ref, kslc_ref, kwin_ref, vt_ref, kc_ref, vc_ref, gates_ref, ob_ref,
                   vext_ref, *, tm, tiles_per_batch):
    i = pl.program_id(0)
    chunks = range(tm // LANES)

    def transposed(a, c):
        return jnp.concatenate([a[r * LANES:(r + 1) * LANES, c * LANES:(c + 1) * LANES].T for r in chunks], axis=1)

    def per_group(k):
        low = lax.broadcasted_iota(jnp.int32, (1, LANES), 1) < HEAD_DIM
        return [jnp.where(low, k, 0.0), jnp.where(low, pltpu.roll(k, HEAD_DIM, axis=1), 0.0)]

    @pl.when(i % tiles_per_batch == 0)
    def _():
        vext_ref[0:CONV_HALO, :] = jnp.zeros((CONV_HALO, A_WIDTH), F32)

    h = _rms(x_ref[...], g_ref[...]).astype(BF16)

    def proj(c0, width):
        return _dot(h, w_ref[:, c0:c0 + width])

    q_raw = proj(_C_Q, A_WIDTH)
    k2 = proj(_C_K, 2 * KV_WIDTH)
    cos_t, sin_t = _rope_tables(pos_ref[...].astype(F32), freq_ref[...])
    v = proj(_C_GC, A_WIDTH) * proj(_C_U, A_WIDTH)
    gate_b = proj(_C_GB, A_WIDTH)

    q = _group_rms(q_raw, qnw_ref[...], pq_ref[...])
    q = _rope(q, cos_t, sin_t) * (HEAD_DIM ** -0.5)
    for c in range(A_WIDTH // LANES):
        qt_ref[c * LANES:(c + 1) * LANES, :] = transposed(q, c).astype(BF16)

    tok = (i % tiles_per_batch) * tm + lax.broadcasted_iota(jnp.int32, (tm, 1), 0)
    blk_onehot = (lax.broadcasted_iota(jnp.int32, (1, LANES), 1) == (tok >> int(math.log2(SEL_BLOCK))))
    blk_onehot = jnp.where(blk_onehot, 1.0, 0.0).astype(BF16)
    ks = _rope(_group_rms(k2[:, :KV_WIDTH], knw_ref[1:2, :], pk_ref[...]), cos_t, sin_t)
    for g, kg in enumerate(per_group(ks)):
        kslc_ref[g] = jnp.concatenate([blk_onehot, kg.astype(BF16)], axis=1)
    kw = _rope(_group_rms(k2[:, KV_WIDTH:], knw_ref[2:3, :], pk_ref[...]), cos_t, sin_t)
    for g, kg in enumerate(per_group(kw)):
        kwin_ref[g] = kg.astype(BF16)

    v2 = proj(_C_V, 2 * KV_WIDTH)
    ones_rows = jnp.where(lax.broadcasted_iota(jnp.int32, (V_ROWS - HEAD_DIM, 1), 0) == 0, 1.0, 0.0)
    ones_rows = jnp.broadcast_to(ones_rows, (V_ROWS - HEAD_DIM, tm)).astype(BF16)
    for branch in range(2):
        vt = transposed(v2, branch).astype(BF16)
        for g in range(N_GROUPS):
            vt_ref[branch, g, 0:HEAD_DIM, :] = vt[g * HEAD_DIM:(g + 1) * HEAD_DIM, :]
            vt_ref[branch, g, HEAD_DIM:, :] = ones_rows
    c2 = proj(_C_C, 2 * KV_WIDTH)
    kc_ref[...] = c2[:, :KV_WIDTH]
    vc_ref[...] = c2[:, KV_WIDTH:]
    gate_logits = proj(_C_GATE, GATE_WIDTH)

    vext_ref[CONV_HALO:CONV_HALO + tm, :] = v
    y = (cw_ref[2:3, :] * v
         + cw_ref[1:2, :] * vext_ref[pl.ds(CONV_HALO - 1, tm), :]
         + cw_ref[0:1, :] * vext_ref[pl.ds(CONV_HALO - 2, tm), :])
    ob_ref[...] = (gate_b * y).astype(BF16)
    vext_ref[0:CONV_HALO, :] = vext_ref[tm:tm + CONV_HALO, :]
    gates_ref[...] = 1.0 / (1.0 + jnp.exp(-gate_logits))


def _inproj(x, pos_col, g, w, qnw, knw, cw, pq, pk, freq, *, seq, tm=512):
    n, d = x.shape
    bsz, tpb = n // seq, seq // tm
    row = lambda i: (i, 0)
    fixed = lambda i: (0, 0)
    widths = [(KV_WIDTH, F32), (KV_WIDTH, F32), (GATE_WIDTH, F32), (A_WIDTH, BF16)]
    out_shape = [jax.ShapeDtypeStruct((bsz, A_WIDTH, seq), BF16),
                 jax.ShapeDtypeStruct((bsz, N_GROUPS, seq, 2 * LANES), BF16),
                 jax.ShapeDtypeStruct((bsz, N_GROUPS, seq, LANES), BF16),
                 jax.ShapeDtypeStruct((bsz, 2, N_GROUPS, V_ROWS, seq), BF16)]
    out_specs = [pl.BlockSpec((None, A_WIDTH, tm), lambda i: (i // tpb, 0, i % tpb)),
                 pl.BlockSpec((None, N_GROUPS, tm, 2 * LANES), lambda i: (i // tpb, 0, i % tpb, 0)),
                 pl.BlockSpec((None, N_GROUPS, tm, LANES), lambda i: (i // tpb, 0, i % tpb, 0)),
                 pl.BlockSpec((None, 2, N_GROUPS, V_ROWS, tm), lambda i: (i // tpb, 0, 0, 0, i % tpb))]
    return pl.pallas_call(
        functools.partial(_inproj_kernel, tm=tm, tiles_per_batch=tpb),
        out_shape=out_shape + [jax.ShapeDtypeStruct((n, wd), dt) for wd, dt in widths],
        grid=(n // tm,),
        in_specs=[
            pl.BlockSpec((tm, d), row),
            pl.BlockSpec((tm, 1), row),
            pl.BlockSpec((1, d), fixed),
            pl.BlockSpec(w.shape, fixed),
            pl.BlockSpec(qnw.shape, fixed),
            pl.BlockSpec(knw.shape, fixed),
            pl.BlockSpec(cw.shape, fixed),
            pl.BlockSpec(pq.shape, fixed),
            pl.BlockSpec(pk.shape, fixed),
            pl.BlockSpec(freq.shape, fixed),
        ],
        out_specs=out_specs + [pl.BlockSpec((tm, wd), row) for wd, _ in widths],
        scratch_shapes=[pltpu.VMEM((tm + CONV_HALO, A_WIDTH), F32)],
        compiler_params=_cparams(("arbitrary",)),
        name="inproj",
    )(x, pos_col, g.reshape(1, d), w, qnw, knw, cw, pq, pk, freq)


def _gelu_tanh(x):
    return 0.5 * x * (1.0 + jnp.tanh(math.sqrt(2.0 / math.pi) * (x + 0.044715 * (x * x * x))))


def _compress_kernel(xk_ref, xv_ref, pos_ref, pe_ref, wk_ref, wv_ref, w2_ref, w2vt_ref, knw_ref, pk_ref, freq_ref,
                     kc_ref, vct_ref, *, ncp):
    def hidden(x_ref, pe_row, w_ref):
        x = jnp.concatenate([x_ref[pl.ds(j, ncp, stride=CMP_STRIDE), :] for j in range(CMP_STRIDE)], axis=1)
        a = _dot((x + pe_ref[pe_row:pe_row + 1, :]).astype(BF16), w_ref[0])
        b = _dot((x + pe_ref[pe_row + 1:pe_row + 2, :]).astype(BF16), w_ref[1])
        hid = a + pltpu.roll(b, ncp - 1, axis=0)
        return _gelu_tanh(hid)

    low = lax.broadcasted_iota(jnp.int32, (1, LANES), 1) < HEAD_DIM
    kc = _dot(hidden(xk_ref, 0, wk_ref).astype(BF16), w2_ref[...])
    kc = _group_rms(kc, knw_ref[0:1, :], pk_ref[...])
    cos_t, sin_t = _rope_tables(pos_ref[...].astype(F32), freq_ref[...])
    kc = _rope(kc, cos_t, sin_t)
    kc_ref[0] = jnp.where(low, kc, 0.0).astype(BF16)
    kc_ref[1] = jnp.where(low, pltpu.roll(kc, HEAD_DIM, axis=1), 0.0).astype(BF16)
    hv = hidden(xv_ref, 2, wv_ref)
    hv_t = jnp.concatenate(
        [jnp.concatenate([hv[r * LANES:(r + 1) * LANES, c * LANES:(c + 1) * LANES].T for r in range(ncp // LANES)],
                         axis=1) for c in range(hv.shape[1] // LANES)], axis=0)
    vct = _dot(w2vt_ref[...], hv_t.astype(BF16))
    for g in range(N_GROUPS):
        vct_ref[g] = vct[g * HEAD_DIM:(g + 1) * HEAD_DIM, :].astype(BF16)


def _compress(xk, xv, pos_c, pe, wk, wv, w2k, w2vt, knw, pk, freq, *, seq):
    bsz, ncp = pos_c.shape[:2]
    bat = lambda b: (b, 0, 0)
    fix2 = lambda b: (0, 0)
    fix3 = lambda b: (0, 0, 0)
    return pl.pallas_call(
        functools.partial(_compress_kernel, ncp=ncp),
        out_shape=[jax.ShapeDtypeStruct((bsz, N_GROUPS, ncp, LANES), BF16),
                   jax.ShapeDtypeStruct((bsz, N_GROUPS, HEAD_DIM, ncp), BF16)],
        grid=(bsz,),
        in_specs=[
            pl.BlockSpec((seq, KV_WIDTH), lambda b: (b, 0)),
            pl.BlockSpec((seq, KV_WIDTH), lambda b: (b, 0)),
            pl.BlockSpec((None, ncp, 1), bat),
            pl.BlockSpec(pe.shape, fix2),
            pl.BlockSpec(wk.shape, fix3),
            pl.BlockSpec(wv.shape, fix3),
            pl.BlockSpec(w2k.shape, fix2),
            pl.BlockSpec(w2vt.shape, fix2),
            pl.BlockSpec(knw.shape, fix2),
            pl.BlockSpec(pk.shape, fix2),
            pl.BlockSpec(freq.shape, fix2),
        ],
        out_specs=[pl.BlockSpec((None, N_GROUPS, ncp, LANES), lambda b: (b, 0, 0, 0)),
                   pl.BlockSpec((None, N_GROUPS, HEAD_DIM, ncp), lambda b: (b, 0, 0, 0))],
        compiler_params=_cparams(("parallel",)),
        name="compress",
    )(xk, xv, pos_c, pe, wk, wv, w2k, w2vt, knw, pk, freq)


def _head_columns(qt_ref):
    cols = []
    for pair in range(qt_ref.shape[0] // LANES):
        both = qt_ref[pair * LANES:(pair + 1) * LANES, :]
        cols += [both, jnp.concatenate([both[HEAD_DIM:, :], both[:HEAD_DIM, :]], axis=0)]
    return cols


def _software_pipeline(n, scores, probs, finish, after=None):
    s, p = {}, {}
    for step in range(n + 2):
        if step >= 2:
            finish(step - 2, p.pop(step - 2))
        if 1 <= step <= n:
            p[step - 1] = probs(step - 1, s.pop(step - 1))
        if step < n:
            s[step] = scores(step)
        for extra in (after or {}).get(step, ()):
            extra()


def _gate_row(gt_ref, branch, hd, group=0):
    r = group * GATE_ROWS + branch * HEADS_PER_GROUP + hd
    return gt_ref[r:r + 1, :]


def _cmp_kernel(qt_ref, kc_ref, vct_ref, ovt_ref, kwin_ref, vtwin_ref, band_ref, gt_ref, ot_ref, sbt_ref, *, tq, ncp):
    i = pl.program_id(1)
    span = WINDOW + tq
    wstart = pl.multiple_of(jnp.maximum(i - WINDOW // tq, 0) * tq, tq)
    t = i * tq + lax.broadcasted_iota(jnp.int32, (1, tq), 1)
    any_valid = (t >= CMP_BLOCK - 1).astype(F32)
    heads = _head_columns(qt_ref)
    cmp_per_sel = SEL_BLOCK // CMP_STRIDE
    groups = range(N_GROUPS)

    def run(nrow):
        nblk = nrow // cmp_per_sel
        blk_end = lax.broadcasted_iota(jnp.int32, (nrow, 1), 0) * CMP_STRIDE + (CMP_BLOCK - 1)
        valid = blk_end <= t
        kc = [kc_ref[g, 0:nrow, :] for g in groups]
        vct = [vct_ref[g, :, 0:nrow] for g in groups]
        psum_parts = [[] for _ in groups]

        def scores(unit):
            u = unit % N_HEADS
            if unit < N_HEADS:
                return jnp.where(valid, _dot(kc[u // HEADS_PER_GROUP], heads[u]), NEG)
            k = kwin_ref[u // HEADS_PER_GROUP, pl.ds(wstart, span), :]
            return _dot(k, heads[u]) + band_ref[...]

        def probs(unit, s):
            e = jnp.exp(s - jnp.max(s, axis=0, keepdims=True))
            if unit < N_HEADS:
                return e * (any_valid / jnp.sum(e, axis=0, keepdims=True))
            return e.astype(BF16)

        def finish(unit, p):
            u = unit % N_HEADS
            g, hd = divmod(u, HEADS_PER_GROUP)
            rows = slice(u * HEAD_DIM, (u + 1) * HEAD_DIM)
            if unit < N_HEADS:
                ot_ref[rows, :] = _dot(vct[g], p.astype(BF16)) * _gate_row(gt_ref, _BR_CMP, hd, g)
                psum_parts[g].append(p)
            else:
                acc = _dot(vtwin_ref[g, :, pl.ds(wstart, span)], p)
                scale = _gate_row(gt_ref, _BR_WIN, hd, g) / acc[HEAD_DIM:HEAD_DIM + 1, :]
                ot_ref[rows, :] = ot_ref[rows, :] + acc[:HEAD_DIM, :] * scale

        blk = lax.broadcasted_iota(jnp.int32, (nblk, 1), 0)
        blk_f = blk.astype(F32)
        imp = []
        bias = [jnp.full((nblk, tq), NEG, F32) for _ in groups]

        def importance():
            ovt = ovt_ref[0:nblk, 0:nrow]
            sel_start = blk * SEL_BLOCK
            cur = (t >> int(math.log2(SEL_BLOCK))) << int(math.log2(SEL_BLOCK))
            for parts in psum_parts:
                psum = (parts[0] + parts[1]) + (parts[2] + parts[3])
                hi = psum.astype(BF16)
                lo = (psum - hi.astype(F32)).astype(BF16)
                x = jnp.where(sel_start <= t, _dot(ovt, hi) + _dot(ovt, lo), -1.0)
                imp.append(jnp.where(sel_start == cur, 1e4, jnp.where(sel_start == 0, 1e4, x)))

        def topk_round():
            for g in groups:
                mx = jnp.max(imp[g], axis=0, keepdims=True)
                first = jnp.min(jnp.where(imp[g] == mx, blk_f, float(LANES)), axis=0, keepdims=True)
                pick = blk_f == first
                bias[g] = jnp.where(pick, 0.0, bias[g])
                imp[g] = jnp.where(pick, -3e38, imp[g])

        last_cmp = N_HEADS + 1
        rest = range(last_cmp + 1, 2 * N_HEADS + 2)
        after = {last_cmp: [importance]}
        for r in range(SEL_TOPK):
            after.setdefault(rest[r * len(rest) // SEL_TOPK], []).append(topk_round)
        _software_pipeline(2 * N_HEADS, scores, probs, finish, after)
        for g in groups:
            sbt_ref[g, 0:nblk, :] = bias[g].astype(BF16)
            if nblk < LANES:
                sbt_ref[g, nblk:, :] = jnp.full((LANES - nblk, tq), NEG, BF16)

    sizes = [LANES * (v + 1) for v in range(ncp // LANES)]
    assert sizes[0] // cmp_per_sel >= SEL_TOPK
    tokens_per_variant = LANES * CMP_STRIDE
    lax.switch((i * tq + tq - 1) // tokens_per_variant, [functools.partial(run, nrow) for nrow in sizes])


def _window_band(tq):
    edge = WINDOW // tq
    v = jnp.arange(edge + 1)[:, None, None]
    kpos = jnp.maximum(v - edge, 0) * tq + jnp.arange(WINDOW + tq)[None, :, None]
    t = v * tq + jnp.arange(tq)[None, None, :]
    ok = (t - kpos >= 0) & (t - kpos < WINDOW)
    return jnp.where(ok, 0.0, NEG).astype(F32)


def _cmp_win_select(qt, kc, vct, ovt, k_win, vt, gates_t, *, bsz, seq, tq=256):
    ncp = kc.shape[2]
    nq = seq // tq
    band = _window_band(tq)
    last = band.shape[0] - 1
    whole = lambda b, i: (b, 0, 0, 0)
    return pl.pallas_call(
        functools.partial(_cmp_kernel, tq=tq, ncp=ncp),
        out_shape=[jax.ShapeDtypeStruct((bsz, A_WIDTH, seq), F32),
                   jax.ShapeDtypeStruct((bsz, N_GROUPS, LANES, seq), BF16)],
        grid=(bsz, nq),
        in_specs=[
            pl.BlockSpec((None, A_WIDTH, tq), lambda b, i: (b, 0, i)),
            pl.BlockSpec((None, N_GROUPS, ncp, LANES), whole),
            pl.BlockSpec((None, N_GROUPS, HEAD_DIM, ncp), whole),
            pl.BlockSpec(ovt.shape, lambda b, i: (0, 0)),
            pl.BlockSpec((None, N_GROUPS, seq, LANES), whole),
            pl.BlockSpec((None, None, N_GROUPS, V_ROWS, seq), lambda b, i: (b, 1, 0, 0, 0)),
            pl.BlockSpec((None,) + band.shape[1:], lambda b, i: (jnp.minimum(i, last), 0, 0)),
            pl.BlockSpec((N_GROUPS * GATE_ROWS, tq), lambda b, i: (0, b * nq + i)),
        ],
        out_specs=[pl.BlockSpec((None, A_WIDTH, tq), lambda b, i: (b, 0, i)),
                   pl.BlockSpec((None, N_GROUPS, LANES, tq), lambda b, i: (b, 0, 0, i))],
        compiler_params=_cparams(("parallel", "parallel")),
        name="cmp_win_select",
    )(qt, kc, vct, ovt, k_win, vt, band, gates_t)


_SLC_UNROLL = 2


def _slc_kernel(qt_ref, sbt_ref, k_ref, vt_ref, gt_ref, prev_ref, o_ref,
                qa_ref, s_ref, p_ref, acc_ref, *, tq):
    i = pl.program_id(2)
    rows = HEADS_PER_GROUP * tq
    qa_ref[...] = jnp.concatenate([jnp.concatenate([sbt_ref[...]] * HEADS_PER_GROUP, axis=1),
                                   jnp.concatenate(_head_columns(qt_ref), axis=1)], axis=0)
    @pl.when((pl.program_id(0) == 0) & (pl.program_id(1) == 0) & (i == 0))
    def _():
        p_ref[...] = jnp.zeros(p_ref.shape, BF16)
        acc_ref[...] = jnp.zeros(acc_ref.shape, F32)

    def accumulate(unit, alpha):
        j = jnp.where(unit <= 0, i, unit - 1)
        pv = _dot(vt_ref[:, pl.ds(pl.multiple_of(j * tq, tq), tq)], p_ref[...])
        acc_ref[...] = alpha * acc_ref[...] + pv

    def probs(m):
        s = s_ref[...]
        m_new = jnp.maximum(m, jnp.max(s, axis=0, keepdims=True))
        p_ref[...] = jnp.exp(s - m_new).astype(BF16)
        return m_new, jnp.exp(m - m_new)

    def scores(j, masked):
        s = _dot(k_ref[pl.ds(pl.multiple_of(j * tq, tq), tq), :], qa_ref[...])
        if masked:
            causal = (lax.broadcasted_iota(jnp.int32, (tq, 1), 0)
                      <= (lax.broadcasted_iota(jnp.int32, (1, rows), 1) & (tq - 1)))
            s = jnp.where(causal, s, NEG)
        s_ref[...] = s

    def body(k, carry):
        m, alpha = carry
        accumulate(k - 1, alpha)
        m, alpha = probs(m)
        scores(k, False)
        return m, alpha

    def steps(first, count, carry):
        for k in range(count):
            carry = body(first + k, carry)
        return carry

    scores(i, True)
    carry = (jnp.full((1, rows), -3e38, F32), jnp.ones((1, rows), F32))
    trips = i >> int(math.log2(_SLC_UNROLL))
    carry = lax.fori_loop(0, trips, lambda k, c: steps(k * _SLC_UNROLL, _SLC_UNROLL, c), carry)
    done = trips * _SLC_UNROLL
    size = _SLC_UNROLL // 2
    while size >= 1:
        carry = lax.cond((i & size) != 0, functools.partial(steps, done, size), lambda c: c, carry)
        done = done + (i & size)
        size //= 2
    m, alpha = carry
    accumulate(i - 1, alpha)
    m, alpha = probs(m)
    accumulate(i, alpha)

    gate = jnp.concatenate([_gate_row(gt_ref, _BR_SLC, hd) for hd in range(HEADS_PER_GROUP)], axis=1)
    out_t = acc_ref[0:HEAD_DIM, :] * (gate / acc_ref[HEAD_DIM:HEAD_DIM + 1, :])
    for pair in range(HEADS_PER_GROUP // 2):
        for c in range(tq // LANES):
            col = 2 * pair * tq + c * LANES
            both = jnp.concatenate([out_t[:, col:col + LANES], out_t[:, col + tq:col + tq + LANES]], axis=0)
            both = both + prev_ref[pair * LANES:(pair + 1) * LANES, c * LANES:(c + 1) * LANES]
            o_ref[c * LANES:(c + 1) * LANES, pair * LANES:(pair + 1) * LANES] = both.T.astype(BF16)


def _slc(qt, sbt, k_aug, vt_aug, gates_t, prev, *, bsz, seq, tq=512):
    nq = seq // tq
    gw = HEADS_PER_GROUP * HEAD_DIM
    rows = HEADS_PER_GROUP * tq
    tmap = lambda b, g, i: (b, g, i)
    return pl.pallas_call(
        functools.partial(_slc_kernel, tq=tq),
        out_shape=jax.ShapeDtypeStruct((bsz * seq, A_WIDTH), BF16),
        grid=(bsz, N_GROUPS, nq),
        in_specs=[
            pl.BlockSpec((None, gw, tq), tmap),
            pl.BlockSpec((None, None, LANES, tq), lambda b, g, i: (b, g, 0, i)),
            pl.BlockSpec((None, None, seq, 2 * LANES), lambda b, g, i: (b, g, 0, 0)),
            pl.BlockSpec((None, None, None, V_ROWS, seq), lambda b, g, i: (b, 0, g, 0, 0)),
            pl.BlockSpec((GATE_ROWS, tq), lambda b, g, i: (g, b * nq + i)),
            pl.BlockSpec((None, gw, tq), tmap),
        ],
        out_specs=pl.BlockSpec((tq, gw), lambda b, g, i: (b * nq + i, g)),
        scratch_shapes=[pltpu.VMEM((2 * LANES, rows), BF16), pltpu.VMEM((tq, rows), F32),
                        pltpu.VMEM((tq, rows), BF16), pltpu.VMEM((V_ROWS, rows), F32)],
        compiler_params=_cparams(("arbitrary", "arbitrary", "arbitrary")),
        name="slc_attn",
    )(qt, sbt, k_aug, vt_aug, gates_t, prev)


def _outproj_kernel(x_ref, oa_ref, ob_ref, w_ref, o_ref):
    y = _dot(oa_ref[...], w_ref[0:A_WIDTH, :]) + _dot(ob_ref[...], w_ref[A_WIDTH:, :])
    o_ref[...] = x_ref[...] + y


def _outproj(x, oa, ob, w, *, tm=1024):
    n, d = x.shape
    row = lambda i: (i, 0)
    return pl.pallas_call(
        _outproj_kernel,
        out_shape=jax.ShapeDtypeStruct((n, d), F32),
        grid=(n // tm,),
        in_specs=[
            pl.BlockSpec((tm, d), row),
            pl.BlockSpec((tm, A_WIDTH), row),
            pl.BlockSpec((tm, A_WIDTH), row),
            pl.BlockSpec(w.shape, lambda i: (0, 0)),
        ],
        out_specs=pl.BlockSpec((tm, d), row),
        compiler_params=_cparams(("parallel",)),
        name="outproj",
    )(x, oa, ob, w)


def _ffn_pool_kernel(x_ref, xnext_ref, g_ref, wg_ref, wu_ref, wd_ref, pg_ref, w_ref, sc_ref, o_ref,
                     h_ref, y_ref, ext_ref, tmp_ref, *, ni, tm, tiles_per_batch):
    i = pl.program_id(0)
    slot = i % 2

    @pl.when(i == 0)
    def _():
        h_ref[0] = _rms(x_ref[...], g_ref[...]).astype(BF16)
        ext_ref[...] = jnp.zeros(ext_ref.shape, F32)

    def front():
        h = h_ref[slot]
        a = _dot(h, wg_ref[...])
        b = _dot(h, wu_ref[...])
        act = a * (1.0 / (1.0 + jnp.exp(-a))) * b
        y_ref[slot] = x_ref[...] + 0.5 * _dot(act.astype(BF16), wd_ref[...])
        h_ref[1 - slot] = _rms(xnext_ref[...], g_ref[...]).astype(BF16)

    pl.when(i == 0)(front)

    @pl.when((i > 0) & (i < ni))
    def _():
        _pool_back(i - 1, y_ref.at[1 - slot], pg_ref, w_ref, sc_ref, o_ref, ext_ref, tmp_ref, tm, tiles_per_batch)
        front()

    @pl.when(i == ni)
    def _():
        _pool_back(i - 1, y_ref.at[1 - slot], pg_ref, w_ref, sc_ref, o_ref, ext_ref, tmp_ref, tm, tiles_per_batch)


def _pool_back(j, x_ref, g_ref, w_ref, sc_ref, o_ref, ext_ref, tmp_ref, tm, tiles_per_batch):
    first = j % tiles_per_batch == 0
    x = x_ref[...]
    h = _rms(x, g_ref[...])
    pad, body = _POOL_PAD, _POOL_PAD + POOL_HALO
    halo = ext_ref[body + tm - POOL_HALO:body + tm, :]
    ext_ref[pad:body, :] = jnp.where(first, 0.0, halo)
    ext_ref[body:body + tm, :] = h
    gw = h.shape[1] // len(POOL_WINDOWS)
    for slot in range(2):
        tmp_ref[slot, 0:pad, :] = jnp.zeros((pad, gw), F32)
    t = (j % tiles_per_batch) * tm + lax.broadcasted_iota(jnp.int32, (tm, 1), 0)
    span = POOL_HALO + tm
    for gi, win in enumerate(POOL_WINDOWS):
        c0 = gi * gw
        hg = h[:, c0:c0 + gw]
        read = lambda start, c0=c0: ext_ref[pl.ds(start, span), c0:c0 + gw]
        shift, slot = 1, 0
        while shift < win:
            tmp_ref[slot, pad:pad + span, :] = read(pad) + read(pad - shift)
            read = lambda start, slot=slot: tmp_ref[slot, pl.ds(start, span), :]
            shift, slot = 2 * shift, 1 - slot
        tot = read(pad)[POOL_HALO:, :]
        cnt = jnp.minimum(t + 1, win).astype(F32)
        y = _dot((tot / cnt - hg).astype(BF16), w_ref[gi])
        o_ref[:, c0:c0 + gw] = x[:, c0:c0 + gw] + y * sc_ref[:, c0:c0 + gw]


def _ffn_pool(x, g, wg, wu, wd, layer, half, pool_g, pool_w, pool_scale, *, seq, tm=512):
    n, d = x.shape
    dff = wg.shape[-1]
    ni = n // tm
    once = dict(pipeline_mode=pl.Buffered(1))
    fixed = lambda i: (0, 0)
    rows = _POOL_PAD + POOL_HALO + tm
    return pl.pallas_call(
        functools.partial(_ffn_pool_kernel, ni=ni, tm=tm, tiles_per_batch=seq // tm),
        out_shape=jax.ShapeDtypeStruct((n, d), F32),
        grid=(ni + 1,),
        in_specs=[
            pl.BlockSpec((tm, d), lambda i: (jnp.minimum(i, ni - 1), 0)),
            pl.BlockSpec((tm, d), lambda i: (jnp.minimum(i + 1, ni - 1), 0)),
            pl.BlockSpec((1, d), fixed),
            pl.BlockSpec((None, None, d, dff), lambda i: (layer, half, 0, 0), **once),
            pl.BlockSpec((None, None, d, dff), lambda i: (layer, half, 0, 0), **once),
            pl.BlockSpec((None, None, dff, d), lambda i: (layer, half, 0, 0), **once),
            pl.BlockSpec((1, d), fixed),
            pl.BlockSpec(pool_w.shape, lambda i: (0, 0, 0)),
            pl.BlockSpec((1, d), fixed),
        ],
        out_specs=pl.BlockSpec((tm, d), lambda i: (jnp.maximum(i - 1, 0), 0)),
        scratch_shapes=[pltpu.VMEM((2, tm, d), BF16), pltpu.VMEM((2, tm, d), F32), pltpu.VMEM((rows, d), F32),
                        pltpu.VMEM((2, rows, d // len(POOL_WINDOWS)), F32)],
        compiler_params=_cparams(("arbitrary",)),
        name="ffn_pool",
    )(x, x, g.reshape(1, d), wg, wu, wd, pool_g.reshape(1, d), pool_w, pool_scale.reshape(1, d))


def _pack_w_in(w_in):
    o = 0
    q = w_in[:, o:o + A_WIDTH]; o += A_WIDTH
    kc, vc, ks, vs, kw, vw = [w_in[:, o + k * KV_WIDTH:o + (k + 1) * KV_WIDTH] for k in range(6)]
    o += 6 * KV_WIDTH
    gates = w_in[:, o:o + N_BRANCH * N_HEADS]; o += N_BRANCH * N_HEADS
    u, gb, gc = [w_in[:, o + k * A_WIDTH:o + (k + 1) * A_WIDTH] for k in range(3)]
    gates = gates.reshape(-1, N_GROUPS, HEADS_PER_GROUP, N_BRANCH).transpose(0, 1, 3, 2)
    gates = gates.reshape(-1, N_GROUPS, N_BRANCH * HEADS_PER_GROUP)
    gates = jnp.pad(gates, ((0, 0), (0, 0), (0, LANES - N_BRANCH * HEADS_PER_GROUP))).reshape(-1, GATE_WIDTH)
    return jnp.concatenate([q, ks, kw, vs, vw, kc, vc, gates, u, gb, gc], axis=1).astype(BF16)


def _group_mean_matrix(width):
    r = jnp.arange(width) // HEAD_DIM
    return jnp.where(r[:, None] == r[None, :], 1.0 / HEAD_DIM, 0.0).astype(BF16)


def _rope_freq_row():
    lane = jnp.arange(LANES) % HEAD_DIM
    freqs = ROPE_THETA ** (-jnp.arange(0, ROPE_DIM, 2, dtype=F32) / ROPE_DIM)
    return jnp.where(lane < ROPE_DIM, freqs[lane % ROPE_HALF], 0.0).astype(F32).reshape(1, LANES)


def _hybrid_mixer(x, positions, g, w_in, q_norm, k_norm, cmp_pos, cmp_w1, cmp_w2, conv_w, w_out, *, bsz, seq):
    n = bsz * seq
    ncp = seq // CMP_STRIDE
    freq = _rope_freq_row()
    pq = _group_mean_matrix(A_WIDTH)
    pk = _group_mean_matrix(KV_WIDTH)
    qnw = jnp.tile(q_norm, N_HEADS).reshape(1, A_WIDTH)
    knw = jnp.tile(k_norm, (1, N_GROUPS))

    q_t, k_slc, k_win, v_t, kc_raw, vc_raw, gates, ob = _inproj(
        x, positions.reshape(n, 1), g, _pack_w_in(w_in), qnw, knw, conv_w, pq, pk, freq, seq=seq)

    eye_g = jnp.eye(N_GROUPS, dtype=F32)
    seg = CMP_STRIDE * KV_WIDTH

    def seg_weights(w1):
        w1r = w1.reshape(CMP_BLOCK, HEAD_DIM, CMP_HIDDEN)
        halves = [jnp.einsum('jdh,gk->jgdkh', w1r[a:a + CMP_STRIDE], eye_g).reshape(seg, N_GROUPS * CMP_HIDDEN)
                  for a in (0, CMP_STRIDE)]
        return jnp.stack(halves).astype(BF16)

    def seg_pos(pe):
        return [jnp.broadcast_to(pe[a:a + CMP_STRIDE, None, :], (CMP_STRIDE, N_GROUPS, HEAD_DIM)).reshape(seg)
                for a in (0, CMP_STRIDE)]

    pe = jnp.stack(seg_pos(cmp_pos[0]) + seg_pos(cmp_pos[1]))
    w2 = jnp.stack([jnp.einsum('hd,gk->ghkd', cmp_w2[a], eye_g).reshape(N_GROUPS * CMP_HIDDEN, KV_WIDTH)
                    for a in range(2)]).astype(BF16)
    blk_end = jnp.minimum(jnp.arange(ncp) * CMP_STRIDE + CMP_BLOCK - 1, seq - 1)
    pos_c = positions[:, blk_end].reshape(bsz, ncp, 1)
    kc_g, vc_t = _compress(kc_raw, vc_raw, pos_c, pe, seg_weights(cmp_w1[0]), seg_weights(cmp_w1[1]), w2[0], w2[1].T,
                           knw, pk, freq, seq=seq)

    gates_t = gates.reshape(n, N_GROUPS, LANES)[:, :, :GATE_ROWS].transpose(1, 2, 0).reshape(N_GROUPS * GATE_ROWS, n)
    cstart = jnp.arange(ncp) * CMP_STRIDE
    sstart = jnp.arange(LANES) * SEL_BLOCK
    ovt = ((cstart[None, :] < sstart[:, None] + SEL_BLOCK)
           & (cstart[None, :] + CMP_BLOCK > sstart[:, None])).astype(BF16)
    o_t, sel_bias = _cmp_win_select(q_t, kc_g, vc_t, ovt, k_win, v_t, gates_t, bsz=bsz, seq=seq)
    o_a = _slc(q_t, sel_bias, k_slc, v_t, gates_t, o_t, bsz=bsz, seq=seq)

    return _outproj(x, o_a, ob, w_out.astype(BF16))


def kernel(x, positions, ffn_norm, ffn_w_gate, ffn_w_up, ffn_w_down, mix_norm, hyb_w_in, hyb_q_norm, hyb_k_norm,
           hyb_cmp_pos, hyb_cmp_w1, hyb_cmp_w2, hyb_conv_w, hyb_w_out, pool_w, pool_scale):
    bsz, seq, d = x.shape
    depth = ffn_norm.shape[0]
    assert seq % 2048 == 0 and seq // SEL_BLOCK <= LANES and seq // SEL_BLOCK >= SEL_TOPK
    wg, wu, wd = (w.astype(BF16) for w in (ffn_w_gate, ffn_w_up, ffn_w_down))
    xf = x.reshape(bsz * seq, d)
    for layer in range(depth):
        i = layer // 2
        if layer % 2 == 0:
            xf = _ffn(xf, ffn_norm[layer, 0], wg, wu, wd, layer, 0)
            xf = _hybrid_mixer(xf, positions, mix_norm[layer], hyb_w_in[i], hyb_q_norm[i], hyb_k_norm[i],
                               hyb_cmp_pos[i], hyb_cmp_w1[i], hyb_cmp_w2[i], hyb_conv_w[i], hyb_w_out[i],
                               bsz=bsz, seq=seq)
        else:
            xf = _ffn_pool(xf, ffn_norm[layer, 0], wg, wu, wd, layer, 0, mix_norm[layer], pool_w[i].astype(BF16),
                           pool_scale[i], seq=seq)
        xf = _ffn(xf, ffn_norm[layer, 1], wg, wu, wd, layer, 1)
    return xf.reshape(bsz, seq, d)
```

```python
import functools
import math

import jax
import jax.numpy as jnp
from jax import lax
from jax.experimental import pallas as pl
from jax.experimental.pallas import tpu as pltpu

F32 = jnp.float32
BF16 = jnp.bfloat16

HEAD_DIM = 64
N_GROUPS = 2
HEADS_PER_GROUP = 4
N_HEADS = N_GROUPS * HEADS_PER_GROUP
A_WIDTH = N_HEADS * HEAD_DIM
KV_WIDTH = N_GROUPS * HEAD_DIM
ROPE_DIM = HEAD_DIM // 4
ROPE_HALF = ROPE_DIM // 2
ROPE_THETA = 500000.0
CMP_BLOCK = 32
CMP_STRIDE = 16
CMP_HIDDEN = 2 * HEAD_DIM
SEL_BLOCK = 64
SEL_TOPK = 16
WINDOW = 512
N_BRANCH = 3
_BR_CMP, _BR_SLC, _BR_WIN = 0, 1, 2
CONV_WIDTH = 3
POOL_WINDOWS = (2, 4, 8, 16)
POOL_HALO = 16
_POOL_PAD = 8
EPS = 1e-6
NEG = -1e30

LANES = 128
VMEM_LIMIT_BYTES = 56 * 1024 * 1024


def _cparams(sem):
    return pltpu.CompilerParams(dimension_semantics=sem, vmem_limit_bytes=VMEM_LIMIT_BYTES)


def _dot(a, b):
    return jnp.dot(a, b, preferred_element_type=F32)


def _dot_split(a, b):
    hi = a.astype(BF16)
    lo = (a - hi.astype(F32)).astype(BF16)
    return _dot(hi, b) + _dot(lo, b)


def _rms(x, g):
    ms = jnp.mean(x * x, axis=-1, keepdims=True)
    return x * lax.rsqrt(ms + EPS) * g


def _group_rms(x, g, pmat):
    ms = _dot_split(x * x, pmat)
    return x * lax.rsqrt(ms + EPS) * g


def _rope_tables(pos_col, freq_row):
    ang = pos_col * freq_row
    lane = lax.broadcasted_iota(jnp.int32, (1, LANES), 1) & (HEAD_DIM - 1)
    sign = jnp.where(lane < ROPE_HALF, -1.0, 1.0).astype(F32)
    return jnp.cos(ang), jnp.sin(ang) * sign


def _rope(x, cos_t, sin_t):
    w = x.shape[1]
    reps = w // LANES
    if reps > 1:
        cos_t = jnp.concatenate([cos_t] * reps, axis=1)
        sin_t = jnp.concatenate([sin_t] * reps, axis=1)
    lane = lax.broadcasted_iota(jnp.int32, (1, w), 1) & (HEAD_DIM - 1)
    partner = jnp.where(lane < ROPE_HALF,
                        pltpu.roll(x, w - ROPE_HALF, axis=1),
                        pltpu.roll(x, ROPE_HALF, axis=1))
    return x * cos_t + partner * sin_t


def _ffn_kernel(x_ref, xnext_ref, g_ref, wg_ref, wu_ref, wd_ref, o_ref, h_ref, acc_ref, *, nj):
    i, j = pl.program_id(0), pl.program_id(1)
    slot = i % 2

    @pl.when((i == 0) & (j == 0))
    def _():
        h_ref[0] = _rms(x_ref[...], g_ref[...]).astype(BF16)

    def step(first, last):
        h = h_ref[slot]
        a = _dot(h, wg_ref[...])
        b = _dot(h, wu_ref[...])
        act = a * (1.0 / (1.0 + jnp.exp(-a))) * b
        y = _dot(act.astype(BF16), wd_ref[...])
        if not first:
            y = acc_ref[...] + y
        if last:
            h_ref[1 - slot] = _rms(xnext_ref[...], g_ref[...]).astype(BF16)
            o_ref[...] = x_ref[...] + 0.5 * y
        else:
            acc_ref[...] = y

    for jj in range(nj):
        pl.when(j == jj)(functools.partial(step, jj == 0, jj == nj - 1))


def _ffn(x, g, wg, wu, wd, layer, half, *, tm=512, tf=2816):
    n, d = x.shape
    dff = wg.shape[-1]
    ni, nj = n // tm, dff // tf
    wmode = dict(pipeline_mode=pl.Buffered(1)) if nj == 1 else {}
    return pl.pallas_call(
        functools.partial(_ffn_kernel, nj=nj),
        out_shape=jax.ShapeDtypeStruct((n, d), F32),
        grid=(ni, nj),
        in_specs=[
            pl.BlockSpec((tm, d), lambda i, j: (i, 0)),
            pl.BlockSpec((tm, d), lambda i, j: (jnp.minimum(i + 1, ni - 1), 0)),
            pl.BlockSpec((1, d), lambda i, j: (0, 0)),
            pl.BlockSpec((None, None, d, tf), lambda i, j: (layer, half, 0, j), **wmode),
            pl.BlockSpec((None, None, d, tf), lambda i, j: (layer, half, 0, j), **wmode),
            pl.BlockSpec((None, None, tf, d), lambda i, j: (layer, half, j, 0), **wmode),
        ],
        out_specs=pl.BlockSpec((tm, d), lambda i, j: (i, 0)),
        scratch_shapes=[pltpu.VMEM((2, tm, d), BF16), pltpu.VMEM((tm, d), F32)],
        compiler_params=_cparams(("arbitrary", "arbitrary")),
        name="ffn",
    )(x, x, g.reshape(1, d), wg, wu, wd)


GATE_WIDTH = N_GROUPS * LANES
GATE_ROWS = 16
_C_Q = 0
_C_K = _C_Q + A_WIDTH
_C_V = _C_K + 2 * KV_WIDTH
_C_C = _C_V + 2 * KV_WIDTH
_C_GATE = _C_C + 2 * KV_WIDTH
_C_U = _C_GATE + GATE_WIDTH
_C_GB = _C_U + A_WIDTH
_C_GC = _C_GB + A_WIDTH
CONV_HALO = 8
V_ROWS = HEAD_DIM + 16


def _inproj_kernel(x_ref, pos_ref, g_ref, w_ref, qnw_ref, knw_ref, cw_ref, pq_ref, pk_ref, freq_ref,
                   qt_ref, kslc_ref, kwin_ref, vt_ref, kc_ref, vc_ref, gates_ref, ob_ref,
                   vext_ref, *, tm, tiles_per_batch):
    i = pl.program_id(0)
    chunks = range(tm // LANES)

    def transposed(a, c):
        return jnp.concatenate([a[r * LANES:(r + 1) * LANES, c * LANES:(c + 1) * LANES].T for r in chunks], axis=1)

    def per_group(k):
        low = lax.broadcasted_iota(jnp.int32, (1, LANES), 1) < HEAD_DIM
        return [jnp.where(low, k, 0.0), jnp.where(low, pltpu.roll(k, HEAD_DIM, axis=1), 0.0)]

    @pl.when(i % tiles_per_batch == 0)
    def _():
        vext_ref[0:CONV_HALO, :] = jnp.zeros((CONV_HALO, A_WIDTH), F32)

    h = _rms(x_ref[...], g_ref[...]).astype(BF16)

    def proj(c0, width):
        return _dot(h, w_ref[:, c0:c0 + width])

    q_raw = proj(_C_Q, A_WIDTH)
    k2 = proj(_C_K, 2 * KV_WIDTH)
    cos_t, sin_t = _rope_tables(pos_ref[...].astype(F32), freq_ref[...])
    v = proj(_C_GC, A_WIDTH) * proj(_C_U, A_WIDTH)
    gate_b = proj(_C_GB, A_WIDTH)

    q = _group_rms(q_raw, qnw_ref[...], pq_ref[...])
    q = _rope(q, cos_t, sin_t) * (HEAD_DIM ** -0.5)
    for c in range(A_WIDTH // LANES):
        qt_ref[c * LANES:(c + 1) * LANES, :] = transposed(q, c).astype(BF16)

    tok = (i % tiles_per_batch) * tm + lax.broadcasted_iota(jnp.int32, (tm, 1), 0)
    blk_onehot = (lax.broadcasted_iota(jnp.int32, (1, LANES), 1) == (tok >> int(math.log2(SEL_BLOCK))))
    blk_onehot = jnp.where(blk_onehot, 1.0, 0.0).astype(BF16)
    ks = _rope(_group_rms(k2[:, :KV_WIDTH], knw_ref[1:2, :], pk_ref[...]), cos_t, sin_t)
    for g, kg in enumerate(per_group(ks)):
        kslc_ref[g] = jnp.concatenate([blk_onehot, kg.astype(BF16)], axis=1)
    kw = _rope(_group_rms(k2[:, KV_WIDTH:], knw_ref[2:3, :], pk_ref[...]), cos_t, sin_t)
    for g, kg in enumerate(per_group(kw)):
        kwin_ref[g] = kg.astype(BF16)

    v2 = proj(_C_V, 2 * KV_WIDTH)
    ones_rows = jnp.where(lax.broadcasted_iota(jnp.int32, (V_ROWS - HEAD_DIM, 1), 0) == 0, 1.0, 0.0)
    ones_rows = jnp.broadcast_to(ones_rows, (V_ROWS - HEAD_DIM, tm)).astype(BF16)
    for branch in range(2):
        vt = transposed(v2, branch).astype(BF16)
        for g in range(N_GROUPS):
            vt_ref[branch, g, 0:HEAD_DIM, :] = vt[g * HEAD_DIM:(g + 1) * HEAD_DIM, :]
            vt_ref[branch, g, HEAD_DIM:, :] = ones_rows
    c2 = proj(_C_C, 2 * KV_WIDTH)
    kc_ref[...] = c2[:, :KV_WIDTH]
    vc_ref[...] = c2[:, KV_WIDTH:]
    gate_logits = proj(_C_GATE, GATE_WIDTH)

    vext_ref[CONV_HALO:CONV_HALO + tm, :] = v
    y = (cw_ref[2:3, :] * v
         + cw_ref[1:2, :] * vext_ref[pl.ds(CONV_HALO - 1, tm), :]
         + cw_ref[0:1, :] * vext_ref[pl.ds(CONV_HALO - 2, tm), :])
    ob_ref[...] = (gate_b * y).astype(BF16)
    vext_ref[0:CONV_HALO, :] = vext_ref[tm:tm + CONV_HALO, :]
    gates_ref[...] = 1.0 / (1.0 + jnp.exp(-gate_logits))


def _inproj(x, pos_col, g, w, qnw, knw, cw, pq, pk, freq, *, seq, tm=512):
    n, d = x.shape
    bsz, tpb = n // seq, seq // tm
    row = lambda i: (i, 0)
    fixed = lambda i: (0, 0)
    widths = [(KV_WIDTH, F32), (KV_WIDTH, F32), (GATE_WIDTH, F32), (A_WIDTH, BF16)]
    out_shape = [jax.ShapeDtypeStruct((bsz, A_WIDTH, seq), BF16),
                 jax.ShapeDtypeStruct((bsz, N_GROUPS, seq, 2 * LANES), BF16),
                 jax.ShapeDtypeStruct((bsz, N_GROUPS, seq, LANES), BF16),
                 jax.ShapeDtypeStruct((bsz, 2, N_GROUPS, V_ROWS, seq), BF16)]
    out_specs = [pl.BlockSpec((None, A_WIDTH, tm), lambda i: (i // tpb, 0, i % tpb)),
                 pl.BlockSpec((None, N_GROUPS, tm, 2 * LANES), lambda i: (i // tpb, 0, i % tpb, 0)),
                 pl.BlockSpec((None, N_GROUPS, tm, LANES), lambda i: (i // tpb, 0, i % tpb, 0)),
                 pl.BlockSpec((None, 2, N_GROUPS, V_ROWS, tm), lambda i: (i // tpb, 0, 0, 0, i % tpb))]
    return pl.pallas_call(
        functools.partial(_inproj_kernel, tm=tm, tiles_per_batch=tpb),
        out_shape=out_shape + [jax.ShapeDtypeStruct((n, wd), dt) for wd, dt in widths],
        grid=(n // tm,),
        in_specs=[
            pl.BlockSpec((tm, d), row),
            pl.BlockSpec((tm, 1), row),
            pl.BlockSpec((1, d), fixed),
            pl.BlockSpec(w.shape, fixed),
            pl.BlockSpec(qnw.shape, fixed),
            pl.BlockSpec(knw.shape, fixed),
            pl.BlockSpec(cw.shape, fixed),
            pl.BlockSpec(pq.shape, fixed),
            pl.BlockSpec(pk.shape, fixed),
            pl.BlockSpec(freq.shape, fixed),
        ],
        out_specs=out_specs + [pl.BlockSpec((tm, wd), row) for wd, _ in widths],
        scratch_shapes=[pltpu.VMEM((tm + CONV_HALO, A_WIDTH), F32)],
        compiler_params=_cparams(("arbitrary",)),
        name="inproj",
    )(x, pos_col, g.reshape(1, d), w, qnw, knw, cw, pq, pk, freq)


def _gelu_tanh(x):
    return 0.5 * x * (1.0 + jnp.tanh(math.sqrt(2.0 / math.pi) * (x + 0.044715 * (x * x * x))))


def _compress_kernel(xk_ref, xv_ref, pos_ref, pe_ref, wk_ref, wv_ref, w2_ref, w2vt_ref, knw_ref, pk_ref, freq_ref,
                     kc_ref, vct_ref, *, ncp):
    def hidden(x_ref, pe_row, w_ref):
        x = jnp.concatenate([x_ref[pl.ds(j, ncp, stride=CMP_STRIDE), :] for j in range(CMP_STRIDE)], axis=1)
        a = _dot((x + pe_ref[pe_row:pe_row + 1, :]).astype(BF16), w_ref[0])
        b = _dot((x + pe_ref[pe_row + 1:pe_row + 2, :]).astype(BF16), w_ref[1])
        hid = a + pltpu.roll(b, ncp - 1, axis=0)
        return _gelu_tanh(hid)

    low = lax.broadcasted_iota(jnp.int32, (1, LANES), 1) < HEAD_DIM
    kc = _dot(hidden(xk_ref, 0, wk_ref).astype(BF16), w2_ref[...])
    kc = _group_rms(kc, knw_ref[0:1, :], pk_ref[...])
    cos_t, sin_t = _rope_tables(pos_ref[...].astype(F32), freq_ref[...])
    kc = _rope(kc, cos_t, sin_t)
    kc_ref[0] = jnp.where(low, kc, 0.0).astype(BF16)
    kc_ref[1] = jnp.where(low, pltpu.roll(kc, HEAD_DIM, axis=1), 0.0).astype(BF16)
    hv = hidden(xv_ref, 2, wv_ref)
    hv_t = jnp.concatenate(
        [jnp.concatenate([hv[r * LANES:(r + 1) * LANES, c * LANES:(c + 1) * LANES].T for r in range(ncp // LANES)],
                         axis=1) for c in range(hv.shape[1] // LANES)], axis=0)
    vct = _dot(w2vt_ref[...], hv_t.astype(BF16))
    for g in range(N_GROUPS):
        vct_ref[g] = vct[g * HEAD_DIM:(g + 1) * HEAD_DIM, :].astype(BF16)


def _compress(xk, xv, pos_c, pe, wk, wv, w2k, w2vt, knw, pk, freq, *, seq):
    bsz, ncp = pos_c.shape[:2]
    bat = lambda b: (b, 0, 0)
    fix2 = lambda b: (0, 0)
    fix3 = lambda b: (0, 0, 0)
    return pl.pallas_call(
        functools.partial(_compress_kernel, ncp=ncp),
        out_shape=[jax.ShapeDtypeStruct((bsz, N_GROUPS, ncp, LANES), BF16),
                   jax.ShapeDtypeStruct((bsz, N_GROUPS, HEAD_DIM, ncp), BF16)],
        grid=(bsz,),
        in_specs=[
            pl.BlockSpec((seq, KV_WIDTH), lambda b: (b, 0)),
            pl.BlockSpec((seq, KV_WIDTH), lambda b: (b, 0)),
            pl.BlockSpec((None, ncp, 1), bat),
            pl.BlockSpec(pe.shape, fix2),
            pl.BlockSpec(wk.shape, fix3),
            pl.BlockSpec(wv.shape, fix3),
            pl.BlockSpec(w2k.shape, fix2),
            pl.BlockSpec(w2vt.shape, fix2),
            pl.BlockSpec(knw.shape, fix2),
            pl.BlockSpec(pk.shape, fix2),
            pl.BlockSpec(freq.shape, fix2),
        ],
        out_specs=[pl.BlockSpec((None, N_GROUPS, ncp, LANES), lambda b: (b, 0, 0, 0)),
                   pl.BlockSpec((None, N_GROUPS, HEAD_DIM, ncp), lambda b: (b, 0, 0, 0))],
        compiler_params=_cparams(("parallel",)),
        name="compress",
    )(xk, xv, pos_c, pe, wk, wv, w2k, w2vt, knw, pk, freq)


def _head_columns(qt_ref):
    cols = []
    for pair in range(qt_ref.shape[0] // LANES):
        both = qt_ref[pair * LANES:(pair + 1) * LANES, :]
        cols += [both, jnp.concatenate([both[HEAD_DIM:, :], both[:HEAD_DIM, :]], axis=0)]
    return cols


def _software_pipeline(n, scores, probs, finish, after=None):
    s, p = {}, {}
    for step in range(n + 2):
        if step >= 2:
            finish(step - 2, p.pop(step - 2))
        if 1 <= step <= n:
            p[step - 1] = probs(step - 1, s.pop(step - 1))
        if step < n:
            s[step] = scores(step)
        for extra in (after or {}).get(step, ()):
            extra()


def _gate_row(gt_ref, branch, hd, group=0):
    r = group * GATE_ROWS + branch * HEADS_PER_GROUP + hd
    return gt_ref[r:r + 1, :]


def _cmp_kernel(qt_ref, kc_ref, vct_ref, ovt_ref, kwin_ref, vtwin_ref, band_ref, gt_ref, ot_ref, sbt_ref, *, tq, ncp):
    i = pl.program_id(1)
    span = WINDOW + tq
    wstart = pl.multiple_of(jnp.maximum(i - WINDOW // tq, 0) * tq, tq)
    t = i * tq + lax.broadcasted_iota(jnp.int32, (1, tq), 1)
    any_valid = (t >= CMP_BLOCK - 1).astype(F32)
    heads = _head_columns(qt_ref)
    cmp_per_sel = SEL_BLOCK // CMP_STRIDE
    groups = range(N_GROUPS)

    def run(nrow):
        nblk = nrow // cmp_per_sel
        blk_end = lax.broadcasted_iota(jnp.int32, (nrow, 1), 0) * CMP_STRIDE + (CMP_BLOCK - 1)
        valid = blk_end <= t
        kc = [kc_ref[g, 0:nrow, :] for g in groups]
        vct = [vct_ref[g, :, 0:nrow] for g in groups]
        psum_parts = [[] for _ in groups]

        def scores(unit):
            u = unit % N_HEADS
            if unit < N_HEADS:
                return jnp.where(valid, _dot(kc[u // HEADS_PER_GROUP], heads[u]), NEG)
            k = kwin_ref[u // HEADS_PER_GROUP, pl.ds(wstart, span), :]
            return _dot(k, heads[u]) + band_ref[...]

        def probs(unit, s):
            e = jnp.exp(s - jnp.max(s, axis=0, keepdims=True))
            if unit < N_HEADS:
                return e * (any_valid / jnp.sum(e, axis=0, keepdims=True))
            return e.astype(BF16)

        def finish(unit, p):
            u = unit % N_HEADS
            g, hd = divmod(u, HEADS_PER_GROUP)
            rows = slice(u * HEAD_DIM, (u + 1) * HEAD_DIM)
            if unit < N_HEADS:
                ot_ref[rows, :] = _dot(vct[g], p.astype(BF16)) * _gate_row(gt_ref, _BR_CMP, hd, g)
                psum_parts[g].append(p)
            else:
                acc = _dot(vtwin_ref[g, :, pl.ds(wstart, span)], p)
                scale = _gate_row(gt_ref, _BR_WIN, hd, g) / acc[HEAD_DIM:HEAD_DIM + 1, :]
                ot_ref[rows, :] = ot_ref[rows, :] + acc[:HEAD_DIM, :] * scale

        blk = lax.broadcasted_iota(jnp.int32, (nblk, 1), 0)
        blk_f = blk.astype(F32)
        imp = []
        bias = [jnp.full((nblk, tq), NEG, F32) for _ in groups]

        def importance():
            ovt = ovt_ref[0:nblk, 0:nrow]
            sel_start = blk * SEL_BLOCK
            cur = (t >> int(math.log2(SEL_BLOCK))) << int(math.log2(SEL_BLOCK))
            for parts in psum_parts:
                psum = (parts[0] + parts[1]) + (parts[2] + parts[3])
                hi = psum.astype(BF16)
                lo = (psum - hi.astype(F32)).astype(BF16)
                x = jnp.where(sel_start <= t, _dot(ovt, hi) + _dot(ovt, lo), -1.0)
                imp.append(jnp.where(sel_start == cur, 1e4, jnp.where(sel_start == 0, 1e4, x)))

        def topk_round():
            for g in groups:
                mx = jnp.max(imp[g], axis=0, keepdims=True)
                first = jnp.min(jnp.where(imp[g] == mx, blk_f, float(LANES)), axis=0, keepdims=True)
                pick = blk_f == first
                bias[g] = jnp.where(pick, 0.0, bias[g])
                imp[g] = jnp.where(pick, -3e38, imp[g])

        last_cmp = N_HEADS + 1
        rest = range(last_cmp + 1, 2 * N_HEADS + 2)
        after = {last_cmp: [importance]}
        for r in range(SEL_TOPK):
            after.setdefault(rest[r * len(rest) // SEL_TOPK], []).append(topk_round)
        _software_pipeline(2 * N_HEADS, scores, probs, finish, after)
        for g in groups:
            sbt_ref[g, 0:nblk, :] = bias[g].astype(BF16)
            if nblk < LANES:
                sbt_ref[g, nblk:, :] = jnp.full((LANES - nblk, tq), NEG, BF16)

    sizes = [LANES * (v + 1) for v in range(ncp // LANES)]
    assert sizes[0] // cmp_per_sel >= SEL_TOPK
    tokens_per_variant = LANES * CMP_STRIDE
    lax.switch((i * tq + tq - 1) // tokens_per_variant, [functools.partial(run, nrow) for nrow in sizes])


def _window_band(tq):
    edge = WINDOW // tq
    v = jnp.arange(edge + 1)[:, None, None]
    kpos = jnp.maximum(v - edge, 0) * tq + jnp.arange(WINDOW + tq)[None, :, None]
    t = v * tq + jnp.arange(tq)[None, None, :]
    ok = (t - kpos >= 0) & (t - kpos < WINDOW)
    return jnp.where(ok, 0.0, NEG).astype(F32)


def _cmp_win_select(qt, kc, vct, ovt, k_win, vt, gates_t, *, bsz, seq, tq=256):
    ncp = kc.shape[2]
    nq = seq // tq
    band = _window_band(tq)
    last = band.shape[0] - 1
    whole = lambda b, i: (b, 0, 0, 0)
    return pl.pallas_call(
        functools.partial(_cmp_kernel, tq=tq, ncp=ncp),
        out_shape=[jax.ShapeDtypeStruct((bsz, A_WIDTH, seq), F32),
                   jax.ShapeDtypeStruct((bsz, N_GROUPS, LANES, seq), BF16)],
        grid=(bsz, nq),
        in_specs=[
            pl.BlockSpec((None, A_WIDTH, tq), lambda b, i: (b, 0, i)),
            pl.BlockSpec((None, N_GROUPS, ncp, LANES), whole),
            pl.BlockSpec((None, N_GROUPS, HEAD_DIM, ncp), whole),
            pl.BlockSpec(ovt.shape, lambda b, i: (0, 0)),
            pl.BlockSpec((None, N_GROUPS, seq, LANES), whole),
            pl.BlockSpec((None, None, N_GROUPS, V_ROWS, seq), lambda b, i: (b, 1, 0, 0, 0)),
            pl.BlockSpec((None,) + band.shape[1:], lambda b, i: (jnp.minimum(i, last), 0, 0)),
            pl.BlockSpec((N_GROUPS * GATE_ROWS, tq), lambda b, i: (0, b * nq + i)),
        ],
        out_specs=[pl.BlockSpec((None, A_WIDTH, tq), lambda b, i: (b, 0, i)),
                   pl.BlockSpec((None, N_GROUPS, LANES, tq), lambda b, i: (b, 0, 0, i))],
        compiler_params=_cparams(("parallel", "parallel")),
        name="cmp_win_select",
    )(qt, kc, vct, ovt, k_win, vt, band, gates_t)


_SLC_UNROLL = 2


def _slc_kernel(qt_ref, sbt_ref, k_ref, vt_ref, gt_ref, prev_ref, o_ref,
                qa_ref, s_ref, p_ref, acc_ref, *, tq):
    i = pl.program_id(2)
    rows = HEADS_PER_GROUP * tq
    qa_ref[...] = jnp.concatenate([jnp.concatenate([sbt_ref[...]] * HEADS_PER_GROUP, axis=1),
                                   jnp.concatenate(_head_columns(qt_ref), axis=1)], axis=0)
    @pl.when((pl.program_id(0) == 0) & (pl.program_id(1) == 0) & (i == 0))
    def _():
        p_ref[...] = jnp.zeros(p_ref.shape, BF16)
        acc_ref[...] = jnp.zeros(acc_ref.shape, F32)

    def accumulate(unit, alpha):
        j = jnp.where(unit <= 0, i, unit - 1)
        pv = _dot(vt_ref[:, pl.ds(pl.multiple_of(j * tq, tq), tq)], p_ref[...])
        acc_ref[...] = alpha * acc_ref[...] + pv

    def probs(m):
        s = s_ref[...]
        m_new = jnp.maximum(m, jnp.max(s, axis=0, keepdims=True))
        p_ref[...] = jnp.exp(s - m_new).astype(BF16)
        return m_new, jnp.exp(m - m_new)

    def scores(j, masked):
        s = _dot(k_ref[pl.ds(pl.multiple_of(j * tq, tq), tq), :], qa_ref[...])
        if masked:
            causal = (lax.broadcasted_iota(jnp.int32, (tq, 1), 0)
                      <= (lax.broadcasted_iota(jnp.int32, (1, rows), 1) & (tq - 1)))
            s = jnp.where(causal, s, NEG)
        s_ref[...] = s

    def body(k, carry):
        m, alpha = carry
        accumulate(k - 1, alpha)
        m, alpha = probs(m)
        scores(k, False)
        return m, alpha

    def steps(first, count, carry):
        for k in range(count):
            carry = body(first + k, carry)
        return carry

    scores(i, True)
    carry = (jnp.full((1, rows), -3e38, F32), jnp.ones((1, rows), F32))
    trips = i >> int(math.log2(_SLC_UNROLL))
    carry = lax.fori_loop(0, trips, lambda k, c: steps(k * _SLC_UNROLL, _SLC_UNROLL, c), carry)
    done = trips * _SLC_UNROLL
    size = _SLC_UNROLL // 2
    while size >= 1:
        carry = lax.cond((i & size) != 0, functools.partial(steps, done, size), lambda c: c, carry)
        done = done + (i & size)
        size //= 2
    m, alpha = carry
    accumulate(i - 1, alpha)
    m, alpha = probs(m)
    accumulate(i, alpha)

    gate = jnp.concatenate([_gate_row(gt_ref, _BR_SLC, hd) for hd in range(HEADS_PER_GROUP)], axis=1)
    out_t = acc_ref[0:HEAD_DIM, :] * (gate / acc_ref[HEAD_DIM:HEAD_DIM + 1, :])
    for pair in range(HEADS_PER_GROUP // 2):
        for c in range(tq // LANES):
            col = 2 * pair * tq + c * LANES
            both = jnp.concatenate([out_t[:, col:col + LANES], out_t[:, col + tq:col + tq + LANES]], axis=0)
            both = both + prev_ref[pair * LANES:(pair + 1) * LANES, c * LANES:(c + 1) * LANES]
            o_ref[c * LANES:(c + 1) * LANES, pair * LANES:(pair + 1) * LANES] = both.T.astype(BF16)


def _slc(qt, sbt, k_aug, vt_aug, gates_t, prev, *, bsz, seq, tq=512):
    nq = seq // tq
    gw = HEADS_PER_GROUP * HEAD_DIM
    rows = HEADS_PER_GROUP * tq
    tmap = lambda b, g, i: (b, g, i)
    return pl.pallas_call(
        functools.partial(_slc_kernel, tq=tq),
        out_shape=jax.ShapeDtypeStruct((bsz * seq, A_WIDTH), BF16),
        grid=(bsz, N_GROUPS, nq),
        in_specs=[
            pl.BlockSpec((None, gw, tq), tmap),
            pl.BlockSpec((None, None, LANES, tq), lambda b, g, i: (b, g, 0, i)),
            pl.BlockSpec((None, None, seq, 2 * LANES), lambda b, g, i: (b, g, 0, 0)),
            pl.BlockSpec((None, None, None, V_ROWS, seq), lambda b, g, i: (b, 0, g, 0, 0)),
            pl.BlockSpec((GATE_ROWS, tq), lambda b, g, i: (g, b * nq + i)),
            pl.BlockSpec((None, gw, tq), tmap),
        ],
        out_specs=pl.BlockSpec((tq, gw), lambda b, g, i: (b * nq + i, g)),
        scratch_shapes=[pltpu.VMEM((2 * LANES, rows), BF16), pltpu.VMEM((tq, rows), F32),
                        pltpu.VMEM((tq, rows), BF16), pltpu.VMEM((V_ROWS, rows), F32)],
        compiler_params=_cparams(("arbitrary", "arbitrary", "arbitrary")),
        name="slc_attn",
    )(qt, sbt, k_aug, vt_aug, gates_t, prev)


def _proj_ffn_kernel(x_ref, oa_ref, ob_ref, wo_ref, g_ref, wg_ref, wu_ref, wd_ref, o_ref, h_ref, y_ref, *, ni):
    i = pl.program_id(0)
    slot = i % 2

    def project():
        y = _dot(oa_ref[...], wo_ref[0:A_WIDTH, :]) + _dot(ob_ref[...], wo_ref[A_WIDTH:, :])
        y = x_ref[...] + y
        y_ref[slot] = y
        h_ref[slot] = _rms(y, g_ref[...]).astype(BF16)

    def ffn():
        h = h_ref[1 - slot]
        a = _dot(h, wg_ref[...])
        b = _dot(h, wu_ref[...])
        act = a * (1.0 / (1.0 + jnp.exp(-a))) * b
        o_ref[...] = y_ref[1 - slot] + 0.5 * _dot(act.astype(BF16), wd_ref[...])

    pl.when(i == 0)(project)

    @pl.when((i > 0) & (i < ni))
    def _():
        ffn()
        project()

    pl.when(i == ni)(ffn)


def _proj_ffn(x, oa, ob, wo, g, wg, wu, wd, layer, half, *, tm=512):
    n, d = x.shape
    dff = wg.shape[-1]
    ni = n // tm
    row = lambda i: (jnp.minimum(i, ni - 1), 0)
    const = dict(pipeline_mode=pl.Buffered(1))
    return pl.pallas_call(
        functools.partial(_proj_ffn_kernel, ni=ni),
        out_shape=jax.ShapeDtypeStruct((n, d), F32),
        grid=(ni + 1,),
        in_specs=[
            pl.BlockSpec((tm, d), row),
            pl.BlockSpec((tm, A_WIDTH), row),
            pl.BlockSpec((tm, A_WIDTH), row),
            pl.BlockSpec(wo.shape, lambda i: (0, 0), **const),
            pl.BlockSpec((1, d), lambda i: (0, 0)),
            pl.BlockSpec((None, None, d, dff), lambda i: (layer, half, 0, 0), **const),
            pl.BlockSpec((None, None, d, dff), lambda i: (layer, half, 0, 0), **const),
            pl.BlockSpec((None, None, dff, d), lambda i: (layer, half, 0, 0), **const),
        ],
        out_specs=pl.BlockSpec((tm, d), lambda i: (jnp.maximum(i - 1, 0), 0)),
        scratch_shapes=[pltpu.VMEM((2, tm, d), BF16), pltpu.VMEM((2, tm, d), F32)],
        compiler_params=_cparams(("arbitrary",)),
        name="proj_ffn",
    )(x, oa, ob, wo, g.reshape(1, d), wg, wu, wd)


def _ffn_pool_kernel(x_ref, xnext_ref, g_ref, wg_ref, wu_ref, wd_ref, pg_ref, w_ref, sc_ref, o_ref,
                     h_ref, y_ref, ext_ref, tmp_ref, *, ni, tm, tiles_per_batch):
    i = pl.program_id(0)
    slot = i % 2

    @pl.when(i == 0)
    def _():
        h_ref[0] = _rms(x_ref[...], g_ref[...]).astype(BF16)
        ext_ref[...] = jnp.zeros(ext_ref.shape, F32)

    def front():
        h = h_ref[slot]
        a = _dot(h, wg_ref[...])
        b = _dot(h, wu_ref[...])
        act = a * (1.0 / (1.0 + jnp.exp(-a))) * b
        y_ref[slot] = x_ref[...] + 0.5 * _dot(act.astype(BF16), wd_ref[...])
        h_ref[1 - slot] = _rms(xnext_ref[...], g_ref[...]).astype(BF16)

    pl.when(i == 0)(front)

    @pl.when((i > 0) & (i < ni))
    def _():
        _pool_back(i - 1, y_ref.at[1 - slot], pg_ref, w_ref, sc_ref, o_ref, ext_ref, tmp_ref, tm, tiles_per_batch)
        front()

    @pl.when(i == ni)
    def _():
        _pool_back(i - 1, y_ref.at[1 - slot], pg_ref, w_ref, sc_ref, o_ref, ext_ref, tmp_ref, tm, tiles_per_batch)


def _pool_back(j, x_ref, g_ref, w_ref, sc_ref, o_ref, ext_ref, tmp_ref, tm, tiles_per_batch):
    first = j % tiles_per_batch == 0
    x = x_ref[...]
    h = _rms(x, g_ref[...])
    pad, body = _POOL_PAD, _POOL_PAD + POOL_HALO
    halo = ext_ref[body + tm - POOL_HALO:body + tm, :]
    ext_ref[pad:body, :] = jnp.where(first, 0.0, halo)
    ext_ref[body:body + tm, :] = h
    gw = h.shape[1] // len(POOL_WINDOWS)
    for slot in range(2):
        tmp_ref[slot, 0:pad, :] = jnp.zeros((pad, gw), F32)
    t = (j % tiles_per_batch) * tm + lax.broadcasted_iota(jnp.int32, (tm, 1), 0)
    span = POOL_HALO + tm
    for gi, win in enumerate(POOL_WINDOWS):
        c0 = gi * gw
        hg = h[:, c0:c0 + gw]
        read = lambda start, c0=c0: ext_ref[pl.ds(start, span), c0:c0 + gw]
        shift, slot = 1, 0
        while shift < win:
            tmp_ref[slot, pad:pad + span, :] = read(pad) + read(pad - shift)
            read = lambda start, slot=slot: tmp_ref[slot, pl.ds(start, span), :]
            shift, slot = 2 * shift, 1 - slot
        tot = read(pad)[POOL_HALO:, :]
        cnt = jnp.minimum(t + 1, win).astype(F32)
        y = _dot((tot / cnt - hg).astype(BF16), w_ref[gi])
        o_ref[:, c0:c0 + gw] = x[:, c0:c0 + gw] + y * sc_ref[:, c0:c0 + gw]


def _ffn_pool(x, g, wg, wu, wd, layer, half, pool_g, pool_w, pool_scale, *, seq, tm=512):
    n, d = x.shape
    dff = wg.shape[-1]
    ni = n // tm
    once = dict(pipeline_mode=pl.Buffered(1))
    fixed = lambda i: (0, 0)
    rows = _POOL_PAD + POOL_HALO + tm
    return pl.pallas_call(
        functools.partial(_ffn_pool_kernel, ni=ni, tm=tm, tiles_per_batch=seq // tm),
        out_shape=jax.ShapeDtypeStruct((n, d), F32),
        grid=(ni + 1,),
        in_specs=[
            pl.BlockSpec((tm, d), lambda i: (jnp.minimum(i, ni - 1), 0)),
            pl.BlockSpec((tm, d), lambda i: (jnp.minimum(i + 1, ni - 1), 0)),
            pl.BlockSpec((1, d), fixed),
            pl.BlockSpec((None, None, d, dff), lambda i: (layer, half, 0, 0), **once),
            pl.BlockSpec((None, None, d, dff), lambda i: (layer, half, 0, 0), **once),
            pl.BlockSpec((None, None, dff, d), lambda i: (layer, half, 0, 0), **once),
            pl.BlockSpec((1, d), fixed),
            pl.BlockSpec(pool_w.shape, lambda i: (0, 0, 0)),
            pl.BlockSpec((1, d), fixed),
        ],
        out_specs=pl.BlockSpec((tm, d), lambda i: (jnp.maximum(i - 1, 0), 0)),
        scratch_shapes=[pltpu.VMEM((2, tm, d), BF16), pltpu.VMEM((2, tm, d), F32), pltpu.VMEM((rows, d), F32),
                        pltpu.VMEM((2, rows, d // len(POOL_WINDOWS)), F32)],
        compiler_params=_cparams(("arbitrary",)),
        name="ffn_pool",
    )(x, x, g.reshape(1, d), wg, wu, wd, pool_g.reshape(1, d), pool_w, pool_scale.reshape(1, d))


def _pack_w_in(w_in):
    o = 0
    q = w_in[:, o:o + A_WIDTH]; o += A_WIDTH
    kc, vc, ks, vs, kw, vw = [w_in[:, o + k * KV_WIDTH:o + (k + 1) * KV_WIDTH] for k in range(6)]
    o += 6 * KV_WIDTH
    gates = w_in[:, o:o + N_BRANCH * N_HEADS]; o += N_BRANCH * N_HEADS
    u, gb, gc = [w_in[:, o + k * A_WIDTH:o + (k + 1) * A_WIDTH] for k in range(3)]
    gates = gates.reshape(-1, N_GROUPS, HEADS_PER_GROUP, N_BRANCH).transpose(0, 1, 3, 2)
    gates = gates.reshape(-1, N_GROUPS, N_BRANCH * HEADS_PER_GROUP)
    gates = jnp.pad(gates, ((0, 0), (0, 0), (0, LANES - N_BRANCH * HEADS_PER_GROUP))).reshape(-1, GATE_WIDTH)
    return jnp.concatenate([q, ks, kw, vs, vw, kc, vc, gates, u, gb, gc], axis=1).astype(BF16)


def _group_mean_matrix(width):
    r = jnp.arange(width) // HEAD_DIM
    return jnp.where(r[:, None] == r[None, :], 1.0 / HEAD_DIM, 0.0).astype(BF16)


def _rope_freq_row():
    lane = jnp.arange(LANES) % HEAD_DIM
    freqs = ROPE_THETA ** (-jnp.arange(0, ROPE_DIM, 2, dtype=F32) / ROPE_DIM)
    return jnp.where(lane < ROPE_DIM, freqs[lane % ROPE_HALF], 0.0).astype(F32).reshape(1, LANES)


def _hybrid_mixer(x, positions, g, w_in, q_norm, k_norm, cmp_pos, cmp_w1, cmp_w2, conv_w, w_out, *, bsz, seq):
    n = bsz * seq
    ncp = seq // CMP_STRIDE
    freq = _rope_freq_row()
    pq = _group_mean_matrix(A_WIDTH)
    pk = _group_mean_matrix(KV_WIDTH)
    qnw = jnp.tile(q_norm, N_HEADS).reshape(1, A_WIDTH)
    knw = jnp.tile(k_norm, (1, N_GROUPS))

    q_t, k_slc, k_win, v_t, kc_raw, vc_raw, gates, ob = _inproj(
        x, positions.reshape(n, 1), g, _pack_w_in(w_in), qnw, knw, conv_w, pq, pk, freq, seq=seq)

    eye_g = jnp.eye(N_GROUPS, dtype=F32)
    seg = CMP_STRIDE * KV_WIDTH

    def seg_weights(w1):
        w1r = w1.reshape(CMP_BLOCK, HEAD_DIM, CMP_HIDDEN)
        halves = [jnp.einsum('jdh,gk->jgdkh', w1r[a:a + CMP_STRIDE], eye_g).reshape(seg, N_GROUPS * CMP_HIDDEN)
                  for a in (0, CMP_STRIDE)]
        return jnp.stack(halves).astype(BF16)

    def seg_pos(pe):
        return [jnp.broadcast_to(pe[a:a + CMP_STRIDE, None, :], (CMP_STRIDE, N_GROUPS, HEAD_DIM)).reshape(seg)
                for a in (0, CMP_STRIDE)]

    pe = jnp.stack(seg_pos(cmp_pos[0]) + seg_pos(cmp_pos[1]))
    w2 = jnp.stack([jnp.einsum('hd,gk->ghkd', cmp_w2[a], eye_g).reshape(N_GROUPS * CMP_HIDDEN, KV_WIDTH)
                    for a in range(2)]).astype(BF16)
    blk_end = jnp.minimum(jnp.arange(ncp) * CMP_STRIDE + CMP_BLOCK - 1, seq - 1)
    pos_c = positions[:, blk_end].reshape(bsz, ncp, 1)
    kc_g, vc_t = _compress(kc_raw, vc_raw, pos_c, pe, seg_weights(cmp_w1[0]), seg_weights(cmp_w1[1]), w2[0], w2[1].T,
                           knw, pk, freq, seq=seq)

    gates_t = gates.reshape(n, N_GROUPS, LANES)[:, :, :GATE_ROWS].transpose(1, 2, 0).reshape(N_GROUPS * GATE_ROWS, n)
    cstart = jnp.arange(ncp) * CMP_STRIDE
    sstart = jnp.arange(LANES) * SEL_BLOCK
    ovt = ((cstart[None, :] < sstart[:, None] + SEL_BLOCK)
           & (cstart[None, :] + CMP_BLOCK > sstart[:, None])).astype(BF16)
    o_t, sel_bias = _cmp_win_select(q_t, kc_g, vc_t, ovt, k_win, v_t, gates_t, bsz=bsz, seq=seq)
    o_a = _slc(q_t, sel_bias, k_slc, v_t, gates_t, o_t, bsz=bsz, seq=seq)

    return o_a, ob, w_out.astype(BF16)


def kernel(x, positions, ffn_norm, ffn_w_gate, ffn_w_up, ffn_w_down, mix_norm, hyb_w_in, hyb_q_norm, hyb_k_norm,
           hyb_cmp_pos, hyb_cmp_w1, hyb_cmp_w2, hyb_conv_w, hyb_w_out, pool_w, pool_scale):
    bsz, seq, d = x.shape
    depth = ffn_norm.shape[0]
    assert seq % 2048 == 0 and seq // SEL_BLOCK <= LANES and seq // SEL_BLOCK >= SEL_TOPK
    wg, wu, wd = (w.astype(BF16) for w in (ffn_w_gate, ffn_w_up, ffn_w_down))
    xf = x.reshape(bsz * seq, d)
    for layer in range(depth):
        i = layer // 2
        if layer % 2 == 0:
            xf = _ffn(xf, ffn_norm[layer, 0], wg, wu, wd, layer, 0)
            o_a, ob, wo = _hybrid_mixer(xf, positions, mix_norm[layer], hyb_w_in[i], hyb_q_norm[i], hyb_k_norm[i],
                                        hyb_cmp_pos[i], hyb_cmp_w1[i], hyb_cmp_w2[i], hyb_conv_w[i], hyb_w_out[i],
                                        bsz=bsz, seq=seq)
            xf = _proj_ffn(xf, o_a, ob, wo, ffn_norm[layer, 1], wg, wu, wd, layer, 1)
        else:
            xf = _ffn_pool(xf, ffn_norm[layer, 0], wg, wu, wd, layer, 0, mix_norm[layer], pool_w[i].astype(BF16),
                           pool_scale[i], seq=seq)
            xf = _ffn(xf, ffn_norm[layer, 1], wg, wu, wd, layer, 1)
    return xf.reshape(bsz, seq, d)
```
